```python
import math
import jax
import jax.numpy as jnp
from jax import lax
import numpy as np


D_MODEL = 1024
BATCH = 16
SEQ = 2048
DEPTH = 4

HEAD_DIM = 64
MLA_HEADS = 8
MLA_NOPE = 64
MLA_ROPE = 32
MLA_V = 64
MLA_Q_LORA = 384
MLA_KV_LORA = 256
ROPE_THETA = 10000.0
MOBA_HEADS = 8
MOBA_BLOCK = 256
MOBA_TOPK = 3
MOBA_QCHUNK = 32
SB_HEADS = 16
Q_BLOCK = 128
NUM_BUCKETS = 32
MAX_DISTANCE = 128
RMS_EPS = 1e-6
D_FF = -(-8 * D_MODEL // (3 * 256)) * 256
N_EVEN = (DEPTH + 1) // 2
N_ODD = DEPTH // 2
AB_IN = MLA_Q_LORA + MLA_KV_LORA + MLA_ROPE + 3 * MOBA_HEADS * HEAD_DIM
AB_OUT = MLA_HEADS * MLA_V + MOBA_HEADS * HEAD_DIM
SB_WIDTH = SB_HEADS * HEAD_DIM

kernel_name = 'hybrid_mla_moba_stickbreaking'


def rmsnorm(x, g):
    xf = x.astype(jnp.float32)
    y = xf * lax.rsqrt(jnp.mean(xf * xf, axis=-1, keepdims=True) + RMS_EPS)
    return (y * g.astype(jnp.float32)).astype(x.dtype)


def rope(x, pos):
    half = x.shape[-1] // 2
    inv_freq = ROPE_THETA ** (-jnp.arange(half, dtype=jnp.float32) / half)
    ang = pos.astype(jnp.float32)[:, None] * inv_freq[None, :]
    cos, sin = jnp.cos(ang), jnp.sin(ang)
    x1 = x[..., :half].astype(jnp.float32)
    x2 = x[..., half:].astype(jnp.float32)
    return jnp.concatenate([x1 * cos - x2 * sin, x1 * sin + x2 * cos], axis=-1).astype(x.dtype)


def t5_bucket(rel):
    max_exact = NUM_BUCKETS // 2
    rel = jnp.maximum(rel, 0)
    large = max_exact + (jnp.log(jnp.maximum(rel, max_exact).astype(jnp.float32) / max_exact)
                         / math.log(MAX_DISTANCE / max_exact) * (NUM_BUCKETS - max_exact)).astype(jnp.int32)
    large = jnp.minimum(large, NUM_BUCKETS - 1)
    return jnp.where(rel < max_exact, rel, large)


def mla_attention(c_q, c_kv, k_rope_in, q_norm_g, w_uq, kv_norm_g, w_ukv, pos):
    B, S, _ = c_q.shape
    q = jnp.einsum('bsr,rhd->bhsd', rmsnorm(c_q, q_norm_g),
                   w_uq.reshape(MLA_Q_LORA, MLA_HEADS, MLA_NOPE + MLA_ROPE))
    q_nope = q[..., :MLA_NOPE]
    q_rope = rope(q[..., MLA_NOPE:], pos)
    kv = jnp.einsum('bsr,rhd->bhsd', rmsnorm(c_kv, kv_norm_g),
                    w_ukv.reshape(MLA_KV_LORA, MLA_HEADS, MLA_NOPE + MLA_V))
    k_nope, v = kv[..., :MLA_NOPE], kv[..., MLA_NOPE:]
    k_rope = rope(k_rope_in, pos)
    scale = (MLA_NOPE + MLA_ROPE) ** -0.5
    n_blk = S // Q_BLOCK
    qn = q_nope.reshape(B, MLA_HEADS, n_blk, Q_BLOCK, MLA_NOPE).transpose(2, 0, 1, 3, 4)
    qr = q_rope.reshape(B, MLA_HEADS, n_blk, Q_BLOCK, MLA_ROPE).transpose(2, 0, 1, 3, 4)
    kpos = jnp.arange(S)

    def block(args):
        i, qn_b, qr_b = args
        logits = (jnp.einsum('bhqd,bhkd->bhqk', qn_b, k_nope)
                  + jnp.einsum('bhqd,bkd->bhqk', qr_b, k_rope)).astype(jnp.float32) * scale
        qpos = i * Q_BLOCK + jnp.arange(Q_BLOCK)
        logits = jnp.where(kpos[None, :] <= qpos[:, None], logits, -jnp.inf)
        p = jax.nn.softmax(logits, axis=-1).astype(v.dtype)
        return jnp.einsum('bhqk,bhkd->bhqd', p, v)

    out = lax.map(block, (jnp.arange(n_blk), qn, qr))
    return out.transpose(1, 0, 3, 2, 4).reshape(B, S, MLA_HEADS * MLA_V)


def moba_attention(q, k, v, rel_bias):
    B, H, S, d = q.shape
    nb = -(-S // MOBA_BLOCK)
    pad = nb * MOBA_BLOCK - S
    kp = jnp.pad(k, ((0, 0), (0, 0), (0, pad), (0, 0)))
    vp = jnp.pad(v, ((0, 0), (0, 0), (0, pad), (0, 0)))
    kb = kp.reshape(B, H, nb, MOBA_BLOCK, d)
    vb = vp.reshape(B, H, nb, MOBA_BLOCK, d)
    kbar = jnp.mean(kb.astype(jnp.float32), axis=3).astype(k.dtype)
    kk = min(MOBA_TOPK, max(nb - 1, 1))
    scale = d ** -0.5
    n_chunk = S // MOBA_QCHUNK
    qc_all = q.reshape(B, H, n_chunk, MOBA_QCHUNK, d).transpose(2, 0, 1, 3, 4)
    b_idx = jnp.arange(B)[:, None, None, None]
    h_idx = jnp.arange(H)[None, :, None, None]
    bias_t = rel_bias.T
    offs = jnp.arange(MOBA_BLOCK)

    def chunk(args):
        i, q_c = args
        qpos = i * MOBA_QCHUNK + jnp.arange(MOBA_QCHUNK)
        own = (i * MOBA_QCHUNK) // MOBA_BLOCK
        gate = jnp.einsum('bhqd,bhnd->bhqn', q_c, kbar).astype(jnp.float32)
        gate = jnp.where(jnp.arange(nb) < own, gate, -jnp.inf)
        _, idx = lax.top_k(gate, kk)
        valid = idx < own
        kg = kb[b_idx, h_idx, idx]
        vg = vb[b_idx, h_idx, idx]
        rel_sel = qpos[None, None, :, None, None] - (idx[..., None] * MOBA_BLOCK + offs)
        bias_sel = bias_t[h_idx[..., None], t5_bucket(rel_sel)].astype(jnp.float32)
        l_sel = jnp.einsum('bhqd,bhqknd->bhqkn', q_c, kg).astype(jnp.float32) * scale + bias_sel
        l_sel = jnp.where(valid[..., None], l_sel, -jnp.inf)
        k_own = lax.dynamic_slice_in_dim(kp, own * MOBA_BLOCK, MOBA_BLOCK, axis=2)
        v_own = lax.dynamic_slice_in_dim(vp, own * MOBA_BLOCK, MOBA_BLOCK, axis=2)
        rel_own = qpos[:, None] - (own * MOBA_BLOCK + offs)[None, :]
        bias_own = jnp.transpose(rel_bias[t5_bucket(rel_own)], (2, 0, 1)).astype(jnp.float32)
        l_own = jnp.einsum('bhqd,bhnd->bhqn', q_c, k_own).astype(jnp.float32) * scale + bias_own
        l_own = jnp.where(rel_own >= 0, l_own, -jnp.inf)
        logits = jnp.concatenate([l_sel.reshape(B, H, MOBA_QCHUNK, kk * MOBA_BLOCK), l_own], axis=-1)
        p = jax.nn.softmax(logits, axis=-1).astype(v.dtype)
        p_sel = p[..., :kk * MOBA_BLOCK].reshape(B, H, MOBA_QCHUNK, kk, MOBA_BLOCK)
        p_own = p[..., kk * MOBA_BLOCK:]
        return (jnp.einsum('bhqkn,bhqknd->bhqd', p_sel, vg)
                + jnp.einsum('bhqn,bhnd->bhqd', p_own, v_own))

    out = lax.map(chunk, (jnp.arange(n_chunk), qc_all))
    return out.transpose(1, 0, 3, 2, 4).reshape(B, S, H * d)


def stick_breaking_attention(q, k, v):
    B, H, S, d = q.shape
    scale = d ** -0.5
    n_blk = S // Q_BLOCK
    qb = q.reshape(B, H, n_blk, Q_BLOCK, d).transpose(2, 0, 1, 3, 4)
    kpos = jnp.arange(S)

    def block(args):
        i, q_b = args
        qpos = i * Q_BLOCK + jnp.arange(Q_BLOCK)
        strict = kpos[None, :] < qpos[:, None]
        z = jnp.einsum('bhqd,bhkd->bhqk', q_b, k).astype(jnp.float32) * scale
        log_beta = jax.nn.log_sigmoid(z)
        log_1m = jnp.where(strict, jax.nn.log_sigmoid(-z), 0.0)
        acc = lax.cumsum(log_1m, axis=3, reverse=True) - log_1m
        w = jnp.where(strict, jnp.exp(log_beta + acc), 0.0)
        return jnp.einsum('bhqk,bhkd->bhqd', w.astype(v.dtype), v)

    out = lax.map(block, (jnp.arange(n_blk), qb))
    return out.transpose(1, 0, 3, 2, 4).reshape(B, S, H * d)


def setup_inputs(seed: int = 0) -> dict:
    key = jax.random.key(seed)
    ks = jax.random.split(key, 17)

    def nrm(k, shape, fan_in):
        return jax.random.normal(k, shape, jnp.float32) * (fan_in ** -0.5)

    def gain(k, shape):
        return 1.0 + 0.05 * jax.random.normal(k, shape, jnp.float32)

    return {
        'x': jax.random.normal(ks[0], (BATCH, SEQ, D_MODEL), jnp.float32),
        'mix_pre_g': gain(ks[1], (DEPTH, D_MODEL)),
        'mix_post_g': gain(ks[2], (DEPTH, D_MODEL)),
        'ffn_pre_g': gain(ks[3], (DEPTH, D_MODEL)),
        'ffn_post_g': gain(ks[4], (DEPTH, D_MODEL)),
        'ab_w_in': nrm(ks[5], (N_EVEN, D_MODEL, AB_IN), D_MODEL),
        'mla_q_norm_g': gain(ks[6], (N_EVEN, MLA_Q_LORA)),
        'mla_w_uq': nrm(ks[7], (N_EVEN, MLA_Q_LORA, MLA_HEADS * (MLA_NOPE + MLA_ROPE)), MLA_Q_LORA),
        'mla_kv_norm_g': gain(ks[8], (N_EVEN, MLA_KV_LORA)),
        'mla_w_ukv': nrm(ks[9], (N_EVEN, MLA_KV_LORA, MLA_HEADS * (MLA_NOPE + MLA_V)), MLA_KV_LORA),
        'ab_w_out': nrm(ks[10], (N_EVEN, AB_OUT, D_MODEL), AB_OUT),
        'rel_bias': 0.5 * jax.random.normal(ks[11], (NUM_BUCKETS, MOBA_HEADS), jnp.float32),
        'sb_w_qkv': nrm(ks[12], (N_ODD, D_MODEL, 3 * SB_WIDTH), D_MODEL),
        'sb_w_out': nrm(ks[13], (N_ODD, SB_WIDTH, D_MODEL), SB_WIDTH),
        'ffn_w_gate_up': nrm(ks[14], (DEPTH, D_MODEL, 2 * D_FF), D_MODEL),
        'ffn_w_down': nrm(ks[15], (DEPTH, D_FF, D_MODEL), D_FF),
    }


def reference(x, mix_pre_g, mix_post_g, ffn_pre_g, ffn_post_g, ab_w_in, mla_q_norm_g, mla_w_uq,
              mla_kv_norm_g, mla_w_ukv, ab_w_out, rel_bias, sb_w_qkv, sb_w_out, ffn_w_gate_up, ffn_w_down):
    B, S, _ = x.shape
    pos = jnp.arange(S)
    o1 = MLA_Q_LORA
    o2 = o1 + MLA_KV_LORA
    o3 = o2 + MLA_ROPE
    h = x
    for layer in range(DEPTH):
        u = rmsnorm(h, mix_pre_g[layer])
        if layer % 2 == 0:
            e = layer // 2
            proj = u @ ab_w_in[e]
            mla_out = mla_attention(proj[..., :o1], proj[..., o1:o2], proj[..., o2:o3],
                                    mla_q_norm_g[e], mla_w_uq[e], mla_kv_norm_g[e], mla_w_ukv[e], pos)
            qkv = proj[..., o3:].reshape(B, S, 3, MOBA_HEADS, HEAD_DIM).transpose(2, 0, 3, 1, 4)
            moba_out = moba_attention(qkv[0], qkv[1], qkv[2], rel_bias)
            mixed = jnp.concatenate([mla_out, moba_out], axis=-1) @ ab_w_out[e]
        else:
            o = layer // 2
            qkv = (u @ sb_w_qkv[o]).reshape(B, S, 3, SB_HEADS, HEAD_DIM).transpose(2, 0, 3, 1, 4)
            mixed = stick_breaking_attention(qkv[0], qkv[1], qkv[2]) @ sb_w_out[o]
        h = h + rmsnorm(mixed, mix_post_g[layer])
        u = rmsnorm(h, ffn_pre_g[layer])
        gu = u @ ffn_w_gate_up[layer]
        f = (jax.nn.silu(gu[..., :D_FF]) * gu[..., D_FF:]) @ ffn_w_down[layer]
        h = h + rmsnorm(f, ffn_post_g[layer])
    return h
```

```python
import functools
import math

import numpy as np
import jax
import jax.numpy as jnp
from jax import lax
from jax.experimental import pallas as pl
from jax.experimental.pallas import tpu as pltpu

D_MODEL = 1024
DEPTH = 4
HEAD_DIM = 64
MLA_HEADS = 8
MLA_NOPE = 64
MLA_ROPE = 32
MLA_V = 64
MLA_Q_LORA = 384
MLA_KV_LORA = 256
ROPE_THETA = 10000.0
MOBA_HEADS = 8
MOBA_BLOCK = 256
MOBA_TOPK = 3
SB_HEADS = 16
NUM_BUCKETS = 32
MAX_DISTANCE = 128
RMS_EPS = 1e-6
D_FF = 2816

LANES = 128
TQ = 256
TM = 512
FF_CHUNK = 256
MASK_PENALTY = -1e30
VMEM_LIMIT = 48 * 1024 * 1024

F32 = jnp.float32
BF16 = jnp.bfloat16


def _rms(x, g):
    return x * lax.rsqrt(jnp.mean(x * x, axis=-1, keepdims=True) + RMS_EPS) * g


def _dot(a, b):
    return jnp.dot(a, b, preferred_element_type=F32)


def _dot_nt(a, b):
    return lax.dot_general(a, b, (((1,), (1,)), ((), ())), preferred_element_type=F32)


def _params(*semantics):
    return pltpu.CompilerParams(dimension_semantics=semantics, vmem_limit_bytes=VMEM_LIMIT)


def _whole(shape):
    nd = len(shape)
    return pl.BlockSpec(shape, lambda *_: (0,) * nd)


def _norm_matmul_kernel(h_ref, g_ref, w_ref, o_ref):
    u = _rms(h_ref[...], g_ref[...]).astype(BF16)
    for c in range(0, o_ref.shape[-1], 512):
        o_ref[:, c:c + 512] = _dot(u, w_ref[:, c:c + 512]).astype(o_ref.dtype)


def _norm_matmul(h, g, w):
    n, d = h.shape
    n_out = w.shape[1]
    return pl.pallas_call(
        _norm_matmul_kernel,
        grid=(n // TM,),
        in_specs=[pl.BlockSpec((TM, d), lambda t: (t, 0)), _whole((1, d)), _whole(w.shape)],
        out_specs=pl.BlockSpec((TM, n_out), lambda t: (t, 0)),
        out_shape=jax.ShapeDtypeStruct((n, n_out), BF16),
        compiler_params=_params("arbitrary"),
        name="norm_matmul",
    )(h, g.reshape(1, d), w)


_O_CQ = 0
_O_CKV = MLA_Q_LORA
_O_KR = _O_CKV + MLA_KV_LORA
_O_KRR = _O_KR + LANES
_O_MOBA = _O_KRR + LANES
_W_IN_EXT = _O_MOBA + 3 * MOBA_HEADS * HEAD_DIM
_QW = MLA_HEADS * LANES


def _even_in_kernel(h_ref, g_ref, w_in_ref, gq_ref, w_uq_ref, gkv_ref, w_k_ref, w_v_ref,
                    cq_ref, sq_ref, ck_ref, sk_ref, q_ref, k_ref, v_ref, moba_ref):
    u = _rms(h_ref[...], g_ref[...]).astype(BF16)
    c_q = _dot(u, w_in_ref[:, _O_CQ:_O_CKV])
    c_kv = _dot(u, w_in_ref[:, _O_CKV:_O_KR])
    kr = _dot(u, w_in_ref[:, _O_KR:_O_KRR])
    krr = _dot(u, w_in_ref[:, _O_KRR:_O_MOBA])
    for c in range(0, 3 * MOBA_HEADS * HEAD_DIM, 512):
        moba_ref[:, c:c + 512] = _dot(u, w_in_ref[:, _O_MOBA + c:_O_MOBA + c + 512]).astype(BF16)

    cqn = _rms(c_q, gq_ref[...]).astype(BF16)
    qa = _dot(cqn, w_uq_ref[:, :_QW])
    qb = _dot(cqn, w_uq_ref[:, _QW:])
    cq, sq = cq_ref[...], sq_ref[...]
    for h in range(MLA_HEADS):
        s = slice(h * LANES, (h + 1) * LANES)
        q_ref[:, s] = (qa[:, s] * cq + qb[:, s] * sq).astype(BF16)

    ckvn = _rms(c_kv, gkv_ref[...]).astype(BF16)
    kn = _dot(ckvn, w_k_ref[...])
    v_ref[...] = _dot(ckvn, w_v_ref[...]).astype(BF16)
    k_rope = kr * ck_ref[...] + krr * sk_ref[...]
    for h in range(MLA_HEADS):
        s = slice(h * LANES, (h + 1) * LANES)
        k_ref[:, s] = (kn[:, s] + k_rope).astype(BF16)


def _even_in(h, g, w_in, gq, w_uq, gkv, w_k, w_v, tables, seq):
    n, d = h.shape
    row = lambda t: (t, 0)
    pos = lambda t: (t % (seq // TM), 0)
    tab = pl.BlockSpec((TM, LANES), pos)
    outs = [(_QW, "q"), (_QW, "k"), (_QW, "v"), (3 * MOBA_HEADS * HEAD_DIM, "moba")]
    return pl.pallas_call(
        _even_in_kernel,
        grid=(n // TM,),
        in_specs=[pl.BlockSpec((TM, d), row), _whole((1, d)), _whole(w_in.shape),
                  _whole((1, MLA_Q_LORA)), _whole(w_uq.shape), _whole((1, MLA_KV_LORA)),
                  _whole(w_k.shape), _whole(w_v.shape), tab, tab, tab, tab],
        out_specs=[pl.BlockSpec((TM, w), row) for w, _ in outs],
        out_shape=[jax.ShapeDtypeStruct((n, w), BF16) for w, _ in outs],
        compiler_params=_params("arbitrary"),
        name="even_in",
    )(h, g.reshape(1, d), w_in, gq.reshape(1, -1), w_uq, gkv.reshape(1, -1), w_k, w_v, *tables)


def _mla_kernel(q_ref, k_ref, v_ref, o_ref):
    i = pl.program_id(2)
    row = lax.broadcasted_iota(jnp.int32, (TQ, TQ), 0)
    col = lax.broadcasted_iota(jnp.int32, (TQ, TQ), 1)
    causal = col <= row
    out = jnp.zeros((TQ, LANES), F32)
    for hh in range(2):
        lanes = slice(hh * LANES, (hh + 1) * LANES)
        q = q_ref[0, :, lanes]

        def scores(j):
            rows = pl.ds(pl.multiple_of(j * TQ, TQ), TQ)
            return _dot_nt(q, k_ref[0, rows, lanes]), v_ref[0, rows, lanes]

        s, v = scores(i)
        s = jnp.where(causal, s, -jnp.inf)
        m = jnp.max(s, axis=-1, keepdims=True)
        p = jnp.exp(s - m)
        l = jnp.sum(p, axis=-1, keepdims=True)
        acc = _dot(p.astype(BF16), v)

        def body(j, carry):
            m, l, acc = carry
            s, v = scores(j)
            m_new = jnp.maximum(m, jnp.max(s, axis=-1, keepdims=True))
            alpha = jnp.exp(m - m_new)
            p = jnp.exp(s - m_new)
            l = alpha * l + jnp.sum(p, axis=-1, keepdims=True)
            acc = alpha * acc + _dot(p.astype(BF16), v)
            return m_new, l, acc

        m, l, acc = lax.fori_loop(0, i, body, (m, l, acc))
        out = out + acc / l
    o_ref[0] = out.astype(o_ref.dtype)


def _mla_attention(q, k, v):
    b, s, _ = q.shape
    pairs = MLA_HEADS // 2
    kv = pl.BlockSpec((1, s, 2 * LANES), lambda bi, p, i: (bi, 0, p))
    return pl.pallas_call(
        _mla_kernel,
        grid=(b, pairs, s // TQ),
        in_specs=[pl.BlockSpec((1, TQ, 2 * LANES), lambda bi, p, i: (bi, i, p)), kv, kv],
        out_specs=pl.BlockSpec((1, TQ, LANES), lambda bi, p, i: (bi, i, p)),
        out_shape=jax.ShapeDtypeStruct((b, s, MLA_HEADS * MLA_V), BF16),
        compiler_params=_params("arbitrary", "arbitrary", "arbitrary"),
        name="mla_attention",
    )(q, k, v)


def _head_lanes(shape, hh):
    lane = lax.broadcasted_iota(jnp.int32, shape, 1)
    return (lane >= hh * HEAD_DIM) & (lane < (hh + 1) * HEAD_DIM)


def _store_masked_v(v_ref, vm_scr):
    v = v_ref[0].astype(F32)
    for hh in range(2):
        vm_scr[hh] = jnp.where(_head_lanes(v.shape, hh), v, 0.0).astype(BF16)


def _sb_kernel(q_ref, k_ref, v_ref, o_ref, vm_scr):
    i = pl.program_id(2)

    @pl.when(i == 0)
    def _():
        _store_masked_v(v_ref, vm_scr)

    row = lax.broadcasted_iota(jnp.int32, (TQ, TQ), 0)
    col = lax.broadcasted_iota(jnp.int32, (TQ, TQ), 1)
    strict = col < row
    tri = jnp.where(row >= col, 1.0, 0.0).astype(BF16)
    qf = q_ref[0].astype(F32)
    out = jnp.zeros((TQ, LANES), F32)
    for hh in range(2):
        q = jnp.where(_head_lanes(qf.shape, hh), qf, 0.0).astype(BF16)

        def tile(j, carry, acc, diag):
            rows = pl.ds(pl.multiple_of(j * TQ, TQ), TQ)
            z = _dot_nt(q, k_ref[0, rows, :])
            sp = jnp.maximum(z, 0.0) + jnp.log(1.0 + jnp.exp(-jnp.abs(z)))
            if diag:
                sp = jnp.where(strict, sp, 0.0)
            hi = sp.astype(BF16)
            lo = (sp - hi.astype(F32)).astype(BF16)
            c = _dot(hi, tri) + _dot(lo, tri)
            w = jnp.exp(z - c - carry)
            if diag:
                w = jnp.where(strict, w, 0.0)
            acc = acc + _dot(w.astype(BF16), vm_scr[hh, rows, :])
            return carry + c[:, 0:1], acc

        carry, acc = tile(i, jnp.zeros((TQ, 1), F32), jnp.zeros((TQ, LANES), F32), True)
        carry, acc = lax.fori_loop(
            0, i, lambda t, ca: tile(i - 1 - t, ca[0], ca[1], False), (carry, acc))
        out = out + acc
    o_ref[0] = out.astype(o_ref.dtype)


def _sb_attention(qkv):
    b, s, _ = qkv.shape
    pairs = SB_HEADS // 2
    return pl.pallas_call(
        _sb_kernel,
        grid=(b, pairs, s // TQ),
        in_specs=[pl.BlockSpec((1, TQ, LANES), lambda bi, p, i: (bi, i, p)),
                  pl.BlockSpec((1, s, LANES), lambda bi, p, i: (bi, 0, pairs + p)),
                  pl.BlockSpec((1, s, LANES), lambda bi, p, i: (bi, 0, 2 * pairs + p))],
        out_specs=pl.BlockSpec((1, TQ, LANES), lambda bi, p, i: (bi, i, p)),
        out_shape=jax.ShapeDtypeStruct((b, s, SB_HEADS * HEAD_DIM), BF16),
        scratch_shapes=[pltpu.VMEM((2, s, LANES), BF16)],
        compiler_params=_params("arbitrary", "arbitrary", "arbitrary"),
        name="sb_attention",
    )(qkv, qkv, qkv)


def _t5_bucket_np(rel):
    max_exact = NUM_BUCKETS // 2
    rel = np.maximum(rel, 0)
    large = max_exact + (np.log(np.maximum(rel, max_exact) / max_exact)
                         / math.log(MAX_DISTANCE / max_exact) * (NUM_BUCKETS - max_exact)).astype(np.int64)
    return np.where(rel < max_exact, rel, np.minimum(large, NUM_BUCKETS - 1)).astype(np.int32)


def _bias_kernel(bucket_ref, rel_bias_ref, o_ref):
    h = pl.program_id(0)
    for plane in range(2):
        bk = bucket_ref[plane]
        acc = jnp.full(bk.shape, -jnp.inf, F32)
        for b in range(NUM_BUCKETS):
            acc = jnp.where(bk == b, rel_bias_ref[b, h], acc)
        o_ref[0, plane] = acc
    o_ref[0, 2] = jnp.full((TQ, TQ), rel_bias_ref[NUM_BUCKETS - 1, h], F32)


def _moba_bias_tables(rel_bias):
    off = np.arange(TQ)
    rel_own = off[:, None] - off[None, :]
    own = np.where(rel_own >= 0, _t5_bucket_np(rel_own), -1)
    adj = _t5_bucket_np(rel_own + MOBA_BLOCK)
    buckets = jnp.asarray(np.stack([own, adj]).astype(np.int32))
    return pl.pallas_call(
        _bias_kernel,
        grid=(MOBA_HEADS,),
        in_specs=[_whole(buckets.shape), pl.BlockSpec(memory_space=pltpu.SMEM)],
        out_specs=pl.BlockSpec((1, 3, TQ, TQ), lambda h: (h, 0, 0, 0)),
        out_shape=jax.ShapeDtypeStruct((MOBA_HEADS, 3, TQ, TQ), F32),
        compiler_params=_params("arbitrary"),
        name="moba_bias_tables",
    )(buckets, rel_bias)


_N_BLK_LANES = 8


def _sel_lane_base(hh):
    return HEAD_DIM * (1 - hh)


def _moba_constants(nb):
    assert nb <= _N_BLK_LANES
    diff = np.zeros((LANES, LANES), np.float32)
    count = np.zeros((2, LANES, LANES), np.float32)
    for a in range(nb):
        for b in range(nb):
            if a != b:
                diff[a * 8 + b, a] += 1.0
                diff[a * 8 + b, b] -= 1.0
                for hh in range(2):
                    count[hh, a * 8 + b, _sel_lane_base(hh) + b] = 1.0
    return jnp.asarray(diff, BF16), jnp.asarray(count, BF16)


def _moba_kernel(q_ref, k_ref, v_ref, bias_ref, diff_ref, count_ref, o_ref,
                 kaug_scr, vm_scr, kdiff_scr, *, nb):
    i = pl.program_id(2)
    lane = lax.broadcasted_iota(jnp.int32, (TQ, LANES), 1)

    @pl.when(i == 0)
    def _():
        _store_masked_v(v_ref, vm_scr)
        kf = k_ref[0].astype(F32)
        s = kf.shape[0]
        kbar = jnp.mean(kf.reshape(nb, MOBA_BLOCK, LANES), axis=1)
        kbar = jnp.concatenate([kbar, jnp.zeros((LANES - nb, LANES), F32)], axis=0)
        p0 = kbar.astype(BF16)
        r1 = kbar - p0.astype(F32)
        p1 = r1.astype(BF16)
        p2 = (r1 - p1.astype(F32)).astype(BF16)
        d = diff_ref[...]
        kdiff = _dot(d, p0) + _dot(d, p1) + _dot(d, p2)
        big_lane = lax.broadcasted_iota(jnp.int32, (s, LANES), 1)
        blk = lax.broadcasted_iota(jnp.int32, (s, LANES), 0) // MOBA_BLOCK
        for hh in range(2):
            kdiff_scr[hh] = jnp.where(_head_lanes(kdiff.shape, hh), kdiff, 0.0).astype(BF16)
            base = _sel_lane_base(hh)
            in_sel = (big_lane >= base) & (big_lane < base + _N_BLK_LANES)
            ind = jnp.where(big_lane - base == blk, 1.0, 0.0)
            kaug_scr[hh] = jnp.where(in_sel, ind, kf).astype(BF16)

    qf = q_ref[0].astype(F32)
    pair_a = lane // 8
    pair_b = lane % 8
    out = jnp.zeros((TQ, LANES), F32)
    for hh in range(2):
        qm = jnp.where(_head_lanes(qf.shape, hh), qf, 0.0)
        g = _dot_nt(qm.astype(BF16), kdiff_scr[hh])
        beats = (g > 0.0) | ((g == 0.0) & (pair_a < pair_b))
        beats = beats & (pair_a < i) & (lane < 8 * _N_BLK_LANES)
        rank = _dot(jnp.where(beats, 1.0, 0.0).astype(BF16), count_ref[hh])
        base = _sel_lane_base(hh)
        blk_b = lane - base
        in_sel = (blk_b >= 0) & (blk_b < _N_BLK_LANES)
        chosen = (blk_b == i) | ((blk_b < i) & (rank < float(MOBA_TOPK)))
        q = jnp.where(in_sel, jnp.where(chosen, 0.0, MASK_PENALTY), qm).astype(BF16)

        def scores(j, plane):
            rows = pl.ds(pl.multiple_of(j * TQ, TQ), TQ)
            s = _dot_nt(q, kaug_scr[hh, rows, :]) + bias_ref[hh, plane]
            return s, vm_scr[hh, rows, :]

        s, v = scores(i, 0)
        m = jnp.max(s, axis=-1, keepdims=True)
        p = jnp.exp(s - m)
        l = jnp.sum(p, axis=-1, keepdims=True)
        acc = _dot(p.astype(BF16), v)

        def body(j, carry):
            m, l, acc = carry
            s, v = scores(j, jnp.where(j == i - 1, 1, 2))
            m_new = jnp.maximum(m, jnp.max(s, axis=-1, keepdims=True))
            alpha = jnp.exp(m - m_new)
            p = jnp.exp(s - m_new)
            l = alpha * l + jnp.sum(p, axis=-1, keepdims=True)
            acc = alpha * acc + _dot(p.astype(BF16), v)
            return m_new, l, acc

        m, l, acc = lax.fori_loop(0, i, body, (m, l, acc))
        out = out + acc / l
    o_ref[0] = out.astype(o_ref.dtype)


def _moba_attention(qkv, bias_tables):
    b, s, _ = qkv.shape
    nb = s // MOBA_BLOCK
    pairs = MOBA_HEADS // 2
    diff, count = _moba_constants(nb)
    return pl.pallas_call(
        functools.partial(_moba_kernel, nb=nb),
        grid=(b, pairs, nb),
        in_specs=[pl.BlockSpec((1, TQ, LANES), lambda bi, p, i: (bi, i, p)),
                  pl.BlockSpec((1, s, LANES), lambda bi, p, i: (bi, 0, pairs + p)),
                  pl.BlockSpec((1, s, LANES), lambda bi, p, i: (bi, 0, 2 * pairs + p)),
                  pl.BlockSpec((2, 3, TQ, TQ), lambda bi, p, i: (p, 0, 0, 0)),
                  _whole(diff.shape), _whole(count.shape)],
        out_specs=pl.BlockSpec((1, TQ, LANES), lambda bi, p, i: (bi, i, p)),
        out_shape=jax.ShapeDtypeStruct((b, s, MOBA_HEADS * HEAD_DIM), BF16),
        scratch_shapes=[pltpu.VMEM((2, s, LANES), BF16), pltpu.VMEM((2, s, LANES), BF16),
                        pltpu.VMEM((2, LANES, LANES), BF16)],
        compiler_params=_params("arbitrary", "arbitrary", "arbitrary"),
        name="moba_attention",
    )(qkv, qkv, qkv, bias_tables, diff, count)


def _proj_residual_kernel(*refs, n_in):
    a_refs, w_refs = refs[:n_in], refs[n_in:2 * n_in]
    g_ref, h_ref, o_ref = refs[2 * n_in:]
    mixed = _dot(a_refs[0][...], w_refs[0][...])
    for a_ref, w_ref in zip(a_refs[1:], w_refs[1:]):
        mixed = mixed + _dot(a_ref[...], w_ref[...])
    o_ref[...] = h_ref[...] + _rms(mixed, g_ref[...])


def _proj_residual(acts, weights, g, h):
    n, d = h.shape
    row = lambda t: (t, 0)
    return pl.pallas_call(
        functools.partial(_proj_residual_kernel, n_in=len(acts)),
        grid=(n // TM,),
        in_specs=([pl.BlockSpec((TM, a.shape[1]), row) for a in acts]
                  + [_whole(w.shape) for w in weights]
                  + [_whole((1, d)), pl.BlockSpec((TM, d), row)]),
        out_specs=pl.BlockSpec((TM, d), row),
        out_shape=jax.ShapeDtypeStruct((n, d), F32),
        compiler_params=_params("arbitrary"),
        name="proj_residual",
    )(*acts, *weights, g.reshape(1, d), h)


def _ffn_kernel(h_ref, gpre_ref, wgu_ref, wd_ref, gpost_ref, o_ref, act_scr):
    x = h_ref[...]
    u = _rms(x, gpre_ref[...]).astype(BF16)
    for c in range(0, D_FF, FF_CHUNK):
        gate = _dot(u, wgu_ref[:, c:c + FF_CHUNK])
        up = _dot(u, wgu_ref[:, D_FF + c:D_FF + c + FF_CHUNK])
        act_scr[:, c:c + FF_CHUNK] = (gate * jax.nn.sigmoid(gate) * up).astype(BF16)
    f = _dot(act_scr[...], wd_ref[...])
    o_ref[...] = x + _rms(f, gpost_ref[...])


def _ffn(h, gpre, wgu, wd, gpost):
    n, d = h.shape
    row = lambda t: (t, 0)
    return pl.pallas_call(
        _ffn_kernel,
        grid=(n // TM,),
        in_specs=[pl.BlockSpec((TM, d), row), _whole((1, d)), _whole(wgu.shape),
                  _whole(wd.shape), _whole((1, d))],
        out_specs=pl.BlockSpec((TM, d), row),
        out_shape=jax.ShapeDtypeStruct((n, d), F32),
        scratch_shapes=[pltpu.VMEM((TM, D_FF), BF16)],
        compiler_params=_params("arbitrary"),
        name="ffn",
    )(h, gpre.reshape(1, d), wgu, wd, gpost.reshape(1, d))


def _rot_half_cols(w):
    half = w.shape[-1] // 2
    return jnp.concatenate([-w[..., half:], w[..., :half]], axis=-1)


def _prep_even_weights(w_in, w_uq, w_ukv):
    d = w_in.shape[0]
    o1, o2, o3 = MLA_Q_LORA, MLA_Q_LORA + MLA_KV_LORA, MLA_Q_LORA + MLA_KV_LORA + MLA_ROPE
    hw = MOBA_HEADS * HEAD_DIM
    z = lambda *shape: jnp.zeros(shape, F32)
    kr = w_in[:, o2:o3]
    w_in_ext = jnp.concatenate(
        [w_in[:, :o2],
         z(d, MLA_NOPE), kr, z(d, LANES - MLA_NOPE - MLA_ROPE),
         z(d, MLA_NOPE), _rot_half_cols(kr), z(d, LANES - MLA_NOPE - MLA_ROPE),
         w_in[:, o3:o3 + hw] * (HEAD_DIM ** -0.5), w_in[:, o3 + hw:]], axis=1).astype(BF16)

    wq = w_uq.reshape(MLA_Q_LORA, MLA_HEADS, MLA_NOPE + MLA_ROPE)
    nope, rp = wq[..., :MLA_NOPE], wq[..., MLA_NOPE:]
    tail = z(MLA_Q_LORA, MLA_HEADS, LANES - MLA_NOPE - MLA_ROPE)
    qa = jnp.concatenate([nope, rp, tail], axis=-1).reshape(MLA_Q_LORA, _QW)
    qb = jnp.concatenate([jnp.zeros_like(nope), _rot_half_cols(rp), tail], axis=-1).reshape(MLA_Q_LORA, _QW)
    w_uq_ext = jnp.concatenate([qa, qb], axis=1).astype(BF16)

    wkv = w_ukv.reshape(MLA_KV_LORA, MLA_HEADS, MLA_NOPE + MLA_V)
    k_nope, v = wkv[..., :MLA_NOPE], wkv[..., MLA_NOPE:]
    w_k = jnp.concatenate([k_nope, jnp.zeros_like(k_nope)], axis=-1).reshape(MLA_KV_LORA, _QW).astype(BF16)
    v = v.reshape(MLA_KV_LORA, MLA_HEADS // 2, 2, MLA_V)
    zero = jnp.zeros_like(v[:, :, 0])
    w_v = jnp.stack([jnp.concatenate([v[:, :, 0], zero], axis=-1),
                     jnp.concatenate([zero, v[:, :, 1]], axis=-1)], axis=2)
    w_v = w_v.reshape(MLA_KV_LORA, _QW).astype(BF16)
    return w_in_ext, w_uq_ext, w_k, w_v


def _rope_tables(seq):
    half = MLA_ROPE // 2
    inv_freq = ROPE_THETA ** (-jnp.arange(half, dtype=F32) / half)
    ang = jnp.arange(seq).astype(F32)[:, None] * inv_freq[None, :]
    cos, sin = jnp.cos(ang), jnp.sin(ang)
    ones, zeros = jnp.ones((seq, MLA_NOPE), F32), jnp.zeros((seq, MLA_NOPE), F32)
    tail = jnp.zeros((seq, LANES - MLA_NOPE - MLA_ROPE), F32)
    scale = (MLA_NOPE + MLA_ROPE) ** -0.5
    cq = jnp.concatenate([ones, cos, cos, tail], axis=1) * scale
    sq = jnp.concatenate([zeros, sin, sin, tail], axis=1) * scale
    ck = jnp.concatenate([zeros, cos, cos, tail], axis=1)
    sk = jnp.concatenate([zeros, sin, sin, tail], axis=1)
    return cq, sq, ck, sk


def kernel(x, mix_pre_g, mix_post_g, ffn_pre_g, ffn_post_g, ab_w_in, mla_q_norm_g, mla_w_uq,
           mla_kv_norm_g, mla_w_ukv, ab_w_out, rel_bias, sb_w_qkv, sb_w_out, ffn_w_gate_up, ffn_w_down):
    b, s, d = x.shape
    n = b * s
    h = x.reshape(n, d)
    tables = _rope_tables(s)
    bias_tables = _moba_bias_tables(rel_bias)
    mla_w = MLA_HEADS * MLA_V
    for layer in range(DEPTH):
        if layer % 2 == 0:
            e = layer // 2
            w_in, w_uq, w_k, w_v = _prep_even_weights(ab_w_in[e], mla_w_uq[e], mla_w_ukv[e])
            q, k, v, moba_qkv = _even_in(h, mix_pre_g[layer], w_in, mla_q_norm_g[e], w_uq,
                                         mla_kv_norm_g[e], w_k, w_v, tables, s)
            mla_out = _mla_attention(q.reshape(b, s, -1), k.reshape(b, s, -1), v.reshape(b, s, -1))
            moba_out = _moba_attention(moba_qkv.reshape(b, s, -1), bias_tables)
            w_out = ab_w_out[e].astype(BF16)
            h = _proj_residual([mla_out.reshape(n, -1), moba_out.reshape(n, -1)],
                               [w_out[:mla_w], w_out[mla_w:]], mix_post_g[layer], h)
        else:
            o = layer // 2
            width = SB_HEADS * HEAD_DIM
            w_qkv = jnp.concatenate([sb_w_qkv[o][:, :width] * (HEAD_DIM ** -0.5),
                                     sb_w_qkv[o][:, width:]], axis=1).astype(BF16)
            qkv = _norm_matmul(h, mix_pre_g[layer], w_qkv)
            sb_out = _sb_attention(qkv.reshape(b, s, -1))
            h = _proj_residual([sb_out.reshape(n, -1)], [sb_w_out[o].astype(BF16)], mix_post_g[layer], h)
        h = _ffn(h, ffn_pre_g[layer], ffn_w_gate_up[layer].astype(BF16),
                 ffn_w_down[layer].astype(BF16), ffn_post_g[layer])
    return h.reshape(b, s, d)
```

```python
import functools
import math

import numpy as np
import jax
import jax.numpy as jnp
from jax import lax
from jax.experimental import pallas as pl
from jax.experimental.pallas import tpu as pltpu

D_MODEL = 1024
DEPTH = 4
HEAD_DIM = 64
MLA_HEADS = 8
MLA_NOPE = 64
MLA_ROPE = 32
MLA_V = 64
MLA_Q_LORA = 384
MLA_KV_LORA = 256
ROPE_THETA = 10000.0
MOBA_HEADS = 8
MOBA_BLOCK = 256
MOBA_TOPK = 3
SB_HEADS = 16
NUM_BUCKETS = 32
MAX_DISTANCE = 128
RMS_EPS = 1e-6
D_FF = 2816

LANES = 128
TQ = 256
TM = 512
FF_CHUNK = 256
MASK_PENALTY = -1e30
LOG2_E = 1.0 / math.log(2.0)
VMEM_LIMIT = 48 * 1024 * 1024

F32 = jnp.float32
BF16 = jnp.bfloat16


def _rms(x, g):
    return x * lax.rsqrt(jnp.mean(x * x, axis=-1, keepdims=True) + RMS_EPS) * g


def _dot(a, b):
    return jnp.dot(a, b, preferred_element_type=F32)


def _dot_nt(a, b):
    return lax.dot_general(a, b, (((1,), (1,)), ((), ())), preferred_element_type=F32)


def _params(*semantics):
    return pltpu.CompilerParams(dimension_semantics=semantics, vmem_limit_bytes=VMEM_LIMIT)


def _whole(shape):
    nd = len(shape)
    return pl.BlockSpec(shape, lambda *_: (0,) * nd)


def _norm_matmul_kernel(h_ref, g_ref, w_ref, o_ref, *, q_cols, q_scale):
    u = _rms(h_ref[...], g_ref[...]).astype(BF16)
    for c in range(0, o_ref.shape[-1], 512):
        y = _dot(u, w_ref[:, c:c + 512])
        if c < q_cols:
            y = y * q_scale
        o_ref[:, c:c + 512] = y.astype(o_ref.dtype)


def _norm_matmul(h, g, w, q_cols, q_scale):
    n, d = h.shape
    n_out = w.shape[1]
    return pl.pallas_call(
        functools.partial(_norm_matmul_kernel, q_cols=q_cols, q_scale=q_scale),
        grid=(n // TM,),
        in_specs=[pl.BlockSpec((TM, d), lambda t: (t, 0)), _whole((1, d)), _whole(w.shape)],
        out_specs=pl.BlockSpec((TM, n_out), lambda t: (t, 0)),
        out_shape=jax.ShapeDtypeStruct((n, n_out), BF16),
        compiler_params=_params("arbitrary"),
        name="norm_matmul",
    )(h, g.reshape(1, d), w)


_O_CQ = 0
_O_CKV = MLA_Q_LORA
_O_KR = _O_CKV + MLA_KV_LORA
_O_KRR = _O_KR + LANES
_O_MOBA = _O_KRR + LANES
_W_IN_EXT = _O_MOBA + 3 * MOBA_HEADS * HEAD_DIM
_QW = MLA_HEADS * LANES


def _even_in_kernel(h_ref, g_ref, w_in_ref, gq_ref, w_uq_ref, gkv_ref, w_k_ref, w_v_ref,
                    cq_ref, sq_ref, ck_ref, sk_ref, q_ref, k_ref, v_ref, moba_ref):
    u = _rms(h_ref[...], g_ref[...]).astype(BF16)
    c_q = _dot(u, w_in_ref[:, _O_CQ:_O_CKV])
    c_kv = _dot(u, w_in_ref[:, _O_CKV:_O_KR])
    kr = _dot(u, w_in_ref[:, _O_KR:_O_KRR])
    krr = _dot(u, w_in_ref[:, _O_KRR:_O_MOBA])
    for c in range(0, 3 * MOBA_HEADS * HEAD_DIM, 512):
        moba_ref[:, c:c + 512] = _dot(u, w_in_ref[:, _O_MOBA + c:_O_MOBA + c + 512]).astype(BF16)

    cqn = _rms(c_q, gq_ref[...]).astype(BF16)
    qa = _dot(cqn, w_uq_ref[:, :_QW])
    qb = _dot(cqn, w_uq_ref[:, _QW:])
    cq, sq = cq_ref[...], sq_ref[...]
    for h in range(MLA_HEADS):
        s = slice(h * LANES, (h + 1) * LANES)
        q_ref[:, s] = (qa[:, s] * cq + qb[:, s] * sq).astype(BF16)

    ckvn = _rms(c_kv, gkv_ref[...]).astype(BF16)
    kn = _dot(ckvn, w_k_ref[...])
    v_ref[...] = _dot(ckvn, w_v_ref[...]).astype(BF16)
    k_rope = kr * ck_ref[...] + krr * sk_ref[...]
    for h in range(MLA_HEADS):
        s = slice(h * LANES, (h + 1) * LANES)
        k_ref[:, s] = (kn[:, s] + k_rope).astype(BF16)


def _even_in(h, g, w_in, gq, w_uq, gkv, w_k, w_v, tables, seq):
    n, d = h.shape
    row = lambda t: (t, 0)
    pos = lambda t: (t % (seq // TM), 0)
    tab = pl.BlockSpec((TM, LANES), pos)
    outs = [(_QW, "q"), (_QW, "k"), (_QW, "v"), (3 * MOBA_HEADS * HEAD_DIM, "moba")]
    return pl.pallas_call(
        _even_in_kernel,
        grid=(n // TM,),
        in_specs=[pl.BlockSpec((TM, d), row), _whole((1, d)), _whole(w_in.shape),
                  _whole((1, MLA_Q_LORA)), _whole(w_uq.shape), _whole((1, MLA_KV_LORA)),
                  _whole(w_k.shape), _whole(w_v.shape), tab, tab, tab, tab],
        out_specs=[pl.BlockSpec((TM, w), row) for w, _ in outs],
        out_shape=[jax.ShapeDtypeStruct((n, w), BF16) for w, _ in outs],
        compiler_params=_params("arbitrary"),
        name="even_in",
    )(h, g.reshape(1, d), w_in, gq.reshape(1, -1), w_uq, gkv.reshape(1, -1), w_k, w_v, *tables)


def _mla_kernel(q_ref, k_ref, v_ref, o_ref):
    s_len = q_ref.shape[1]
    row = lax.broadcasted_iota(jnp.int32, (TQ, TQ), 0)
    col = lax.broadcasted_iota(jnp.int32, (TQ, TQ), 1)
    causal = col <= row
    for i in range(s_len // TQ):
        rows = slice(i * TQ, (i + 1) * TQ)
        kv_len = (i + 1) * TQ
        out = jnp.zeros((TQ, LANES), F32)
        for hh in range(2):
            lanes = slice(hh * LANES, (hh + 1) * LANES)
            s = _dot_nt(q_ref[0, rows, lanes], k_ref[0, :kv_len, lanes])
            diag = jnp.where(causal, s[:, i * TQ:], -jnp.inf)
            s = diag if i == 0 else jnp.concatenate([s[:, :i * TQ], diag], axis=1)
            p = jnp.exp(s - jnp.max(s, axis=-1, keepdims=True))
            l = jnp.sum(p, axis=-1, keepdims=True)
            out = out + _dot(p.astype(BF16), v_ref[0, :kv_len, lanes]) / l
        o_ref[0, rows, :] = out.astype(o_ref.dtype)


def _mla_attention(q, k, v):
    b, s, _ = q.shape
    pairs = MLA_HEADS // 2
    qkv = pl.BlockSpec((1, s, 2 * LANES), lambda bi, p: (bi, 0, p))
    return pl.pallas_call(
        _mla_kernel,
        grid=(b, pairs),
        in_specs=[qkv, qkv, qkv],
        out_specs=pl.BlockSpec((1, s, LANES), lambda bi, p: (bi, 0, p)),
        out_shape=jax.ShapeDtypeStruct((b, s, MLA_HEADS * MLA_V), BF16),
        compiler_params=_params("arbitrary", "arbitrary"),
        name="mla_attention",
    )(q, k, v)


def _head_lanes(shape, hh):
    lane = lax.broadcasted_iota(jnp.int32, shape, 1)
    return (lane >= hh * HEAD_DIM) & (lane < (hh + 1) * HEAD_DIM)


def _store_masked_v(v_ref, vm_scr):
    v = v_ref[0].astype(F32)
    for hh in range(2):
        vm_scr[hh] = jnp.where(_head_lanes(v.shape, hh), v, 0.0).astype(BF16)


_SOFTPLUS2_CLAMP = 64.0


def _softplus2(z):
    return jnp.maximum(z, jnp.log2(1.0 + jnp.exp2(jnp.minimum(z, _SOFTPLUS2_CLAMP))))


def _sb_kernel(q_ref, k_ref, v_ref, o_ref, qm_scr, vm_scr, acc_scr):
    n_blk = q_ref.shape[1] // TQ
    _store_masked_v(v_ref, vm_scr)
    qf = q_ref[0].astype(F32)
    for hh in range(2):
        qm_scr[hh] = jnp.where(_head_lanes(qf.shape, hh), qf, 0.0).astype(BF16)
    acc_scr[...] = jnp.zeros(acc_scr.shape, F32)

    row = lax.broadcasted_iota(jnp.int32, (TQ, TQ), 0)
    col = lax.broadcasted_iota(jnp.int32, (TQ, TQ), 1)
    strict = col < row
    tri = jnp.where(row >= col, 1.0, 0.0).astype(BF16)

    def head(hh, _):
        for i in range(n_blk):
            rows = slice(i * TQ, (i + 1) * TQ)
            kv_len = (i + 1) * TQ
            tile = lambda x, j: x[:, j * TQ:(j + 1) * TQ]
            z = _dot_nt(qm_scr[hh, rows, :], k_ref[0, :kv_len, :])
            sp = _softplus2(z)
            sp = jnp.concatenate([tile(sp, j) for j in range(i)]
                                 + [jnp.where(strict, tile(sp, i), 0.0)], axis=0)
            hi = sp.astype(BF16)
            lo = (sp - hi.astype(F32)).astype(BF16)
            c = _dot(hi, tri) + _dot(lo, tri)
            carry = None
            w = [None] * (i + 1)
            for j in range(i, -1, -1):
                cj = c[j * TQ:(j + 1) * TQ]
                e = tile(z, j) - cj
                if carry is not None:
                    e = e - jnp.concatenate([carry, carry], axis=1)
                wj = jnp.exp2(e)
                w[j] = (jnp.where(strict, wj, 0.0) if j == i else wj).astype(BF16)
                tot = jnp.broadcast_to(cj[:, 0:1], (TQ, LANES))
                carry = tot if carry is None else carry + tot
            acc_scr[rows, :] += _dot(jnp.concatenate(w, axis=1), vm_scr[hh, :kv_len, :])
        return 0

    lax.fori_loop(0, 2, head, 0)
    o_ref[0] = acc_scr[...].astype(o_ref.dtype)


def _sb_attention(qkv):
    b, s, _ = qkv.shape
    pairs = SB_HEADS // 2
    return pl.pallas_call(
        _sb_kernel,
        grid=(b, pairs),
        in_specs=[pl.BlockSpec((1, s, LANES), lambda bi, p: (bi, 0, p)),
                  pl.BlockSpec((1, s, LANES), lambda bi, p: (bi, 0, pairs + p)),
                  pl.BlockSpec((1, s, LANES), lambda bi, p: (bi, 0, 2 * pairs + p))],
        out_specs=pl.BlockSpec((1, s, LANES), lambda bi, p: (bi, 0, p)),
        out_shape=jax.ShapeDtypeStruct((b, s, SB_HEADS * HEAD_DIM), BF16),
        scratch_shapes=[pltpu.VMEM((2, s, LANES), BF16), pltpu.VMEM((2, s, LANES), BF16),
                        pltpu.VMEM((s, LANES), F32)],
        compiler_params=_params("arbitrary", "arbitrary"),
        name="sb_attention",
    )(qkv, qkv, qkv)


def _t5_bucket_np(rel):
    max_exact = NUM_BUCKETS // 2
    rel = np.maximum(rel, 0)
    large = max_exact + (np.log(np.maximum(rel, max_exact) / max_exact)
                         / math.log(MAX_DISTANCE / max_exact) * (NUM_BUCKETS - max_exact)).astype(np.int64)
    return np.where(rel < max_exact, rel, np.minimum(large, NUM_BUCKETS - 1)).astype(np.int32)


def _bias_kernel(bucket_ref, rel_bias_ref, o_ref):
    h = pl.program_id(0)
    for plane in range(2):
        bk = bucket_ref[plane]
        acc = jnp.full(bk.shape, -jnp.inf, F32)
        for b in range(NUM_BUCKETS):
            acc = jnp.where(bk == b, rel_bias_ref[b, h], acc)
        o_ref[0, plane] = acc
    o_ref[0, 2] = jnp.full((TQ, TQ), rel_bias_ref[NUM_BUCKETS - 1, h], F32)


def _moba_bias_tables(rel_bias):
    off = np.arange(TQ)
    rel_own = off[:, None] - off[None, :]
    own = np.where(rel_own >= 0, _t5_bucket_np(rel_own), -1)
    adj = _t5_bucket_np(rel_own + MOBA_BLOCK)
    buckets = jnp.asarray(np.stack([own, adj]).astype(np.int32))
    return pl.pallas_call(
        _bias_kernel,
        grid=(MOBA_HEADS,),
        in_specs=[_whole(buckets.shape), pl.BlockSpec(memory_space=pltpu.SMEM)],
        out_specs=pl.BlockSpec((1, 3, TQ, TQ), lambda h: (h, 0, 0, 0)),
        out_shape=jax.ShapeDtypeStruct((MOBA_HEADS, 3, TQ, TQ), F32),
        compiler_params=_params("arbitrary"),
        name="moba_bias_tables",
    )(buckets, rel_bias)


_N_BLK_LANES = 8


def _sel_lane_base(hh):
    return HEAD_DIM * (1 - hh)


def _moba_constants(nb):
    assert nb <= _N_BLK_LANES
    diff = np.zeros((LANES, LANES), np.float32)
    count = np.zeros((2, LANES, LANES), np.float32)
    for a in range(nb):
        for b in range(nb):
            if a != b:
                diff[a * 8 + b, a] += 1.0
                diff[a * 8 + b, b] -= 1.0
                for hh in range(2):
                    count[hh, a * 8 + b, _sel_lane_base(hh) + b] = 1.0
    return jnp.asarray(diff, BF16), jnp.asarray(count, BF16)


def _moba_kernel(q_ref, k_ref, v_ref, bias_ref, diff_ref, count_ref, o_ref,
                 kaug_scr, vm_scr, kdiff_scr, *, nb):
    lane = lax.broadcasted_iota(jnp.int32, (TQ, LANES), 1)

    _store_masked_v(v_ref, vm_scr)
    kf = k_ref[0].astype(F32)
    s_len = kf.shape[0]
    kbar = jnp.mean(kf.reshape(nb, MOBA_BLOCK, LANES), axis=1)
    kbar = jnp.concatenate([kbar, jnp.zeros((LANES - nb, LANES), F32)], axis=0)
    p0 = kbar.astype(BF16)
    r1 = kbar - p0.astype(F32)
    p1 = r1.astype(BF16)
    p2 = (r1 - p1.astype(F32)).astype(BF16)
    d = diff_ref[...]
    kdiff = _dot(d, p0) + _dot(d, p1) + _dot(d, p2)
    big_lane = lax.broadcasted_iota(jnp.int32, (s_len, LANES), 1)
    blk = lax.broadcasted_iota(jnp.int32, (s_len, LANES), 0) // MOBA_BLOCK
    for hh in range(2):
        kdiff_scr[hh] = jnp.where(_head_lanes(kdiff.shape, hh), kdiff, 0.0).astype(BF16)
        base = _sel_lane_base(hh)
        in_sel = (big_lane >= base) & (big_lane < base + _N_BLK_LANES)
        ind = jnp.where(big_lane - base == blk, 1.0, 0.0)
        kaug_scr[hh] = jnp.where(in_sel, ind, kf).astype(BF16)

    pair_a = lane // 8
    pair_b = lane % 8
    for i in range(nb):
        rows = slice(i * TQ, (i + 1) * TQ)
        kv_len = (i + 1) * TQ
        qf = q_ref[0, rows, :].astype(F32)
        out = jnp.zeros((TQ, LANES), F32)
        for hh in range(2):
            qm = jnp.where(_head_lanes(qf.shape, hh), qf, 0.0)
            g = _dot_nt(qm.astype(BF16), kdiff_scr[hh])
            beats = (g > 0.0) | ((g == 0.0) & (pair_a < pair_b))
            beats = beats & (pair_a < i) & (lane < 8 * _N_BLK_LANES)
            rank = _dot(jnp.where(beats, 1.0, 0.0).astype(BF16), count_ref[hh])
            blk_b = lane - _sel_lane_base(hh)
            in_sel = (blk_b >= 0) & (blk_b < _N_BLK_LANES)
            chosen = (blk_b == i) | ((blk_b < i) & (rank < float(MOBA_TOPK)))
            q = jnp.where(in_sel, jnp.where(chosen, 0.0, MASK_PENALTY), qm).astype(BF16)

            s = _dot_nt(q, kaug_scr[hh, :kv_len, :])
            plane = lambda j: 0 if j == i else (1 if j == i - 1 else 2)
            s = jnp.concatenate([s[:, j * TQ:(j + 1) * TQ] + bias_ref[hh, plane(j)]
                                 for j in range(i + 1)], axis=1)
            p = jnp.exp(s - jnp.max(s, axis=-1, keepdims=True))
            l = jnp.sum(p, axis=-1, keepdims=True)
            out = out + _dot(p.astype(BF16), vm_scr[hh, :kv_len, :]) / l
        o_ref[0, rows, :] = out.astype(o_ref.dtype)


def _moba_attention(qkv, bias_tables):
    b, s, _ = qkv.shape
    nb = s // MOBA_BLOCK
    pairs = MOBA_HEADS // 2
    diff, count = _moba_constants(nb)
    return pl.pallas_call(
        functools.partial(_moba_kernel, nb=nb),
        grid=(b, pairs),
        in_specs=[pl.BlockSpec((1, s, LANES), lambda bi, p: (bi, 0, p)),
                  pl.BlockSpec((1, s, LANES), lambda bi, p: (bi, 0, pairs + p)),
                  pl.BlockSpec((1, s, LANES), lambda bi, p: (bi, 0, 2 * pairs + p)),
                  pl.BlockSpec((2, 3, TQ, TQ), lambda bi, p: (p, 0, 0, 0)),
                  _whole(diff.shape), _whole(count.shape)],
        out_specs=pl.BlockSpec((1, s, LANES), lambda bi, p: (bi, 0, p)),
        out_shape=jax.ShapeDtypeStruct((b, s, MOBA_HEADS * HEAD_DIM), BF16),
        scratch_shapes=[pltpu.VMEM((2, s, LANES), BF16), pltpu.VMEM((2, s, LANES), BF16),
                        pltpu.VMEM((2, LANES, LANES), BF16)],
        compiler_params=_params("arbitrary", "arbitrary"),
        name="moba_attention",
    )(qkv, qkv, qkv, bias_tables, diff, count)


def _proj_residual_kernel(*refs, n_in):
    a_refs, w_refs = refs[:n_in], refs[n_in:2 * n_in]
    g_ref, h_ref, o_ref = refs[2 * n_in:]
    mixed = _dot(a_refs[0][...], w_refs[0][...])
    for a_ref, w_ref in zip(a_refs[1:], w_refs[1:]):
        mixed = mixed + _dot(a_ref[...], w_ref[...])
    o_ref[...] = h_ref[...] + _rms(mixed, g_ref[...])


def _proj_residual(acts, weights, g, h):
    n, d = h.shape
    row = lambda t: (t, 0)
    return pl.pallas_call(
        functools.partial(_proj_residual_kernel, n_in=len(acts)),
        grid=(n // TM,),
        in_specs=([pl.BlockSpec((TM, a.shape[1]), row) for a in acts]
                  + [_whole(w.shape) for w in weights]
                  + [_whole((1, d)), pl.BlockSpec((TM, d), row)]),
        out_specs=pl.BlockSpec((TM, d), row),
        out_shape=jax.ShapeDtypeStruct((n, d), F32),
        compiler_params=_params("arbitrary"),
        name="proj_residual",
    )(*acts, *weights, g.reshape(1, d), h)


def _ffn_kernel(h_ref, gpre_ref, wgu_ref, wd_ref, gpost_ref, o_ref, act_scr):
    x = h_ref[...]
    u = _rms(x, gpre_ref[...]).astype(BF16)
    for c in range(0, D_FF, FF_CHUNK):
        gate = _dot(u, wgu_ref[:, c:c + FF_CHUNK])
        up = _dot(u, wgu_ref[:, D_FF + c:D_FF + c + FF_CHUNK])
        act_scr[:, c:c + FF_CHUNK] = (gate * jax.nn.sigmoid(gate) * up).astype(BF16)
    f = _dot(act_scr[...], wd_ref[...])
    o_ref[...] = x + _rms(f, gpost_ref[...])


def _ffn(h, gpre, wgu, wd, gpost):
    n, d = h.shape
    row = lambda t: (t, 0)
    return pl.pallas_call(
        _ffn_kernel,
        grid=(n // TM,),
        in_specs=[pl.BlockSpec((TM, d), row), _whole((1, d)), _whole(wgu.shape),
                  _whole(wd.shape), _whole((1, d))],
        out_specs=pl.BlockSpec((TM, d), row),
        out_shape=jax.ShapeDtypeStruct((n, d), F32),
        scratch_shapes=[pltpu.VMEM((TM, D_FF), BF16)],
        compiler_params=_params("arbitrary"),
        name="ffn",
    )(h, gpre.reshape(1, d), wgu, wd, gpost.reshape(1, d))


def _rot_half_cols(w):
    half = w.shape[-1] // 2
    return jnp.concatenate([-w[..., half:], w[..., :half]], axis=-1)


def _prep_even_weights(w_in, w_uq, w_ukv):
    d = w_in.shape[0]
    o1, o2, o3 = MLA_Q_LORA, MLA_Q_LORA + MLA_KV_LORA, MLA_Q_LORA + MLA_KV_LORA + MLA_ROPE
    hw = MOBA_HEADS * HEAD_DIM
    z = lambda *shape: jnp.zeros(shape, F32)
    kr = w_in[:, o2:o3]
    w_in_ext = jnp.concatenate(
        [w_in[:, :o2],
         z(d, MLA_NOPE), kr, z(d, LANES - MLA_NOPE - MLA_ROPE),
         z(d, MLA_NOPE), _rot_half_cols(kr), z(d, LANES - MLA_NOPE - MLA_ROPE),
         w_in[:, o3:o3 + hw] * (HEAD_DIM ** -0.5), w_in[:, o3 + hw:]], axis=1).astype(BF16)

    wq = w_uq.reshape(MLA_Q_LORA, MLA_HEADS, MLA_NOPE + MLA_ROPE)
    nope, rp = wq[..., :MLA_NOPE], wq[..., MLA_NOPE:]
    tail = z(MLA_Q_LORA, MLA_HEADS, LANES - MLA_NOPE - MLA_ROPE)
    qa = jnp.concatenate([nope, rp, tail], axis=-1).reshape(MLA_Q_LORA, _QW)
    qb = jnp.concatenate([jnp.zeros_like(nope), _rot_half_cols(rp), tail], axis=-1).reshape(MLA_Q_LORA, _QW)
    w_uq_ext = jnp.concatenate([qa, qb], axis=1).astype(BF16)

    wkv = w_ukv.reshape(MLA_KV_LORA, MLA_HEADS, MLA_NOPE + MLA_V)
    k_nope, v = wkv[..., :MLA_NOPE], wkv[..., MLA_NOPE:]
    w_k = jnp.concatenate([k_nope, jnp.zeros_like(k_nope)], axis=-1).reshape(MLA_KV_LORA, _QW).astype(BF16)
    v = v.reshape(MLA_KV_LORA, MLA_HEADS // 2, 2, MLA_V)
    zero = jnp.zeros_like(v[:, :, 0])
    w_v = jnp.stack([jnp.concatenate([v[:, :, 0], zero], axis=-1),
                     jnp.concatenate([zero, v[:, :, 1]], axis=-1)], axis=2)
    w_v = w_v.reshape(MLA_KV_LORA, _QW).astype(BF16)
    return w_in_ext, w_uq_ext, w_k, w_v


def _rope_tables(seq):
    half = MLA_ROPE // 2
    inv_freq = ROPE_THETA ** (-jnp.arange(half, dtype=F32) / half)
    ang = jnp.arange(seq).astype(F32)[:, None] * inv_freq[None, :]
    cos, sin = jnp.cos(ang), jnp.sin(ang)
    ones, zeros = jnp.ones((seq, MLA_NOPE), F32), jnp.zeros((seq, MLA_NOPE), F32)
    tail = jnp.zeros((seq, LANES - MLA_NOPE - MLA_ROPE), F32)
    scale = (MLA_NOPE + MLA_ROPE) ** -0.5
    cq = jnp.concatenate([ones, cos, cos, tail], axis=1) * scale
    sq = jnp.concatenate([zeros, sin, sin, tail], axis=1) * scale
    ck = jnp.concatenate([zeros, cos, cos, tail], axis=1)
    sk = jnp.concatenate([zeros, sin, sin, tail], axis=1)
    return cq, sq, ck, sk


def kernel(x, mix_pre_g, mix_post_g, ffn_pre_g, ffn_post_g, ab_w_in, mla_q_norm_g, mla_w_uq,
           mla_kv_norm_g, mla_w_ukv, ab_w_out, rel_bias, sb_w_qkv, sb_w_out, ffn_w_gate_up, ffn_w_down):
    b, s, d = x.shape
    n = b * s
    h = x.reshape(n, d)
    tables = _rope_tables(s)
    bias_tables = _moba_bias_tables(rel_bias)
    mla_w = MLA_HEADS * MLA_V
    for layer in range(DEPTH):
        if layer % 2 == 0:
            e = layer // 2
            w_in, w_uq, w_k, w_v = _prep_even_weights(ab_w_in[e], mla_w_uq[e], mla_w_ukv[e])
            q, k, v, moba_qkv = _even_in(h, mix_pre_g[layer], w_in, mla_q_norm_g[e], w_uq,
                                         mla_kv_norm_g[e], w_k, w_v, tables, s)
            mla_out = _mla_attention(q.reshape(b, s, -1), k.reshape(b, s, -1), v.reshape(b, s, -1))
            moba_out = _moba_attention(moba_qkv.reshape(b, s, -1), bias_tables)
            w_out = ab_w_out[e].astype(BF16)
            h = _proj_residual([mla_out.reshape(n, -1), moba_out.reshape(n, -1)],
                               [w_out[:mla_w], w_out[mla_w:]], mix_post_g[layer], h)
        else:
            o = layer // 2
            width = SB_HEADS * HEAD_DIM
            qkv = _norm_matmul(h, mix_pre_g[layer], sb_w_qkv[o].astype(BF16),
                               width, HEAD_DIM ** -0.5 * LOG2_E)
            sb_out = _sb_attention(qkv.reshape(b, s, -1))
            h = _proj_residual([sb_out.reshape(n, -1)], [sb_w_out[o].astype(BF16)], mix_post_g[layer], h)
        h = _ffn(h, ffn_pre_g[layer], ffn_w_gate_up[layer].astype(BF16),
                 ffn_w_down[layer].astype(BF16), ffn_post_g[layer])
    return h.reshape(b, s, d)
```

```python
import functools
import math

import numpy as np
import jax
import jax.numpy as jnp
from jax import lax
from jax.experimental import pallas as pl
from jax.experimental.pallas import tpu as pltpu

D_MODEL = 1024
DEPTH = 4
HEAD_DIM = 64
MLA_HEADS = 8
MLA_NOPE = 64
MLA_ROPE = 32
MLA_V = 64
MLA_Q_LORA = 384
MLA_KV_LORA = 256
ROPE_THETA = 10000.0
MOBA_HEADS = 8
MOBA_BLOCK = 256
MOBA_TOPK = 3
SB_HEADS = 16
NUM_BUCKETS = 32
MAX_DISTANCE = 128
RMS_EPS = 1e-6
D_FF = 2816

LANES = 128
TQ = 256
MLA_ROWS = 512
TM = 512
FF_CHUNK = 256
MASK_PENALTY = -1e30
LOG2_E = 1.0 / math.log(2.0)
VMEM_LIMIT = 48 * 1024 * 1024

F32 = jnp.float32
BF16 = jnp.bfloat16


def _rms(x, g):
    return x * lax.rsqrt(jnp.mean(x * x, axis=-1, keepdims=True) + RMS_EPS) * g


def _dot(a, b):
    return jnp.dot(a, b, preferred_element_type=F32)


def _dot_nt(a, b):
    return lax.dot_general(a, b, (((1,), (1,)), ((), ())), preferred_element_type=F32)


def _params(*semantics):
    return pltpu.CompilerParams(dimension_semantics=semantics, vmem_limit_bytes=VMEM_LIMIT)


def _whole(shape):
    nd = len(shape)
    return pl.BlockSpec(shape, lambda *_: (0,) * nd)


def _norm_matmul_kernel(h_ref, g_ref, w_ref, o_ref, *, q_cols, q_scale):
    u = _rms(h_ref[...], g_ref[...]).astype(BF16)
    for c in range(0, o_ref.shape[-1], 512):
        y = _dot(u, w_ref[:, c:c + 512])
        if c < q_cols:
            y = y * q_scale
        o_ref[:, c:c + 512] = y.astype(o_ref.dtype)


def _norm_matmul(h, g, w, q_cols, q_scale):
    n, d = h.shape
    n_out = w.shape[1]
    return pl.pallas_call(
        functools.partial(_norm_matmul_kernel, q_cols=q_cols, q_scale=q_scale),
        grid=(n // TM,),
        in_specs=[pl.BlockSpec((TM, d), lambda t: (t, 0)), _whole((1, d)), _whole(w.shape)],
        out_specs=pl.BlockSpec((TM, n_out), lambda t: (t, 0)),
        out_shape=jax.ShapeDtypeStruct((n, n_out), BF16),
        compiler_params=_params("arbitrary"),
        name="norm_matmul",
    )(h, g.reshape(1, d), w)


_O_CQ = 0
_O_CKV = MLA_Q_LORA
_O_KR = _O_CKV + MLA_KV_LORA
_O_KRR = _O_KR + LANES
_O_MOBA = _O_KRR + LANES
_W_IN_EXT = _O_MOBA + 3 * MOBA_HEADS * HEAD_DIM
_QW = MLA_HEADS * LANES


def _even_in_kernel(h_ref, g_ref, w_in_ref, gq_ref, w_uq_ref, gkv_ref, w_k_ref, w_v_ref,
                    cq_ref, sq_ref, ck_ref, sk_ref, q_ref, k_ref, v_ref, moba_ref):
    u = _rms(h_ref[...], g_ref[...]).astype(BF16)
    c_q = _dot(u, w_in_ref[:, _O_CQ:_O_CKV])
    c_kv = _dot(u, w_in_ref[:, _O_CKV:_O_KR])
    kr = _dot(u, w_in_ref[:, _O_KR:_O_KRR])
    krr = _dot(u, w_in_ref[:, _O_KRR:_O_MOBA])
    for c in range(0, 3 * MOBA_HEADS * HEAD_DIM, 512):
        moba_ref[:, c:c + 512] = _dot(u, w_in_ref[:, _O_MOBA + c:_O_MOBA + c + 512]).astype(BF16)

    cqn = _rms(c_q, gq_ref[...]).astype(BF16)
    qa = _dot(cqn, w_uq_ref[:, :_QW])
    qb = _dot(cqn, w_uq_ref[:, _QW:])
    cq, sq = cq_ref[...], sq_ref[...]
    for h in range(MLA_HEADS):
        s = slice(h * LANES, (h + 1) * LANES)
        q_ref[:, s] = (qa[:, s] * cq + qb[:, s] * sq).astype(BF16)

    ckvn = _rms(c_kv, gkv_ref[...]).astype(BF16)
    kn = _dot(ckvn, w_k_ref[...])
    v_ref[...] = _dot(ckvn, w_v_ref[...]).astype(BF16)
    k_rope = kr * ck_ref[...] + krr * sk_ref[...]
    for h in range(MLA_HEADS):
        s = slice(h * LANES, (h + 1) * LANES)
        k_ref[:, s] = (kn[:, s] + k_rope).astype(BF16)


def _even_in(h, g, w_in, gq, w_uq, gkv, w_k, w_v, tables, seq):
    n, d = h.shape
    row = lambda t: (t, 0)
    pos = lambda t: (t % (seq // TM), 0)
    tab = pl.BlockSpec((TM, LANES), pos)
    outs = [(_QW, "q"), (_QW, "k"), (_QW, "v"), (3 * MOBA_HEADS * HEAD_DIM, "moba")]
    return pl.pallas_call(
        _even_in_kernel,
        grid=(n // TM,),
        in_specs=[pl.BlockSpec((TM, d), row), _whole((1, d)), _whole(w_in.shape),
                  _whole((1, MLA_Q_LORA)), _whole(w_uq.shape), _whole((1, MLA_KV_LORA)),
                  _whole(w_k.shape), _whole(w_v.shape), tab, tab, tab, tab],
        out_specs=[pl.BlockSpec((TM, w), row) for w, _ in outs],
        out_shape=[jax.ShapeDtypeStruct((n, w), BF16) for w, _ in outs],
        compiler_params=_params("arbitrary"),
        name="even_in",
    )(h, g.reshape(1, d), w_in, gq.reshape(1, -1), w_uq, gkv.reshape(1, -1), w_k, w_v, *tables)


def _mla_kernel(q_ref, k_ref, v_ref, o_ref):
    s_len = q_ref.shape[1]
    tr = MLA_ROWS
    row = lax.broadcasted_iota(jnp.int32, (tr, tr), 0)
    col = lax.broadcasted_iota(jnp.int32, (tr, tr), 1)
    causal = col <= row
    for i in range(s_len // tr):
        rows = slice(i * tr, (i + 1) * tr)
        kv_len = (i + 1) * tr
        out = jnp.zeros((tr, LANES), F32)
        for hh in range(2):
            lanes = slice(hh * LANES, (hh + 1) * LANES)
            s = _dot_nt(q_ref[0, rows, lanes], k_ref[0, :kv_len, lanes])
            diag = jnp.where(causal, s[:, i * tr:], -jnp.inf)
            s = diag if i == 0 else jnp.concatenate([s[:, :i * tr], diag], axis=1)
            p = jnp.exp(s - jnp.max(s, axis=-1, keepdims=True))
            l = jnp.sum(p, axis=-1, keepdims=True)
            out = out + _dot(p.astype(BF16), v_ref[0, :kv_len, lanes]) / l
        o_ref[0, rows, :] = out.astype(o_ref.dtype)


def _mla_attention(q, k, v):
    b, s, _ = q.shape
    pairs = MLA_HEADS // 2
    qkv = pl.BlockSpec((1, s, 2 * LANES), lambda bi, p: (bi, 0, p))
    return pl.pallas_call(
        _mla_kernel,
        grid=(b, pairs),
        in_specs=[qkv, qkv, qkv],
        out_specs=pl.BlockSpec((1, s, LANES), lambda bi, p: (bi, 0, p)),
        out_shape=jax.ShapeDtypeStruct((b, s, MLA_HEADS * MLA_V), BF16),
        compiler_params=_params("arbitrary", "arbitrary"),
        name="mla_attention",
    )(q, k, v)


def _head_lanes(shape, hh):
    lane = lax.broadcasted_iota(jnp.int32, shape, 1)
    return (lane >= hh * HEAD_DIM) & (lane < (hh + 1) * HEAD_DIM)


def _store_masked_v(v_ref, vm_scr):
    v = v_ref[0].astype(F32)
    for hh in range(2):
        vm_scr[hh] = jnp.where(_head_lanes(v.shape, hh), v, 0.0).astype(BF16)


_SOFTPLUS2_CLAMP = 64.0


def _softplus2(z):
    return jnp.maximum(z, jnp.log2(1.0 + jnp.exp2(jnp.minimum(z, _SOFTPLUS2_CLAMP))))


def _sb_kernel(q_ref, k_ref, v_ref, o_ref, qm_scr, vm_scr, acc_scr, carry_scr):
    s_len = q_ref.shape[1]
    n_blk = s_len // TQ
    _store_masked_v(v_ref, vm_scr)
    qf = q_ref[0].astype(F32)
    for hh in range(2):
        qm_scr[hh] = jnp.where(_head_lanes(qf.shape, hh), qf, 0.0).astype(BF16)
    acc_scr[...] = jnp.zeros(acc_scr.shape, F32)

    row = lax.broadcasted_iota(jnp.int32, (TQ, TQ), 0)
    col = lax.broadcasted_iota(jnp.int32, (TQ, TQ), 1)
    strict = col < row
    tri = jnp.where(row >= col, 1.0, 0.0).astype(BF16)

    def head(hh, _):
        for j in range(n_blk - 1, -1, -1):
            keys = slice(j * TQ, (j + 1) * TQ)
            below = slice((j + 1) * TQ, s_len)
            z = _dot_nt(qm_scr[hh, j * TQ:, :], k_ref[0, keys, :])
            sp = _softplus2(z)
            sp_diag = jnp.where(strict, sp[:TQ], 0.0)
            sp = sp_diag if j == n_blk - 1 else jnp.concatenate([sp_diag, sp[TQ:]], axis=0)
            c = _dot(sp.astype(BF16), tri)
            tot = jnp.broadcast_to(c[:, 0:1], (c.shape[0], LANES))
            w = jnp.where(strict, jnp.exp2(z[:TQ] - c[:TQ]), 0.0)
            if j < n_blk - 1:
                carry = carry_scr[below, :]
                w_below = jnp.exp2(z[TQ:] - c[TQ:] - jnp.concatenate([carry, carry], axis=1))
                w = jnp.concatenate([w, w_below], axis=0)
                carry_scr[below, :] = carry + tot[TQ:]
            carry_scr[keys, :] = tot[:TQ]
            acc_scr[j * TQ:, :] += _dot(w.astype(BF16), vm_scr[hh, keys, :])
        return 0

    lax.fori_loop(0, 2, head, 0)
    o_ref[0] = acc_scr[...].astype(o_ref.dtype)


def _sb_attention(qkv):
    b, s, _ = qkv.shape
    pairs = SB_HEADS // 2
    return pl.pallas_call(
        _sb_kernel,
        grid=(b, pairs),
        in_specs=[pl.BlockSpec((1, s, LANES), lambda bi, p: (bi, 0, p)),
                  pl.BlockSpec((1, s, LANES), lambda bi, p: (bi, 0, pairs + p)),
                  pl.BlockSpec((1, s, LANES), lambda bi, p: (bi, 0, 2 * pairs + p))],
        out_specs=pl.BlockSpec((1, s, LANES), lambda bi, p: (bi, 0, p)),
        out_shape=jax.ShapeDtypeStruct((b, s, SB_HEADS * HEAD_DIM), BF16),
        scratch_shapes=[pltpu.VMEM((2, s, LANES), BF16), pltpu.VMEM((2, s, LANES), BF16),
                        pltpu.VMEM((s, LANES), F32), pltpu.VMEM((s, LANES), F32)],
        compiler_params=_params("arbitrary", "arbitrary"),
        name="sb_attention",
    )(qkv, qkv, qkv)


def _t5_bucket_np(rel):
    max_exact = NUM_BUCKETS // 2
    rel = np.maximum(rel, 0)
    large = max_exact + (np.log(np.maximum(rel, max_exact) / max_exact)
                         / math.log(MAX_DISTANCE / max_exact) * (NUM_BUCKETS - max_exact)).astype(np.int64)
    return np.where(rel < max_exact, rel, np.minimum(large, NUM_BUCKETS - 1)).astype(np.int32)


def _bias_kernel(bucket_ref, rel_bias_ref, o_ref):
    h = pl.program_id(0)
    for plane in range(2):
        bk = bucket_ref[plane]
        acc = jnp.full(bk.shape, -jnp.inf, F32)
        for b in range(NUM_BUCKETS):
            acc = jnp.where(bk == b, rel_bias_ref[b, h], acc)
        o_ref[0, plane] = acc
    o_ref[0, 2] = jnp.full((TQ, TQ), rel_bias_ref[NUM_BUCKETS - 1, h], F32)


def _moba_bias_tables(rel_bias):
    off = np.arange(TQ)
    rel_own = off[:, None] - off[None, :]
    own = np.where(rel_own >= 0, _t5_bucket_np(rel_own), -1)
    adj = _t5_bucket_np(rel_own + MOBA_BLOCK)
    buckets = jnp.asarray(np.stack([own, adj]).astype(np.int32))
    return pl.pallas_call(
        _bias_kernel,
        grid=(MOBA_HEADS,),
        in_specs=[_whole(buckets.shape), pl.BlockSpec(memory_space=pltpu.SMEM)],
        out_specs=pl.BlockSpec((1, 3, TQ, TQ), lambda h: (h, 0, 0, 0)),
        out_shape=jax.ShapeDtypeStruct((MOBA_HEADS, 3, TQ, TQ), F32),
        compiler_params=_params("arbitrary"),
        name="moba_bias_tables",
    )(buckets, rel_bias)


_N_BLK_LANES = 8


def _sel_lane_base(hh):
    return HEAD_DIM * (1 - hh)


def _moba_constants(nb):
    assert nb <= _N_BLK_LANES
    diff = np.zeros((LANES, LANES), np.float32)
    count = np.zeros((2, LANES, LANES), np.float32)
    for a in range(nb):
        for b in range(nb):
            if a != b:
                diff[a * 8 + b, a] += 1.0
                diff[a * 8 + b, b] -= 1.0
                for hh in range(2):
                    count[hh, a * 8 + b, _sel_lane_base(hh) + b] = 1.0
    return jnp.asarray(diff, BF16), jnp.asarray(count, BF16)


def _moba_kernel(q_ref, k_ref, v_ref, bias_ref, diff_ref, count_ref, o_ref,
                 kaug_scr, vm_scr, kdiff_scr, *, nb):
    lane = lax.broadcasted_iota(jnp.int32, (TQ, LANES), 1)

    _store_masked_v(v_ref, vm_scr)
    kf = k_ref[0].astype(F32)
    s_len = kf.shape[0]
    kbar = jnp.mean(kf.reshape(nb, MOBA_BLOCK, LANES), axis=1)
    kbar = jnp.concatenate([kbar, jnp.zeros((LANES - nb, LANES), F32)], axis=0)
    p0 = kbar.astype(BF16)
    r1 = kbar - p0.astype(F32)
    p1 = r1.astype(BF16)
    p2 = (r1 - p1.astype(F32)).astype(BF16)
    d = diff_ref[...]
    kdiff = _dot(d, p0) + _dot(d, p1) + _dot(d, p2)
    big_lane = lax.broadcasted_iota(jnp.int32, (s_len, LANES), 1)
    blk = lax.broadcasted_iota(jnp.int32, (s_len, LANES), 0) // MOBA_BLOCK
    for hh in range(2):
        kdiff_scr[hh] = jnp.where(_head_lanes(kdiff.shape, hh), kdiff, 0.0).astype(BF16)
        base = _sel_lane_base(hh)
        in_sel = (big_lane >= base) & (big_lane < base + _N_BLK_LANES)
        ind = jnp.where(big_lane - base == blk, 1.0, 0.0)
        kaug_scr[hh] = jnp.where(in_sel, ind, kf).astype(BF16)

    pair_a = lane // 8
    pair_b = lane % 8
    for i in range(nb):
        rows = slice(i * TQ, (i + 1) * TQ)
        kv_len = (i + 1) * TQ
        qf = q_ref[0, rows, :].astype(F32)
        out = jnp.zeros((TQ, LANES), F32)
        for hh in range(2):
            qm = jnp.where(_head_lanes(qf.shape, hh), qf, 0.0)
            g = _dot_nt(qm.astype(BF16), kdiff_scr[hh])
            beats = (g > 0.0) | ((g == 0.0) & (pair_a < pair_b))
            beats = beats & (pair_a < i) & (lane < 8 * _N_BLK_LANES)
            rank = _dot(jnp.where(beats, 1.0, 0.0).astype(BF16), count_ref[hh])
            blk_b = lane - _sel_lane_base(hh)
            in_sel = (blk_b >= 0) & (blk_b < _N_BLK_LANES)
            chosen = (blk_b == i) | ((blk_b < i) & (rank < float(MOBA_TOPK)))
            q = jnp.where(in_sel, jnp.where(chosen, 0.0, MASK_PENALTY), qm).astype(BF16)

            s = _dot_nt(q, kaug_scr[hh, :kv_len, :])
            plane = lambda j: 0 if j == i else (1 if j == i - 1 else 2)
            s = jnp.concatenate([s[:, j * TQ:(j + 1) * TQ] + bias_ref[hh, plane(j)]
                                 for j in range(i + 1)], axis=1)
            p = jnp.exp(s - jnp.max(s, axis=-1, keepdims=True))
            l = jnp.sum(p, axis=-1, keepdims=True)
            out = out + _dot(p.astype(BF16), vm_scr[hh, :kv_len, :]) / l
        o_ref[0, rows, :] = out.astype(o_ref.dtype)


def _moba_attention(qkv, bias_tables):
    b, s, _ = qkv.shape
    nb = s // MOBA_BLOCK
    pairs = MOBA_HEADS // 2
    diff, count = _moba_constants(nb)
    return pl.pallas_call(
        functools.partial(_moba_kernel, nb=nb),
        grid=(b, pairs),
        in_specs=[pl.BlockSpec((1, s, LANES), lambda bi, p: (bi, 0, p)),
                  pl.BlockSpec((1, s, LANES), lambda bi, p: (bi, 0, pairs + p)),
                  pl.BlockSpec((1, s, LANES), lambda bi, p: (bi, 0, 2 * pairs + p)),
                  pl.BlockSpec((2, 3, TQ, TQ), lambda bi, p: (p, 0, 0, 0)),
                  _whole(diff.shape), _whole(count.shape)],
        out_specs=pl.BlockSpec((1, s, LANES), lambda bi, p: (bi, 0, p)),
        out_shape=jax.ShapeDtypeStruct((b, s, MOBA_HEADS * HEAD_DIM), BF16),
        scratch_shapes=[pltpu.VMEM((2, s, LANES), BF16), pltpu.VMEM((2, s, LANES), BF16),
                        pltpu.VMEM((2, LANES, LANES), BF16)],
        compiler_params=_params("arbitrary", "arbitrary"),
        name="moba_attention",
    )(qkv, qkv, qkv, bias_tables, diff, count)


def _proj_residual_kernel(*refs, n_in):
    a_refs, w_refs = refs[:n_in], refs[n_in:2 * n_in]
    g_ref, h_ref, o_ref = refs[2 * n_in:]
    mixed = _dot(a_refs[0][...], w_refs[0][...])
    for a_ref, w_ref in zip(a_refs[1:], w_refs[1:]):
        mixed = mixed + _dot(a_ref[...], w_ref[...])
    o_ref[...] = h_ref[...] + _rms(mixed, g_ref[...])


def _proj_residual(acts, weights, g, h):
    n, d = h.shape
    row = lambda t: (t, 0)
    return pl.pallas_call(
        functools.partial(_proj_residual_kernel, n_in=len(acts)),
        grid=(n // TM,),
        in_specs=([pl.BlockSpec((TM, a.shape[1]), row) for a in acts]
                  + [_whole(w.shape) for w in weights]
                  + [_whole((1, d)), pl.BlockSpec((TM, d), row)]),
        out_specs=pl.BlockSpec((TM, d), row),
        out_shape=jax.ShapeDtypeStruct((n, d), F32),
        compiler_params=_params("arbitrary"),
        name="proj_residual",
    )(*acts, *weights, g.reshape(1, d), h)


def _ffn_kernel(h_ref, gpre_ref, wgu_ref, wd_ref, gpost_ref, o_ref, act_scr):
    x = h_ref[...]
    u = _rms(x, gpre_ref[...]).astype(BF16)
    for c in range(0, D_FF, FF_CHUNK):
        gate = _dot(u, wgu_ref[:, c:c + FF_CHUNK])
        up = _dot(u, wgu_ref[:, D_FF + c:D_FF + c + FF_CHUNK])
        act_scr[:, c:c + FF_CHUNK] = (gate * jax.nn.sigmoid(gate) * up).astype(BF16)
    f = _dot(act_scr[...], wd_ref[...])
    o_ref[...] = x + _rms(f, gpost_ref[...])


def _ffn(h, gpre, wgu, wd, gpost):
    n, d = h.shape
    row = lambda t: (t, 0)
    return pl.pallas_call(
        _ffn_kernel,
        grid=(n // TM,),
        in_specs=[pl.BlockSpec((TM, d), row), _whole((1, d)), _whole(wgu.shape),
                  _whole(wd.shape), _whole((1, d))],
        out_specs=pl.BlockSpec((TM, d), row),
        out_shape=jax.ShapeDtypeStruct((n, d), F32),
        scratch_shapes=[pltpu.VMEM((TM, D_FF), BF16)],
        compiler_params=_params("arbitrary"),
        name="ffn",
    )(h, gpre.reshape(1, d), wgu, wd, gpost.reshape(1, d))


def _rot_half_cols(w):
    half = w.shape[-1] // 2
    return jnp.concatenate([-w[..., half:], w[..., :half]], axis=-1)


def _prep_even_weights(w_in, w_uq, w_ukv):
    d = w_in.shape[0]
    o1, o2, o3 = MLA_Q_LORA, MLA_Q_LORA + MLA_KV_LORA, MLA_Q_LORA + MLA_KV_LORA + MLA_ROPE
    hw = MOBA_HEADS * HEAD_DIM
    z = lambda *shape: jnp.zeros(shape, F32)
    kr = w_in[:, o2:o3]
    w_in_ext = jnp.concatenate(
        [w_in[:, :o2],
         z(d, MLA_NOPE), kr, z(d, LANES - MLA_NOPE - MLA_ROPE),
         z(d, MLA_NOPE), _rot_half_cols(kr), z(d, LANES - MLA_NOPE - MLA_ROPE),
         w_in[:, o3:o3 + hw] * (HEAD_DIM ** -0.5), w_in[:, o3 + hw:]], axis=1).astype(BF16)

    wq = w_uq.reshape(MLA_Q_LORA, MLA_HEADS, MLA_NOPE + MLA_ROPE)
    nope, rp = wq[..., :MLA_NOPE], wq[..., MLA_NOPE:]
    tail = z(MLA_Q_LORA, MLA_HEADS, LANES - MLA_NOPE - MLA_ROPE)
    qa = jnp.concatenate([nope, rp, tail], axis=-1).reshape(MLA_Q_LORA, _QW)
    qb = jnp.concatenate([jnp.zeros_like(nope), _rot_half_cols(rp), tail], axis=-1).reshape(MLA_Q_LORA, _QW)
    w_uq_ext = jnp.concatenate([qa, qb], axis=1).astype(BF16)

    wkv = w_ukv.reshape(MLA_KV_LORA, MLA_HEADS, MLA_NOPE + MLA_V)
    k_nope, v = wkv[..., :MLA_NOPE], wkv[..., MLA_NOPE:]
    w_k = jnp.concatenate([k_nope, jnp.zeros_like(k_nope)], axis=-1).reshape(MLA_KV_LORA, _QW).astype(BF16)
    v = v.reshape(MLA_KV_LORA, MLA_HEADS // 2, 2, MLA_V)
    zero = jnp.zeros_like(v[:, :, 0])
    w_v = jnp.stack([jnp.concatenate([v[:, :, 0], zero], axis=-1),
                     jnp.concatenate([zero, v[:, :, 1]], axis=-1)], axis=2)
    w_v = w_v.reshape(MLA_KV_LORA, _QW).astype(BF16)
    return w_in_ext, w_uq_ext, w_k, w_v


def _rope_tables(seq):
    half = MLA_ROPE // 2
    inv_freq = ROPE_THETA ** (-jnp.arange(half, dtype=F32) / half)
    ang = jnp.arange(seq).astype(F32)[:, None] * inv_freq[None, :]
    cos, sin = jnp.cos(ang), jnp.sin(ang)
    ones, zeros = jnp.ones((seq, MLA_NOPE), F32), jnp.zeros((seq, MLA_NOPE), F32)
    tail = jnp.zeros((seq, LANES - MLA_NOPE - MLA_ROPE), F32)
    scale = (MLA_NOPE + MLA_ROPE) ** -0.5
    cq = jnp.concatenate([ones, cos, cos, tail], axis=1) * scale
    sq = jnp.concatenate([zeros, sin, sin, tail], axis=1) * scale
    ck = jnp.concatenate([zeros, cos, cos, tail], axis=1)
    sk = jnp.concatenate([zeros, sin, sin, tail], axis=1)
    return cq, sq, ck, sk


def kernel(x, mix_pre_g, mix_post_g, ffn_pre_g, ffn_post_g, ab_w_in, mla_q_norm_g, mla_w_uq,
           mla_kv_norm_g, mla_w_ukv, ab_w_out, rel_bias, sb_w_qkv, sb_w_out, ffn_w_gate_up, ffn_w_down):
    b, s, d = x.shape
    n = b * s
    h = x.reshape(n, d)
    tables = _rope_tables(s)
    bias_tables = _moba_bias_tables(rel_bias)
    mla_w = MLA_HEADS * MLA_V
    for layer in range(DEPTH):
        if layer % 2 == 0:
            e = layer // 2
            w_in, w_uq, w_k, w_v = _prep_even_weights(ab_w_in[e], mla_w_uq[e], mla_w_ukv[e])
            q, k, v, moba_qkv = _even_in(h, mix_pre_g[layer], w_in, mla_q_norm_g[e], w_uq,
                                         mla_kv_norm_g[e], w_k, w_v, tables, s)
            mla_out = _mla_attention(q.reshape(b, s, -1), k.reshape(b, s, -1), v.reshape(b, s, -1))
            moba_out = _moba_attention(moba_qkv.reshape(b, s, -1), bias_tables)
            w_out = ab_w_out[e].astype(BF16)
            h = _proj_residual([mla_out.reshape(n, -1), moba_out.reshape(n, -1)],
                               [w_out[:mla_w], w_out[mla_w:]], mix_post_g[layer], h)
        else:
            o = layer // 2
            width = SB_HEADS * HEAD_DIM
            qkv = _norm_matmul(h, mix_pre_g[layer], sb_w_qkv[o].astype(BF16),
                               width, HEAD_DIM ** -0.5 * LOG2_E)
            sb_out = _sb_attention(qkv.reshape(b, s, -1))
            h = _proj_residual([sb_out.reshape(n, -1)], [sb_w_out[o].astype(BF16)], mix_post_g[layer], h)
        h = _ffn(h, ffn_pre_g[layer], ffn_w_gate_up[layer].astype(BF16),
                 ffn_w_down[layer].astype(BF16), ffn_post_g[layer])
    return h.reshape(b, s, d)
```

```python
import functools
import math

import numpy as np
import jax
import jax.numpy as jnp
from jax import lax
from jax.experimental import pallas as pl
from jax.experimental.pallas import tpu as pltpu

D_MODEL = 1024
DEPTH = 4
HEAD_DIM = 64
MLA_HEADS = 8
MLA_NOPE = 64
MLA_ROPE = 32
MLA_V = 64
MLA_Q_LORA = 384
MLA_KV_LORA = 256
ROPE_THETA = 10000.0
MOBA_HEADS = 8
MOBA_BLOCK = 256
MOBA_TOPK = 3
SB_HEADS = 16
NUM_BUCKETS = 32
MAX_DISTANCE = 128
RMS_EPS = 1e-6
D_FF = 2816

LANES = 128
TQ = 256
MLA_ROWS = 512
TM = 512
FF_CHUNK = 256
MASK_PENALTY = -1e30
LOG2_E = 1.0 / math.log(2.0)
VMEM_LIMIT = 48 * 1024 * 1024

F32 = jnp.float32
BF16 = jnp.bfloat16


def _rms(x, g):
    return x * lax.rsqrt(jnp.mean(x * x, axis=-1, keepdims=True) + RMS_EPS) * g


def _dot(a, b):
    return jnp.dot(a, b, preferred_element_type=F32)


def _dot_nt(a, b):
    return lax.dot_general(a, b, (((1,), (1,)), ((), ())), preferred_element_type=F32)


def _params(*semantics):
    return pltpu.CompilerParams(dimension_semantics=semantics, vmem_limit_bytes=VMEM_LIMIT)


def _whole(shape):
    nd = len(shape)
    return pl.BlockSpec(shape, lambda *_: (0,) * nd, pipeline_mode=pl.Buffered(1))


def _norm_matmul_kernel(h_ref, g_ref, w_ref, o_ref, *, q_cols, q_scale):
    u = _rms(h_ref[...], g_ref[...]).astype(BF16)
    for c in range(0, o_ref.shape[-1], 512):
        y = _dot(u, w_ref[:, c:c + 512])
        if c < q_cols:
            y = y * q_scale
        o_ref[:, c:c + 512] = y.astype(o_ref.dtype)


def _norm_matmul(h, g, w, q_cols, q_scale):
    n, d = h.shape
    n_out = w.shape[1]
    return pl.pallas_call(
        functools.partial(_norm_matmul_kernel, q_cols=q_cols, q_scale=q_scale),
        grid=(n // TM,),
        in_specs=[pl.BlockSpec((TM, d), lambda t: (t, 0)), _whole((1, d)), _whole(w.shape)],
        out_specs=pl.BlockSpec((TM, n_out), lambda t: (t, 0)),
        out_shape=jax.ShapeDtypeStruct((n, n_out), BF16),
        compiler_params=_params("arbitrary"),
        name="norm_matmul",
    )(h, g.reshape(1, d), w)


def _head_lanes(shape, hh):
    lane = lax.broadcasted_iota(jnp.int32, shape, 1)
    return (lane >= hh * HEAD_DIM) & (lane < (hh + 1) * HEAD_DIM)


def _spare_lane_base(hh):
    return HEAD_DIM * (1 - hh)


def _store_masked_v(v_ref, vm_scr, ones_lane):
    v = v_ref[0].astype(F32)
    lane = lax.broadcasted_iota(jnp.int32, v.shape, 1)
    for hh in range(2):
        spare = jnp.where(lane == _spare_lane_base(hh), 1.0, 0.0) if ones_lane else 0.0
        vm_scr[hh] = jnp.where(_head_lanes(v.shape, hh), v, spare).astype(BF16)


def _normalized(acc, hh):
    base = _spare_lane_base(hh)
    return jnp.where(_head_lanes(acc.shape, hh), acc / acc[:, base:base + 1], 0.0)


_O_CQ = 0
_O_CKV = MLA_Q_LORA
_O_KR = _O_CKV + MLA_KV_LORA
_O_KRR = _O_KR + LANES
_O_MOBA = _O_KRR + LANES
_W_IN_EXT = _O_MOBA + 3 * MOBA_HEADS * HEAD_DIM
_QW = MLA_HEADS * LANES


def _even_in_kernel(h_ref, g_ref, w_in_ref, gq_ref, w_uq_ref, gkv_ref, w_k_ref, w_v_ref,
                    cq_ref, sq_ref, ck_ref, sk_ref, q_ref, k_ref, v_ref, moba_ref):
    u = _rms(h_ref[...], g_ref[...]).astype(BF16)
    c_q = _dot(u, w_in_ref[:, _O_CQ:_O_CKV])
    c_kv = _dot(u, w_in_ref[:, _O_CKV:_O_KR])
    kr = _dot(u, w_in_ref[:, _O_KR:_O_KRR])
    krr = _dot(u, w_in_ref[:, _O_KRR:_O_MOBA])
    hw = MOBA_HEADS * HEAD_DIM
    for c in range(0, 3 * hw, hw):
        y = _dot(u, w_in_ref[:, _O_MOBA + c:_O_MOBA + c + hw])
        if c == 0:
            y = y * LOG2_E
        moba_ref[:, c:c + hw] = y.astype(BF16)

    cqn = _rms(c_q, gq_ref[...]).astype(BF16)
    qa = _dot(cqn, w_uq_ref[:, :_QW])
    qb = _dot(cqn, w_uq_ref[:, _QW:])
    cq, sq = cq_ref[...], sq_ref[...]
    for h in range(MLA_HEADS):
        s = slice(h * LANES, (h + 1) * LANES)
        q_ref[:, s] = (qa[:, s] * cq + qb[:, s] * sq).astype(BF16)

    ckvn = _rms(c_kv, gkv_ref[...]).astype(BF16)
    kn = _dot(ckvn, w_k_ref[...])
    v = _dot(ckvn, w_v_ref[...])
    lane = lax.broadcasted_iota(jnp.int32, v.shape, 1)
    spare = jnp.where((lane // LANES) % 2 == 0, _spare_lane_base(0), _spare_lane_base(1))
    v_ref[...] = jnp.where(lane % LANES == spare, 1.0, v).astype(BF16)
    k_rope = kr * ck_ref[...] + krr * sk_ref[...]
    for h in range(MLA_HEADS):
        s = slice(h * LANES, (h + 1) * LANES)
        k_ref[:, s] = (kn[:, s] + k_rope).astype(BF16)


def _even_in(h, g, w_in, gq, w_uq, gkv, w_k, w_v, tables, seq):
    n, d = h.shape
    row = lambda t: (t, 0)
    pos = lambda t: (t % (seq // TM), 0)
    tab = pl.BlockSpec((TM, LANES), pos)
    outs = [(_QW, "q"), (_QW, "k"), (_QW, "v"), (3 * MOBA_HEADS * HEAD_DIM, "moba")]
    return pl.pallas_call(
        _even_in_kernel,
        grid=(n // TM,),
        in_specs=[pl.BlockSpec((TM, d), row), _whole((1, d)), _whole(w_in.shape),
                  _whole((1, MLA_Q_LORA)), _whole(w_uq.shape), _whole((1, MLA_KV_LORA)),
                  _whole(w_k.shape), _whole(w_v.shape), tab, tab, tab, tab],
        out_specs=[pl.BlockSpec((TM, w), row) for w, _ in outs],
        out_shape=[jax.ShapeDtypeStruct((n, w), BF16) for w, _ in outs],
        compiler_params=_params("arbitrary"),
        name="even_in",
    )(h, g.reshape(1, d), w_in, gq.reshape(1, -1), w_uq, gkv.reshape(1, -1), w_k, w_v, *tables)


def _mla_kernel(q_ref, k_ref, v_ref, o_ref):
    s_len = q_ref.shape[1]
    tr = MLA_ROWS
    row = lax.broadcasted_iota(jnp.int32, (tr, tr), 0)
    col = lax.broadcasted_iota(jnp.int32, (tr, tr), 1)
    causal = col <= row
    for i in range(s_len // tr):
        rows = slice(i * tr, (i + 1) * tr)
        kv_len = (i + 1) * tr
        out = jnp.zeros((tr, LANES), F32)
        for hh in range(2):
            lanes = slice(hh * LANES, (hh + 1) * LANES)
            s = _dot_nt(q_ref[0, rows, lanes], k_ref[0, :kv_len, lanes])
            diag = jnp.where(causal, s[:, i * tr:], -jnp.inf)
            s = diag if i == 0 else jnp.concatenate([s[:, :i * tr], diag], axis=1)
            p = jnp.exp2(s - jnp.max(s, axis=-1, keepdims=True))
            out = out + _normalized(_dot(p.astype(BF16), v_ref[0, :kv_len, lanes]), hh)
        o_ref[0, rows, :] = out.astype(o_ref.dtype)


def _mla_attention(q, k, v):
    b, s, _ = q.shape
    pairs = MLA_HEADS // 2
    qkv = pl.BlockSpec((1, s, 2 * LANES), lambda bi, p: (bi, 0, p))
    return pl.pallas_call(
        _mla_kernel,
        grid=(b, pairs),
        in_specs=[qkv, qkv, qkv],
        out_specs=pl.BlockSpec((1, s, LANES), lambda bi, p: (bi, 0, p)),
        out_shape=jax.ShapeDtypeStruct((b, s, MLA_HEADS * MLA_V), BF16),
        compiler_params=_params("arbitrary", "arbitrary"),
        name="mla_attention",
    )(q, k, v)


_SOFTPLUS2_CLAMP = 64.0


def _softplus2(z):
    return jnp.maximum(z, jnp.log2(1.0 + jnp.exp2(jnp.minimum(z, _SOFTPLUS2_CLAMP))))


def _sb_kernel(q_ref, k_ref, v_ref, o_ref, qm_scr, vm_scr, acc_scr, carry_scr):
    s_len = q_ref.shape[1]
    n_blk = s_len // TQ
    _store_masked_v(v_ref, vm_scr, ones_lane=False)
    qf = q_ref[0].astype(F32)
    for hh in range(2):
        qm_scr[hh] = jnp.where(_head_lanes(qf.shape, hh), qf, 0.0).astype(BF16)
    acc_scr[...] = jnp.zeros(acc_scr.shape, F32)

    row = lax.broadcasted_iota(jnp.int32, (TQ, TQ), 0)
    col = lax.broadcasted_iota(jnp.int32, (TQ, TQ), 1)
    strict = col < row
    tri = jnp.where(row >= col, 1.0, 0.0).astype(BF16)

    def head(hh, _):
        for j in range(n_blk - 1, -1, -1):
            keys = slice(j * TQ, (j + 1) * TQ)
            below = slice((j + 1) * TQ, s_len)
            z = _dot_nt(qm_scr[hh, j * TQ:, :], k_ref[0, keys, :])
            sp = _softplus2(z)
            sp_diag = jnp.where(strict, sp[:TQ], 0.0)
            sp = sp_diag if j == n_blk - 1 else jnp.concatenate([sp_diag, sp[TQ:]], axis=0)
            c = _dot(sp.astype(BF16), tri)
            tot = jnp.broadcast_to(c[:, 0:1], (c.shape[0], LANES))
            w = jnp.where(strict, jnp.exp2(z[:TQ] - c[:TQ]), 0.0)
            if j < n_blk - 1:
                carry = carry_scr[below, :]
                w_below = jnp.exp2(z[TQ:] - c[TQ:] - jnp.concatenate([carry, carry], axis=1))
                w = jnp.concatenate([w, w_below], axis=0)
                carry_scr[below, :] = carry + tot[TQ:]
            carry_scr[keys, :] = tot[:TQ]
            acc_scr[j * TQ:, :] += _dot(w.astype(BF16), vm_scr[hh, keys, :])
        return 0

    lax.fori_loop(0, 2, head, 0)
    o_ref[0] = acc_scr[...].astype(o_ref.dtype)


def _sb_attention(qkv):
    b, s, _ = qkv.shape
    pairs = SB_HEADS // 2
    return pl.pallas_call(
        _sb_kernel,
        grid=(b, pairs),
        in_specs=[pl.BlockSpec((1, s, LANES), lambda bi, p: (bi, 0, p)),
                  pl.BlockSpec((1, s, LANES), lambda bi, p: (bi, 0, pairs + p)),
                  pl.BlockSpec((1, s, LANES), lambda bi, p: (bi, 0, 2 * pairs + p))],
        out_specs=pl.BlockSpec((1, s, LANES), lambda bi, p: (bi, 0, p)),
        out_shape=jax.ShapeDtypeStruct((b, s, SB_HEADS * HEAD_DIM), BF16),
        scratch_shapes=[pltpu.VMEM((2, s, LANES), BF16), pltpu.VMEM((2, s, LANES), BF16),
                        pltpu.VMEM((s, LANES), F32), pltpu.VMEM((s, LANES), F32)],
        compiler_params=_params("arbitrary", "arbitrary"),
        name="sb_attention",
    )(qkv, qkv, qkv)


def _t5_bucket_np(rel):
    max_exact = NUM_BUCKETS // 2
    rel = np.maximum(rel, 0)
    large = max_exact + (np.log(np.maximum(rel, max_exact) / max_exact)
                         / math.log(MAX_DISTANCE / max_exact) * (NUM_BUCKETS - max_exact)).astype(np.int64)
    return np.where(rel < max_exact, rel, np.minimum(large, NUM_BUCKETS - 1)).astype(np.int32)


def _bias_kernel(bucket_ref, rel_bias_ref, o_ref):
    h = pl.program_id(0)
    for plane in range(2):
        bk = bucket_ref[plane]
        acc = jnp.full(bk.shape, -jnp.inf, F32)
        for b in range(NUM_BUCKETS):
            acc = jnp.where(bk == b, rel_bias_ref[b, h] * LOG2_E, acc)
        o_ref[0, plane] = acc
    o_ref[0, 2] = jnp.full((TQ, TQ), rel_bias_ref[NUM_BUCKETS - 1, h] * LOG2_E, F32)


def _moba_bias_tables(rel_bias):
    off = np.arange(TQ)
    rel_own = off[:, None] - off[None, :]
    own = np.where(rel_own >= 0, _t5_bucket_np(rel_own), -1)
    adj = _t5_bucket_np(rel_own + MOBA_BLOCK)
    buckets = jnp.asarray(np.stack([own, adj]).astype(np.int32))
    return pl.pallas_call(
        _bias_kernel,
        grid=(MOBA_HEADS,),
        in_specs=[_whole(buckets.shape), pl.BlockSpec(memory_space=pltpu.SMEM)],
        out_specs=pl.BlockSpec((1, 3, TQ, TQ), lambda h: (h, 0, 0, 0)),
        out_shape=jax.ShapeDtypeStruct((MOBA_HEADS, 3, TQ, TQ), F32),
        compiler_params=_params("arbitrary"),
        name="moba_bias_tables",
    )(buckets, rel_bias)


_N_BLK_LANES = 8


def _moba_constants(nb):
    assert nb <= _N_BLK_LANES
    diff = np.zeros((LANES, LANES), np.float32)
    count = np.zeros((2, LANES, LANES), np.float32)
    for a in range(nb):
        for b in range(nb):
            if a != b:
                diff[a * 8 + b, a] += 1.0
                diff[a * 8 + b, b] -= 1.0
                for hh in range(2):
                    count[hh, a * 8 + b, _spare_lane_base(hh) + b] = 1.0
                    count[hh, a * 8 + b, _spare_lane_base(hh) + _N_BLK_LANES + b] = 1.0
    return jnp.asarray(diff, BF16), jnp.asarray(count, BF16)


def _moba_kernel(q_ref, k_ref, v_ref, bias_ref, diff_ref, count_ref, o_ref,
                 qaug_scr, kaug_scr, vm_scr, *, nb):
    _store_masked_v(v_ref, vm_scr, ones_lane=True)
    kf = k_ref[0].astype(F32)
    qf = q_ref[0].astype(F32)
    s_len = kf.shape[0]
    kbar = jnp.mean(kf.reshape(nb, MOBA_BLOCK, LANES), axis=1)
    kbar = jnp.concatenate([kbar, jnp.zeros((LANES - nb, LANES), F32)], axis=0)
    p0 = kbar.astype(BF16)
    r1 = kbar - p0.astype(F32)
    p1 = r1.astype(BF16)
    p2 = (r1 - p1.astype(F32)).astype(BF16)
    d = diff_ref[...]
    kdiff = _dot(d, p0) + _dot(d, p1) + _dot(d, p2)

    lane = lax.broadcasted_iota(jnp.int32, (s_len, LANES), 1)
    row_blk = lax.broadcasted_iota(jnp.int32, (s_len, LANES), 0) // MOBA_BLOCK
    pair_a = lane // 8
    pair_b = lane % 8
    for hh in range(2):
        base = _spare_lane_base(hh)
        in_hi = (lane >= base) & (lane < base + _N_BLK_LANES)
        in_lo = (lane >= base + _N_BLK_LANES) & (lane < base + 2 * _N_BLK_LANES)
        blk_b = jnp.where(in_lo, lane - base - _N_BLK_LANES, lane - base)
        kaug_scr[hh] = jnp.where(in_hi | in_lo, jnp.where(blk_b == row_blk, 1.0, 0.0), kf).astype(BF16)

        qm = jnp.where(_head_lanes(qf.shape, hh), qf, 0.0)
        kd = jnp.where(_head_lanes(kdiff.shape, hh), kdiff, 0.0).astype(BF16)
        g = _dot_nt(qm.astype(BF16), kd)
        beats = (g > 0.0) | ((g == 0.0) & (pair_a < pair_b))
        beats = beats & (pair_a < row_blk) & (lane < 8 * _N_BLK_LANES)
        rank = _dot(jnp.where(beats, 1.0, 0.0).astype(BF16), count_ref[hh])
        chosen = (blk_b == row_blk) | ((blk_b < row_blk) & (rank < float(MOBA_TOPK)))
        far = jnp.broadcast_to(bias_ref[hh, 2, 0:1, 0:LANES], (s_len, LANES))
        far_hi = far.astype(BF16).astype(F32)
        is_far = chosen & (blk_b < row_blk - 1)
        hi_val = jnp.where(chosen, jnp.where(is_far, far_hi, 0.0), MASK_PENALTY)
        lo_val = jnp.where(is_far, far - far_hi, 0.0)
        qaug_scr[hh] = jnp.where(in_hi, hi_val, jnp.where(in_lo, lo_val, qm)).astype(BF16)

    for i in range(nb):
        rows = slice(i * TQ, (i + 1) * TQ)
        kv_len = (i + 1) * TQ
        out = jnp.zeros((TQ, LANES), F32)
        for hh in range(2):
            s = _dot_nt(qaug_scr[hh, rows, :], kaug_scr[hh, :kv_len, :])
            near = [s[:, j * TQ:(j + 1) * TQ] + bias_ref[hh, i - j] for j in range(max(i - 1, 0), i + 1)]
            s = jnp.concatenate(([s[:, :(i - 1) * TQ]] if i >= 2 else []) + near, axis=1)
            p = jnp.exp2(s - jnp.max(s, axis=-1, keepdims=True))
            out = out + _normalized(_dot(p.astype(BF16), vm_scr[hh, :kv_len, :]), hh)
        o_ref[0, rows, :] = out.astype(o_ref.dtype)


def _moba_attention(qkv, bias_tables):
    b, s, _ = qkv.shape
    nb = s // MOBA_BLOCK
    pairs = MOBA_HEADS // 2
    diff, count = _moba_constants(nb)
    return pl.pallas_call(
        functools.partial(_moba_kernel, nb=nb),
        grid=(b, pairs),
        in_specs=[pl.BlockSpec((1, s, LANES), lambda bi, p: (bi, 0, p)),
                  pl.BlockSpec((1, s, LANES), lambda bi, p: (bi, 0, pairs + p)),
                  pl.BlockSpec((1, s, LANES), lambda bi, p: (bi, 0, 2 * pairs + p)),
                  pl.BlockSpec((2, 3, TQ, TQ), lambda bi, p: (p, 0, 0, 0)),
                  _whole(diff.shape), _whole(count.shape)],
        out_specs=pl.BlockSpec((1, s, LANES), lambda bi, p: (bi, 0, p)),
        out_shape=jax.ShapeDtypeStruct((b, s, MOBA_HEADS * HEAD_DIM), BF16),
        scratch_shapes=[pltpu.VMEM((2, s, LANES), BF16)] * 3,
        compiler_params=_params("arbitrary", "arbitrary"),
        name="moba_attention",
    )(qkv, qkv, qkv, bias_tables, diff, count)


def _mix_ffn_kernel(*refs, n_in):
    a_refs = refs[:n_in]
    h_ref, w_out_ref, gmix_ref, gpre_ref, wgu_ref, wd_ref, gpost_ref, o_ref, act_scr = refs[n_in:]
    mixed, k0 = None, 0
    for a_ref in a_refs:
        k1 = k0 + a_ref.shape[1]
        part = _dot(a_ref[...], w_out_ref[k0:k1, :])
        mixed, k0 = part if mixed is None else mixed + part, k1
    x = h_ref[...] + _rms(mixed, gmix_ref[...])
    u = _rms(x, gpre_ref[...]).astype(BF16)
    for c in range(0, D_FF, FF_CHUNK):
        gate = _dot(u, wgu_ref[:, c:c + FF_CHUNK])
        up = _dot(u, wgu_ref[:, D_FF + c:D_FF + c + FF_CHUNK])
        act_scr[:, c:c + FF_CHUNK] = (gate * jax.nn.sigmoid(gate) * up).astype(BF16)
    f = _dot(act_scr[...], wd_ref[...])
    o_ref[...] = x + _rms(f, gpost_ref[...])


def _mix_ffn(acts, h, w_out, gmix, gpre, wgu, wd, gpost):
    n, d = h.shape
    row = lambda t: (t, 0)
    vec = _whole((1, d))
    return pl.pallas_call(
        functools.partial(_mix_ffn_kernel, n_in=len(acts)),
        grid=(n // TM,),
        in_specs=([pl.BlockSpec((TM, a.shape[1]), row) for a in acts]
                  + [pl.BlockSpec((TM, d), row), _whole(w_out.shape), vec, vec,
                     _whole(wgu.shape), _whole(wd.shape), vec]),
        out_specs=pl.BlockSpec((TM, d), row),
        out_shape=jax.ShapeDtypeStruct((n, d), F32),
        scratch_shapes=[pltpu.VMEM((TM, D_FF), BF16)],
        compiler_params=_params("arbitrary"),
        name="mix_ffn",
    )(*acts, h, w_out, gmix.reshape(1, d), gpre.reshape(1, d), wgu, wd, gpost.reshape(1, d))


def _rot_half_cols(w):
    half = w.shape[-1] // 2
    return jnp.concatenate([-w[..., half:], w[..., :half]], axis=-1)


def _prep_even_weights(w_in, w_uq, w_ukv):
    d = w_in.shape[0]
    o1, o2, o3 = MLA_Q_LORA, MLA_Q_LORA + MLA_KV_LORA, MLA_Q_LORA + MLA_KV_LORA + MLA_ROPE
    hw = MOBA_HEADS * HEAD_DIM
    z = lambda *shape: jnp.zeros(shape, F32)
    kr = w_in[:, o2:o3]
    w_in_ext = jnp.concatenate(
        [w_in[:, :o2],
         z(d, MLA_NOPE), kr, z(d, LANES - MLA_NOPE - MLA_ROPE),
         z(d, MLA_NOPE), _rot_half_cols(kr), z(d, LANES - MLA_NOPE - MLA_ROPE),
         w_in[:, o3:o3 + hw] * (HEAD_DIM ** -0.5), w_in[:, o3 + hw:]], axis=1).astype(BF16)

    wq = w_uq.reshape(MLA_Q_LORA, MLA_HEADS, MLA_NOPE + MLA_ROPE)
    nope, rp = wq[..., :MLA_NOPE], wq[..., MLA_NOPE:]
    tail = z(MLA_Q_LORA, MLA_HEADS, LANES - MLA_NOPE - MLA_ROPE)
    qa = jnp.concatenate([nope, rp, tail], axis=-1).reshape(MLA_Q_LORA, _QW)
    qb = jnp.concatenate([jnp.zeros_like(nope), _rot_half_cols(rp), tail], axis=-1).reshape(MLA_Q_LORA, _QW)
    w_uq_ext = jnp.concatenate([qa, qb], axis=1).astype(BF16)

    wkv = w_ukv.reshape(MLA_KV_LORA, MLA_HEADS, MLA_NOPE + MLA_V)
    k_nope, v = wkv[..., :MLA_NOPE], wkv[..., MLA_NOPE:]
    w_k = jnp.concatenate([k_nope, jnp.zeros_like(k_nope)], axis=-1).reshape(MLA_KV_LORA, _QW).astype(BF16)
    v = v.reshape(MLA_KV_LORA, MLA_HEADS // 2, 2, MLA_V)
    zero = jnp.zeros_like(v[:, :, 0])
    w_v = jnp.stack([jnp.concatenate([v[:, :, 0], zero], axis=-1),
                     jnp.concatenate([zero, v[:, :, 1]], axis=-1)], axis=2)
    w_v = w_v.reshape(MLA_KV_LORA, _QW).astype(BF16)
    return w_in_ext, w_uq_ext, w_k, w_v


def _rope_tables(seq):
    half = MLA_ROPE // 2
    inv_freq = ROPE_THETA ** (-jnp.arange(half, dtype=F32) / half)
    ang = jnp.arange(seq).astype(F32)[:, None] * inv_freq[None, :]
    cos, sin = jnp.cos(ang), jnp.sin(ang)
    ones, zeros = jnp.ones((seq, MLA_NOPE), F32), jnp.zeros((seq, MLA_NOPE), F32)
    tail = jnp.zeros((seq, LANES - MLA_NOPE - MLA_ROPE), F32)
    scale = (MLA_NOPE + MLA_ROPE) ** -0.5 * LOG2_E
    cq = jnp.concatenate([ones, cos, cos, tail], axis=1) * scale
    sq = jnp.concatenate([zeros, sin, sin, tail], axis=1) * scale
    ck = jnp.concatenate([zeros, cos, cos, tail], axis=1)
    sk = jnp.concatenate([zeros, sin, sin, tail], axis=1)
    return cq, sq, ck, sk


def kernel(x, mix_pre_g, mix_post_g, ffn_pre_g, ffn_post_g, ab_w_in, mla_q_norm_g, mla_w_uq,
           mla_kv_norm_g, mla_w_ukv, ab_w_out, rel_bias, sb_w_qkv, sb_w_out, ffn_w_gate_up, ffn_w_down):
    b, s, d = x.shape
    n = b * s
    h = x.reshape(n, d)
    tables = _rope_tables(s)
    bias_tables = _moba_bias_tables(rel_bias)
    for layer in range(DEPTH):
        if layer % 2 == 0:
            e = layer // 2
            w_in, w_uq, w_k, w_v = _prep_even_weights(ab_w_in[e], mla_w_uq[e], mla_w_ukv[e])
            q, k, v, moba_qkv = _even_in(h, mix_pre_g[layer], w_in, mla_q_norm_g[e], w_uq,
                                         mla_kv_norm_g[e], w_k, w_v, tables, s)
            mla_out = _mla_attention(q.reshape(b, s, -1), k.reshape(b, s, -1), v.reshape(b, s, -1))
            moba_out = _moba_attention(moba_qkv.reshape(b, s, -1), bias_tables)
            mixed, w_out = [mla_out.reshape(n, -1), moba_out.reshape(n, -1)], ab_w_out[e]
        else:
            o = layer // 2
            width = SB_HEADS * HEAD_DIM
            qkv = _norm_matmul(h, mix_pre_g[layer], sb_w_qkv[o].astype(BF16),
                               width, HEAD_DIM ** -0.5 * LOG2_E)
            mixed, w_out = [_sb_attention(qkv.reshape(b, s, -1)).reshape(n, -1)], sb_w_out[o]
        h = _mix_ffn(mixed, h, w_out.astype(BF16), mix_post_g[layer], ffn_pre_g[layer],
                     ffn_w_gate_up[layer].astype(BF16), ffn_w_down[layer].astype(BF16),
                     ffn_post_g[layer])
    return h.reshape(b, s, d)
```

```python
import functools
import math

import numpy as np
import jax
import jax.numpy as jnp
from jax import lax
from jax.experimental import pallas as pl
from jax.experimental.pallas import tpu as pltpu

D_MODEL = 1024
DEPTH = 4
HEAD_DIM = 64
MLA_HEADS = 8
MLA_NOPE = 64
MLA_ROPE = 32
MLA_V = 64
MLA_Q_LORA = 384
MLA_KV_LORA = 256
ROPE_THETA = 10000.0
MOBA_HEADS = 8
MOBA_BLOCK = 256
MOBA_TOPK = 3
SB_HEADS = 16
NUM_BUCKETS = 32
MAX_DISTANCE = 128
RMS_EPS = 1e-6
D_FF = 2816

LANES = 128
TQ = 256
MLA_ROWS = 512
TM = 512
FF_CHUNK = 256
MASK_PENALTY = -1e30
LOG2_E = 1.0 / math.log(2.0)
VMEM_LIMIT = 48 * 1024 * 1024

F32 = jnp.float32
BF16 = jnp.bfloat16


def _rms(x, g):
    return x * lax.rsqrt(jnp.mean(x * x, axis=-1, keepdims=True) + RMS_EPS) * g


def _dot(a, b):
    return jnp.dot(a, b, preferred_element_type=F32)


def _dot_nt(a, b):
    return lax.dot_general(a, b, (((1,), (1,)), ((), ())), preferred_element_type=F32)


def _params(*semantics):
    return pltpu.CompilerParams(dimension_semantics=semantics, vmem_limit_bytes=VMEM_LIMIT)


def _whole(shape):
    nd = len(shape)
    return pl.BlockSpec(shape, lambda *_: (0,) * nd, pipeline_mode=pl.Buffered(1))


def _norm_matmul_kernel(h_ref, g_ref, w_ref, o_ref, *, q_cols, q_scale):
    u = _rms(h_ref[...], g_ref[...]).astype(BF16)
    for c in range(0, o_ref.shape[-1], 512):
        y = _dot(u, w_ref[:, c:c + 512])
        if c < q_cols:
            y = y * q_scale
        o_ref[:, c:c + 512] = y.astype(o_ref.dtype)


def _norm_matmul(h, g, w, q_cols, q_scale):
    n, d = h.shape
    n_out = w.shape[1]
    return pl.pallas_call(
        functools.partial(_norm_matmul_kernel, q_cols=q_cols, q_scale=q_scale),
        grid=(n // TM,),
        in_specs=[pl.BlockSpec((TM, d), lambda t: (t, 0)), _whole((1, d)), _whole(w.shape)],
        out_specs=pl.BlockSpec((TM, n_out), lambda t: (t, 0)),
        out_shape=jax.ShapeDtypeStruct((n, n_out), BF16),
        compiler_params=_params("arbitrary"),
        name="norm_matmul",
    )(h, g.reshape(1, d), w)


def _head_lanes(shape, hh):
    lane = lax.broadcasted_iota(jnp.int32, shape, 1)
    return (lane >= hh * HEAD_DIM) & (lane < (hh + 1) * HEAD_DIM)


def _spare_lane_base(hh):
    return HEAD_DIM * (1 - hh)


def _store_masked_v(v_ref, vm_scr, ones_lane):
    v = v_ref[0].astype(F32)
    lane = lax.broadcasted_iota(jnp.int32, v.shape, 1)
    for hh in range(2):
        spare = jnp.where(lane == _spare_lane_base(hh), 1.0, 0.0) if ones_lane else 0.0
        vm_scr[hh] = jnp.where(_head_lanes(v.shape, hh), v, spare).astype(BF16)


def _normalized(acc, hh):
    base = _spare_lane_base(hh)
    return jnp.where(_head_lanes(acc.shape, hh), acc / acc[:, base:base + 1], 0.0)


_O_CQ = 0
_O_CKV = MLA_Q_LORA
_O_KR = _O_CKV + MLA_KV_LORA
_O_KRR = _O_KR + LANES
_O_MOBA = _O_KRR + LANES
_W_IN_EXT = _O_MOBA + 3 * MOBA_HEADS * HEAD_DIM
_QW = MLA_HEADS * LANES


def _even_in_kernel(h_ref, g_ref, w_in_ref, gq_ref, w_uq_ref, gkv_ref, w_k_ref, w_v_ref,
                    cq_ref, sq_ref, ck_ref, sk_ref, q_ref, k_ref, v_ref, moba_ref):
    u = _rms(h_ref[...], g_ref[...]).astype(BF16)
    c_q = _dot(u, w_in_ref[:, _O_CQ:_O_CKV])
    c_kv = _dot(u, w_in_ref[:, _O_CKV:_O_KR])
    kr = _dot(u, w_in_ref[:, _O_KR:_O_KRR])
    krr = _dot(u, w_in_ref[:, _O_KRR:_O_MOBA])
    hw = MOBA_HEADS * HEAD_DIM
    for c in range(0, 3 * hw, hw):
        y = _dot(u, w_in_ref[:, _O_MOBA + c:_O_MOBA + c + hw])
        if c == 0:
            y = y * LOG2_E
        moba_ref[:, c:c + hw] = y.astype(BF16)

    cqn = _rms(c_q, gq_ref[...]).astype(BF16)
    qa = _dot(cqn, w_uq_ref[:, :_QW])
    qb = _dot(cqn, w_uq_ref[:, _QW:])
    cq, sq = cq_ref[...], sq_ref[...]
    for h in range(MLA_HEADS):
        s = slice(h * LANES, (h + 1) * LANES)
        q_ref[:, s] = (qa[:, s] * cq + qb[:, s] * sq).astype(BF16)

    ckvn = _rms(c_kv, gkv_ref[...]).astype(BF16)
    kn = _dot(ckvn, w_k_ref[...])
    v = _dot(ckvn, w_v_ref[...])
    lane = lax.broadcasted_iota(jnp.int32, v.shape, 1)
    spare = jnp.where((lane // LANES) % 2 == 0, _spare_lane_base(0), _spare_lane_base(1))
    v_ref[...] = jnp.where(lane % LANES == spare, 1.0, v).astype(BF16)
    k_rope = kr * ck_ref[...] + krr * sk_ref[...]
    for h in range(MLA_HEADS):
        s = slice(h * LANES, (h + 1) * LANES)
        k_ref[:, s] = (kn[:, s] + k_rope).astype(BF16)


def _even_in(h, g, w_in, gq, w_uq, gkv, w_k, w_v, tables, seq):
    n, d = h.shape
    row = lambda t: (t, 0)
    pos = lambda t: (t % (seq // TM), 0)
    tab = pl.BlockSpec((TM, LANES), pos)
    outs = [(_QW, "q"), (_QW, "k"), (_QW, "v"), (3 * MOBA_HEADS * HEAD_DIM, "moba")]
    return pl.pallas_call(
        _even_in_kernel,
        grid=(n // TM,),
        in_specs=[pl.BlockSpec((TM, d), row), _whole((1, d)), _whole(w_in.shape),
                  _whole((1, MLA_Q_LORA)), _whole(w_uq.shape), _whole((1, MLA_KV_LORA)),
                  _whole(w_k.shape), _whole(w_v.shape), tab, tab, tab, tab],
        out_specs=[pl.BlockSpec((TM, w), row) for w, _ in outs],
        out_shape=[jax.ShapeDtypeStruct((n, w), BF16) for w, _ in outs],
        compiler_params=_params("arbitrary"),
        name="even_in",
    )(h, g.reshape(1, d), w_in, gq.reshape(1, -1), w_uq, gkv.reshape(1, -1), w_k, w_v, *tables)


def _mla_kernel(q_ref, k_ref, v_ref, o_ref):
    s_len = q_ref.shape[1]
    tr = MLA_ROWS
    row = lax.broadcasted_iota(jnp.int32, (tr, tr), 0)
    col = lax.broadcasted_iota(jnp.int32, (tr, tr), 1)
    causal = col <= row
    for i in range(s_len // tr):
        rows = slice(i * tr, (i + 1) * tr)
        kv_len = (i + 1) * tr
        out = jnp.zeros((tr, LANES), F32)
        for hh in range(2):
            lanes = slice(hh * LANES, (hh + 1) * LANES)
            s = _dot_nt(q_ref[0, rows, lanes], k_ref[0, :kv_len, lanes])
            diag = jnp.where(causal, s[:, i * tr:], -jnp.inf)
            s = diag if i == 0 else jnp.concatenate([s[:, :i * tr], diag], axis=1)
            p = jnp.exp2(s - jnp.max(s, axis=-1, keepdims=True))
            out = out + _normalized(_dot(p.astype(BF16), v_ref[0, :kv_len, lanes]), hh)
        o_ref[0, rows, :] = out.astype(o_ref.dtype)


def _mla_attention(q, k, v):
    b, s, _ = q.shape
    pairs = MLA_HEADS // 2
    qkv = pl.BlockSpec((1, s, 2 * LANES), lambda bi, p: (bi, 0, p))
    return pl.pallas_call(
        _mla_kernel,
        grid=(b, pairs),
        in_specs=[qkv, qkv, qkv],
        out_specs=pl.BlockSpec((1, s, LANES), lambda bi, p: (bi, 0, p)),
        out_shape=jax.ShapeDtypeStruct((b, s, MLA_HEADS * MLA_V), BF16),
        compiler_params=_params("arbitrary", "arbitrary"),
        name="mla_attention",
    )(q, k, v)


_SOFTPLUS2_CLAMP = 64.0


def _softplus2(z):
    return jnp.maximum(z, jnp.log2(1.0 + jnp.exp2(jnp.minimum(z, _SOFTPLUS2_CLAMP))))


SB_UNDERFLOW = 160.0
SB_NEAR_BLOCKS = 2


def _sb_kernel(q_ref, k_ref, v_ref, o_ref, qm_scr, vm_scr, acc_scr, carry_scr, limit_scr):
    s_len = q_ref.shape[1]
    n_blk = s_len // TQ
    q, k, v = q_ref[0], k_ref[0], v_ref[0]
    feat = lax.broadcasted_iota(jnp.int32, (LANES, LANES), 0) // HEAD_DIM
    lane = lax.broadcasted_iota(jnp.int32, (LANES, LANES), 1) // HEAD_DIM
    same_head = jnp.where(feat == lane, 1.0, 0.0).astype(BF16)
    q_max2 = jnp.max(_dot(q * q, same_head), axis=0, keepdims=True)
    k_max2 = jnp.max(_dot(k * k, same_head), axis=0, keepdims=True)
    z_bound = jnp.sqrt(q_max2 * k_max2) * 1.02
    for hh in range(2):
        in_head = _head_lanes((1, LANES), hh)
        lane_mask = jnp.where(in_head, 1.0, 0.0).astype(BF16)
        qm_scr[hh] = q * lane_mask
        vm_scr[hh] = v * lane_mask
        bound = jnp.max(jnp.where(in_head, z_bound, 0.0), axis=1, keepdims=True)
        limit_scr[hh] = jnp.broadcast_to(bound + SB_UNDERFLOW, limit_scr.shape[1:])
    acc_scr[...] = jnp.zeros(acc_scr.shape, F32)

    row = lax.broadcasted_iota(jnp.int32, (TQ, TQ), 0)
    col = lax.broadcasted_iota(jnp.int32, (TQ, TQ), 1)
    strict = col < row
    tri = jnp.where(row >= col, 1.0, 0.0).astype(BF16)

    def head(hh, _):
        def tile_rows(j, r0, r1):
            keys = slice(j * TQ, (j + 1) * TQ)
            rows = slice(r0 * TQ, r1 * TQ)
            z = _dot_nt(qm_scr[hh, rows, :], k_ref[0, keys, :])
            sp = _softplus2(z)
            if r0 == j:
                sp_diag = jnp.where(strict, sp[:TQ], 0.0)
                sp = sp_diag if r1 == r0 + 1 else jnp.concatenate([sp_diag, sp[TQ:]], axis=0)
            c = _dot(sp.astype(BF16), tri)
            tot = jnp.broadcast_to(c[:, 0:1], (c.shape[0], LANES))
            first = TQ if r0 == j else 0
            w = []
            if r0 == j:
                w.append(jnp.where(strict, jnp.exp2(z[:TQ] - c[:TQ]), 0.0))
                carry_scr[keys, :] = tot[:TQ]
            if r1 * TQ - r0 * TQ > first:
                below = slice(r0 * TQ + first, r1 * TQ)
                carry = carry_scr[below, :]
                w.append(jnp.exp2(z[first:] - c[first:] - jnp.concatenate([carry, carry], axis=1)))
                carry_scr[below, :] = carry + tot[first:]
            w = (w[0] if len(w) == 1 else jnp.concatenate(w, axis=0)).astype(BF16)
            acc_scr[rows, :] += _dot(w, vm_scr[hh, keys, :])

        for j in range(n_blk - 1, -1, -1):
            tile_rows(j, j, min(j + SB_NEAR_BLOCKS, n_blk))

        def distance(dist):
            least = jnp.min(carry_scr[dist * TQ:, :], axis=0, keepdims=True)
            live = jnp.min(least - limit_scr[hh, 0:1, :]) < 0.0

            @pl.when(live)
            def _():
                for i in range(dist, n_blk):
                    tile_rows(i - dist, i, i + 1)
                if dist + 1 < n_blk:
                    distance(dist + 1)

        if SB_NEAR_BLOCKS < n_blk:
            distance(SB_NEAR_BLOCKS)
        return 0

    lax.fori_loop(0, 2, head, 0)
    o_ref[0] = acc_scr[...].astype(o_ref.dtype)


def _sb_attention(qkv):
    b, s, _ = qkv.shape
    pairs = SB_HEADS // 2
    return pl.pallas_call(
        _sb_kernel,
        grid=(b, pairs),
        in_specs=[pl.BlockSpec((1, s, LANES), lambda bi, p: (bi, 0, p)),
                  pl.BlockSpec((1, s, LANES), lambda bi, p: (bi, 0, pairs + p)),
                  pl.BlockSpec((1, s, LANES), lambda bi, p: (bi, 0, 2 * pairs + p))],
        out_specs=pl.BlockSpec((1, s, LANES), lambda bi, p: (bi, 0, p)),
        out_shape=jax.ShapeDtypeStruct((b, s, SB_HEADS * HEAD_DIM), BF16),
        scratch_shapes=[pltpu.VMEM((2, s, LANES), BF16), pltpu.VMEM((2, s, LANES), BF16),
                        pltpu.VMEM((s, LANES), F32), pltpu.VMEM((s, LANES), F32),
                        pltpu.VMEM((2, 8, LANES), F32)],
        compiler_params=_params("arbitrary", "arbitrary"),
        name="sb_attention",
    )(qkv, qkv, qkv)


def _t5_bucket_np(rel):
    max_exact = NUM_BUCKETS // 2
    rel = np.maximum(rel, 0)
    large = max_exact + (np.log(np.maximum(rel, max_exact) / max_exact)
                         / math.log(MAX_DISTANCE / max_exact) * (NUM_BUCKETS - max_exact)).astype(np.int64)
    return np.where(rel < max_exact, rel, np.minimum(large, NUM_BUCKETS - 1)).astype(np.int32)


def _bias_kernel(bucket_ref, rel_bias_ref, o_ref):
    h = pl.program_id(0)
    for plane in range(2):
        bk = bucket_ref[plane]
        acc = jnp.full(bk.shape, -jnp.inf, F32)
        for b in range(NUM_BUCKETS):
            acc = jnp.where(bk == b, rel_bias_ref[b, h] * LOG2_E, acc)
        o_ref[0, plane] = acc
    o_ref[0, 2] = jnp.full((TQ, TQ), rel_bias_ref[NUM_BUCKETS - 1, h] * LOG2_E, F32)


def _moba_bias_tables(rel_bias):
    off = np.arange(TQ)
    rel_own = off[:, None] - off[None, :]
    own = np.where(rel_own >= 0, _t5_bucket_np(rel_own), -1)
    adj = _t5_bucket_np(rel_own + MOBA_BLOCK)
    buckets = jnp.asarray(np.stack([own, adj]).astype(np.int32))
    return pl.pallas_call(
        _bias_kernel,
        grid=(MOBA_HEADS,),
        in_specs=[_whole(buckets.shape), pl.BlockSpec(memory_space=pltpu.SMEM)],
        out_specs=pl.BlockSpec((1, 3, TQ, TQ), lambda h: (h, 0, 0, 0)),
        out_shape=jax.ShapeDtypeStruct((MOBA_HEADS, 3, TQ, TQ), F32),
        compiler_params=_params("arbitrary"),
        name="moba_bias_tables",
    )(buckets, rel_bias)


_N_BLK_LANES = 8


def _moba_constants(nb):
    assert nb <= _N_BLK_LANES
    diff = np.zeros((LANES, LANES), np.float32)
    count = np.zeros((2, LANES, LANES), np.float32)
    for a in range(nb):
        for b in range(nb):
            if a != b:
                diff[a * 8 + b, a] += 1.0
                diff[a * 8 + b, b] -= 1.0
                for hh in range(2):
                    count[hh, a * 8 + b, _spare_lane_base(hh) + b] = 1.0
                    count[hh, a * 8 + b, _spare_lane_base(hh) + _N_BLK_LANES + b] = 1.0
    return jnp.asarray(diff, BF16), jnp.asarray(count, BF16)


def _moba_kernel(q_ref, k_ref, v_ref, bias_ref, diff_ref, count_ref, o_ref,
                 qaug_scr, kaug_scr, vm_scr, *, nb):
    _store_masked_v(v_ref, vm_scr, ones_lane=True)
    kf = k_ref[0].astype(F32)
    qf = q_ref[0].astype(F32)
    s_len = kf.shape[0]
    kbar = jnp.mean(kf.reshape(nb, MOBA_BLOCK, LANES), axis=1)
    kbar = jnp.concatenate([kbar, jnp.zeros((LANES - nb, LANES), F32)], axis=0)
    p0 = kbar.astype(BF16)
    r1 = kbar - p0.astype(F32)
    p1 = r1.astype(BF16)
    p2 = (r1 - p1.astype(F32)).astype(BF16)
    d = diff_ref[...]
    kdiff = _dot(d, p0) + _dot(d, p1) + _dot(d, p2)

    lane = lax.broadcasted_iota(jnp.int32, (s_len, LANES), 1)
    row_blk = lax.broadcasted_iota(jnp.int32, (s_len, LANES), 0) // MOBA_BLOCK
    pair_a = lane // 8
    pair_b = lane % 8
    for hh in range(2):
        base = _spare_lane_base(hh)
        in_hi = (lane >= base) & (lane < base + _N_BLK_LANES)
        in_lo = (lane >= base + _N_BLK_LANES) & (lane < base + 2 * _N_BLK_LANES)
        blk_b = jnp.where(in_lo, lane - base - _N_BLK_LANES, lane - base)
        kaug_scr[hh] = jnp.where(in_hi | in_lo, jnp.where(blk_b == row_blk, 1.0, 0.0), kf).astype(BF16)

        qm = jnp.where(_head_lanes(qf.shape, hh), qf, 0.0)
        kd = jnp.where(_head_lanes(kdiff.shape, hh), kdiff, 0.0).astype(BF16)
        g = _dot_nt(qm.astype(BF16), kd)
        beats = (g > 0.0) | ((g == 0.0) & (pair_a < pair_b))
        beats = beats & (pair_a < row_blk) & (lane < 8 * _N_BLK_LANES)
        rank = _dot(jnp.where(beats, 1.0, 0.0).astype(BF16), count_ref[hh])
        chosen = (blk_b == row_blk) | ((blk_b < row_blk) & (rank < float(MOBA_TOPK)))
        far = jnp.broadcast_to(bias_ref[hh, 2, 0:1, 0:LANES], (s_len, LANES))
        far_hi = far.astype(BF16).astype(F32)
        is_far = chosen & (blk_b < row_blk - 1)
        hi_val = jnp.where(chosen, jnp.where(is_far, far_hi, 0.0), MASK_PENALTY)
        lo_val = jnp.where(is_far, far - far_hi, 0.0)
        qaug_scr[hh] = jnp.where(in_hi, hi_val, jnp.where(in_lo, lo_val, qm)).astype(BF16)

    for i in range(nb):
        rows = slice(i * TQ, (i + 1) * TQ)
        kv_len = (i + 1) * TQ
        out = jnp.zeros((TQ, LANES), F32)
        for hh in range(2):
            s = _dot_nt(qaug_scr[hh, rows, :], kaug_scr[hh, :kv_len, :])
            near = [s[:, j * TQ:(j + 1) * TQ] + bias_ref[hh, i - j] for j in range(max(i - 1, 0), i + 1)]
            s = jnp.concatenate(([s[:, :(i - 1) * TQ]] if i >= 2 else []) + near, axis=1)
            p = jnp.exp2(s - jnp.max(s, axis=-1, keepdims=True))
            out = out + _normalized(_dot(p.astype(BF16), vm_scr[hh, :kv_len, :]), hh)
        o_ref[0, rows, :] = out.astype(o_ref.dtype)


def _moba_attention(qkv, bias_tables):
    b, s, _ = qkv.shape
    nb = s // MOBA_BLOCK
    pairs = MOBA_HEADS // 2
    diff, count = _moba_constants(nb)
    return pl.pallas_call(
        functools.partial(_moba_kernel, nb=nb),
        grid=(b, pairs),
        in_specs=[pl.BlockSpec((1, s, LANES), lambda bi, p: (bi, 0, p)),
                  pl.BlockSpec((1, s, LANES), lambda bi, p: (bi, 0, pairs + p)),
                  pl.BlockSpec((1, s, LANES), lambda bi, p: (bi, 0, 2 * pairs + p)),
                  pl.BlockSpec((2, 3, TQ, TQ), lambda bi, p: (p, 0, 0, 0)),
                  _whole(diff.shape), _whole(count.shape)],
        out_specs=pl.BlockSpec((1, s, LANES), lambda bi, p: (bi, 0, p)),
        out_shape=jax.ShapeDtypeStruct((b, s, MOBA_HEADS * HEAD_DIM), BF16),
        scratch_shapes=[pltpu.VMEM((2, s, LANES), BF16)] * 3,
        compiler_params=_params("arbitrary", "arbitrary"),
        name="moba_attention",
    )(qkv, qkv, qkv, bias_tables, diff, count)


def _mix_ffn_kernel(*refs, n_in):
    a_refs = refs[:n_in]
    h_ref, w_out_ref, gmix_ref, gpre_ref, wgu_ref, wd_ref, gpost_ref, o_ref, act_scr = refs[n_in:]
    mixed, k0 = None, 0
    for a_ref in a_refs:
        k1 = k0 + a_ref.shape[1]
        part = _dot(a_ref[...], w_out_ref[k0:k1, :])
        mixed, k0 = part if mixed is None else mixed + part, k1
    x = h_ref[...] + _rms(mixed, gmix_ref[...])
    u = _rms(x, gpre_ref[...]).astype(BF16)
    for c in range(0, D_FF, FF_CHUNK):
        gate = _dot(u, wgu_ref[:, c:c + FF_CHUNK])
        up = _dot(u, wgu_ref[:, D_FF + c:D_FF + c + FF_CHUNK])
        act_scr[:, c:c + FF_CHUNK] = (gate * jax.nn.sigmoid(gate) * up).astype(BF16)
    f = _dot(act_scr[...], wd_ref[...])
    o_ref[...] = x + _rms(f, gpost_ref[...])


def _mix_ffn(acts, h, w_out, gmix, gpre, wgu, wd, gpost):
    n, d = h.shape
    row = lambda t: (t, 0)
    vec = _whole((1, d))
    return pl.pallas_call(
        functools.partial(_mix_ffn_kernel, n_in=len(acts)),
        grid=(n // TM,),
        in_specs=([pl.BlockSpec((TM, a.shape[1]), row) for a in acts]
                  + [pl.BlockSpec((TM, d), row), _whole(w_out.shape), vec, vec,
                     _whole(wgu.shape), _whole(wd.shape), vec]),
        out_specs=pl.BlockSpec((TM, d), row),
        out_shape=jax.ShapeDtypeStruct((n, d), F32),
        scratch_shapes=[pltpu.VMEM((TM, D_FF), BF16)],
        compiler_params=_params("arbitrary"),
        name="mix_ffn",
    )(*acts, h, w_out, gmix.reshape(1, d), gpre.reshape(1, d), wgu, wd, gpost.reshape(1, d))


def _rot_half_cols(w):
    half = w.shape[-1] // 2
    return jnp.concatenate([-w[..., half:], w[..., :half]], axis=-1)


def _prep_even_weights(w_in, w_uq, w_ukv):
    d = w_in.shape[0]
    o1, o2, o3 = MLA_Q_LORA, MLA_Q_LORA + MLA_KV_LORA, MLA_Q_LORA + MLA_KV_LORA + MLA_ROPE
    hw = MOBA_HEADS * HEAD_DIM
    z = lambda *shape: jnp.zeros(shape, F32)
    kr = w_in[:, o2:o3]
    w_in_ext = jnp.concatenate(
        [w_in[:, :o2],
         z(d, MLA_NOPE), kr, z(d, LANES - MLA_NOPE - MLA_ROPE),
         z(d, MLA_NOPE), _rot_half_cols(kr), z(d, LANES - MLA_NOPE - MLA_ROPE),
         w_in[:, o3:o3 + hw] * (HEAD_DIM ** -0.5), w_in[:, o3 + hw:]], axis=1).astype(BF16)

    wq = w_uq.reshape(MLA_Q_LORA, MLA_HEADS, MLA_NOPE + MLA_ROPE)
    nope, rp = wq[..., :MLA_NOPE], wq[..., MLA_NOPE:]
    tail = z(MLA_Q_LORA, MLA_HEADS, LANES - MLA_NOPE - MLA_ROPE)
    qa = jnp.concatenate([nope, rp, tail], axis=-1).reshape(MLA_Q_LORA, _QW)
    qb = jnp.concatenate([jnp.zeros_like(nope), _rot_half_cols(rp), tail], axis=-1).reshape(MLA_Q_LORA, _QW)
    w_uq_ext = jnp.concatenate([qa, qb], axis=1).astype(BF16)

    wkv = w_ukv.reshape(MLA_KV_LORA, MLA_HEADS, MLA_NOPE + MLA_V)
    k_nope, v = wkv[..., :MLA_NOPE], wkv[..., MLA_NOPE:]
    w_k = jnp.concatenate([k_nope, jnp.zeros_like(k_nope)], axis=-1).reshape(MLA_KV_LORA, _QW).astype(BF16)
    v = v.reshape(MLA_KV_LORA, MLA_HEADS // 2, 2, MLA_V)
    zero = jnp.zeros_like(v[:, :, 0])
    w_v = jnp.stack([jnp.concatenate([v[:, :, 0], zero], axis=-1),
                     jnp.concatenate([zero, v[:, :, 1]], axis=-1)], axis=2)
    w_v = w_v.reshape(MLA_KV_LORA, _QW).astype(BF16)
    return w_in_ext, w_uq_ext, w_k, w_v


def _rope_tables(seq):
    half = MLA_ROPE // 2
    inv_freq = ROPE_THETA ** (-jnp.arange(half, dtype=F32) / half)
    ang = jnp.arange(seq).astype(F32)[:, None] * inv_freq[None, :]
    cos, sin = jnp.cos(ang), jnp.sin(ang)
    ones, zeros = jnp.ones((seq, MLA_NOPE), F32), jnp.zeros((seq, MLA_NOPE), F32)
    tail = jnp.zeros((seq, LANES - MLA_NOPE - MLA_ROPE), F32)
    scale = (MLA_NOPE + MLA_ROPE) ** -0.5 * LOG2_E
    cq = jnp.concatenate([ones, cos, cos, tail], axis=1) * scale
    sq = jnp.concatenate([zeros, sin, sin, tail], axis=1) * scale
    ck = jnp.concatenate([zeros, cos, cos, tail], axis=1)
    sk = jnp.concatenate([zeros, sin, sin, tail], axis=1)
    return cq, sq, ck, sk


def kernel(x, mix_pre_g, mix_post_g, ffn_pre_g, ffn_post_g, ab_w_in, mla_q_norm_g, mla_w_uq,
           mla_kv_norm_g, mla_w_ukv, ab_w_out, rel_bias, sb_w_qkv, sb_w_out, ffn_w_gate_up, ffn_w_down):
    b, s, d = x.shape
    n = b * s
    h = x.reshape(n, d)
    tables = _rope_tables(s)
    bias_tables = _moba_bias_tables(rel_bias)
    for layer in range(DEPTH):
        if layer % 2 == 0:
            e = layer // 2
            w_in, w_uq, w_k, w_v = _prep_even_weights(ab_w_in[e], mla_w_uq[e], mla_w_ukv[e])
            q, k, v, moba_qkv = _even_in(h, mix_pre_g[layer], w_in, mla_q_norm_g[e], w_uq,
                                         mla_kv_norm_g[e], w_k, w_v, tables, s)
            mla_out = _mla_attention(q.reshape(b, s, -1), k.reshape(b, s, -1), v.reshape(b, s, -1))
            moba_out = _moba_attention(moba_qkv.reshape(b, s, -1), bias_tables)
            mixed, w_out = [mla_out.reshape(n, -1), moba_out.reshape(n, -1)], ab_w_out[e]
        else:
            o = layer // 2
            width = SB_HEADS * HEAD_DIM
            qkv = _norm_matmul(h, mix_pre_g[layer], sb_w_qkv[o].astype(BF16),
                               width, HEAD_DIM ** -0.5 * LOG2_E)
            mixed, w_out = [_sb_attention(qkv.reshape(b, s, -1)).reshape(n, -1)], sb_w_out[o]
        h = _mix_ffn(mixed, h, w_out.astype(BF16), mix_post_g[layer], ffn_pre_g[layer],
                     ffn_w_gate_up[layer].astype(BF16), ffn_w_down[layer].astype(BF16),
                     ffn_post_g[layer])
    return h.reshape(b, s, d)
```

```python
import functools
import math

import numpy as np
import jax
import jax.numpy as jnp
from jax import lax
from jax.experimental import pallas as pl
from jax.experimental.pallas import tpu as pltpu

D_MODEL = 1024
DEPTH = 4
HEAD_DIM = 64
MLA_HEADS = 8
MLA_NOPE = 64
MLA_ROPE = 32
MLA_V = 64
MLA_Q_LORA = 384
MLA_KV_LORA = 256
ROPE_THETA = 10000.0
MOBA_HEADS = 8
MOBA_BLOCK = 256
MOBA_TOPK = 3
SB_HEADS = 16
NUM_BUCKETS = 32
MAX_DISTANCE = 128
RMS_EPS = 1e-6
D_FF = 2816

LANES = 128
TQ = 256
MLA_ROWS = 512
TM = 512
FF_CHUNK = 256
MASK_PENALTY = -1e30
LOG2_E = 1.0 / math.log(2.0)
VMEM_LIMIT = 48 * 1024 * 1024

F32 = jnp.float32
BF16 = jnp.bfloat16


def _rms(x, g):
    return x * lax.rsqrt(jnp.mean(x * x, axis=-1, keepdims=True) + RMS_EPS) * g


def _dot(a, b):
    return jnp.dot(a, b, preferred_element_type=F32)


def _dot_nt(a, b):
    return lax.dot_general(a, b, (((1,), (1,)), ((), ())), preferred_element_type=F32)


def _params(*semantics):
    return pltpu.CompilerParams(dimension_semantics=semantics, vmem_limit_bytes=VMEM_LIMIT)


def _whole(shape):
    nd = len(shape)
    return pl.BlockSpec(shape, lambda *_: (0,) * nd, pipeline_mode=pl.Buffered(1))


def _norm_matmul_kernel(h_ref, g_ref, w_ref, o_ref, *, q_cols, q_scale):
    u = _rms(h_ref[...], g_ref[...]).astype(BF16)
    for c in range(0, o_ref.shape[-1], 512):
        y = _dot(u, w_ref[:, c:c + 512])
        if c < q_cols:
            y = y * q_scale
        o_ref[:, c:c + 512] = y.astype(o_ref.dtype)


def _norm_matmul(h, g, w, q_cols, q_scale):
    n, d = h.shape
    n_out = w.shape[1]
    return pl.pallas_call(
        functools.partial(_norm_matmul_kernel, q_cols=q_cols, q_scale=q_scale),
        grid=(n // TM,),
        in_specs=[pl.BlockSpec((TM, d), lambda t: (t, 0)), _whole((1, d)), _whole(w.shape)],
        out_specs=pl.BlockSpec((TM, n_out), lambda t: (t, 0)),
        out_shape=jax.ShapeDtypeStruct((n, n_out), BF16),
        compiler_params=_params("arbitrary"),
        name="norm_matmul",
    )(h, g.reshape(1, d), w)


def _head_lanes(shape, hh):
    lane = lax.broadcasted_iota(jnp.int32, shape, 1)
    return (lane >= hh * HEAD_DIM) & (lane < (hh + 1) * HEAD_DIM)


def _spare_lane_base(hh):
    return HEAD_DIM * (1 - hh)


def _store_masked_v(v_ref, vm_scr, ones_lane):
    v = v_ref[0].astype(F32)
    lane = lax.broadcasted_iota(jnp.int32, v.shape, 1)
    for hh in range(2):
        spare = jnp.where(lane == _spare_lane_base(hh), 1.0, 0.0) if ones_lane else 0.0
        vm_scr[hh] = jnp.where(_head_lanes(v.shape, hh), v, spare).astype(BF16)


def _normalized(acc, hh):
    base = _spare_lane_base(hh)
    return jnp.where(_head_lanes(acc.shape, hh), acc / acc[:, base:base + 1], 0.0)


_O_CQ = 0
_O_CKV = MLA_Q_LORA
_O_KR = _O_CKV + MLA_KV_LORA
_O_KRR = _O_KR + LANES
_O_MOBA = _O_KRR + LANES
_W_IN_EXT = _O_MOBA + 3 * MOBA_HEADS * HEAD_DIM
_QW = MLA_HEADS * LANES


def _even_in_kernel(h_ref, g_ref, w_in_ref, gq_ref, w_uq_ref, gkv_ref, w_k_ref, w_v_ref,
                    cq_ref, sq_ref, ck_ref, sk_ref, q_ref, k_ref, v_ref, moba_ref):
    u = _rms(h_ref[...], g_ref[...]).astype(BF16)
    c_q = _dot(u, w_in_ref[:, _O_CQ:_O_CKV])
    c_kv = _dot(u, w_in_ref[:, _O_CKV:_O_KR])
    kr = _dot(u, w_in_ref[:, _O_KR:_O_KRR])
    krr = _dot(u, w_in_ref[:, _O_KRR:_O_MOBA])
    hw = MOBA_HEADS * HEAD_DIM
    for c in range(0, 3 * hw, hw):
        y = _dot(u, w_in_ref[:, _O_MOBA + c:_O_MOBA + c + hw])
        if c == 0:
            y = y * LOG2_E
        moba_ref[:, c:c + hw] = y.astype(BF16)

    cqn = _rms(c_q, gq_ref[...]).astype(BF16)
    qa = _dot(cqn, w_uq_ref[:, :_QW])
    qb = _dot(cqn, w_uq_ref[:, _QW:])
    cq, sq = cq_ref[...], sq_ref[...]
    for h in range(MLA_HEADS):
        s = slice(h * LANES, (h + 1) * LANES)
        q_ref[:, s] = (qa[:, s] * cq + qb[:, s] * sq).astype(BF16)

    ckvn = _rms(c_kv, gkv_ref[...]).astype(BF16)
    kn = _dot(ckvn, w_k_ref[...])
    v = _dot(ckvn, w_v_ref[...])
    lane = lax.broadcasted_iota(jnp.int32, v.shape, 1)
    spare = jnp.where((lane // LANES) % 2 == 0, _spare_lane_base(0), _spare_lane_base(1))
    v_ref[...] = jnp.where(lane % LANES == spare, 1.0, v).astype(BF16)
    k_rope = kr * ck_ref[...] + krr * sk_ref[...]
    for h in range(MLA_HEADS):
        s = slice(h * LANES, (h + 1) * LANES)
        k_ref[:, s] = (kn[:, s] + k_rope).astype(BF16)


def _even_in(h, g, w_in, gq, w_uq, gkv, w_k, w_v, tables, seq):
    n, d = h.shape
    row = lambda t: (t, 0)
    pos = lambda t: (t % (seq // TM), 0)
    tab = pl.BlockSpec((TM, LANES), pos)
    outs = [(_QW, "q"), (_QW, "k"), (_QW, "v"), (3 * MOBA_HEADS * HEAD_DIM, "moba")]
    return pl.pallas_call(
        _even_in_kernel,
        grid=(n // TM,),
        in_specs=[pl.BlockSpec((TM, d), row), _whole((1, d)), _whole(w_in.shape),
                  _whole((1, MLA_Q_LORA)), _whole(w_uq.shape), _whole((1, MLA_KV_LORA)),
                  _whole(w_k.shape), _whole(w_v.shape), tab, tab, tab, tab],
        out_specs=[pl.BlockSpec((TM, w), row) for w, _ in outs],
        out_shape=[jax.ShapeDtypeStruct((n, w), BF16) for w, _ in outs],
        compiler_params=_params("arbitrary"),
        name="even_in",
    )(h, g.reshape(1, d), w_in, gq.reshape(1, -1), w_uq, gkv.reshape(1, -1), w_k, w_v, *tables)


def _mla_kernel(q_ref, k_ref, v_ref, o_ref):
    s_len = q_ref.shape[1]
    tr = MLA_ROWS
    row = lax.broadcasted_iota(jnp.int32, (tr, tr), 0)
    col = lax.broadcasted_iota(jnp.int32, (tr, tr), 1)
    causal = col <= row
    for i in range(s_len // tr):
        rows = slice(i * tr, (i + 1) * tr)
        kv_len = (i + 1) * tr
        out = jnp.zeros((tr, LANES), F32)
        for hh in range(2):
            lanes = slice(hh * LANES, (hh + 1) * LANES)
            s = _dot_nt(q_ref[0, rows, lanes], k_ref[0, :kv_len, lanes])
            diag = jnp.where(causal, s[:, i * tr:], -jnp.inf)
            s = diag if i == 0 else jnp.concatenate([s[:, :i * tr], diag], axis=1)
            p = jnp.exp2(s - jnp.max(s, axis=-1, keepdims=True))
            out = out + _normalized(_dot(p.astype(BF16), v_ref[0, :kv_len, lanes]), hh)
        o_ref[0, rows, :] = out.astype(o_ref.dtype)


def _mla_attention(q, k, v):
    b, s, _ = q.shape
    pairs = MLA_HEADS // 2
    qkv = pl.BlockSpec((1, s, 2 * LANES), lambda bi, p: (bi, 0, p))
    return pl.pallas_call(
        _mla_kernel,
        grid=(b, pairs),
        in_specs=[qkv, qkv, qkv],
        out_specs=pl.BlockSpec((1, s, LANES), lambda bi, p: (bi, 0, p)),
        out_shape=jax.ShapeDtypeStruct((b, s, MLA_HEADS * MLA_V), BF16),
        compiler_params=_params("arbitrary", "arbitrary"),
        name="mla_attention",
    )(q, k, v)


_SOFTPLUS2_CLAMP = 64.0


def _softplus2(z):
    return jnp.maximum(z, jnp.log2(1.0 + jnp.exp2(jnp.minimum(z, _SOFTPLUS2_CLAMP))))


SB_UNDERFLOW = 160.0
SB_NEAR_BLOCKS = 2


def _sb_kernel(q_ref, k_ref, v_ref, o_ref, qm_scr, vm_scr, acc_scr, carry_scr, limit_scr):
    s_len = q_ref.shape[1]
    n_blk = s_len // TQ
    q, k, v = q_ref[0], k_ref[0], v_ref[0]
    feat = lax.broadcasted_iota(jnp.int32, (LANES, LANES), 0) // HEAD_DIM
    lane = lax.broadcasted_iota(jnp.int32, (LANES, LANES), 1) // HEAD_DIM
    same_head = jnp.where(feat == lane, 1.0, 0.0).astype(BF16)
    q_max2 = jnp.max(_dot(q * q, same_head), axis=0, keepdims=True)
    k_max2 = jnp.max(_dot(k * k, same_head), axis=0, keepdims=True)
    z_bound = jnp.sqrt(q_max2 * k_max2) * 1.02
    for hh in range(2):
        in_head = _head_lanes((1, LANES), hh)
        lane_mask = jnp.where(in_head, 1.0, 0.0).astype(BF16)
        qm_scr[hh] = q * lane_mask
        vm_scr[hh] = v * lane_mask
        bound = jnp.max(jnp.where(in_head, z_bound, 0.0), axis=1, keepdims=True)
        limit_scr[hh] = jnp.broadcast_to(bound + SB_UNDERFLOW, limit_scr.shape[1:])
    acc_scr[...] = jnp.zeros(acc_scr.shape, F32)

    row = lax.broadcasted_iota(jnp.int32, (TQ, TQ), 0)
    col = lax.broadcasted_iota(jnp.int32, (TQ, TQ), 1)
    strict = col < row
    tri = jnp.where(row >= col, 1.0, 0.0).astype(BF16)

    def head(hh, _):
        def tile_rows(j, r0, r1):
            keys = slice(j * TQ, (j + 1) * TQ)
            rows = slice(r0 * TQ, r1 * TQ)
            z = _dot_nt(qm_scr[hh, rows, :], k_ref[0, keys, :])
            sp = _softplus2(z)
            if r0 == j:
                sp_diag = jnp.where(strict, sp[:TQ], 0.0)
                sp = sp_diag if r1 == r0 + 1 else jnp.concatenate([sp_diag, sp[TQ:]], axis=0)
            c = _dot(sp.astype(BF16), tri)
            tot = jnp.broadcast_to(c[:, 0:1], (c.shape[0], LANES))
            first = TQ if r0 == j else 0
            w = []
            if r0 == j:
                w.append(jnp.where(strict, jnp.exp2(z[:TQ] - c[:TQ]), 0.0))
                carry_scr[keys, :] = tot[:TQ]
            if r1 * TQ - r0 * TQ > first:
                below = slice(r0 * TQ + first, r1 * TQ)
                carry = carry_scr[below, :]
                w.append(jnp.exp2(z[first:] - c[first:] - jnp.concatenate([carry, carry], axis=1)))
                carry_scr[below, :] = carry + tot[first:]
            w = (w[0] if len(w) == 1 else jnp.concatenate(w, axis=0)).astype(BF16)
            acc_scr[rows, :] += _dot(w, vm_scr[hh, keys, :])

        for j in range(n_blk - 1, -1, -1):
            tile_rows(j, j, min(j + SB_NEAR_BLOCKS, n_blk))

        def live(i):
            least = jnp.min(carry_scr[i * TQ:(i + 1) * TQ, :], axis=0, keepdims=True)
            return jnp.min(least - limit_scr[hh, 0:1, :]) < 0.0

        def farther(i, dist, is_live):
            @pl.when(is_live)
            def _():
                tile_rows(i - dist, i, i + 1)
                if dist < i:
                    farther(i, dist + 1, live(i))

        first = range(SB_NEAR_BLOCKS, n_blk)
        for i, is_live in [(i, live(i)) for i in first]:
            farther(i, SB_NEAR_BLOCKS, is_live)
        return 0

    lax.fori_loop(0, 2, head, 0)
    o_ref[0] = acc_scr[...].astype(o_ref.dtype)


def _sb_attention(qkv):
    b, s, _ = qkv.shape
    pairs = SB_HEADS // 2
    return pl.pallas_call(
        _sb_kernel,
        grid=(b, pairs),
        in_specs=[pl.BlockSpec((1, s, LANES), lambda bi, p: (bi, 0, p)),
                  pl.BlockSpec((1, s, LANES), lambda bi, p: (bi, 0, pairs + p)),
                  pl.BlockSpec((1, s, LANES), lambda bi, p: (bi, 0, 2 * pairs + p))],
        out_specs=pl.BlockSpec((1, s, LANES), lambda bi, p: (bi, 0, p)),
        out_shape=jax.ShapeDtypeStruct((b, s, SB_HEADS * HEAD_DIM), BF16),
        scratch_shapes=[pltpu.VMEM((2, s, LANES), BF16), pltpu.VMEM((2, s, LANES), BF16),
                        pltpu.VMEM((s, LANES), F32), pltpu.VMEM((s, LANES), F32),
                        pltpu.VMEM((2, 8, LANES), F32)],
        compiler_params=_params("arbitrary", "arbitrary"),
        name="sb_attention",
    )(qkv, qkv, qkv)


def _t5_bucket_np(rel):
    max_exact = NUM_BUCKETS // 2
    rel = np.maximum(rel, 0)
    large = max_exact + (np.log(np.maximum(rel, max_exact) / max_exact)
                         / math.log(MAX_DISTANCE / max_exact) * (NUM_BUCKETS - max_exact)).astype(np.int64)
    return np.where(rel < max_exact, rel, np.minimum(large, NUM_BUCKETS - 1)).astype(np.int32)


def _bias_kernel(bucket_ref, rel_bias_ref, o_ref):
    h = pl.program_id(0)
    for plane in range(2):
        bk = bucket_ref[plane]
        acc = jnp.full(bk.shape, -jnp.inf, F32)
        for b in range(NUM_BUCKETS):
            acc = jnp.where(bk == b, rel_bias_ref[b, h] * LOG2_E, acc)
        o_ref[0, plane] = acc
    o_ref[0, 2] = jnp.full((TQ, TQ), rel_bias_ref[NUM_BUCKETS - 1, h] * LOG2_E, F32)


def _moba_bias_tables(rel_bias):
    off = np.arange(TQ)
    rel_own = off[:, None] - off[None, :]
    own = np.where(rel_own >= 0, _t5_bucket_np(rel_own), -1)
    adj = _t5_bucket_np(rel_own + MOBA_BLOCK)
    buckets = jnp.asarray(np.stack([own, adj]).astype(np.int32))
    return pl.pallas_call(
        _bias_kernel,
        grid=(MOBA_HEADS,),
        in_specs=[_whole(buckets.shape), pl.BlockSpec(memory_space=pltpu.SMEM)],
        out_specs=pl.BlockSpec((1, 3, TQ, TQ), lambda h: (h, 0, 0, 0)),
        out_shape=jax.ShapeDtypeStruct((MOBA_HEADS, 3, TQ, TQ), F32),
        compiler_params=_params("arbitrary"),
        name="moba_bias_tables",
    )(buckets, rel_bias)


_N_BLK_LANES = 8


def _moba_constants(nb):
    assert nb <= _N_BLK_LANES
    diff = np.zeros((LANES, LANES), np.float32)
    count = np.zeros((LANES, LANES), np.float32)
    for hh in range(2):
        for a in range(nb):
            for b in range(nb):
                if a != b:
                    pair = hh * HEAD_DIM + a * 8 + b
                    diff[pair, a] += 1.0
                    diff[pair, b] -= 1.0
                    count[pair, _spare_lane_base(hh) + b] = 1.0
                    count[pair, _spare_lane_base(hh) + _N_BLK_LANES + b] = 1.0
    return jnp.asarray(diff, BF16), jnp.asarray(count, BF16)


_T_KIND, _T_FAR_HI, _T_FAR_LO, _T_FUTURE, _T_UNSEL, _N_HEAD_TABLES = 0, 1, 2, 3, 4, 5


def _moba_fill_tables(head_tab, gate_tab, s_len):
    lane = lax.broadcasted_iota(jnp.int32, (s_len, LANES), 1)
    row_blk = lax.broadcasted_iota(jnp.int32, (s_len, LANES), 0) // MOBA_BLOCK
    for hh in range(2):
        base = _spare_lane_base(hh)
        in_hi = (lane >= base) & (lane < base + _N_BLK_LANES)
        in_lo = (lane >= base + _N_BLK_LANES) & (lane < base + 2 * _N_BLK_LANES)
        blk = jnp.where(in_lo, lane - base - _N_BLK_LANES, lane - base)
        far = blk < row_blk - 1
        store = lambda t, x: head_tab.__setitem__((hh, t), x.astype(BF16))
        store(_T_KIND, jnp.where(in_hi | in_lo, jnp.where(blk == row_blk, 1.0, 0.0), -1.0))
        store(_T_FAR_HI, jnp.where(in_hi & far, 1.0, 0.0))
        store(_T_FAR_LO, jnp.where(in_lo & far, 1.0, 0.0))
        store(_T_FUTURE, jnp.where(in_hi & (blk > row_blk), MASK_PENALTY, 0.0))
        store(_T_UNSEL, jnp.where(in_hi & (blk != row_blk), MASK_PENALTY, 0.0))
    pair_a, pair_b = (lane % HEAD_DIM) // 8, lane % 8
    valid = pair_a < row_blk
    gate_tab[0] = jnp.where(valid, 1.0, 0.0)
    gate_tab[1] = jnp.where(valid & (pair_a < pair_b), 1.0, 0.0)


def _moba_kernel(q_ref, k_ref, v_ref, bias_ref, diff_ref, count_ref, o_ref,
                 qaug_scr, kaug_scr, vm_scr, head_tab, gate_tab, *, nb):
    s_len = q_ref.shape[1]

    @pl.when((pl.program_id(0) == 0) & (pl.program_id(1) == 0))
    def _():
        _moba_fill_tables(head_tab, gate_tab, s_len)

    q, k, v = q_ref[0], k_ref[0], v_ref[0]
    kbar = jnp.mean(k.astype(F32).reshape(nb, MOBA_BLOCK, LANES), axis=1)
    kbar = jnp.concatenate([kbar, jnp.zeros((LANES - nb, LANES), F32)], axis=0)
    p0 = kbar.astype(BF16)
    r1 = kbar - p0.astype(F32)
    p1 = r1.astype(BF16)
    p2 = (r1 - p1.astype(F32)).astype(BF16)
    d = diff_ref[...]
    kdiff = _dot(d, p0) + _dot(d, p1) + _dot(d, p2)
    pair_head = lax.broadcasted_iota(jnp.int32, (LANES, LANES), 0) // HEAD_DIM
    feat_head = lax.broadcasted_iota(jnp.int32, (LANES, LANES), 1) // HEAD_DIM
    kdiff = jnp.where(pair_head == feat_head, kdiff, 0.0).astype(BF16)

    g = _dot_nt(q, kdiff)
    beats = jnp.where(g > 0.0, gate_tab[0], jnp.where(g == 0.0, gate_tab[1], 0.0))
    rank = _dot(beats.astype(BF16), count_ref[...]).astype(BF16)
    top = rank < float(MOBA_TOPK)

    lane = lax.broadcasted_iota(jnp.int32, (1, LANES), 1)
    for hh in range(2):
        lane_mask = jnp.where(_head_lanes((1, LANES), hh), 1.0, 0.0).astype(BF16)
        ones_lane = jnp.where(lane == _spare_lane_base(hh), 1.0, 0.0).astype(BF16)
        vm_scr[hh] = v * lane_mask + ones_lane
        kind = head_tab[hh, _T_KIND]
        spare = kind >= 0.0
        kaug_scr[hh] = jnp.where(spare, kind, k)
        far = bias_ref[hh, 2, 0:1, 0:LANES]
        far_hi = far.astype(BF16)
        far_lo = (far - far_hi.astype(F32)).astype(BF16)
        chosen_val = head_tab[hh, _T_FAR_HI] * far_hi + head_tab[hh, _T_FAR_LO] * far_lo + head_tab[hh, _T_FUTURE]
        qaug_scr[hh] = jnp.where(spare, jnp.where(top, chosen_val, head_tab[hh, _T_UNSEL]), q * lane_mask)

    for i in range(nb):
        rows = slice(i * TQ, (i + 1) * TQ)
        kv_len = (i + 1) * TQ
        out = jnp.zeros((TQ, LANES), F32)
        for hh in range(2):
            s = _dot_nt(qaug_scr[hh, rows, :], kaug_scr[hh, :kv_len, :])
            near = [s[:, j * TQ:(j + 1) * TQ] + bias_ref[hh, i - j] for j in range(max(i - 1, 0), i + 1)]
            s = jnp.concatenate(([s[:, :(i - 1) * TQ]] if i >= 2 else []) + near, axis=1)
            p = jnp.exp2(s - jnp.max(s, axis=-1, keepdims=True))
            out = out + _normalized(_dot(p.astype(BF16), vm_scr[hh, :kv_len, :]), hh)
        o_ref[0, rows, :] = out.astype(o_ref.dtype)


def _moba_attention(qkv, bias_tables):
    b, s, _ = qkv.shape
    nb = s // MOBA_BLOCK
    pairs = MOBA_HEADS // 2
    diff, count = _moba_constants(nb)
    return pl.pallas_call(
        functools.partial(_moba_kernel, nb=nb),
        grid=(b, pairs),
        in_specs=[pl.BlockSpec((1, s, LANES), lambda bi, p: (bi, 0, p)),
                  pl.BlockSpec((1, s, LANES), lambda bi, p: (bi, 0, pairs + p)),
                  pl.BlockSpec((1, s, LANES), lambda bi, p: (bi, 0, 2 * pairs + p)),
                  pl.BlockSpec((2, 3, TQ, TQ), lambda bi, p: (p, 0, 0, 0)),
                  _whole(diff.shape), _whole(count.shape)],
        out_specs=pl.BlockSpec((1, s, LANES), lambda bi, p: (bi, 0, p)),
        out_shape=jax.ShapeDtypeStruct((b, s, MOBA_HEADS * HEAD_DIM), BF16),
        scratch_shapes=[pltpu.VMEM((2, s, LANES), BF16)] * 3
        + [pltpu.VMEM((2, _N_HEAD_TABLES, s, LANES), BF16), pltpu.VMEM((2, s, LANES), F32)],
        compiler_params=_params("arbitrary", "arbitrary"),
        name="moba_attention",
    )(qkv, qkv, qkv, bias_tables, diff, count)


def _mix_ffn_kernel(*refs, n_in):
    a_refs = refs[:n_in]
    h_ref, w_out_ref, gmix_ref, gpre_ref, wgu_ref, wd_ref, gpost_ref, o_ref, act_scr = refs[n_in:]
    mixed, k0 = None, 0
    for a_ref in a_refs:
        k1 = k0 + a_ref.shape[1]
        part = _dot(a_ref[...], w_out_ref[k0:k1, :])
        mixed, k0 = part if mixed is None else mixed + part, k1
    x = h_ref[...] + _rms(mixed, gmix_ref[...])
    u = _rms(x, gpre_ref[...]).astype(BF16)
    for c in range(0, D_FF, FF_CHUNK):
        gate = _dot(u, wgu_ref[:, c:c + FF_CHUNK])
        up = _dot(u, wgu_ref[:, D_FF + c:D_FF + c + FF_CHUNK])
        act_scr[:, c:c + FF_CHUNK] = (gate * jax.nn.sigmoid(gate) * up).astype(BF16)
    f = _dot(act_scr[...], wd_ref[...])
    o_ref[...] = x + _rms(f, gpost_ref[...])


def _mix_ffn(acts, h, w_out, gmix, gpre, wgu, wd, gpost):
    n, d = h.shape
    row = lambda t: (t, 0)
    vec = _whole((1, d))
    return pl.pallas_call(
        functools.partial(_mix_ffn_kernel, n_in=len(acts)),
        grid=(n // TM,),
        in_specs=([pl.BlockSpec((TM, a.shape[1]), row) for a in acts]
                  + [pl.BlockSpec((TM, d), row), _whole(w_out.shape), vec, vec,
                     _whole(wgu.shape), _whole(wd.shape), vec]),
        out_specs=pl.BlockSpec((TM, d), row),
        out_shape=jax.ShapeDtypeStruct((n, d), F32),
        scratch_shapes=[pltpu.VMEM((TM, D_FF), BF16)],
        compiler_params=_params("arbitrary"),
        name="mix_ffn",
    )(*acts, h, w_out, gmix.reshape(1, d), gpre.reshape(1, d), wgu, wd, gpost.reshape(1, d))


def _rot_half_cols(w):
    half = w.shape[-1] // 2
    return jnp.concatenate([-w[..., half:], w[..., :half]], axis=-1)


def _prep_even_weights(w_in, w_uq, w_ukv):
    d = w_in.shape[0]
    o1, o2, o3 = MLA_Q_LORA, MLA_Q_LORA + MLA_KV_LORA, MLA_Q_LORA + MLA_KV_LORA + MLA_ROPE
    hw = MOBA_HEADS * HEAD_DIM
    z = lambda *shape: jnp.zeros(shape, F32)
    kr = w_in[:, o2:o3]
    w_in_ext = jnp.concatenate(
        [w_in[:, :o2],
         z(d, MLA_NOPE), kr, z(d, LANES - MLA_NOPE - MLA_ROPE),
         z(d, MLA_NOPE), _rot_half_cols(kr), z(d, LANES - MLA_NOPE - MLA_ROPE),
         w_in[:, o3:o3 + hw] * (HEAD_DIM ** -0.5), w_in[:, o3 + hw:]], axis=1).astype(BF16)

    wq = w_uq.reshape(MLA_Q_LORA, MLA_HEADS, MLA_NOPE + MLA_ROPE)
    nope, rp = wq[..., :MLA_NOPE], wq[..., MLA_NOPE:]
    tail = z(MLA_Q_LORA, MLA_HEADS, LANES - MLA_NOPE - MLA_ROPE)
    qa = jnp.concatenate([nope, rp, tail], axis=-1).reshape(MLA_Q_LORA, _QW)
    qb = jnp.concatenate([jnp.zeros_like(nope), _rot_half_cols(rp), tail], axis=-1).reshape(MLA_Q_LORA, _QW)
    w_uq_ext = jnp.concatenate([qa, qb], axis=1).astype(BF16)

    wkv = w_ukv.reshape(MLA_KV_LORA, MLA_HEADS, MLA_NOPE + MLA_V)
    k_nope, v = wkv[..., :MLA_NOPE], wkv[..., MLA_NOPE:]
    w_k = jnp.concatenate([k_nope, jnp.zeros_like(k_nope)], axis=-1).reshape(MLA_KV_LORA, _QW).astype(BF16)
    v = v.reshape(MLA_KV_LORA, MLA_HEADS // 2, 2, MLA_V)
    zero = jnp.zeros_like(v[:, :, 0])
    w_v = jnp.stack([jnp.concatenate([v[:, :, 0], zero], axis=-1),
                     jnp.concatenate([zero, v[:, :, 1]], axis=-1)], axis=2)
    w_v = w_v.reshape(MLA_KV_LORA, _QW).astype(BF16)
    return w_in_ext, w_uq_ext, w_k, w_v


def _rope_tables(seq):
    half = MLA_ROPE // 2
    inv_freq = ROPE_THETA ** (-jnp.arange(half, dtype=F32) / half)
    ang = jnp.arange(seq).astype(F32)[:, None] * inv_freq[None, :]
    cos, sin = jnp.cos(ang), jnp.sin(ang)
    ones, zeros = jnp.ones((seq, MLA_NOPE), F32), jnp.zeros((seq, MLA_NOPE), F32)
    tail = jnp.zeros((seq, LANES - MLA_NOPE - MLA_ROPE), F32)
    scale = (MLA_NOPE + MLA_ROPE) ** -0.5 * LOG2_E
    cq = jnp.concatenate([ones, cos, cos, tail], axis=1) * scale
    sq = jnp.concatenate([zeros, sin, sin, tail], axis=1) * scale
    ck = jnp.concatenate([zeros, cos, cos, tail], axis=1)
    sk = jnp.concatenate([zeros, sin, sin, tail], axis=1)
    return cq, sq, ck, sk


def kernel(x, mix_pre_g, mix_post_g, ffn_pre_g, ffn_post_g, ab_w_in, mla_q_norm_g, mla_w_uq,
           mla_kv_norm_g, mla_w_ukv, ab_w_out, rel_bias, sb_w_qkv, sb_w_out, ffn_w_gate_up, ffn_w_down):
    b, s, d = x.shape
    n = b * s
    h = x.reshape(n, d)
    tables = _rope_tables(s)
    bias_tables = _moba_bias_tables(rel_bias)
    for layer in range(DEPTH):
        if layer % 2 == 0:
            e = layer // 2
            w_in, w_uq, w_k, w_v = _prep_even_weights(ab_w_in[e], mla_w_uq[e], mla_w_ukv[e])
            q, k, v, moba_qkv = _even_in(h, mix_pre_g[layer], w_in, mla_q_norm_g[e], w_uq,
                                         mla_kv_norm_g[e], w_k, w_v, tables, s)
            mla_out = _mla_attention(q.reshape(b, s, -1), k.reshape(b, s, -1), v.reshape(b, s, -1))
            moba_out = _moba_attention(moba_qkv.reshape(b, s, -1), bias_tables)
            mixed, w_out = [mla_out.reshape(n, -1), moba_out.reshape(n, -1)], ab_w_out[e]
        else:
            o = layer // 2
            width = SB_HEADS * HEAD_DIM
            qkv = _norm_matmul(h, mix_pre_g[layer], sb_w_qkv[o].astype(BF16),
                               width, HEAD_DIM ** -0.5 * LOG2_E)
            mixed, w_out = [_sb_attention(qkv.reshape(b, s, -1)).reshape(n, -1)], sb_w_out[o]
        h = _mix_ffn(mixed, h, w_out.astype(BF16), mix_post_g[layer], ffn_pre_g[layer],
                     ffn_w_gate_up[layer].astype(BF16), ffn_w_down[layer].astype(BF16),
                     ffn_post_g[layer])
    return h.reshape(b, s, d)
```

```python
import functools
import math

import numpy as np
import jax
import jax.numpy as jnp
from jax import lax
from jax.experimental import pallas as pl
from jax.experimental.pallas import tpu as pltpu

D_MODEL = 1024
DEPTH = 4
HEAD_DIM = 64
MLA_HEADS = 8
MLA_NOPE = 64
MLA_ROPE = 32
MLA_V = 64
MLA_Q_LORA = 384
MLA_KV_LORA = 256
ROPE_THETA = 10000.0
MOBA_HEADS = 8
MOBA_BLOCK = 256
MOBA_TOPK = 3
SB_HEADS = 16
NUM_BUCKETS = 32
MAX_DISTANCE = 128
RMS_EPS = 1e-6
D_FF = 2816

LANES = 128
TQ = 256
MLA_ROWS = 512
TM = 512
FF_CHUNK = 256
MASK_PENALTY = -1e30
LOG2_E = 1.0 / math.log(2.0)
VMEM_LIMIT = 48 * 1024 * 1024

F32 = jnp.float32
BF16 = jnp.bfloat16


def _rms(x, g):
    return x * lax.rsqrt(jnp.mean(x * x, axis=-1, keepdims=True) + RMS_EPS) * g


def _dot(a, b):
    return jnp.dot(a, b, preferred_element_type=F32)


def _dot_nt(a, b):
    return lax.dot_general(a, b, (((1,), (1,)), ((), ())), preferred_element_type=F32)


def _params(*semantics):
    return pltpu.CompilerParams(dimension_semantics=semantics, vmem_limit_bytes=VMEM_LIMIT)


def _whole(shape):
    nd = len(shape)
    return pl.BlockSpec(shape, lambda *_: (0,) * nd, pipeline_mode=pl.Buffered(1))


def _norm_matmul_kernel(h_ref, g_ref, w_ref, o_ref, *, q_cols, q_scale):
    u = _rms(h_ref[...], g_ref[...]).astype(BF16)
    for c in range(0, o_ref.shape[-1], 512):
        y = _dot(u, w_ref[:, c:c + 512])
        if c < q_cols:
            y = y * q_scale
        o_ref[:, c:c + 512] = y.astype(o_ref.dtype)


def _norm_matmul(h, g, w, q_cols, q_scale):
    n, d = h.shape
    n_out = w.shape[1]
    return pl.pallas_call(
        functools.partial(_norm_matmul_kernel, q_cols=q_cols, q_scale=q_scale),
        grid=(n // TM,),
        in_specs=[pl.BlockSpec((TM, d), lambda t: (t, 0)), _whole((1, d)), _whole(w.shape)],
        out_specs=pl.BlockSpec((TM, n_out), lambda t: (t, 0)),
        out_shape=jax.ShapeDtypeStruct((n, n_out), BF16),
        compiler_params=_params("arbitrary"),
        name="norm_matmul",
    )(h, g.reshape(1, d), w)


def _head_lanes(shape, hh):
    lane = lax.broadcasted_iota(jnp.int32, shape, 1)
    return (lane >= hh * HEAD_DIM) & (lane < (hh + 1) * HEAD_DIM)


def _spare_lane_base(hh):
    return HEAD_DIM * (1 - hh)


def _store_masked_v(v_ref, vm_scr, ones_lane):
    v = v_ref[0].astype(F32)
    lane = lax.broadcasted_iota(jnp.int32, v.shape, 1)
    for hh in range(2):
        spare = jnp.where(lane == _spare_lane_base(hh), 1.0, 0.0) if ones_lane else 0.0
        vm_scr[hh] = jnp.where(_head_lanes(v.shape, hh), v, spare).astype(BF16)


def _normalized(acc, hh):
    base = _spare_lane_base(hh)
    return jnp.where(_head_lanes(acc.shape, hh), acc / acc[:, base:base + 1], 0.0)


_O_CQ = 0
_O_CKV = MLA_Q_LORA
_O_KR = _O_CKV + MLA_KV_LORA
_O_KRR = _O_KR + LANES
_O_MOBA = _O_KRR + LANES
_W_IN_EXT = _O_MOBA + 3 * MOBA_HEADS * HEAD_DIM
_QW = MLA_HEADS * LANES


def _even_in_kernel(h_ref, g_ref, w_in_ref, gq_ref, w_uq_ref, gkv_ref, w_k_ref, w_v_ref,
                    cq_ref, sq_ref, ck_ref, sk_ref, q_ref, k_ref, v_ref, moba_ref):
    u = _rms(h_ref[...], g_ref[...]).astype(BF16)
    c_q = _dot(u, w_in_ref[:, _O_CQ:_O_CKV])
    c_kv = _dot(u, w_in_ref[:, _O_CKV:_O_KR])
    kr = _dot(u, w_in_ref[:, _O_KR:_O_KRR])
    krr = _dot(u, w_in_ref[:, _O_KRR:_O_MOBA])
    hw = MOBA_HEADS * HEAD_DIM
    for c in range(0, 3 * hw, hw):
        y = _dot(u, w_in_ref[:, _O_MOBA + c:_O_MOBA + c + hw])
        if c == 0:
            y = y * LOG2_E
        moba_ref[:, c:c + hw] = y.astype(BF16)

    cqn = _rms(c_q, gq_ref[...]).astype(BF16)
    qa = _dot(cqn, w_uq_ref[:, :_QW])
    qb = _dot(cqn, w_uq_ref[:, _QW:])
    cq, sq = cq_ref[...], sq_ref[...]
    for h in range(MLA_HEADS):
        s = slice(h * LANES, (h + 1) * LANES)
        q_ref[:, s] = (qa[:, s] * cq + qb[:, s] * sq).astype(BF16)

    ckvn = _rms(c_kv, gkv_ref[...]).astype(BF16)
    kn = _dot(ckvn, w_k_ref[...])
    v = _dot(ckvn, w_v_ref[...])
    lane = lax.broadcasted_iota(jnp.int32, v.shape, 1)
    spare = jnp.where((lane // LANES) % 2 == 0, _spare_lane_base(0), _spare_lane_base(1))
    v_ref[...] = jnp.where(lane % LANES == spare, 1.0, v).astype(BF16)
    k_rope = kr * ck_ref[...] + krr * sk_ref[...]
    for h in range(MLA_HEADS):
        s = slice(h * LANES, (h + 1) * LANES)
        k_ref[:, s] = (kn[:, s] + k_rope).astype(BF16)


def _even_in(h, g, w_in, gq, w_uq, gkv, w_k, w_v, tables, seq):
    n, d = h.shape
    row = lambda t: (t, 0)
    pos = lambda t: (t % (seq // TM), 0)
    tab = pl.BlockSpec((TM, LANES), pos)
    outs = [(_QW, "q"), (_QW, "k"), (_QW, "v"), (3 * MOBA_HEADS * HEAD_DIM, "moba")]
    return pl.pallas_call(
        _even_in_kernel,
        grid=(n // TM,),
        in_specs=[pl.BlockSpec((TM, d), row), _whole((1, d)), _whole(w_in.shape),
                  _whole((1, MLA_Q_LORA)), _whole(w_uq.shape), _whole((1, MLA_KV_LORA)),
                  _whole(w_k.shape), _whole(w_v.shape), tab, tab, tab, tab],
        out_specs=[pl.BlockSpec((TM, w), row) for w, _ in outs],
        out_shape=[jax.ShapeDtypeStruct((n, w), BF16) for w, _ in outs],
        compiler_params=_params("arbitrary"),
        name="even_in",
    )(h, g.reshape(1, d), w_in, gq.reshape(1, -1), w_uq, gkv.reshape(1, -1), w_k, w_v, *tables)


def _mla_kernel(q_ref, k_ref, v_ref, o_ref):
    s_len = q_ref.shape[1]
    tr = MLA_ROWS
    row = lax.broadcasted_iota(jnp.int32, (tr, tr), 0)
    col = lax.broadcasted_iota(jnp.int32, (tr, tr), 1)
    causal = col <= row
    for i in range(s_len // tr):
        rows = slice(i * tr, (i + 1) * tr)
        kv_len = (i + 1) * tr
        out = jnp.zeros((tr, LANES), F32)
        for hh in range(2):
            lanes = slice(hh * LANES, (hh + 1) * LANES)
            s = _dot_nt(q_ref[0, rows, lanes], k_ref[0, :kv_len, lanes])
            diag = jnp.where(causal, s[:, i * tr:], -jnp.inf)
            s = diag if i == 0 else jnp.concatenate([s[:, :i * tr], diag], axis=1)
            p = jnp.exp2(s - jnp.max(s, axis=-1, keepdims=True))
            out = out + _normalized(_dot(p.astype(BF16), v_ref[0, :kv_len, lanes]), hh)
        o_ref[0, rows, :] = out.astype(o_ref.dtype)


def _mla_attention(q, k, v):
    b, s, _ = q.shape
    pairs = MLA_HEADS // 2
    qkv = pl.BlockSpec((1, s, 2 * LANES), lambda bi, p: (bi, 0, p))
    return pl.pallas_call(
        _mla_kernel,
        grid=(b, pairs),
        in_specs=[qkv, qkv, qkv],
        out_specs=pl.BlockSpec((1, s, LANES), lambda bi, p: (bi, 0, p)),
        out_shape=jax.ShapeDtypeStruct((b, s, MLA_HEADS * MLA_V), BF16),
        compiler_params=_params("arbitrary", "arbitrary"),
        name="mla_attention",
    )(q, k, v)


_SOFTPLUS2_CLAMP = 64.0


def _softplus2(z):
    return jnp.maximum(z, jnp.log2(1.0 + jnp.exp2(jnp.minimum(z, _SOFTPLUS2_CLAMP))))


SB_UNDERFLOW = 160.0
SB_NEAR_BLOCKS = 2
SB_PROBE_STEPS = 4


def _sb_kernel(q_ref, k_ref, v_ref, o_ref, qm_scr, vm_scr, acc_scr, carry_scr, limit_scr, plan_scr):
    s_len = q_ref.shape[1]
    n_blk = s_len // TQ
    q, k, v = q_ref[0], k_ref[0], v_ref[0]
    feat = lax.broadcasted_iota(jnp.int32, (LANES, LANES), 0) // HEAD_DIM
    lane = lax.broadcasted_iota(jnp.int32, (LANES, LANES), 1) // HEAD_DIM
    same_head = jnp.where(feat == lane, 1.0, 0.0).astype(BF16)
    q_max2 = jnp.max(_dot(q * q, same_head), axis=0, keepdims=True)
    k_max2 = jnp.max(_dot(k * k, same_head), axis=0, keepdims=True)
    z_bound = jnp.sqrt(q_max2 * k_max2) * 1.02
    for hh in range(2):
        in_head = _head_lanes((1, LANES), hh)
        lane_mask = jnp.where(in_head, 1.0, 0.0).astype(BF16)
        qm_scr[hh] = q * lane_mask
        vm_scr[hh] = v * lane_mask
        bound = jnp.max(jnp.where(in_head, z_bound, 0.0), axis=1, keepdims=True)
        limit_scr[hh] = jnp.broadcast_to(bound + SB_UNDERFLOW, limit_scr.shape[1:])
    acc_scr[...] = jnp.zeros(acc_scr.shape, F32)

    row = lax.broadcasted_iota(jnp.int32, (TQ, TQ), 0)
    col = lax.broadcasted_iota(jnp.int32, (TQ, TQ), 1)
    strict = col < row
    tri = jnp.where(row >= col, 1.0, 0.0).astype(BF16)

    def head(hh, _):
        def tile_rows(j, r0, r1):
            keys = slice(j * TQ, (j + 1) * TQ)
            rows = slice(r0 * TQ, r1 * TQ)
            z = _dot_nt(qm_scr[hh, rows, :], k_ref[0, keys, :])
            sp = _softplus2(z)
            if r0 == j:
                sp_diag = jnp.where(strict, sp[:TQ], 0.0)
                sp = sp_diag if r1 == r0 + 1 else jnp.concatenate([sp_diag, sp[TQ:]], axis=0)
            c = _dot(sp.astype(BF16), tri)
            tot = jnp.broadcast_to(c[:, 0:1], (c.shape[0], LANES))
            first = TQ if r0 == j else 0
            w = []
            if r0 == j:
                w.append(jnp.where(strict, jnp.exp2(z[:TQ] - c[:TQ]), 0.0))
                carry_scr[keys, :] = tot[:TQ]
            if r1 * TQ - r0 * TQ > first:
                below = slice(r0 * TQ + first, r1 * TQ)
                carry = carry_scr[below, :]
                w.append(jnp.exp2(z[first:] - c[first:] - jnp.concatenate([carry, carry], axis=1)))
                carry_scr[below, :] = carry + tot[first:]
            w = (w[0] if len(w) == 1 else jnp.concatenate(w, axis=0)).astype(BF16)
            acc_scr[rows, :] += _dot(w, vm_scr[hh, keys, :])

        def near_tiles(n_near):
            for j in range(n_blk - 1, -1, -1):
                tile_rows(j, j, min(j + n_near, n_blk))

        def live(r0, r1):
            least = jnp.min(carry_scr[r0 * TQ:r1 * TQ, :], axis=0, keepdims=True)
            return jnp.min(least - limit_scr[hh, 0:1, :]) < 0.0

        def diagonal(dist):
            for i in range(dist, n_blk):
                tile_rows(i - dist, i, i + 1)

        wide = plan_scr[0] == 1

        @pl.when(jnp.logical_not(wide))
        def _():
            near_tiles(SB_NEAR_BLOCKS)
            flags = [live(i, i + 1) for i in range(SB_NEAR_BLOCKS, n_blk)]
            n_live = sum(f.astype(jnp.int32) for f in flags)
            plan_scr[1] += (n_live >= 2).astype(jnp.int32)

            @pl.when(n_live >= 2)
            def _():
                diagonal(SB_NEAR_BLOCKS)

            @pl.when(n_live == 1)
            def _():
                for i, f in zip(range(SB_NEAR_BLOCKS, n_blk), flags):
                    pl.when(f)(functools.partial(tile_rows, i - SB_NEAR_BLOCKS, i, i + 1))

        @pl.when(wide)
        def _():
            near_tiles(SB_NEAR_BLOCKS + 1)

        def farther(dist):
            @pl.when(live(dist, n_blk))
            def _():
                diagonal(dist)
                if dist + 1 < n_blk:
                    farther(dist + 1)

        if SB_NEAR_BLOCKS + 1 < n_blk:
            farther(SB_NEAR_BLOCKS + 1)
        return 0

    step = pl.program_id(0) * pl.num_programs(1) + pl.program_id(1)

    @pl.when(step == 0)
    def _():
        plan_scr[0] = 0
        plan_scr[1] = 0

    lax.fori_loop(0, 2, head, 0)

    @pl.when(step == SB_PROBE_STEPS - 1)
    def _():
        plan_scr[0] = (plan_scr[1] >= SB_PROBE_STEPS).astype(jnp.int32)

    o_ref[0] = acc_scr[...].astype(o_ref.dtype)


def _sb_attention(qkv):
    b, s, _ = qkv.shape
    pairs = SB_HEADS // 2
    return pl.pallas_call(
        _sb_kernel,
        grid=(b, pairs),
        in_specs=[pl.BlockSpec((1, s, LANES), lambda bi, p: (bi, 0, p)),
                  pl.BlockSpec((1, s, LANES), lambda bi, p: (bi, 0, pairs + p)),
                  pl.BlockSpec((1, s, LANES), lambda bi, p: (bi, 0, 2 * pairs + p))],
        out_specs=pl.BlockSpec((1, s, LANES), lambda bi, p: (bi, 0, p)),
        out_shape=jax.ShapeDtypeStruct((b, s, SB_HEADS * HEAD_DIM), BF16),
        scratch_shapes=[pltpu.VMEM((2, s, LANES), BF16), pltpu.VMEM((2, s, LANES), BF16),
                        pltpu.VMEM((s, LANES), F32), pltpu.VMEM((s, LANES), F32),
                        pltpu.VMEM((2, 8, LANES), F32), pltpu.SMEM((2,), jnp.int32)],
        compiler_params=_params("arbitrary", "arbitrary"),
        name="sb_attention",
    )(qkv, qkv, qkv)


def _t5_bucket_np(rel):
    max_exact = NUM_BUCKETS // 2
    rel = np.maximum(rel, 0)
    large = max_exact + (np.log(np.maximum(rel, max_exact) / max_exact)
                         / math.log(MAX_DISTANCE / max_exact) * (NUM_BUCKETS - max_exact)).astype(np.int64)
    return np.where(rel < max_exact, rel, np.minimum(large, NUM_BUCKETS - 1)).astype(np.int32)


def _bias_kernel(bucket_ref, rel_bias_ref, o_ref):
    h = pl.program_id(0)
    for plane in range(2):
        bk = bucket_ref[plane]
        acc = jnp.full(bk.shape, -jnp.inf, F32)
        for b in range(NUM_BUCKETS):
            acc = jnp.where(bk == b, rel_bias_ref[b, h] * LOG2_E, acc)
        o_ref[0, plane] = acc
    o_ref[0, 2] = jnp.full((TQ, TQ), rel_bias_ref[NUM_BUCKETS - 1, h] * LOG2_E, F32)


def _moba_bias_tables(rel_bias):
    off = np.arange(TQ)
    rel_own = off[:, None] - off[None, :]
    own = np.where(rel_own >= 0, _t5_bucket_np(rel_own), -1)
    adj = _t5_bucket_np(rel_own + MOBA_BLOCK)
    buckets = jnp.asarray(np.stack([own, adj]).astype(np.int32))
    return pl.pallas_call(
        _bias_kernel,
        grid=(MOBA_HEADS,),
        in_specs=[_whole(buckets.shape), pl.BlockSpec(memory_space=pltpu.SMEM)],
        out_specs=pl.BlockSpec((1, 3, TQ, TQ), lambda h: (h, 0, 0, 0)),
        out_shape=jax.ShapeDtypeStruct((MOBA_HEADS, 3, TQ, TQ), F32),
        compiler_params=_params("arbitrary"),
        name="moba_bias_tables",
    )(buckets, rel_bias)


_N_BLK_LANES = 8


def _moba_constants(nb):
    assert nb <= _N_BLK_LANES
    diff = np.zeros((LANES, LANES), np.float32)
    count = np.zeros((LANES, LANES), np.float32)
    for hh in range(2):
        for a in range(nb):
            for b in range(nb):
                if a != b:
                    pair = hh * HEAD_DIM + a * 8 + b
                    diff[pair, a] += 1.0
                    diff[pair, b] -= 1.0
                    count[pair, _spare_lane_base(hh) + b] = 1.0
                    count[pair, _spare_lane_base(hh) + _N_BLK_LANES + b] = 1.0
    return jnp.asarray(diff, BF16), jnp.asarray(count, BF16)


_T_KIND, _T_FAR_HI, _T_FAR_LO, _T_FUTURE, _T_UNSEL, _N_HEAD_TABLES = 0, 1, 2, 3, 4, 5


def _moba_fill_tables(head_tab, gate_tab, s_len):
    lane = lax.broadcasted_iota(jnp.int32, (s_len, LANES), 1)
    row_blk = lax.broadcasted_iota(jnp.int32, (s_len, LANES), 0) // MOBA_BLOCK
    for hh in range(2):
        base = _spare_lane_base(hh)
        in_hi = (lane >= base) & (lane < base + _N_BLK_LANES)
        in_lo = (lane >= base + _N_BLK_LANES) & (lane < base + 2 * _N_BLK_LANES)
        blk = jnp.where(in_lo, lane - base - _N_BLK_LANES, lane - base)
        far = blk < row_blk - 1
        store = lambda t, x: head_tab.__setitem__((hh, t), x.astype(BF16))
        store(_T_KIND, jnp.where(in_hi | in_lo, jnp.where(blk == row_blk, 1.0, 0.0), -1.0))
        store(_T_FAR_HI, jnp.where(in_hi & far, 1.0, 0.0))
        store(_T_FAR_LO, jnp.where(in_lo & far, 1.0, 0.0))
        store(_T_FUTURE, jnp.where(in_hi & (blk > row_blk), MASK_PENALTY, 0.0))
        store(_T_UNSEL, jnp.where(in_hi & (blk != row_blk), MASK_PENALTY, 0.0))
    pair_a, pair_b = (lane % HEAD_DIM) // 8, lane % 8
    valid = pair_a < row_blk
    gate_tab[0] = jnp.where(valid, 1.0, 0.0)
    gate_tab[1] = jnp.where(valid & (pair_a < pair_b), 1.0, 0.0)


def _moba_kernel(q_ref, k_ref, v_ref, bias_ref, diff_ref, count_ref, o_ref,
                 qaug_scr, kaug_scr, vm_scr, head_tab, gate_tab, *, nb):
    s_len = q_ref.shape[1]

    @pl.when((pl.program_id(0) == 0) & (pl.program_id(1) == 0))
    def _():
        _moba_fill_tables(head_tab, gate_tab, s_len)

    q, k, v = q_ref[0], k_ref[0], v_ref[0]
    kbar = jnp.mean(k.astype(F32).reshape(nb, MOBA_BLOCK, LANES), axis=1)
    kbar = jnp.concatenate([kbar, jnp.zeros((LANES - nb, LANES), F32)], axis=0)
    p0 = kbar.astype(BF16)
    r1 = kbar - p0.astype(F32)
    p1 = r1.astype(BF16)
    p2 = (r1 - p1.astype(F32)).astype(BF16)
    d = diff_ref[...]
    kdiff = _dot(d, p0) + _dot(d, p1) + _dot(d, p2)
    pair_head = lax.broadcasted_iota(jnp.int32, (LANES, LANES), 0) // HEAD_DIM
    feat_head = lax.broadcasted_iota(jnp.int32, (LANES, LANES), 1) // HEAD_DIM
    kdiff = jnp.where(pair_head == feat_head, kdiff, 0.0).astype(BF16)

    g = _dot_nt(q, kdiff)
    beats = jnp.where(g > 0.0, gate_tab[0], jnp.where(g == 0.0, gate_tab[1], 0.0))
    rank = _dot(beats.astype(BF16), count_ref[...]).astype(BF16)
    top = rank < float(MOBA_TOPK)

    lane = lax.broadcasted_iota(jnp.int32, (1, LANES), 1)
    for hh in range(2):
        lane_mask = jnp.where(_head_lanes((1, LANES), hh), 1.0, 0.0).astype(BF16)
        ones_lane = jnp.where(lane == _spare_lane_base(hh), 1.0, 0.0).astype(BF16)
        vm_scr[hh] = v * lane_mask + ones_lane
        kind = head_tab[hh, _T_KIND]
        spare = kind >= 0.0
        kaug_scr[hh] = jnp.where(spare, kind, k)
        far = bias_ref[hh, 2, 0:1, 0:LANES]
        far_hi = far.astype(BF16)
        far_lo = (far - far_hi.astype(F32)).astype(BF16)
        chosen_val = head_tab[hh, _T_FAR_HI] * far_hi + head_tab[hh, _T_FAR_LO] * far_lo + head_tab[hh, _T_FUTURE]
        qaug_scr[hh] = jnp.where(spare, jnp.where(top, chosen_val, head_tab[hh, _T_UNSEL]), q * lane_mask)

    for i in range(nb):
        rows = slice(i * TQ, (i + 1) * TQ)
        kv_len = (i + 1) * TQ
        out = jnp.zeros((TQ, LANES), F32)
        for hh in range(2):
            s = _dot_nt(qaug_scr[hh, rows, :], kaug_scr[hh, :kv_len, :])
            near = [s[:, j * TQ:(j + 1) * TQ] + bias_ref[hh, i - j] for j in range(max(i - 1, 0), i + 1)]
            s = jnp.concatenate(([s[:, :(i - 1) * TQ]] if i >= 2 else []) + near, axis=1)
            p = jnp.exp2(s - jnp.max(s, axis=-1, keepdims=True))
            out = out + _normalized(_dot(p.astype(BF16), vm_scr[hh, :kv_len, :]), hh)
        o_ref[0, rows, :] = out.astype(o_ref.dtype)


def _moba_attention(qkv, bias_tables):
    b, s, _ = qkv.shape
    nb = s // MOBA_BLOCK
    pairs = MOBA_HEADS // 2
    diff, count = _moba_constants(nb)
    return pl.pallas_call(
        functools.partial(_moba_kernel, nb=nb),
        grid=(b, pairs),
        in_specs=[pl.BlockSpec((1, s, LANES), lambda bi, p: (bi, 0, p)),
                  pl.BlockSpec((1, s, LANES), lambda bi, p: (bi, 0, pairs + p)),
                  pl.BlockSpec((1, s, LANES), lambda bi, p: (bi, 0, 2 * pairs + p)),
                  pl.BlockSpec((2, 3, TQ, TQ), lambda bi, p: (p, 0, 0, 0)),
                  _whole(diff.shape), _whole(count.shape)],
        out_specs=pl.BlockSpec((1, s, LANES), lambda bi, p: (bi, 0, p)),
        out_shape=jax.ShapeDtypeStruct((b, s, MOBA_HEADS * HEAD_DIM), BF16),
        scratch_shapes=[pltpu.VMEM((2, s, LANES), BF16)] * 3
        + [pltpu.VMEM((2, _N_HEAD_TABLES, s, LANES), BF16), pltpu.VMEM((2, s, LANES), F32)],
        compiler_params=_params("arbitrary", "arbitrary"),
        name="moba_attention",
    )(qkv, qkv, qkv, bias_tables, diff, count)


def _mix_ffn_kernel(*refs, n_in):
    a_refs = refs[:n_in]
    h_ref, w_out_ref, gmix_ref, gpre_ref, wgu_ref, wd_ref, gpost_ref, o_ref, act_scr = refs[n_in:]
    mixed, k0 = None, 0
    for a_ref in a_refs:
        k1 = k0 + a_ref.shape[1]
        part = _dot(a_ref[...], w_out_ref[k0:k1, :])
        mixed, k0 = part if mixed is None else mixed + part, k1
    x = h_ref[...] + _rms(mixed, gmix_ref[...])
    u = _rms(x, gpre_ref[...]).astype(BF16)
    for c in range(0, D_FF, FF_CHUNK):
        gate = _dot(u, wgu_ref[:, c:c + FF_CHUNK])
        up = _dot(u, wgu_ref[:, D_FF + c:D_FF + c + FF_CHUNK])
        act_scr[:, c:c + FF_CHUNK] = (gate * jax.nn.sigmoid(gate) * up).astype(BF16)
    f = _dot(act_scr[...], wd_ref[...])
    o_ref[...] = x + _rms(f, gpost_ref[...])


def _mix_ffn(acts, h, w_out, gmix, gpre, wgu, wd, gpost):
    n, d = h.shape
    row = lambda t: (t, 0)
    vec = _whole((1, d))
    return pl.pallas_call(
        functools.partial(_mix_ffn_kernel, n_in=len(acts)),
        grid=(n // TM,),
        in_specs=([pl.BlockSpec((TM, a.shape[1]), row) for a in acts]
                  + [pl.BlockSpec((TM, d), row), _whole(w_out.shape), vec, vec,
                     _whole(wgu.shape), _whole(wd.shape), vec]),
        out_specs=pl.BlockSpec((TM, d), row),
        out_shape=jax.ShapeDtypeStruct((n, d), F32),
        scratch_shapes=[pltpu.VMEM((TM, D_FF), BF16)],
        compiler_params=_params("arbitrary"),
        name="mix_ffn",
    )(*acts, h, w_out, gmix.reshape(1, d), gpre.reshape(1, d), wgu, wd, gpost.reshape(1, d))


def _rot_half_cols(w):
    half = w.shape[-1] // 2
    return jnp.concatenate([-w[..., half:], w[..., :half]], axis=-1)


def _prep_even_weights(w_in, w_uq, w_ukv):
    d = w_in.shape[0]
    o1, o2, o3 = MLA_Q_LORA, MLA_Q_LORA + MLA_KV_LORA, MLA_Q_LORA + MLA_KV_LORA + MLA_ROPE
    hw = MOBA_HEADS * HEAD_DIM
    z = lambda *shape: jnp.zeros(shape, F32)
    kr = w_in[:, o2:o3]
    w_in_ext = jnp.concatenate(
        [w_in[:, :o2],
         z(d, MLA_NOPE), kr, z(d, LANES - MLA_NOPE - MLA_ROPE),
         z(d, MLA_NOPE), _rot_half_cols(kr), z(d, LANES - MLA_NOPE - MLA_ROPE),
         w_in[:, o3:o3 + hw] * (HEAD_DIM ** -0.5), w_in[:, o3 + hw:]], axis=1).astype(BF16)

    wq = w_uq.reshape(MLA_Q_LORA, MLA_HEADS, MLA_NOPE + MLA_ROPE)
    nope, rp = wq[..., :MLA_NOPE], wq[..., MLA_NOPE:]
    tail = z(MLA_Q_LORA, MLA_HEADS, LANES - MLA_NOPE - MLA_ROPE)
    qa = jnp.concatenate([nope, rp, tail], axis=-1).reshape(MLA_Q_LORA, _QW)
    qb = jnp.concatenate([jnp.zeros_like(nope), _rot_half_cols(rp), tail], axis=-1).reshape(MLA_Q_LORA, _QW)
    w_uq_ext = jnp.concatenate([qa, qb], axis=1).astype(BF16)

    wkv = w_ukv.reshape(MLA_KV_LORA, MLA_HEADS, MLA_NOPE + MLA_V)
    k_nope, v = wkv[..., :MLA_NOPE], wkv[..., MLA_NOPE:]
    w_k = jnp.concatenate([k_nope, jnp.zeros_like(k_nope)], axis=-1).reshape(MLA_KV_LORA, _QW).astype(BF16)
    v = v.reshape(MLA_KV_LORA, MLA_HEADS // 2, 2, MLA_V)
    zero = jnp.zeros_like(v[:, :, 0])
    w_v = jnp.stack([jnp.concatenate([v[:, :, 0], zero], axis=-1),
                     jnp.concatenate([zero, v[:, :, 1]], axis=-1)], axis=2)
    w_v = w_v.reshape(MLA_KV_LORA, _QW).astype(BF16)
    return w_in_ext, w_uq_ext, w_k, w_v


def _rope_tables(seq):
    half = MLA_ROPE // 2
    inv_freq = ROPE_THETA ** (-jnp.arange(half, dtype=F32) / half)
    ang = jnp.arange(seq).astype(F32)[:, None] * inv_freq[None, :]
    cos, sin = jnp.cos(ang), jnp.sin(ang)
    ones, zeros = jnp.ones((seq, MLA_NOPE), F32), jnp.zeros((seq, MLA_NOPE), F32)
    tail = jnp.zeros((seq, LANES - MLA_NOPE - MLA_ROPE), F32)
    scale = (MLA_NOPE + MLA_ROPE) ** -0.5 * LOG2_E
    cq = jnp.concatenate([ones, cos, cos, tail], axis=1) * scale
    sq = jnp.concatenate([zeros, sin, sin, tail], axis=1) * scale
    ck = jnp.concatenate([zeros, cos, cos, tail], axis=1)
    sk = jnp.concatenate([zeros, sin, sin, tail], axis=1)
    return cq, sq, ck, sk


def kernel(x, mix_pre_g, mix_post_g, ffn_pre_g, ffn_post_g, ab_w_in, mla_q_norm_g, mla_w_uq,
           mla_kv_norm_g, mla_w_ukv, ab_w_out, rel_bias, sb_w_qkv, sb_w_out, ffn_w_gate_up, ffn_w_down):
    b, s, d = x.shape
    n = b * s
    h = x.reshape(n, d)
    tables = _rope_tables(s)
    bias_tables = _moba_bias_tables(rel_bias)
    for layer in range(DEPTH):
        if layer % 2 == 0:
            e = layer // 2
            w_in, w_uq, w_k, w_v = _prep_even_weights(ab_w_in[e], mla_w_uq[e], mla_w_ukv[e])
            q, k, v, moba_qkv = _even_in(h, mix_pre_g[layer], w_in, mla_q_norm_g[e], w_uq,
                                         mla_kv_norm_g[e], w_k, w_v, tables, s)
            mla_out = _mla_attention(q.reshape(b, s, -1), k.reshape(b, s, -1), v.reshape(b, s, -1))
            moba_out = _moba_attention(moba_qkv.reshape(b, s, -1), bias_tables)
            mixed, w_out = [mla_out.reshape(n, -1), moba_out.reshape(n, -1)], ab_w_out[e]
        else:
            o = layer // 2
            width = SB_HEADS * HEAD_DIM
            qkv = _norm_matmul(h, mix_pre_g[layer], sb_w_qkv[o].astype(BF16),
                               width, HEAD_DIM ** -0.5 * LOG2_E)
            mixed, w_out = [_sb_attention(qkv.reshape(b, s, -1)).reshape(n, -1)], sb_w_out[o]
        h = _mix_ffn(mixed, h, w_out.astype(BF16), mix_post_g[layer], ffn_pre_g[layer],
                     ffn_w_gate_up[layer].astype(BF16), ffn_w_down[layer].astype(BF16),
                     ffn_post_g[layer])
    return h.reshape(b, s, d)
```

```python
import functools
import math

import numpy as np
import jax
import jax.numpy as jnp
from jax import lax
from jax.experimental import pallas as pl
from jax.experimental.pallas import tpu as pltpu

D_MODEL = 1024
DEPTH = 4
HEAD_DIM = 64
MLA_HEADS = 8
MLA_NOPE = 64
MLA_ROPE = 32
MLA_V = 64
MLA_Q_LORA = 384
MLA_KV_LORA = 256
ROPE_THETA = 10000.0
MOBA_HEADS = 8
MOBA_BLOCK = 256
MOBA_TOPK = 3
SB_HEADS = 16
NUM_BUCKETS = 32
MAX_DISTANCE = 128
RMS_EPS = 1e-6
D_FF = 2816

LANES = 128
TQ = 256
TM = 512
FF_CHUNK = 256
MASK_PENALTY = -1e30
LOG2_E = 1.0 / math.log(2.0)
VMEM_LIMIT = 48 * 1024 * 1024

F32 = jnp.float32
BF16 = jnp.bfloat16


def _rms(x, g):
    return x * lax.rsqrt(jnp.mean(x * x, axis=-1, keepdims=True) + RMS_EPS) * g


def _dot(a, b):
    return jnp.dot(a, b, preferred_element_type=F32)


def _dot_nt(a, b):
    return lax.dot_general(a, b, (((1,), (1,)), ((), ())), preferred_element_type=F32)


def _params(*semantics):
    return pltpu.CompilerParams(dimension_semantics=semantics, vmem_limit_bytes=VMEM_LIMIT)


def _whole(shape):
    nd = len(shape)
    return pl.BlockSpec(shape, lambda *_: (0,) * nd, pipeline_mode=pl.Buffered(1))


def _norm_matmul_kernel(h_ref, g_ref, w_ref, o_ref, *, q_cols, q_scale):
    u = _rms(h_ref[...], g_ref[...]).astype(BF16)
    for c in range(0, o_ref.shape[-1], 512):
        y = _dot(u, w_ref[:, c:c + 512])
        if c < q_cols:
            y = y * q_scale
        o_ref[:, c:c + 512] = y.astype(o_ref.dtype)


def _norm_matmul(h, g, w, q_cols, q_scale):
    n, d = h.shape
    n_out = w.shape[1]
    return pl.pallas_call(
        functools.partial(_norm_matmul_kernel, q_cols=q_cols, q_scale=q_scale),
        grid=(n // TM,),
        in_specs=[pl.BlockSpec((TM, d), lambda t: (t, 0)), _whole((1, d)), _whole(w.shape)],
        out_specs=pl.BlockSpec((TM, n_out), lambda t: (t, 0)),
        out_shape=jax.ShapeDtypeStruct((n, n_out), BF16),
        compiler_params=_params("arbitrary"),
        name="norm_matmul",
    )(h, g.reshape(1, d), w)


def _head_lanes(shape, hh):
    lane = lax.broadcasted_iota(jnp.int32, shape, 1)
    return (lane >= hh * HEAD_DIM) & (lane < (hh + 1) * HEAD_DIM)


def _spare_lane_base(hh):
    return HEAD_DIM * (1 - hh)


def _store_masked_v(v_ref, vm_scr, ones_lane):
    v = v_ref[0].astype(F32)
    lane = lax.broadcasted_iota(jnp.int32, v.shape, 1)
    for hh in range(2):
        spare = jnp.where(lane == _spare_lane_base(hh), 1.0, 0.0) if ones_lane else 0.0
        vm_scr[hh] = jnp.where(_head_lanes(v.shape, hh), v, spare).astype(BF16)


def _normalized(acc, hh):
    base = _spare_lane_base(hh)
    return jnp.where(_head_lanes(acc.shape, hh), acc / acc[:, base:base + 1], 0.0)


_O_CQ = 0
_O_CKV = MLA_Q_LORA
_O_KR = _O_CKV + MLA_KV_LORA
_O_KRR = _O_KR + LANES
_O_MOBA = _O_KRR + LANES
_W_IN_EXT = _O_MOBA + 3 * MOBA_HEADS * HEAD_DIM
_QW = MLA_HEADS * LANES


def _even_in_kernel(h_ref, g_ref, w_in_ref, gq_ref, w_uq_ref, gkv_ref, w_k_ref, w_v_ref,
                    cq_ref, sq_ref, ck_ref, sk_ref, q_ref, k_ref, v_ref, moba_ref):
    u = _rms(h_ref[...], g_ref[...]).astype(BF16)
    c_q = _dot(u, w_in_ref[:, _O_CQ:_O_CKV])
    c_kv = _dot(u, w_in_ref[:, _O_CKV:_O_KR])
    kr_both = _dot(u, w_in_ref[:, _O_KR:_O_MOBA])
    kr, krr = kr_both[:, :LANES], kr_both[:, LANES:]
    hw = MOBA_HEADS * HEAD_DIM
    for c in range(0, 3 * hw, hw):
        y = _dot(u, w_in_ref[:, _O_MOBA + c:_O_MOBA + c + hw])
        if c == 0:
            y = y * LOG2_E
        moba_ref[:, c:c + hw] = y.astype(BF16)

    cqn = _rms(c_q, gq_ref[...]).astype(BF16)
    qa = _dot(cqn, w_uq_ref[:, :_QW])
    qb = _dot(cqn, w_uq_ref[:, _QW:])
    cq, sq = cq_ref[...], sq_ref[...]
    for h in range(MLA_HEADS):
        s = slice(h * LANES, (h + 1) * LANES)
        q_ref[:, s] = (qa[:, s] * cq + qb[:, s] * sq).astype(BF16)

    ckvn = _rms(c_kv, gkv_ref[...]).astype(BF16)
    kn = _dot(ckvn, w_k_ref[...])
    v = _dot(ckvn, w_v_ref[...])
    lane = lax.broadcasted_iota(jnp.int32, v.shape, 1)
    spare = jnp.where((lane // LANES) % 2 == 0, _spare_lane_base(0), _spare_lane_base(1))
    v_ref[...] = jnp.where(lane % LANES == spare, 1.0, v).astype(BF16)
    k_rope = kr * ck_ref[...] + krr * sk_ref[...]
    for h in range(MLA_HEADS):
        s = slice(h * LANES, (h + 1) * LANES)
        k_ref[:, s] = (kn[:, s] + k_rope).astype(BF16)


def _even_in(h, g, w_in, gq, w_uq, gkv, w_k, w_v, tables, seq):
    n, d = h.shape
    row = lambda t: (t, 0)
    pos = lambda t: (t % (seq // TM), 0)
    tab = pl.BlockSpec((TM, LANES), pos)
    outs = [(_QW, "q"), (_QW, "k"), (_QW, "v"), (3 * MOBA_HEADS * HEAD_DIM, "moba")]
    return pl.pallas_call(
        _even_in_kernel,
        grid=(n // TM,),
        in_specs=[pl.BlockSpec((TM, d), row), _whole((1, d)), _whole(w_in.shape),
                  _whole((1, MLA_Q_LORA)), _whole(w_uq.shape), _whole((1, MLA_KV_LORA)),
                  _whole(w_k.shape), _whole(w_v.shape), tab, tab, tab, tab],
        out_specs=[pl.BlockSpec((TM, w), row) for w, _ in outs],
        out_shape=[jax.ShapeDtypeStruct((n, w), BF16) for w, _ in outs],
        compiler_params=_params("arbitrary"),
        name="even_in",
    )(h, g.reshape(1, d), w_in, gq.reshape(1, -1), w_uq, gkv.reshape(1, -1), w_k, w_v, *tables)


def _two_block_attention(q, k_of, v_of, r0, add_near):
    end = r0 + 2 * TQ
    kd, vd = k_of(r0, end), v_of(r0, end)
    s_a = add_near(_dot_nt(q[:TQ], kd[:TQ]), 0)
    s_b = _dot_nt(q[TQ:], kd)
    s_b = jnp.concatenate([add_near(s_b[:, :TQ], 1), add_near(s_b[:, TQ:], 0)], axis=1)
    if r0 > 0:
        s_off = _dot_nt(q, k_of(0, r0))
        s_a = jnp.concatenate([s_off[:TQ, :r0 - TQ], add_near(s_off[:TQ, r0 - TQ:], 1), s_a], axis=1)
        s_b = jnp.concatenate([s_off[TQ:], s_b], axis=1)
    p_a = jnp.exp2(s_a - jnp.max(s_a, axis=-1, keepdims=True)).astype(BF16)
    p_b = jnp.exp2(s_b - jnp.max(s_b, axis=-1, keepdims=True)).astype(BF16)
    acc = jnp.concatenate([_dot(p_a[:, r0:], vd[:TQ]), _dot(p_b[:, r0:], vd)], axis=0)
    if r0 > 0:
        acc = acc + _dot(jnp.concatenate([p_a[:, :r0], p_b[:, :r0]], axis=0), v_of(0, r0))
    return acc


def _mla_kernel(q_ref, k_ref, v_ref, o_ref):
    s_len = q_ref.shape[1]
    row = lax.broadcasted_iota(jnp.int32, (TQ, TQ), 0)
    col = lax.broadcasted_iota(jnp.int32, (TQ, TQ), 1)
    causal = col <= row
    add_near = lambda s, d: jnp.where(causal, s, -jnp.inf) if d == 0 else s
    for r0 in range(0, s_len, 2 * TQ):
        rows = slice(r0, r0 + 2 * TQ)
        out = jnp.zeros((2 * TQ, LANES), F32)
        for hh in range(2):
            lanes = slice(hh * LANES, (hh + 1) * LANES)
            acc = _two_block_attention(q_ref[0, rows, lanes],
                                       lambda a, b: k_ref[0, a:b, lanes],
                                       lambda a, b: v_ref[0, a:b, lanes], r0, add_near)
            out = out + _normalized(acc, hh)
        o_ref[0, rows, :] = out.astype(o_ref.dtype)


def _mla_attention(q, k, v):
    b, s, _ = q.shape
    pairs = MLA_HEADS // 2
    qkv = pl.BlockSpec((1, s, 2 * LANES), lambda bi, p: (bi, 0, p))
    return pl.pallas_call(
        _mla_kernel,
        grid=(b, pairs),
        in_specs=[qkv, qkv, qkv],
        out_specs=pl.BlockSpec((1, s, LANES), lambda bi, p: (bi, 0, p)),
        out_shape=jax.ShapeDtypeStruct((b, s, MLA_HEADS * MLA_V), BF16),
        compiler_params=_params("arbitrary", "arbitrary"),
        name="mla_attention",
    )(q, k, v)


_SOFTPLUS2_CLAMP = 64.0


def _softplus2(z):
    return jnp.maximum(z, jnp.log2(1.0 + jnp.exp2(jnp.minimum(z, _SOFTPLUS2_CLAMP))))


SB_UNDERFLOW = 160.0
SB_NEAR_BLOCKS = 2
SB_PROBE_STEPS = 4


def _sb_kernel(q_ref, k_ref, v_ref, o_ref, qm_scr, vm_scr, acc_scr, carry_scr, limit_scr, plan_scr):
    s_len = q_ref.shape[1]
    n_blk = s_len // TQ
    q, k, v = q_ref[0], k_ref[0], v_ref[0]
    feat = lax.broadcasted_iota(jnp.int32, (LANES, LANES), 0) // HEAD_DIM
    lane = lax.broadcasted_iota(jnp.int32, (LANES, LANES), 1) // HEAD_DIM
    same_head = jnp.where(feat == lane, 1.0, 0.0).astype(BF16)
    q_max2 = jnp.max(_dot(q * q, same_head), axis=0, keepdims=True)
    k_max2 = jnp.max(_dot(k * k, same_head), axis=0, keepdims=True)
    z_bound = jnp.sqrt(q_max2 * k_max2) * 1.02
    for hh in range(2):
        in_head = _head_lanes((1, LANES), hh)
        lane_mask = jnp.where(in_head, 1.0, 0.0).astype(BF16)
        qm_scr[hh] = q * lane_mask
        vm_scr[hh] = v * lane_mask
        bound = jnp.max(jnp.where(in_head, z_bound, 0.0), axis=1, keepdims=True)
        limit_scr[hh] = jnp.broadcast_to(bound + SB_UNDERFLOW, limit_scr.shape[1:])
    acc_scr[...] = jnp.zeros(acc_scr.shape, F32)

    row = lax.broadcasted_iota(jnp.int32, (TQ, TQ), 0)
    col = lax.broadcasted_iota(jnp.int32, (TQ, TQ), 1)
    strict = col < row
    tri = jnp.where(row >= col, 1.0, 0.0).astype(BF16)

    def head(hh, _):
        def tile_rows(j, r0, r1):
            keys = slice(j * TQ, (j + 1) * TQ)
            rows = slice(r0 * TQ, r1 * TQ)
            z = _dot_nt(qm_scr[hh, rows, :], k_ref[0, keys, :])
            sp = _softplus2(z)
            if r0 == j:
                sp_diag = jnp.where(strict, sp[:TQ], 0.0)
                sp = sp_diag if r1 == r0 + 1 else jnp.concatenate([sp_diag, sp[TQ:]], axis=0)
            c = _dot(sp.astype(BF16), tri)
            tot = jnp.broadcast_to(c[:, 0:1], (c.shape[0], LANES))
            first = TQ if r0 == j else 0
            w = []
            if r0 == j:
                w.append(jnp.where(strict, jnp.exp2(z[:TQ] - c[:TQ]), 0.0))
                carry_scr[keys, :] = tot[:TQ]
            if r1 * TQ - r0 * TQ > first:
                below = slice(r0 * TQ + first, r1 * TQ)
                carry = carry_scr[below, :]
                w.append(jnp.exp2(z[first:] - c[first:] - jnp.concatenate([carry, carry], axis=1)))
                carry_scr[below, :] = carry + tot[first:]
            w = (w[0] if len(w) == 1 else jnp.concatenate(w, axis=0)).astype(BF16)
            acc_scr[rows, :] += _dot(w, vm_scr[hh, keys, :])

        def near_tiles(n_near):
            for j in range(n_blk - 1, -1, -1):
                tile_rows(j, j, min(j + n_near, n_blk))

        def live(r0, r1):
            least = jnp.min(carry_scr[r0 * TQ:r1 * TQ, :], axis=0, keepdims=True)
            return jnp.min(least - limit_scr[hh, 0:1, :]) < 0.0

        def diagonal(dist):
            for i in range(dist, n_blk):
                tile_rows(i - dist, i, i + 1)

        wide = plan_scr[0] == 1

        @pl.when(jnp.logical_not(wide))
        def _():
            near_tiles(SB_NEAR_BLOCKS)
            flags = [live(i, i + 1) for i in range(SB_NEAR_BLOCKS, n_blk)]
            n_live = sum(f.astype(jnp.int32) for f in flags)
            plan_scr[1] += (n_live >= 2).astype(jnp.int32)
            plan_scr[2] = n_live

            @pl.when(n_live >= 2)
            def _():
                diagonal(SB_NEAR_BLOCKS)

            @pl.when(n_live == 1)
            def _():
                for i, f in zip(range(SB_NEAR_BLOCKS, n_blk), flags):
                    pl.when(f)(functools.partial(tile_rows, i - SB_NEAR_BLOCKS, i, i + 1))

        @pl.when(wide)
        def _():
            near_tiles(SB_NEAR_BLOCKS + 1)
            plan_scr[2] = 1

        def farther(dist):
            @pl.when(live(dist, n_blk))
            def _():
                diagonal(dist)
                if dist + 1 < n_blk:
                    farther(dist + 1)

        if SB_NEAR_BLOCKS + 1 < n_blk:
            pl.when(plan_scr[2] > 0)(functools.partial(farther, SB_NEAR_BLOCKS + 1))
        return 0

    step = pl.program_id(0) * pl.num_programs(1) + pl.program_id(1)

    @pl.when(step == 0)
    def _():
        plan_scr[0] = 0
        plan_scr[1] = 0

    lax.fori_loop(0, 2, head, 0)

    @pl.when(step == SB_PROBE_STEPS - 1)
    def _():
        plan_scr[0] = (plan_scr[1] >= SB_PROBE_STEPS).astype(jnp.int32)

    o_ref[0] = acc_scr[...].astype(o_ref.dtype)


def _sb_attention(qkv):
    b, s, _ = qkv.shape
    pairs = SB_HEADS // 2
    return pl.pallas_call(
        _sb_kernel,
        grid=(b, pairs),
        in_specs=[pl.BlockSpec((1, s, LANES), lambda bi, p: (bi, 0, p)),
                  pl.BlockSpec((1, s, LANES), lambda bi, p: (bi, 0, pairs + p)),
                  pl.BlockSpec((1, s, LANES), lambda bi, p: (bi, 0, 2 * pairs + p))],
        out_specs=pl.BlockSpec((1, s, LANES), lambda bi, p: (bi, 0, p)),
        out_shape=jax.ShapeDtypeStruct((b, s, SB_HEADS * HEAD_DIM), BF16),
        scratch_shapes=[pltpu.VMEM((2, s, LANES), BF16), pltpu.VMEM((2, s, LANES), BF16),
                        pltpu.VMEM((s, LANES), F32), pltpu.VMEM((s, LANES), F32),
                        pltpu.VMEM((2, 8, LANES), F32), pltpu.SMEM((3,), jnp.int32)],
        compiler_params=_params("arbitrary", "arbitrary"),
        name="sb_attention",
    )(qkv, qkv, qkv)


def _t5_bucket_np(rel):
    max_exact = NUM_BUCKETS // 2
    rel = np.maximum(rel, 0)
    large = max_exact + (np.log(np.maximum(rel, max_exact) / max_exact)
                         / math.log(MAX_DISTANCE / max_exact) * (NUM_BUCKETS - max_exact)).astype(np.int64)
    return np.where(rel < max_exact, rel, np.minimum(large, NUM_BUCKETS - 1)).astype(np.int32)


def _bias_kernel(bucket_ref, rel_bias_ref, o_ref):
    h = pl.program_id(0)
    for plane in range(2):
        bk = bucket_ref[plane]
        acc = jnp.full(bk.shape, -jnp.inf, F32)
        for b in range(NUM_BUCKETS):
            acc = jnp.where(bk == b, rel_bias_ref[b, h] * LOG2_E, acc)
        o_ref[0, plane] = acc
    o_ref[0, 2] = jnp.full((TQ, TQ), rel_bias_ref[NUM_BUCKETS - 1, h] * LOG2_E, F32)


def _moba_bias_tables(rel_bias):
    off = np.arange(TQ)
    rel_own = off[:, None] - off[None, :]
    own = np.where(rel_own >= 0, _t5_bucket_np(rel_own), -1)
    adj = _t5_bucket_np(rel_own + MOBA_BLOCK)
    buckets = jnp.asarray(np.stack([own, adj]).astype(np.int32))
    return pl.pallas_call(
        _bias_kernel,
        grid=(MOBA_HEADS,),
        in_specs=[_whole(buckets.shape), pl.BlockSpec(memory_space=pltpu.SMEM)],
        out_specs=pl.BlockSpec((1, 3, TQ, TQ), lambda h: (h, 0, 0, 0)),
        out_shape=jax.ShapeDtypeStruct((MOBA_HEADS, 3, TQ, TQ), F32),
        compiler_params=_params("arbitrary"),
        name="moba_bias_tables",
    )(buckets, rel_bias)


_N_BLK_LANES = 8


def _moba_constants(nb):
    assert nb <= _N_BLK_LANES
    diff = np.zeros((LANES, LANES), np.float32)
    count = np.zeros((LANES, LANES), np.float32)
    for hh in range(2):
        for a in range(nb):
            for b in range(nb):
                if a != b:
                    pair = hh * HEAD_DIM + a * 8 + b
                    diff[pair, a] += 1.0
                    diff[pair, b] -= 1.0
                    count[pair, _spare_lane_base(hh) + b] = 1.0
                    count[pair, _spare_lane_base(hh) + _N_BLK_LANES + b] = 1.0
    return jnp.asarray(diff, BF16), jnp.asarray(count, BF16)


_T_KIND, _T_FAR_HI, _T_FAR_LO, _T_FUTURE, _T_UNSEL, _N_HEAD_TABLES = 0, 1, 2, 3, 4, 5


def _moba_fill_tables(head_tab, gate_tab, s_len):
    lane = lax.broadcasted_iota(jnp.int32, (s_len, LANES), 1)
    row_blk = lax.broadcasted_iota(jnp.int32, (s_len, LANES), 0) // MOBA_BLOCK
    for hh in range(2):
        base = _spare_lane_base(hh)
        in_hi = (lane >= base) & (lane < base + _N_BLK_LANES)
        in_lo = (lane >= base + _N_BLK_LANES) & (lane < base + 2 * _N_BLK_LANES)
        blk = jnp.where(in_lo, lane - base - _N_BLK_LANES, lane - base)
        far = blk < row_blk - 1
        store = lambda t, x: head_tab.__setitem__((hh, t), x.astype(BF16))
        store(_T_KIND, jnp.where(in_hi | in_lo, jnp.where(blk == row_blk, 1.0, 0.0), -1.0))
        store(_T_FAR_HI, jnp.where(in_hi & far, 1.0, 0.0))
        store(_T_FAR_LO, jnp.where(in_lo & far, 1.0, 0.0))
        store(_T_FUTURE, jnp.where(in_hi & (blk > row_blk), MASK_PENALTY, 0.0))
        store(_T_UNSEL, jnp.where(in_hi & (blk != row_blk), MASK_PENALTY, 0.0))
    pair_a, pair_b = (lane % HEAD_DIM) // 8, lane % 8
    valid = pair_a < row_blk
    gate_tab[0] = jnp.where(valid, 1.0, 0.0)
    gate_tab[1] = jnp.where(valid & (pair_a < pair_b), 1.0, 0.0)


def _moba_kernel(q_ref, k_ref, v_ref, bias_ref, diff_ref, count_ref, o_ref,
                 qaug_scr, kaug_scr, vm_scr, head_tab, gate_tab, *, nb):
    s_len = q_ref.shape[1]

    @pl.when((pl.program_id(0) == 0) & (pl.program_id(1) == 0))
    def _():
        _moba_fill_tables(head_tab, gate_tab, s_len)

    q, k, v = q_ref[0], k_ref[0], v_ref[0]
    kbar = jnp.mean(k.astype(F32).reshape(nb, MOBA_BLOCK, LANES), axis=1)
    kbar = jnp.concatenate([kbar, jnp.zeros((LANES - nb, LANES), F32)], axis=0)
    p0 = kbar.astype(BF16)
    r1 = kbar - p0.astype(F32)
    p1 = r1.astype(BF16)
    p2 = (r1 - p1.astype(F32)).astype(BF16)
    d = diff_ref[...]
    kdiff = _dot(d, p0) + _dot(d, p1) + _dot(d, p2)
    pair_head = lax.broadcasted_iota(jnp.int32, (LANES, LANES), 0) // HEAD_DIM
    feat_head = lax.broadcasted_iota(jnp.int32, (LANES, LANES), 1) // HEAD_DIM
    kdiff = jnp.where(pair_head == feat_head, kdiff, 0.0).astype(BF16)

    g = _dot_nt(q, kdiff)
    beats = jnp.where(g > 0.0, gate_tab[0], jnp.where(g == 0.0, gate_tab[1], 0.0))
    rank = _dot(beats.astype(BF16), count_ref[...]).astype(BF16)
    top = rank < float(MOBA_TOPK)

    lane = lax.broadcasted_iota(jnp.int32, (1, LANES), 1)
    for hh in range(2):
        lane_mask = jnp.where(_head_lanes((1, LANES), hh), 1.0, 0.0).astype(BF16)
        ones_lane = jnp.where(lane == _spare_lane_base(hh), 1.0, 0.0).astype(BF16)
        vm_scr[hh] = v * lane_mask + ones_lane
        kind = head_tab[hh, _T_KIND]
        spare = kind >= 0.0
        kaug_scr[hh] = jnp.where(spare, kind, k)
        far = bias_ref[hh, 2, 0:1, 0:LANES]
        far_hi = far.astype(BF16)
        far_lo = (far - far_hi.astype(F32)).astype(BF16)
        chosen_val = head_tab[hh, _T_FAR_HI] * far_hi + head_tab[hh, _T_FAR_LO] * far_lo + head_tab[hh, _T_FUTURE]
        qaug_scr[hh] = jnp.where(spare, jnp.where(top, chosen_val, head_tab[hh, _T_UNSEL]), q * lane_mask)

    for r0 in range(0, s_len, 2 * TQ):
        rows = slice(r0, r0 + 2 * TQ)
        out = jnp.zeros((2 * TQ, LANES), F32)
        for hh in range(2):
            acc = _two_block_attention(qaug_scr[hh, rows, :],
                                       lambda a, b: kaug_scr[hh, a:b, :],
                                       lambda a, b: vm_scr[hh, a:b, :], r0,
                                       lambda s, d: s + bias_ref[hh, d])
            out = out + _normalized(acc, hh)
        o_ref[0, rows, :] = out.astype(o_ref.dtype)


def _moba_attention(qkv, bias_tables):
    b, s, _ = qkv.shape
    nb = s // MOBA_BLOCK
    pairs = MOBA_HEADS // 2
    diff, count = _moba_constants(nb)
    return pl.pallas_call(
        functools.partial(_moba_kernel, nb=nb),
        grid=(b, pairs),
        in_specs=[pl.BlockSpec((1, s, LANES), lambda bi, p: (bi, 0, p)),
                  pl.BlockSpec((1, s, LANES), lambda bi, p: (bi, 0, pairs + p)),
                  pl.BlockSpec((1, s, LANES), lambda bi, p: (bi, 0, 2 * pairs + p)),
                  pl.BlockSpec((2, 3, TQ, TQ), lambda bi, p: (p, 0, 0, 0)),
                  _whole(diff.shape), _whole(count.shape)],
        out_specs=pl.BlockSpec((1, s, LANES), lambda bi, p: (bi, 0, p)),
        out_shape=jax.ShapeDtypeStruct((b, s, MOBA_HEADS * HEAD_DIM), BF16),
        scratch_shapes=[pltpu.VMEM((2, s, LANES), BF16)] * 3
        + [pltpu.VMEM((2, _N_HEAD_TABLES, s, LANES), BF16), pltpu.VMEM((2, s, LANES), F32)],
        compiler_params=_params("arbitrary", "arbitrary"),
        name="moba_attention",
    )(qkv, qkv, qkv, bias_tables, diff, count)


def _mix_ffn_kernel(*refs, n_in):
    a_refs = refs[:n_in]
    h_ref, w_out_ref, gmix_ref, gpre_ref, wgu_ref, wd_ref, gpost_ref, o_ref, act_scr = refs[n_in:]
    mixed, k0 = None, 0
    for a_ref in a_refs:
        k1 = k0 + a_ref.shape[1]
        part = _dot(a_ref[...], w_out_ref[k0:k1, :])
        mixed, k0 = part if mixed is None else mixed + part, k1
    x = h_ref[...] + _rms(mixed, gmix_ref[...])
    u = _rms(x, gpre_ref[...]).astype(BF16)
    for c in range(0, D_FF, FF_CHUNK):
        gate = _dot(u, wgu_ref[:, c:c + FF_CHUNK])
        up = _dot(u, wgu_ref[:, D_FF + c:D_FF + c + FF_CHUNK])
        act_scr[:, c:c + FF_CHUNK] = (gate * jax.nn.sigmoid(gate) * up).astype(BF16)
    f = _dot(act_scr[...], wd_ref[...])
    o_ref[...] = x + _rms(f, gpost_ref[...])


def _mix_ffn(acts, h, w_out, gmix, gpre, wgu, wd, gpost):
    n, d = h.shape
    row = lambda t: (t, 0)
    vec = _whole((1, d))
    return pl.pallas_call(
        functools.partial(_mix_ffn_kernel, n_in=len(acts)),
        grid=(n // TM,),
        in_specs=([pl.BlockSpec((TM, a.shape[1]), row) for a in acts]
                  + [pl.BlockSpec((TM, d), row), _whole(w_out.shape), vec, vec,
                     _whole(wgu.shape), _whole(wd.shape), vec]),
        out_specs=pl.BlockSpec((TM, d), row),
        out_shape=jax.ShapeDtypeStruct((n, d), F32),
        scratch_shapes=[pltpu.VMEM((TM, D_FF), BF16)],
        compiler_params=_params("arbitrary"),
        name="mix_ffn",
    )(*acts, h, w_out, gmix.reshape(1, d), gpre.reshape(1, d), wgu, wd, gpost.reshape(1, d))


def _rot_half_cols(w):
    half = w.shape[-1] // 2
    return jnp.concatenate([-w[..., half:], w[..., :half]], axis=-1)


def _prep_even_weights(w_in, w_uq, w_ukv):
    d = w_in.shape[0]
    o1, o2, o3 = MLA_Q_LORA, MLA_Q_LORA + MLA_KV_LORA, MLA_Q_LORA + MLA_KV_LORA + MLA_ROPE
    hw = MOBA_HEADS * HEAD_DIM
    z = lambda *shape: jnp.zeros(shape, F32)
    kr = w_in[:, o2:o3]
    w_in_ext = jnp.concatenate(
        [w_in[:, :o2],
         z(d, MLA_NOPE), kr, z(d, LANES - MLA_NOPE - MLA_ROPE),
         z(d, MLA_NOPE), _rot_half_cols(kr), z(d, LANES - MLA_NOPE - MLA_ROPE),
         w_in[:, o3:o3 + hw] * (HEAD_DIM ** -0.5), w_in[:, o3 + hw:]], axis=1).astype(BF16)

    wq = w_uq.reshape(MLA_Q_LORA, MLA_HEADS, MLA_NOPE + MLA_ROPE)
    nope, rp = wq[..., :MLA_NOPE], wq[..., MLA_NOPE:]
    tail = z(MLA_Q_LORA, MLA_HEADS, LANES - MLA_NOPE - MLA_ROPE)
    qa = jnp.concatenate([nope, rp, tail], axis=-1).reshape(MLA_Q_LORA, _QW)
    qb = jnp.concatenate([jnp.zeros_like(nope), _rot_half_cols(rp), tail], axis=-1).reshape(MLA_Q_LORA, _QW)
    w_uq_ext = jnp.concatenate([qa, qb], axis=1).astype(BF16)

    wkv = w_ukv.reshape(MLA_KV_LORA, MLA_HEADS, MLA_NOPE + MLA_V)
    k_nope, v = wkv[..., :MLA_NOPE], wkv[..., MLA_NOPE:]
    w_k = jnp.concatenate([k_nope, jnp.zeros_like(k_nope)], axis=-1).reshape(MLA_KV_LORA, _QW).astype(BF16)
    v = v.reshape(MLA_KV_LORA, MLA_HEADS // 2, 2, MLA_V)
    zero = jnp.zeros_like(v[:, :, 0])
    w_v = jnp.stack([jnp.concatenate([v[:, :, 0], zero], axis=-1),
                     jnp.concatenate([zero, v[:, :, 1]], axis=-1)], axis=2)
    w_v = w_v.reshape(MLA_KV_LORA, _QW).astype(BF16)
    return w_in_ext, w_uq_ext, w_k, w_v


def _rope_tables(seq):
    half = MLA_ROPE // 2
    inv_freq = ROPE_THETA ** (-jnp.arange(half, dtype=F32) / half)
    ang = jnp.arange(seq).astype(F32)[:, None] * inv_freq[None, :]
    cos, sin = jnp.cos(ang), jnp.sin(ang)
    ones, zeros = jnp.ones((seq, MLA_NOPE), F32), jnp.zeros((seq, MLA_NOPE), F32)
    tail = jnp.zeros((seq, LANES - MLA_NOPE - MLA_ROPE), F32)
    scale = (MLA_NOPE + MLA_ROPE) ** -0.5 * LOG2_E
    cq = jnp.concatenate([ones, cos, cos, tail], axis=1) * scale
    sq = jnp.concatenate([zeros, sin, sin, tail], axis=1) * scale
    ck = jnp.concatenate([zeros, cos, cos, tail], axis=1)
    sk = jnp.concatenate([zeros, sin, sin, tail], axis=1)
    return cq, sq, ck, sk


def kernel(x, mix_pre_g, mix_post_g, ffn_pre_g, ffn_post_g, ab_w_in, mla_q_norm_g, mla_w_uq,
           mla_kv_norm_g, mla_w_ukv, ab_w_out, rel_bias, sb_w_qkv, sb_w_out, ffn_w_gate_up, ffn_w_down):
    b, s, d = x.shape
    n = b * s
    h = x.reshape(n, d)
    tables = _rope_tables(s)
    bias_tables = _moba_bias_tables(rel_bias)
    for layer in range(DEPTH):
        if layer % 2 == 0:
            e = layer // 2
            w_in, w_uq, w_k, w_v = _prep_even_weights(ab_w_in[e], mla_w_uq[e], mla_w_ukv[e])
            q, k, v, moba_qkv = _even_in(h, mix_pre_g[layer], w_in, mla_q_norm_g[e], w_uq,
                                         mla_kv_norm_g[e], w_k, w_v, tables, s)
            mla_out = _mla_attention(q.reshape(b, s, -1), k.reshape(b, s, -1), v.reshape(b, s, -1))
            moba_out = _moba_attention(moba_qkv.reshape(b, s, -1), bias_tables)
            mixed, w_out = [mla_out.reshape(n, -1), moba_out.reshape(n, -1)], ab_w_out[e]
        else:
            o = layer // 2
            width = SB_HEADS * HEAD_DIM
            qkv = _norm_matmul(h, mix_pre_g[layer], sb_w_qkv[o].astype(BF16),
                               width, HEAD_DIM ** -0.5 * LOG2_E)
            mixed, w_out = [_sb_attention(qkv.reshape(b, s, -1)).reshape(n, -1)], sb_w_out[o]
        h = _mix_ffn(mixed, h, w_out.astype(BF16), mix_post_g[layer], ffn_pre_g[layer],
                     ffn_w_gate_up[layer].astype(BF16), ffn_w_down[layer].astype(BF16),
                     ffn_post_g[layer])
    return h.reshape(b, s, d)
```

```python
import functools
import math

import numpy as np
import jax
import jax.numpy as jnp
from jax import lax
from jax.experimental import pallas as pl
from jax.experimental.pallas import tpu as pltpu

D_MODEL = 1024
DEPTH = 4
HEAD_DIM = 64
MLA_HEADS = 8
MLA_NOPE = 64
MLA_ROPE = 32
MLA_V = 64
MLA_Q_LORA = 384
MLA_KV_LORA = 256
ROPE_THETA = 10000.0
MOBA_HEADS = 8
MOBA_BLOCK = 256
MOBA_TOPK = 3
SB_HEADS = 16
NUM_BUCKETS = 32
MAX_DISTANCE = 128
RMS_EPS = 1e-6
D_FF = 2816

LANES = 128
TQ = 256
TM = 512
TM_FFN = 512
VMEM_LIMIT_FFN = 48 * 1024 * 1024
FF_CHUNK = 256
MASK_PENALTY = -1e30
LOG2_E = 1.0 / math.log(2.0)
VMEM_LIMIT = 48 * 1024 * 1024

F32 = jnp.float32
BF16 = jnp.bfloat16


def _rms(x, g):
    return x * lax.rsqrt(jnp.mean(x * x, axis=-1, keepdims=True) + RMS_EPS) * g


def _dot(a, b):
    return jnp.dot(a, b, preferred_element_type=F32)


def _dot_nt(a, b):
    return lax.dot_general(a, b, (((1,), (1,)), ((), ())), preferred_element_type=F32)


def _params(*semantics):
    return pltpu.CompilerParams(dimension_semantics=semantics, vmem_limit_bytes=VMEM_LIMIT)


def _whole(shape):
    nd = len(shape)
    return pl.BlockSpec(shape, lambda *_: (0,) * nd, pipeline_mode=pl.Buffered(1))


def _norm_matmul_kernel(h_ref, g_ref, w_ref, o_ref, *, q_cols, q_scale):
    u = _rms(h_ref[...], g_ref[...]).astype(BF16)
    for c in range(0, o_ref.shape[-1], 512):
        y = _dot(u, w_ref[:, c:c + 512])
        if c < q_cols:
            y = y * q_scale
        o_ref[:, c:c + 512] = y.astype(o_ref.dtype)


def _norm_matmul(h, g, w, q_cols, q_scale):
    n, d = h.shape
    n_out = w.shape[1]
    return pl.pallas_call(
        functools.partial(_norm_matmul_kernel, q_cols=q_cols, q_scale=q_scale),
        grid=(n // TM,),
        in_specs=[pl.BlockSpec((TM, d), lambda t: (t, 0)), _whole((1, d)), _whole(w.shape)],
        out_specs=pl.BlockSpec((TM, n_out), lambda t: (t, 0)),
        out_shape=jax.ShapeDtypeStruct((n, n_out), BF16),
        compiler_params=_params("arbitrary"),
        name="norm_matmul",
    )(h, g.reshape(1, d), w)


def _head_lanes(shape, hh):
    lane = lax.broadcasted_iota(jnp.int32, shape, 1)
    return (lane >= hh * HEAD_DIM) & (lane < (hh + 1) * HEAD_DIM)


def _spare_lane_base(hh):
    return HEAD_DIM * (1 - hh)


def _normalized(acc, hh):
    base = _spare_lane_base(hh)
    return jnp.where(_head_lanes(acc.shape, hh), acc / acc[:, base:base + 1], 0.0)


_O_CQ = 0
_O_CKV = MLA_Q_LORA
_O_KR = _O_CKV + MLA_KV_LORA
_O_KRR = _O_KR + LANES
_O_MOBA = _O_KRR + LANES
_W_IN_EXT = _O_MOBA + 3 * MOBA_HEADS * HEAD_DIM
_QW = MLA_HEADS * LANES


def _even_in_kernel(h_ref, g_ref, w_in_ref, gq_ref, w_uq_ref, gkv_ref, w_k_ref, w_v_ref,
                    cq_ref, sq_ref, ck_ref, sk_ref, q_ref, k_ref, v_ref, moba_ref):
    u = _rms(h_ref[...], g_ref[...]).astype(BF16)
    c_q = _dot(u, w_in_ref[:, _O_CQ:_O_CKV])
    c_kv = _dot(u, w_in_ref[:, _O_CKV:_O_KR])
    kr_both = _dot(u, w_in_ref[:, _O_KR:_O_MOBA])
    kr, krr = kr_both[:, :LANES], kr_both[:, LANES:]
    hw = MOBA_HEADS * HEAD_DIM
    for c in range(0, 3 * hw, hw):
        y = _dot(u, w_in_ref[:, _O_MOBA + c:_O_MOBA + c + hw])
        if c == 0:
            y = y * LOG2_E
        moba_ref[:, c:c + hw] = y.astype(BF16)

    cqn = _rms(c_q, gq_ref[...]).astype(BF16)
    qa = _dot(cqn, w_uq_ref[...])
    qb = pltpu.roll(qa, _QW - MLA_ROPE, axis=1)
    cq, sq = cq_ref[...], sq_ref[...]
    for h in range(MLA_HEADS):
        s = slice(h * LANES, (h + 1) * LANES)
        q_ref[:, s] = (qa[:, s] * cq + qb[:, s] * sq).astype(BF16)

    ckvn = _rms(c_kv, gkv_ref[...]).astype(BF16)
    kn = _dot(ckvn, w_k_ref[...])
    v = _dot(ckvn, w_v_ref[...])
    lane = lax.broadcasted_iota(jnp.int32, v.shape, 1)
    spare = jnp.where((lane // LANES) % 2 == 0, _spare_lane_base(0), _spare_lane_base(1))
    v_ref[...] = jnp.where(lane % LANES == spare, 1.0, v).astype(BF16)
    k_rope = kr * ck_ref[...] + krr * sk_ref[...]
    for h in range(MLA_HEADS):
        s = slice(h * LANES, (h + 1) * LANES)
        k_ref[:, s] = (kn[:, s] + k_rope).astype(BF16)


def _even_in(h, g, w_in, gq, w_uq, gkv, w_k, w_v, tables, seq):
    n, d = h.shape
    row = lambda t: (t, 0)
    pos = lambda t: (t % (seq // TM), 0)
    tab = pl.BlockSpec((TM, LANES), pos)
    outs = [(_QW, "q"), (_QW, "k"), (_QW, "v"), (3 * MOBA_HEADS * HEAD_DIM, "moba")]
    return pl.pallas_call(
        _even_in_kernel,
        grid=(n // TM,),
        in_specs=[pl.BlockSpec((TM, d), row), _whole((1, d)), _whole(w_in.shape),
                  _whole((1, MLA_Q_LORA)), _whole(w_uq.shape), _whole((1, MLA_KV_LORA)),
                  _whole(w_k.shape), _whole(w_v.shape), tab, tab, tab, tab],
        out_specs=[pl.BlockSpec((TM, w), row) for w, _ in outs],
        out_shape=[jax.ShapeDtypeStruct((n, w), BF16) for w, _ in outs],
        compiler_params=_params("arbitrary"),
        name="even_in",
    )(h, g.reshape(1, d), w_in, gq.reshape(1, -1), w_uq, gkv.reshape(1, -1), w_k, w_v, *tables)


def _two_block_attention(q, k_of, v_of, r0, add_near):
    end = r0 + 2 * TQ
    kd, vd = k_of(r0, end), v_of(r0, end)
    s_a = add_near(_dot_nt(q[:TQ], kd[:TQ]), 0)
    s_b = _dot_nt(q[TQ:], kd)
    s_b = jnp.concatenate([add_near(s_b[:, :TQ], 1), add_near(s_b[:, TQ:], 0)], axis=1)
    if r0 > 0:
        s_off = _dot_nt(q, k_of(0, r0))
        s_a = jnp.concatenate([s_off[:TQ, :r0 - TQ], add_near(s_off[:TQ, r0 - TQ:], 1), s_a], axis=1)
        s_b = jnp.concatenate([s_off[TQ:], s_b], axis=1)
    p_a = jnp.exp2(s_a - jnp.max(s_a, axis=-1, keepdims=True)).astype(BF16)
    p_b = jnp.exp2(s_b - jnp.max(s_b, axis=-1, keepdims=True)).astype(BF16)
    acc = jnp.concatenate([_dot(p_a[:, r0:], vd[:TQ]), _dot(p_b[:, r0:], vd)], axis=0)
    if r0 > 0:
        acc = acc + _dot(jnp.concatenate([p_a[:, :r0], p_b[:, :r0]], axis=0), v_of(0, r0))
    return acc


def _mla_kernel(q_ref, k_ref, v_ref, o_ref):
    s_len = q_ref.shape[1]
    row = lax.broadcasted_iota(jnp.int32, (TQ, TQ), 0)
    col = lax.broadcasted_iota(jnp.int32, (TQ, TQ), 1)
    causal = col <= row
    add_near = lambda s, d: jnp.where(causal, s, -jnp.inf) if d == 0 else s
    for r0 in range(0, s_len, 2 * TQ):
        rows = slice(r0, r0 + 2 * TQ)
        out = jnp.zeros((2 * TQ, LANES), F32)
        for hh in range(2):
            lanes = slice(hh * LANES, (hh + 1) * LANES)
            acc = _two_block_attention(q_ref[0, rows, lanes],
                                       lambda a, b: k_ref[0, a:b, lanes],
                                       lambda a, b: v_ref[0, a:b, lanes], r0, add_near)
            out = out + _normalized(acc, hh)
        o_ref[0, rows, :] = out.astype(o_ref.dtype)


def _mla_attention(q, k, v):
    b, s, _ = q.shape
    pairs = MLA_HEADS // 2
    qkv = pl.BlockSpec((1, s, 2 * LANES), lambda bi, p: (bi, 0, p))
    return pl.pallas_call(
        _mla_kernel,
        grid=(b, pairs),
        in_specs=[qkv, qkv, qkv],
        out_specs=pl.BlockSpec((1, s, LANES), lambda bi, p: (bi, 0, p)),
        out_shape=jax.ShapeDtypeStruct((b, s, MLA_HEADS * MLA_V), BF16),
        compiler_params=_params("arbitrary", "arbitrary"),
        name="mla_attention",
    )(q, k, v)


_SOFTPLUS2_CLAMP = 64.0


def _softplus2(z):
    return jnp.maximum(z, jnp.log2(1.0 + jnp.exp2(jnp.minimum(z, _SOFTPLUS2_CLAMP))))


SB_UNDERFLOW = 160.0
SB_NEAR_BLOCKS = 2
SB_PROBE_STEPS = 4


def _sb_kernel(q_ref, k_ref, v_ref, o_ref, qm_scr, vm_scr, acc_scr, carry_scr, limit_scr, plan_scr):
    s_len = q_ref.shape[1]
    n_blk = s_len // TQ
    q, k, v = q_ref[0], k_ref[0], v_ref[0]
    feat = lax.broadcasted_iota(jnp.int32, (LANES, LANES), 0) // HEAD_DIM
    lane = lax.broadcasted_iota(jnp.int32, (LANES, LANES), 1) // HEAD_DIM
    same_head = jnp.where(feat == lane, 1.0, 0.0).astype(BF16)
    q_max2 = jnp.max(_dot(q * q, same_head), axis=0, keepdims=True)
    k_max2 = jnp.max(_dot(k * k, same_head), axis=0, keepdims=True)
    z_bound = jnp.sqrt(q_max2 * k_max2) * 1.02
    for hh in range(2):
        in_head = _head_lanes((1, LANES), hh)
        lane_mask = jnp.where(in_head, 1.0, 0.0).astype(BF16)
        qm_scr[hh] = q * lane_mask
        vm_scr[hh] = v * lane_mask
        bound = jnp.max(jnp.where(in_head, z_bound, 0.0), axis=1, keepdims=True)
        limit_scr[hh] = jnp.broadcast_to(bound + SB_UNDERFLOW, limit_scr.shape[1:])
    acc_scr[...] = jnp.zeros(acc_scr.shape, F32)

    row = lax.broadcasted_iota(jnp.int32, (TQ, TQ), 0)
    col = lax.broadcasted_iota(jnp.int32, (TQ, TQ), 1)
    strict = col < row
    tri = jnp.where(row >= col, 1.0, 0.0).astype(BF16)

    def head(hh, _):
        def tile_rows(j, r0, r1):
            keys = slice(j * TQ, (j + 1) * TQ)
            rows = slice(r0 * TQ, r1 * TQ)
            z = _dot_nt(qm_scr[hh, rows, :], k_ref[0, keys, :])
            sp = _softplus2(z)
            if r0 == j:
                sp_diag = jnp.where(strict, sp[:TQ], 0.0)
                sp = sp_diag if r1 == r0 + 1 else jnp.concatenate([sp_diag, sp[TQ:]], axis=0)
            c = _dot(sp.astype(BF16), tri)
            tot = jnp.broadcast_to(c[:, 0:1], (c.shape[0], LANES))
            first = TQ if r0 == j else 0
            w = []
            if r0 == j:
                w.append(jnp.where(strict, jnp.exp2(z[:TQ] - c[:TQ]), 0.0))
                carry_scr[keys, :] = tot[:TQ]
            if r1 * TQ - r0 * TQ > first:
                below = slice(r0 * TQ + first, r1 * TQ)
                carry = carry_scr[below, :]
                w.append(jnp.exp2(z[first:] - c[first:] - jnp.concatenate([carry, carry], axis=1)))
                carry_scr[below, :] = carry + tot[first:]
            w = (w[0] if len(w) == 1 else jnp.concatenate(w, axis=0)).astype(BF16)
            acc_scr[rows, :] += _dot(w, vm_scr[hh, keys, :])

        def near_tiles(n_near):
            for j in range(n_blk - 1, -1, -1):
                tile_rows(j, j, min(j + n_near, n_blk))

        def live(r0, r1):
            least = jnp.min(carry_scr[r0 * TQ:r1 * TQ, :], axis=0, keepdims=True)
            return jnp.min(least - limit_scr[hh, 0:1, :]) < 0.0

        def diagonal(dist):
            for i in range(dist, n_blk):
                tile_rows(i - dist, i, i + 1)

        wide = plan_scr[0] == 1

        @pl.when(jnp.logical_not(wide))
        def _():
            near_tiles(SB_NEAR_BLOCKS)
            flags = [live(i, i + 1) for i in range(SB_NEAR_BLOCKS, n_blk)]
            n_live = sum(f.astype(jnp.int32) for f in flags)
            plan_scr[1] += (n_live >= 2).astype(jnp.int32)
            plan_scr[2] = n_live

            @pl.when(n_live >= 2)
            def _():
                diagonal(SB_NEAR_BLOCKS)

            @pl.when(n_live == 1)
            def _():
                for i, f in zip(range(SB_NEAR_BLOCKS, n_blk), flags):
                    pl.when(f)(functools.partial(tile_rows, i - SB_NEAR_BLOCKS, i, i + 1))

        @pl.when(wide)
        def _():
            near_tiles(SB_NEAR_BLOCKS + 1)
            plan_scr[2] = 1

        def farther(dist):
            @pl.when(live(dist, n_blk))
            def _():
                diagonal(dist)
                if dist + 1 < n_blk:
                    farther(dist + 1)

        if SB_NEAR_BLOCKS + 1 < n_blk:
            pl.when(plan_scr[2] > 0)(functools.partial(farther, SB_NEAR_BLOCKS + 1))
        return 0

    step = pl.program_id(0) * pl.num_programs(1) + pl.program_id(1)

    @pl.when(step == 0)
    def _():
        plan_scr[0] = 0
        plan_scr[1] = 0

    lax.fori_loop(0, 2, head, 0)

    @pl.when(step == SB_PROBE_STEPS - 1)
    def _():
        plan_scr[0] = (plan_scr[1] >= SB_PROBE_STEPS).astype(jnp.int32)

    o_ref[0] = acc_scr[...].astype(o_ref.dtype)


def _sb_attention(qkv):
    b, s, _ = qkv.shape
    pairs = SB_HEADS // 2
    return pl.pallas_call(
        _sb_kernel,
        grid=(b, pairs),
        in_specs=[pl.BlockSpec((1, s, LANES), lambda bi, p: (bi, 0, p)),
                  pl.BlockSpec((1, s, LANES), lambda bi, p: (bi, 0, pairs + p)),
                  pl.BlockSpec((1, s, LANES), lambda bi, p: (bi, 0, 2 * pairs + p))],
        out_specs=pl.BlockSpec((1, s, LANES), lambda bi, p: (bi, 0, p)),
        out_shape=jax.ShapeDtypeStruct((b, s, SB_HEADS * HEAD_DIM), BF16),
        scratch_shapes=[pltpu.VMEM((2, s, LANES), BF16), pltpu.VMEM((2, s, LANES), BF16),
                        pltpu.VMEM((s, LANES), F32), pltpu.VMEM((s, LANES), F32),
                        pltpu.VMEM((2, 8, LANES), F32), pltpu.SMEM((3,), jnp.int32)],
        compiler_params=_params("arbitrary", "arbitrary"),
        name="sb_attention",
    )(qkv, qkv, qkv)


def _t5_bucket_np(rel):
    max_exact = NUM_BUCKETS // 2
    rel = np.maximum(rel, 0)
    large = max_exact + (np.log(np.maximum(rel, max_exact) / max_exact)
                         / math.log(MAX_DISTANCE / max_exact) * (NUM_BUCKETS - max_exact)).astype(np.int64)
    return np.where(rel < max_exact, rel, np.minimum(large, NUM_BUCKETS - 1)).astype(np.int32)


def _bias_kernel(bucket_ref, rel_bias_ref, o_ref):
    h = pl.program_id(0)
    for plane in range(2):
        bk = bucket_ref[plane]
        acc = jnp.full(bk.shape, -jnp.inf, F32)
        for b in range(NUM_BUCKETS):
            acc = jnp.where(bk == b, rel_bias_ref[b, h] * LOG2_E, acc)
        o_ref[0, plane] = acc
    o_ref[0, 2] = jnp.full((TQ, TQ), rel_bias_ref[NUM_BUCKETS - 1, h] * LOG2_E, F32)


def _moba_bias_tables(rel_bias):
    off = np.arange(TQ)
    rel_own = off[:, None] - off[None, :]
    own = np.where(rel_own >= 0, _t5_bucket_np(rel_own), -1)
    adj = _t5_bucket_np(rel_own + MOBA_BLOCK)
    buckets = jnp.asarray(np.stack([own, adj]).astype(np.int32))
    return pl.pallas_call(
        _bias_kernel,
        grid=(MOBA_HEADS,),
        in_specs=[_whole(buckets.shape), pl.BlockSpec(memory_space=pltpu.SMEM)],
        out_specs=pl.BlockSpec((1, 3, TQ, TQ), lambda h: (h, 0, 0, 0)),
        out_shape=jax.ShapeDtypeStruct((MOBA_HEADS, 3, TQ, TQ), F32),
        compiler_params=_params("arbitrary"),
        name="moba_bias_tables",
    )(buckets, rel_bias)


_N_BLK_LANES = 8


def _moba_constants(nb):
    assert nb <= _N_BLK_LANES
    diff = np.zeros((LANES, LANES), np.float32)
    count = np.zeros((LANES, LANES), np.float32)
    for hh in range(2):
        for a in range(nb):
            for b in range(nb):
                if a != b:
                    pair = hh * HEAD_DIM + a * 8 + b
                    diff[pair, a] += 1.0
                    diff[pair, b] -= 1.0
                    count[pair, _spare_lane_base(hh) + b] = 1.0
                    count[pair, _spare_lane_base(hh) + _N_BLK_LANES + b] = 1.0
    return jnp.asarray(diff, BF16), jnp.asarray(count, BF16)


_T_KIND, _T_FAR_HI, _T_FAR_LO, _T_FUTURE, _T_UNSEL, _N_HEAD_TABLES = 0, 1, 2, 3, 4, 5


def _moba_fill_tables(head_tab, gate_tab, s_len):
    lane = lax.broadcasted_iota(jnp.int32, (s_len, LANES), 1)
    row_blk = lax.broadcasted_iota(jnp.int32, (s_len, LANES), 0) // MOBA_BLOCK
    for hh in range(2):
        base = _spare_lane_base(hh)
        in_hi = (lane >= base) & (lane < base + _N_BLK_LANES)
        in_lo = (lane >= base + _N_BLK_LANES) & (lane < base + 2 * _N_BLK_LANES)
        blk = jnp.where(in_lo, lane - base - _N_BLK_LANES, lane - base)
        far = blk < row_blk - 1
        store = lambda t, x: head_tab.__setitem__((hh, t), x.astype(BF16))
        store(_T_KIND, jnp.where(in_hi | in_lo, jnp.where(blk == row_blk, 1.0, 0.0), -1.0))
        store(_T_FAR_HI, jnp.where(in_hi & far, 1.0, 0.0))
        store(_T_FAR_LO, jnp.where(in_lo & far, 1.0, 0.0))
        store(_T_FUTURE, jnp.where(in_hi & (blk > row_blk), MASK_PENALTY, 0.0))
        store(_T_UNSEL, jnp.where(in_hi & (blk != row_blk), MASK_PENALTY, 0.0))
    pair_a, pair_b = (lane % HEAD_DIM) // 8, lane % 8
    valid = pair_a < row_blk
    gate_tab[0] = jnp.where(valid, 1.0, 0.0)
    gate_tab[1] = jnp.where(valid & (pair_a < pair_b), 1.0, 0.0)


def _moba_kernel(q_ref, k_ref, v_ref, bias_ref, diff_ref, count_ref, o_ref,
                 qaug_scr, kaug_scr, vm_scr, head_tab, gate_tab, *, nb):
    s_len = q_ref.shape[1]

    @pl.when((pl.program_id(0) == 0) & (pl.program_id(1) == 0))
    def _():
        _moba_fill_tables(head_tab, gate_tab, s_len)

    q, k, v = q_ref[0], k_ref[0], v_ref[0]
    kbar = jnp.mean(k.astype(F32).reshape(nb, MOBA_BLOCK, LANES), axis=1)
    kbar = jnp.concatenate([kbar, jnp.zeros((LANES - nb, LANES), F32)], axis=0)
    p0 = kbar.astype(BF16)
    r1 = kbar - p0.astype(F32)
    p1 = r1.astype(BF16)
    p2 = (r1 - p1.astype(F32)).astype(BF16)
    d = diff_ref[...]
    kdiff = _dot(d, p0) + _dot(d, p1) + _dot(d, p2)
    pair_head = lax.broadcasted_iota(jnp.int32, (LANES, LANES), 0) // HEAD_DIM
    feat_head = lax.broadcasted_iota(jnp.int32, (LANES, LANES), 1) // HEAD_DIM
    kdiff = jnp.where(pair_head == feat_head, kdiff, 0.0).astype(BF16)

    g = _dot_nt(q, kdiff)
    beats = jnp.where(g > 0.0, gate_tab[0], jnp.where(g == 0.0, gate_tab[1], 0.0))
    rank = _dot(beats.astype(BF16), count_ref[...]).astype(BF16)
    top = rank < float(MOBA_TOPK)

    lane = lax.broadcasted_iota(jnp.int32, (1, LANES), 1)
    for hh in range(2):
        lane_mask = jnp.where(_head_lanes((1, LANES), hh), 1.0, 0.0).astype(BF16)
        ones_lane = jnp.where(lane == _spare_lane_base(hh), 1.0, 0.0).astype(BF16)
        vm_scr[hh] = v * lane_mask + ones_lane
        kind = head_tab[hh, _T_KIND]
        spare = kind >= 0.0
        kaug_scr[hh] = jnp.where(spare, kind, k)
        far = bias_ref[hh, 2, 0:1, 0:LANES]
        far_hi = far.astype(BF16)
        far_lo = (far - far_hi.astype(F32)).astype(BF16)
        chosen_val = head_tab[hh, _T_FAR_HI] * far_hi + head_tab[hh, _T_FAR_LO] * far_lo + head_tab[hh, _T_FUTURE]
        qaug_scr[hh] = jnp.where(spare, jnp.where(top, chosen_val, head_tab[hh, _T_UNSEL]), q * lane_mask)

    for r0 in range(0, s_len, 2 * TQ):
        rows = slice(r0, r0 + 2 * TQ)
        out = jnp.zeros((2 * TQ, LANES), F32)
        for hh in range(2):
            acc = _two_block_attention(qaug_scr[hh, rows, :],
                                       lambda a, b: kaug_scr[hh, a:b, :],
                                       lambda a, b: vm_scr[hh, a:b, :], r0,
                                       lambda s, d: s + bias_ref[hh, d])
            out = out + _normalized(acc, hh)
        o_ref[0, rows, :] = out.astype(o_ref.dtype)


def _moba_attention(qkv, bias_tables):
    b, s, _ = qkv.shape
    nb = s // MOBA_BLOCK
    pairs = MOBA_HEADS // 2
    diff, count = _moba_constants(nb)
    return pl.pallas_call(
        functools.partial(_moba_kernel, nb=nb),
        grid=(b, pairs),
        in_specs=[pl.BlockSpec((1, s, LANES), lambda bi, p: (bi, 0, p)),
                  pl.BlockSpec((1, s, LANES), lambda bi, p: (bi, 0, pairs + p)),
                  pl.BlockSpec((1, s, LANES), lambda bi, p: (bi, 0, 2 * pairs + p)),
                  pl.BlockSpec((2, 3, TQ, TQ), lambda bi, p: (p, 0, 0, 0)),
                  _whole(diff.shape), _whole(count.shape)],
        out_specs=pl.BlockSpec((1, s, LANES), lambda bi, p: (bi, 0, p)),
        out_shape=jax.ShapeDtypeStruct((b, s, MOBA_HEADS * HEAD_DIM), BF16),
        scratch_shapes=[pltpu.VMEM((2, s, LANES), BF16)] * 3
        + [pltpu.VMEM((2, _N_HEAD_TABLES, s, LANES), BF16), pltpu.VMEM((2, s, LANES), F32)],
        compiler_params=_params("arbitrary", "arbitrary"),
        name="moba_attention",
    )(qkv, qkv, qkv, bias_tables, diff, count)


def _mix_ffn_kernel(*refs, n_in):
    a_refs = refs[:n_in]
    h_ref, w_out_ref, gmix_ref, gpre_ref, wgu_ref, wd_ref, gpost_ref, o_ref, act_scr = refs[n_in:]
    mixed, k0 = None, 0
    for a_ref in a_refs:
        k1 = k0 + a_ref.shape[1]
        part = _dot(a_ref[...], w_out_ref[k0:k1, :])
        mixed, k0 = part if mixed is None else mixed + part, k1
    x = h_ref[...] + _rms(mixed, gmix_ref[...])
    u = _rms(x, gpre_ref[...]).astype(BF16)
    for c in range(0, D_FF, FF_CHUNK):
        gate = _dot(u, wgu_ref[:, c:c + FF_CHUNK])
        up = _dot(u, wgu_ref[:, D_FF + c:D_FF + c + FF_CHUNK])
        act_scr[:, c:c + FF_CHUNK] = (gate * jax.nn.sigmoid(gate) * up).astype(BF16)
    f = _dot(act_scr[...], wd_ref[...])
    o_ref[...] = x + _rms(f, gpost_ref[...])


def _mix_ffn(acts, h, w_out, gmix, gpre, wgu, wd, gpost):
    n, d = h.shape
    row = lambda t: (t, 0)
    vec = _whole((1, d))
    return pl.pallas_call(
        functools.partial(_mix_ffn_kernel, n_in=len(acts)),
        grid=(n // TM_FFN,),
        in_specs=([pl.BlockSpec((TM_FFN, a.shape[1]), row) for a in acts]
                  + [pl.BlockSpec((TM_FFN, d), row), _whole(w_out.shape), vec, vec,
                     _whole(wgu.shape), _whole(wd.shape), vec]),
        out_specs=pl.BlockSpec((TM_FFN, d), row),
        out_shape=jax.ShapeDtypeStruct((n, d), F32),
        scratch_shapes=[pltpu.VMEM((TM_FFN, D_FF), BF16)],
        compiler_params=pltpu.CompilerParams(dimension_semantics=("arbitrary",),
                                             vmem_limit_bytes=VMEM_LIMIT_FFN),
        name="mix_ffn",
    )(*acts, h, w_out, gmix.reshape(1, d), gpre.reshape(1, d), wgu, wd, gpost.reshape(1, d))


def _rot_half_cols(w):
    half = w.shape[-1] // 2
    return jnp.concatenate([-w[..., half:], w[..., :half]], axis=-1)


def _prep_even_weights(w_in, w_uq, w_ukv):
    d = w_in.shape[0]
    o1, o2, o3 = MLA_Q_LORA, MLA_Q_LORA + MLA_KV_LORA, MLA_Q_LORA + MLA_KV_LORA + MLA_ROPE
    hw = MOBA_HEADS * HEAD_DIM
    z = lambda *shape: jnp.zeros(shape, F32)
    kr = w_in[:, o2:o3]
    w_in_ext = jnp.concatenate(
        [w_in[:, :o2],
         z(d, MLA_NOPE), kr, z(d, LANES - MLA_NOPE - MLA_ROPE),
         z(d, MLA_NOPE), _rot_half_cols(kr), z(d, LANES - MLA_NOPE - MLA_ROPE),
         w_in[:, o3:o3 + hw] * (HEAD_DIM ** -0.5), w_in[:, o3 + hw:]], axis=1).astype(BF16)

    wq = w_uq.reshape(MLA_Q_LORA, MLA_HEADS, MLA_NOPE + MLA_ROPE)
    nope, rp = wq[..., :MLA_NOPE], wq[..., MLA_NOPE:]
    assert MLA_NOPE + 2 * MLA_ROPE == LANES
    w_uq_ext = jnp.concatenate([nope, rp, _rot_half_cols(rp)], axis=-1).reshape(MLA_Q_LORA, _QW).astype(BF16)

    wkv = w_ukv.reshape(MLA_KV_LORA, MLA_HEADS, MLA_NOPE + MLA_V)
    k_nope, v = wkv[..., :MLA_NOPE], wkv[..., MLA_NOPE:]
    w_k = jnp.concatenate([k_nope, jnp.zeros_like(k_nope)], axis=-1).reshape(MLA_KV_LORA, _QW).astype(BF16)
    v = v.reshape(MLA_KV_LORA, MLA_HEADS // 2, 2, MLA_V)
    zero = jnp.zeros_like(v[:, :, 0])
    w_v = jnp.stack([jnp.concatenate([v[:, :, 0], zero], axis=-1),
                     jnp.concatenate([zero, v[:, :, 1]], axis=-1)], axis=2)
    w_v = w_v.reshape(MLA_KV_LORA, _QW).astype(BF16)
    return w_in_ext, w_uq_ext, w_k, w_v


def _rope_tables(seq):
    half = MLA_ROPE // 2
    inv_freq = ROPE_THETA ** (-jnp.arange(half, dtype=F32) / half)
    ang = jnp.arange(seq).astype(F32)[:, None] * inv_freq[None, :]
    cos, sin = jnp.cos(ang), jnp.sin(ang)
    ones, zeros = jnp.ones((seq, MLA_NOPE), F32), jnp.zeros((seq, MLA_NOPE), F32)
    tail = jnp.zeros((seq, LANES - MLA_NOPE - MLA_ROPE), F32)
    scale = (MLA_NOPE + MLA_ROPE) ** -0.5 * LOG2_E
    cq = jnp.concatenate([ones, cos, cos, tail], axis=1) * scale
    sq = jnp.concatenate([zeros, sin, sin, tail], axis=1) * scale
    ck = jnp.concatenate([zeros, cos, cos, tail], axis=1)
    sk = jnp.concatenate([zeros, sin, sin, tail], axis=1)
    return cq, sq, ck, sk


def kernel(x, mix_pre_g, mix_post_g, ffn_pre_g, ffn_post_g, ab_w_in, mla_q_norm_g, mla_w_uq,
           mla_kv_norm_g, mla_w_ukv, ab_w_out, rel_bias, sb_w_qkv, sb_w_out, ffn_w_gate_up, ffn_w_down):
    b, s, d = x.shape
    n = b * s
    h = x.reshape(n, d)
    tables = _rope_tables(s)
    bias_tables = _moba_bias_tables(rel_bias)
    for layer in range(DEPTH):
        if layer % 2 == 0:
            e = layer // 2
            w_in, w_uq, w_k, w_v = _prep_even_weights(ab_w_in[e], mla_w_uq[e], mla_w_ukv[e])
            q, k, v, moba_qkv = _even_in(h, mix_pre_g[layer], w_in, mla_q_norm_g[e], w_uq,
                                         mla_kv_norm_g[e], w_k, w_v, tables, s)
            mla_out = _mla_attention(q.reshape(b, s, -1), k.reshape(b, s, -1), v.reshape(b, s, -1))
            moba_out = _moba_attention(moba_qkv.reshape(b, s, -1), bias_tables)
            mixed, w_out = [mla_out.reshape(n, -1), moba_out.reshape(n, -1)], ab_w_out[e]
        else:
            o = layer // 2
            width = SB_HEADS * HEAD_DIM
            qkv = _norm_matmul(h, mix_pre_g[layer], sb_w_qkv[o].astype(BF16),
                               width, HEAD_DIM ** -0.5 * LOG2_E)
            mixed, w_out = [_sb_attention(qkv.reshape(b, s, -1)).reshape(n, -1)], sb_w_out[o]
        h = _mix_ffn(mixed, h, w_out.astype(BF16), mix_post_g[layer], ffn_pre_g[layer],
                     ffn_w_gate_up[layer].astype(BF16), ffn_w_down[layer].astype(BF16),
                     ffn_post_g[layer])
    return h.reshape(b, s, d)
```

```python
import functools
import math

import numpy as np
import jax
import jax.numpy as jnp
from jax import lax
from jax.experimental import pallas as pl
from jax.experimental.pallas import tpu as pltpu

D_MODEL = 1024
DEPTH = 4
HEAD_DIM = 64
MLA_HEADS = 8
MLA_NOPE = 64
MLA_ROPE = 32
MLA_V = 64
MLA_Q_LORA = 384
MLA_KV_LORA = 256
ROPE_THETA = 10000.0
MOBA_HEADS = 8
MOBA_BLOCK = 256
MOBA_TOPK = 3
SB_HEADS = 16
NUM_BUCKETS = 32
MAX_DISTANCE = 128
RMS_EPS = 1e-6
D_FF = 2816

LANES = 128
TQ = 256
TM = 512
TM_FFN = 512
VMEM_LIMIT_FFN = 48 * 1024 * 1024
FF_CHUNK = 256
MASK_PENALTY = -1e30
LOG2_E = 1.0 / math.log(2.0)
VMEM_LIMIT = 48 * 1024 * 1024

F32 = jnp.float32
BF16 = jnp.bfloat16


def _rms(x, g):
    return x * lax.rsqrt(jnp.mean(x * x, axis=-1, keepdims=True) + RMS_EPS) * g


def _dot(a, b):
    return jnp.dot(a, b, preferred_element_type=F32)


def _dot_nt(a, b):
    return lax.dot_general(a, b, (((1,), (1,)), ((), ())), preferred_element_type=F32)


def _params(*semantics):
    return pltpu.CompilerParams(dimension_semantics=semantics, vmem_limit_bytes=VMEM_LIMIT)


def _whole(shape):
    nd = len(shape)
    return pl.BlockSpec(shape, lambda *_: (0,) * nd, pipeline_mode=pl.Buffered(1))


def _norm_matmul_kernel(h_ref, g_ref, w_ref, o_ref, *, q_cols, q_scale):
    u = _rms(h_ref[...], g_ref[...]).astype(BF16)
    for c in range(0, o_ref.shape[-1], 512):
        y = _dot(u, w_ref[:, c:c + 512])
        if c < q_cols:
            y = y * q_scale
        o_ref[:, c:c + 512] = y.astype(o_ref.dtype)


def _norm_matmul(h, g, w, q_cols, q_scale):
    n, d = h.shape
    n_out = w.shape[1]
    return pl.pallas_call(
        functools.partial(_norm_matmul_kernel, q_cols=q_cols, q_scale=q_scale),
        grid=(n // TM,),
        in_specs=[pl.BlockSpec((TM, d), lambda t: (t, 0)), _whole((1, d)), _whole(w.shape)],
        out_specs=pl.BlockSpec((TM, n_out), lambda t: (t, 0)),
        out_shape=jax.ShapeDtypeStruct((n, n_out), BF16),
        compiler_params=_params("arbitrary"),
        name="norm_matmul",
    )(h, g.reshape(1, d), w)


def _head_lanes(shape, hh):
    lane = lax.broadcasted_iota(jnp.int32, shape, 1)
    return (lane >= hh * HEAD_DIM) & (lane < (hh + 1) * HEAD_DIM)


def _spare_lane_base(hh):
    return HEAD_DIM * (1 - hh)


def _normalized(acc, hh):
    base = _spare_lane_base(hh)
    return jnp.where(_head_lanes(acc.shape, hh), acc / acc[:, base:base + 1], 0.0)


_O_CQ = 0
_O_CKV = MLA_Q_LORA
_O_KR = _O_CKV + MLA_KV_LORA
_O_KRR = _O_KR + LANES
_O_MOBA = _O_KRR + LANES
_W_IN_EXT = _O_MOBA + 3 * MOBA_HEADS * HEAD_DIM
_QW = MLA_HEADS * LANES


def _even_in_kernel(h_ref, g_ref, w_in_ref, gq_ref, w_uq_ref, gkv_ref, w_k_ref, w_v_ref,
                    cq_ref, sq_ref, ck_ref, sk_ref, q_ref, k_ref, v_ref, moba_ref):
    u = _rms(h_ref[...], g_ref[...]).astype(BF16)
    c_q = _dot(u, w_in_ref[:, _O_CQ:_O_CKV])
    c_kv = _dot(u, w_in_ref[:, _O_CKV:_O_KR])
    kr_both = _dot(u, w_in_ref[:, _O_KR:_O_MOBA])
    kr, krr = kr_both[:, :LANES], kr_both[:, LANES:]
    hw = MOBA_HEADS * HEAD_DIM
    for c in range(0, 3 * hw, hw):
        y = _dot(u, w_in_ref[:, _O_MOBA + c:_O_MOBA + c + hw])
        if c == 0:
            y = y * LOG2_E
        moba_ref[:, c:c + hw] = y.astype(BF16)

    cqn = _rms(c_q, gq_ref[...]).astype(BF16)
    qa = _dot(cqn, w_uq_ref[...])
    qb = pltpu.roll(qa, _QW - MLA_ROPE, axis=1)
    cq, sq = cq_ref[...], sq_ref[...]
    for h in range(MLA_HEADS):
        s = slice(h * LANES, (h + 1) * LANES)
        q_ref[:, s] = (qa[:, s] * cq + qb[:, s] * sq).astype(BF16)

    ckvn = _rms(c_kv, gkv_ref[...]).astype(BF16)
    kn = _dot(ckvn, w_k_ref[...])
    v = _dot(ckvn, w_v_ref[...])
    lane = lax.broadcasted_iota(jnp.int32, v.shape, 1)
    spare = jnp.where((lane // LANES) % 2 == 0, _spare_lane_base(0), _spare_lane_base(1))
    v_ref[...] = jnp.where(lane % LANES == spare, 1.0, v).astype(BF16)
    k_rope = kr * ck_ref[...] + krr * sk_ref[...]
    for h in range(MLA_HEADS):
        s = slice(h * LANES, (h + 1) * LANES)
        k_ref[:, s] = (kn[:, s] + k_rope).astype(BF16)


def _even_in(h, g, w_in, gq, w_uq, gkv, w_k, w_v, tables, seq):
    n, d = h.shape
    row = lambda t: (t, 0)
    pos = lambda t: (t % (seq // TM), 0)
    tab = pl.BlockSpec((TM, LANES), pos)
    outs = [(_QW, "q"), (_QW, "k"), (_QW, "v"), (3 * MOBA_HEADS * HEAD_DIM, "moba")]
    return pl.pallas_call(
        _even_in_kernel,
        grid=(n // TM,),
        in_specs=[pl.BlockSpec((TM, d), row), _whole((1, d)), _whole(w_in.shape),
                  _whole((1, MLA_Q_LORA)), _whole(w_uq.shape), _whole((1, MLA_KV_LORA)),
                  _whole(w_k.shape), _whole(w_v.shape), tab, tab, tab, tab],
        out_specs=[pl.BlockSpec((TM, w), row) for w, _ in outs],
        out_shape=[jax.ShapeDtypeStruct((n, w), BF16) for w, _ in outs],
        compiler_params=_params("arbitrary"),
        name="even_in",
    )(h, g.reshape(1, d), w_in, gq.reshape(1, -1), w_uq, gkv.reshape(1, -1), w_k, w_v, *tables)


def _two_block_attention(q, k_of, v_of, r0, add_near):
    end = r0 + 2 * TQ
    kd, vd = k_of(r0, end), v_of(r0, end)
    s_a = add_near(_dot_nt(q[:TQ], kd[:TQ]), 0)
    s_b = _dot_nt(q[TQ:], kd)
    s_b = jnp.concatenate([add_near(s_b[:, :TQ], 1), add_near(s_b[:, TQ:], 0)], axis=1)
    if r0 > 0:
        s_off = _dot_nt(q, k_of(0, r0))
        s_a = jnp.concatenate([s_off[:TQ, :r0 - TQ], add_near(s_off[:TQ, r0 - TQ:], 1), s_a], axis=1)
        s_b = jnp.concatenate([s_off[TQ:], s_b], axis=1)
    p_a = jnp.exp2(s_a - jnp.max(s_a, axis=-1, keepdims=True)).astype(BF16)
    p_b = jnp.exp2(s_b - jnp.max(s_b, axis=-1, keepdims=True)).astype(BF16)
    acc = jnp.concatenate([_dot(p_a[:, r0:], vd[:TQ]), _dot(p_b[:, r0:], vd)], axis=0)
    if r0 > 0:
        acc = acc + _dot(jnp.concatenate([p_a[:, :r0], p_b[:, :r0]], axis=0), v_of(0, r0))
    return acc


def _mla_kernel(q_ref, k_ref, v_ref, o_ref):
    s_len = q_ref.shape[1]
    row = lax.broadcasted_iota(jnp.int32, (TQ, TQ), 0)
    col = lax.broadcasted_iota(jnp.int32, (TQ, TQ), 1)
    causal = col <= row
    add_near = lambda s, d: jnp.where(causal, s, -jnp.inf) if d == 0 else s
    for r0 in range(0, s_len, 2 * TQ):
        rows = slice(r0, r0 + 2 * TQ)
        out = jnp.zeros((2 * TQ, LANES), F32)
        for hh in range(2):
            lanes = slice(hh * LANES, (hh + 1) * LANES)
            acc = _two_block_attention(q_ref[0, rows, lanes],
                                       lambda a, b: k_ref[0, a:b, lanes],
                                       lambda a, b: v_ref[0, a:b, lanes], r0, add_near)
            out = out + _normalized(acc, hh)
        o_ref[0, rows, :] = out.astype(o_ref.dtype)


def _mla_attention(q, k, v):
    b, s, _ = q.shape
    pairs = MLA_HEADS // 2
    qkv = pl.BlockSpec((1, s, 2 * LANES), lambda bi, p: (bi, 0, p))
    return pl.pallas_call(
        _mla_kernel,
        grid=(b, pairs),
        in_specs=[qkv, qkv, qkv],
        out_specs=pl.BlockSpec((1, s, LANES), lambda bi, p: (bi, 0, p)),
        out_shape=jax.ShapeDtypeStruct((b, s, MLA_HEADS * MLA_V), BF16),
        compiler_params=_params("arbitrary", "arbitrary"),
        name="mla_attention",
    )(q, k, v)


_SOFTPLUS2_CLAMP = 64.0


def _softplus2(z):
    return jnp.maximum(z, jnp.log2(1.0 + jnp.exp2(jnp.minimum(z, _SOFTPLUS2_CLAMP))))


SB_UNDERFLOW = 160.0
SB_NEAR_BLOCKS = 2
SB_PROBE_STEPS = 4


def _sb_kernel(q_ref, k_ref, v_ref, o_ref, qm_scr, vm_scr, acc_scr, carry_scr, limit_scr, plan_scr):
    s_len = q_ref.shape[1]
    n_blk = s_len // TQ
    q, k, v = q_ref[0], k_ref[0], v_ref[0]
    feat = lax.broadcasted_iota(jnp.int32, (LANES, LANES), 0) // HEAD_DIM
    lane = lax.broadcasted_iota(jnp.int32, (LANES, LANES), 1) // HEAD_DIM
    same_head = jnp.where(feat == lane, 1.0, 0.0).astype(BF16)
    q_max2 = jnp.max(_dot(q * q, same_head), axis=0, keepdims=True)
    k_max2 = jnp.max(_dot(k * k, same_head), axis=0, keepdims=True)
    z_bound = jnp.sqrt(q_max2 * k_max2) * 1.02
    for hh in range(2):
        in_head = _head_lanes((1, LANES), hh)
        lane_mask = jnp.where(in_head, 1.0, 0.0).astype(BF16)
        qm_scr[hh] = q * lane_mask
        vm_scr[hh] = v * lane_mask
        bound = jnp.max(jnp.where(in_head, z_bound, 0.0), axis=1, keepdims=True)
        limit_scr[hh] = jnp.broadcast_to(bound + SB_UNDERFLOW, limit_scr.shape[1:])
    acc_scr[...] = jnp.zeros(acc_scr.shape, F32)

    row = lax.broadcasted_iota(jnp.int32, (TQ, TQ), 0)
    col = lax.broadcasted_iota(jnp.int32, (TQ, TQ), 1)
    strict = col < row
    tri = jnp.where(row >= col, 1.0, 0.0).astype(BF16)

    def tile_rows(j, r0, r1):
        keys = slice(j * TQ, (j + 1) * TQ)
        rows = slice(r0 * TQ, r1 * TQ)
        m = (r1 - r0) * TQ
        first = TQ if r0 == j else 0
        z = _dot_nt(jnp.concatenate([qm_scr[0, rows, :], qm_scr[1, rows, :]], axis=0),
                    k_ref[0, keys, :])
        sp = _softplus2(z)
        if first:
            parts = []
            for base in (0, m):
                parts.append(jnp.where(strict, sp[base:base + TQ], 0.0))
                if m > TQ:
                    parts.append(sp[base + TQ:base + m])
            sp = jnp.concatenate(parts, axis=0)
        c = _dot(sp.astype(BF16), tri)
        tot = jnp.broadcast_to(c[:, 0:1], (2 * m, LANES))
        w = []
        for h, base in enumerate((0, m)):
            w_h = []
            if first:
                d = slice(base, base + TQ)
                w_h.append(jnp.where(strict, jnp.exp2(z[d] - c[d]), 0.0))
                carry_scr[h, keys, :] = tot[d]
            if m > first:
                below = slice(r0 * TQ + first, r1 * TQ)
                o = slice(base + first, base + m)
                carry = carry_scr[h, below, :]
                w_h.append(jnp.exp2(z[o] - c[o] - jnp.concatenate([carry, carry], axis=1)))
                carry_scr[h, below, :] = carry + tot[o]
            w.append(w_h[0] if len(w_h) == 1 else jnp.concatenate(w_h, axis=0))
        vv = jnp.concatenate([vm_scr[0, keys, :], vm_scr[1, keys, :]], axis=0)
        acc_scr[rows, :] += _dot(jnp.concatenate(w, axis=1).astype(BF16), vv)

    def near_tiles(n_near):
        for j in range(n_blk - 1, -1, -1):
            tile_rows(j, j, min(j + n_near, n_blk))

    def live(r0, r1):
        slack = [jnp.min(carry_scr[h, r0 * TQ:r1 * TQ, :], axis=0, keepdims=True) - limit_scr[h, 0:1, :]
                 for h in range(2)]
        return jnp.min(jnp.minimum(slack[0], slack[1])) < 0.0

    def diagonal(dist):
        for i in range(dist, n_blk):
            tile_rows(i - dist, i, i + 1)

    step = pl.program_id(0) * pl.num_programs(1) + pl.program_id(1)

    @pl.when(step == 0)
    def _():
        plan_scr[0] = 0
        plan_scr[1] = 0

    wide = plan_scr[0] == 1

    @pl.when(jnp.logical_not(wide))
    def _():
        near_tiles(SB_NEAR_BLOCKS)
        flags = [live(i, i + 1) for i in range(SB_NEAR_BLOCKS, n_blk)]
        n_live = sum(f.astype(jnp.int32) for f in flags)
        plan_scr[1] += (n_live >= 2).astype(jnp.int32)
        plan_scr[2] = n_live

        @pl.when(n_live >= 2)
        def _():
            diagonal(SB_NEAR_BLOCKS)

        @pl.when(n_live == 1)
        def _():
            for i, f in zip(range(SB_NEAR_BLOCKS, n_blk), flags):
                pl.when(f)(functools.partial(tile_rows, i - SB_NEAR_BLOCKS, i, i + 1))

    @pl.when(wide)
    def _():
        near_tiles(SB_NEAR_BLOCKS + 1)
        plan_scr[2] = 1

    def farther(dist):
        @pl.when(live(dist, n_blk))
        def _():
            diagonal(dist)
            if dist + 1 < n_blk:
                farther(dist + 1)

    if SB_NEAR_BLOCKS + 1 < n_blk:
        pl.when(plan_scr[2] > 0)(functools.partial(farther, SB_NEAR_BLOCKS + 1))

    @pl.when(step == SB_PROBE_STEPS - 1)
    def _():
        plan_scr[0] = (2 * plan_scr[1] >= SB_PROBE_STEPS).astype(jnp.int32)

    o_ref[0] = acc_scr[...].astype(o_ref.dtype)


def _sb_attention(qkv):
    b, s, _ = qkv.shape
    pairs = SB_HEADS // 2
    return pl.pallas_call(
        _sb_kernel,
        grid=(b, pairs),
        in_specs=[pl.BlockSpec((1, s, LANES), lambda bi, p: (bi, 0, p)),
                  pl.BlockSpec((1, s, LANES), lambda bi, p: (bi, 0, pairs + p)),
                  pl.BlockSpec((1, s, LANES), lambda bi, p: (bi, 0, 2 * pairs + p))],
        out_specs=pl.BlockSpec((1, s, LANES), lambda bi, p: (bi, 0, p)),
        out_shape=jax.ShapeDtypeStruct((b, s, SB_HEADS * HEAD_DIM), BF16),
        scratch_shapes=[pltpu.VMEM((2, s, LANES), BF16), pltpu.VMEM((2, s, LANES), BF16),
                        pltpu.VMEM((s, LANES), F32), pltpu.VMEM((2, s, LANES), F32),
                        pltpu.VMEM((2, 8, LANES), F32), pltpu.SMEM((3,), jnp.int32)],
        compiler_params=_params("arbitrary", "arbitrary"),
        name="sb_attention",
    )(qkv, qkv, qkv)


def _t5_bucket_np(rel):
    max_exact = NUM_BUCKETS // 2
    rel = np.maximum(rel, 0)
    large = max_exact + (np.log(np.maximum(rel, max_exact) / max_exact)
                         / math.log(MAX_DISTANCE / max_exact) * (NUM_BUCKETS - max_exact)).astype(np.int64)
    return np.where(rel < max_exact, rel, np.minimum(large, NUM_BUCKETS - 1)).astype(np.int32)


def _bias_kernel(bucket_ref, rel_bias_ref, o_ref):
    h = pl.program_id(0)
    for plane in range(2):
        bk = bucket_ref[plane]
        acc = jnp.full(bk.shape, -jnp.inf, F32)
        for b in range(NUM_BUCKETS):
            acc = jnp.where(bk == b, rel_bias_ref[b, h] * LOG2_E, acc)
        o_ref[0, plane] = acc
    o_ref[0, 2] = jnp.full((TQ, TQ), rel_bias_ref[NUM_BUCKETS - 1, h] * LOG2_E, F32)


def _moba_bias_tables(rel_bias):
    off = np.arange(TQ)
    rel_own = off[:, None] - off[None, :]
    own = np.where(rel_own >= 0, _t5_bucket_np(rel_own), -1)
    adj = _t5_bucket_np(rel_own + MOBA_BLOCK)
    buckets = jnp.asarray(np.stack([own, adj]).astype(np.int32))
    return pl.pallas_call(
        _bias_kernel,
        grid=(MOBA_HEADS,),
        in_specs=[_whole(buckets.shape), pl.BlockSpec(memory_space=pltpu.SMEM)],
        out_specs=pl.BlockSpec((1, 3, TQ, TQ), lambda h: (h, 0, 0, 0)),
        out_shape=jax.ShapeDtypeStruct((MOBA_HEADS, 3, TQ, TQ), F32),
        compiler_params=_params("arbitrary"),
        name="moba_bias_tables",
    )(buckets, rel_bias)


_N_BLK_LANES = 8


def _moba_constants(nb):
    assert nb <= _N_BLK_LANES
    diff = np.zeros((LANES, LANES), np.float32)
    count = np.zeros((LANES, LANES), np.float32)
    for hh in range(2):
        for a in range(nb):
            for b in range(nb):
                if a != b:
                    pair = hh * HEAD_DIM + a * 8 + b
                    diff[pair, a] += 1.0
                    diff[pair, b] -= 1.0
                    count[pair, _spare_lane_base(hh) + b] = 1.0
                    count[pair, _spare_lane_base(hh) + _N_BLK_LANES + b] = 1.0
    return jnp.asarray(diff, BF16), jnp.asarray(count, BF16)


_T_KIND, _T_FAR_HI, _T_FAR_LO, _T_FUTURE, _T_UNSEL, _N_HEAD_TABLES = 0, 1, 2, 3, 4, 5


def _moba_fill_tables(head_tab, gate_tab, s_len):
    lane = lax.broadcasted_iota(jnp.int32, (s_len, LANES), 1)
    row_blk = lax.broadcasted_iota(jnp.int32, (s_len, LANES), 0) // MOBA_BLOCK
    for hh in range(2):
        base = _spare_lane_base(hh)
        in_hi = (lane >= base) & (lane < base + _N_BLK_LANES)
        in_lo = (lane >= base + _N_BLK_LANES) & (lane < base + 2 * _N_BLK_LANES)
        blk = jnp.where(in_lo, lane - base - _N_BLK_LANES, lane - base)
        far = blk < row_blk - 1
        store = lambda t, x: head_tab.__setitem__((hh, t), x.astype(BF16))
        store(_T_KIND, jnp.where(in_hi | in_lo, jnp.where(blk == row_blk, 1.0, 0.0), -1.0))
        store(_T_FAR_HI, jnp.where(in_hi & far, 1.0, 0.0))
        store(_T_FAR_LO, jnp.where(in_lo & far, 1.0, 0.0))
        store(_T_FUTURE, jnp.where(in_hi & (blk > row_blk), MASK_PENALTY, 0.0))
        store(_T_UNSEL, jnp.where(in_hi & (blk != row_blk), MASK_PENALTY, 0.0))
    pair_a, pair_b = (lane % HEAD_DIM) // 8, lane % 8
    valid = pair_a < row_blk
    gate_tab[0] = jnp.where(valid, 1.0, 0.0)
    gate_tab[1] = jnp.where(valid & (pair_a < pair_b), 1.0, 0.0)


def _moba_kernel(q_ref, k_ref, v_ref, bias_ref, diff_ref, count_ref, o_ref,
                 qaug_scr, kaug_scr, vm_scr, head_tab, gate_tab, *, nb):
    s_len = q_ref.shape[1]

    @pl.when((pl.program_id(0) == 0) & (pl.program_id(1) == 0))
    def _():
        _moba_fill_tables(head_tab, gate_tab, s_len)

    q, k, v = q_ref[0], k_ref[0], v_ref[0]
    kbar = jnp.mean(k.astype(F32).reshape(nb, MOBA_BLOCK, LANES), axis=1)
    kbar = jnp.concatenate([kbar, jnp.zeros((LANES - nb, LANES), F32)], axis=0)
    p0 = kbar.astype(BF16)
    r1 = kbar - p0.astype(F32)
    p1 = r1.astype(BF16)
    p2 = (r1 - p1.astype(F32)).astype(BF16)
    d = diff_ref[...]
    kdiff = _dot(d, p0) + _dot(d, p1) + _dot(d, p2)
    pair_head = lax.broadcasted_iota(jnp.int32, (LANES, LANES), 0) // HEAD_DIM
    feat_head = lax.broadcasted_iota(jnp.int32, (LANES, LANES), 1) // HEAD_DIM
    kdiff = jnp.where(pair_head == feat_head, kdiff, 0.0).astype(BF16)

    g = _dot_nt(q, kdiff)
    beats = jnp.where(g > 0.0, gate_tab[0], jnp.where(g == 0.0, gate_tab[1], 0.0))
    rank = _dot(beats.astype(BF16), count_ref[...]).astype(BF16)
    top = rank < float(MOBA_TOPK)

    lane = lax.broadcasted_iota(jnp.int32, (1, LANES), 1)
    for hh in range(2):
        lane_mask = jnp.where(_head_lanes((1, LANES), hh), 1.0, 0.0).astype(BF16)
        ones_lane = jnp.where(lane == _spare_lane_base(hh), 1.0, 0.0).astype(BF16)
        vm_scr[hh] = v * lane_mask + ones_lane
        kind = head_tab[hh, _T_KIND]
        spare = kind >= 0.0
        kaug_scr[hh] = jnp.where(spare, kind, k)
        far = bias_ref[hh, 2, 0:1, 0:LANES]
        far_hi = far.astype(BF16)
        far_lo = (far - far_hi.astype(F32)).astype(BF16)
        chosen_val = head_tab[hh, _T_FAR_HI] * far_hi + head_tab[hh, _T_FAR_LO] * far_lo + head_tab[hh, _T_FUTURE]
        qaug_scr[hh] = jnp.where(spare, jnp.where(top, chosen_val, head_tab[hh, _T_UNSEL]), q * lane_mask)

    for r0 in range(0, s_len, 2 * TQ):
        rows = slice(r0, r0 + 2 * TQ)
        out = jnp.zeros((2 * TQ, LANES), F32)
        for hh in range(2):
            acc = _two_block_attention(qaug_scr[hh, rows, :],
                                       lambda a, b: kaug_scr[hh, a:b, :],
                                       lambda a, b: vm_scr[hh, a:b, :], r0,
                                       lambda s, d: s + bias_ref[hh, d])
            out = out + _normalized(acc, hh)
        o_ref[0, rows, :] = out.astype(o_ref.dtype)


def _moba_attention(qkv, bias_tables):
    b, s, _ = qkv.shape
    nb = s // MOBA_BLOCK
    pairs = MOBA_HEADS // 2
    diff, count = _moba_constants(nb)
    return pl.pallas_call(
        functools.partial(_moba_kernel, nb=nb),
        grid=(b, pairs),
        in_specs=[pl.BlockSpec((1, s, LANES), lambda bi, p: (bi, 0, p)),
                  pl.BlockSpec((1, s, LANES), lambda bi, p: (bi, 0, pairs + p)),
                  pl.BlockSpec((1, s, LANES), lambda bi, p: (bi, 0, 2 * pairs + p)),
                  pl.BlockSpec((2, 3, TQ, TQ), lambda bi, p: (p, 0, 0, 0)),
                  _whole(diff.shape), _whole(count.shape)],
        out_specs=pl.BlockSpec((1, s, LANES), lambda bi, p: (bi, 0, p)),
        out_shape=jax.ShapeDtypeStruct((b, s, MOBA_HEADS * HEAD_DIM), BF16),
        scratch_shapes=[pltpu.VMEM((2, s, LANES), BF16)] * 3
        + [pltpu.VMEM((2, _N_HEAD_TABLES, s, LANES), BF16), pltpu.VMEM((2, s, LANES), F32)],
        compiler_params=_params("arbitrary", "arbitrary"),
        name="moba_attention",
    )(qkv, qkv, qkv, bias_tables, diff, count)


def _mix_ffn_kernel(*refs, n_in):
    a_refs = refs[:n_in]
    h_ref, w_out_ref, gmix_ref, gpre_ref, wgu_ref, wd_ref, gpost_ref, o_ref, act_scr = refs[n_in:]
    mixed, k0 = None, 0
    for a_ref in a_refs:
        k1 = k0 + a_ref.shape[1]
        part = _dot(a_ref[...], w_out_ref[k0:k1, :])
        mixed, k0 = part if mixed is None else mixed + part, k1
    x = h_ref[...] + _rms(mixed, gmix_ref[...])
    u = _rms(x, gpre_ref[...]).astype(BF16)
    for c in range(0, D_FF, FF_CHUNK):
        gate = _dot(u, wgu_ref[:, c:c + FF_CHUNK])
        up = _dot(u, wgu_ref[:, D_FF + c:D_FF + c + FF_CHUNK])
        act_scr[:, c:c + FF_CHUNK] = (gate * jax.nn.sigmoid(gate) * up).astype(BF16)
    f = _dot(act_scr[...], wd_ref[...])
    o_ref[...] = x + _rms(f, gpost_ref[...])


def _mix_ffn(acts, h, w_out, gmix, gpre, wgu, wd, gpost):
    n, d = h.shape
    row = lambda t: (t, 0)
    vec = _whole((1, d))
    return pl.pallas_call(
        functools.partial(_mix_ffn_kernel, n_in=len(acts)),
        grid=(n // TM_FFN,),
        in_specs=([pl.BlockSpec((TM_FFN, a.shape[1]), row) for a in acts]
                  + [pl.BlockSpec((TM_FFN, d), row), _whole(w_out.shape), vec, vec,
                     _whole(wgu.shape), _whole(wd.shape), vec]),
        out_specs=pl.BlockSpec((TM_FFN, d), row),
        out_shape=jax.ShapeDtypeStruct((n, d), F32),
        scratch_shapes=[pltpu.VMEM((TM_FFN, D_FF), BF16)],
        compiler_params=pltpu.CompilerParams(dimension_semantics=("arbitrary",),
                                             vmem_limit_bytes=VMEM_LIMIT_FFN),
        name="mix_ffn",
    )(*acts, h, w_out, gmix.reshape(1, d), gpre.reshape(1, d), wgu, wd, gpost.reshape(1, d))


def _rot_half_cols(w):
    half = w.shape[-1] // 2
    return jnp.concatenate([-w[..., half:], w[..., :half]], axis=-1)


def _prep_even_weights(w_in, w_uq, w_ukv):
    d = w_in.shape[0]
    o1, o2, o3 = MLA_Q_LORA, MLA_Q_LORA + MLA_KV_LORA, MLA_Q_LORA + MLA_KV_LORA + MLA_ROPE
    hw = MOBA_HEADS * HEAD_DIM
    z = lambda *shape: jnp.zeros(shape, F32)
    kr = w_in[:, o2:o3]
    w_in_ext = jnp.concatenate(
        [w_in[:, :o2],
         z(d, MLA_NOPE), kr, z(d, LANES - MLA_NOPE - MLA_ROPE),
         z(d, MLA_NOPE), _rot_half_cols(kr), z(d, LANES - MLA_NOPE - MLA_ROPE),
         w_in[:, o3:o3 + hw] * (HEAD_DIM ** -0.5), w_in[:, o3 + hw:]], axis=1).astype(BF16)

    wq = w_uq.reshape(MLA_Q_LORA, MLA_HEADS, MLA_NOPE + MLA_ROPE)
    nope, rp = wq[..., :MLA_NOPE], wq[..., MLA_NOPE:]
    assert MLA_NOPE + 2 * MLA_ROPE == LANES
    w_uq_ext = jnp.concatenate([nope, rp, _rot_half_cols(rp)], axis=-1).reshape(MLA_Q_LORA, _QW).astype(BF16)

    wkv = w_ukv.reshape(MLA_KV_LORA, MLA_HEADS, MLA_NOPE + MLA_V)
    k_nope, v = wkv[..., :MLA_NOPE], wkv[..., MLA_NOPE:]
    w_k = jnp.concatenate([k_nope, jnp.zeros_like(k_nope)], axis=-1).reshape(MLA_KV_LORA, _QW).astype(BF16)
    v = v.reshape(MLA_KV_LORA, MLA_HEADS // 2, 2, MLA_V)
    zero = jnp.zeros_like(v[:, :, 0])
    w_v = jnp.stack([jnp.concatenate([v[:, :, 0], zero], axis=-1),
                     jnp.concatenate([zero, v[:, :, 1]], axis=-1)], axis=2)
    w_v = w_v.reshape(MLA_KV_LORA, _QW).astype(BF16)
    return w_in_ext, w_uq_ext, w_k, w_v


def _rope_tables(seq):
    half = MLA_ROPE // 2
    inv_freq = ROPE_THETA ** (-jnp.arange(half, dtype=F32) / half)
    ang = jnp.arange(seq).astype(F32)[:, None] * inv_freq[None, :]
    cos, sin = jnp.cos(ang), jnp.sin(ang)
    ones, zeros = jnp.ones((seq, MLA_NOPE), F32), jnp.zeros((seq, MLA_NOPE), F32)
    tail = jnp.zeros((seq, LANES - MLA_NOPE - MLA_ROPE), F32)
    scale = (MLA_NOPE + MLA_ROPE) ** -0.5 * LOG2_E
    cq = jnp.concatenate([ones, cos, cos, tail], axis=1) * scale
    sq = jnp.concatenate([zeros, sin, sin, tail], axis=1) * scale
    ck = jnp.concatenate([zeros, cos, cos, tail], axis=1)
    sk = jnp.concatenate([zeros, sin, sin, tail], axis=1)
    return cq, sq, ck, sk


def kernel(x, mix_pre_g, mix_post_g, ffn_pre_g, ffn_post_g, ab_w_in, mla_q_norm_g, mla_w_uq,
           mla_kv_norm_g, mla_w_ukv, ab_w_out, rel_bias, sb_w_qkv, sb_w_out, ffn_w_gate_up, ffn_w_down):
    b, s, d = x.shape
    n = b * s
    h = x.reshape(n, d)
    tables = _rope_tables(s)
    bias_tables = _moba_bias_tables(rel_bias)
    for layer in range(DEPTH):
        if layer % 2 == 0:
            e = layer // 2
            w_in, w_uq, w_k, w_v = _prep_even_weights(ab_w_in[e], mla_w_uq[e], mla_w_ukv[e])
            q, k, v, moba_qkv = _even_in(h, mix_pre_g[layer], w_in, mla_q_norm_g[e], w_uq,
                                         mla_kv_norm_g[e], w_k, w_v, tables, s)
            mla_out = _mla_attention(q.reshape(b, s, -1), k.reshape(b, s, -1), v.reshape(b, s, -1))
            moba_out = _moba_attention(moba_qkv.reshape(b, s, -1), bias_tables)
            mixed, w_out = [mla_out.reshape(n, -1), moba_out.reshape(n, -1)], ab_w_out[e]
        else:
            o = layer // 2
            width = SB_HEADS * HEAD_DIM
            qkv = _norm_matmul(h, mix_pre_g[layer], sb_w_qkv[o].astype(BF16),
                               width, HEAD_DIM ** -0.5 * LOG2_E)
            mixed, w_out = [_sb_attention(qkv.reshape(b, s, -1)).reshape(n, -1)], sb_w_out[o]
        h = _mix_ffn(mixed, h, w_out.astype(BF16), mix_post_g[layer], ffn_pre_g[layer],
                     ffn_w_gate_up[layer].astype(BF16), ffn_w_down[layer].astype(BF16),
                     ffn_post_g[layer])
    return h.reshape(b, s, d)
```

```python
import functools
import math

import numpy as np
import jax
import jax.numpy as jnp
from jax import lax
from jax.experimental import pallas as pl
from jax.experimental.pallas import tpu as pltpu

D_MODEL = 1024
DEPTH = 4
HEAD_DIM = 64
MLA_HEADS = 8
MLA_NOPE = 64
MLA_ROPE = 32
MLA_V = 64
MLA_Q_LORA = 384
MLA_KV_LORA = 256
ROPE_THETA = 10000.0
MOBA_HEADS = 8
MOBA_BLOCK = 256
MOBA_TOPK = 3
SB_HEADS = 16
NUM_BUCKETS = 32
MAX_DISTANCE = 128
RMS_EPS = 1e-6
D_FF = 2816

LANES = 128
TQ = 256
TM = 512
TM_FFN = 512
VMEM_LIMIT_FFN = 48 * 1024 * 1024
FF_CHUNK = 256
MASK_PENALTY = -1e30
LOG2_E = 1.0 / math.log(2.0)
VMEM_LIMIT = 48 * 1024 * 1024

F32 = jnp.float32
BF16 = jnp.bfloat16


def _rms(x, g):
    return x * lax.rsqrt(jnp.mean(x * x, axis=-1, keepdims=True) + RMS_EPS) * g


def _dot(a, b):
    return jnp.dot(a, b, preferred_element_type=F32)


def _dot_nt(a, b):
    return lax.dot_general(a, b, (((1,), (1,)), ((), ())), preferred_element_type=F32)


def _params(*semantics):
    return pltpu.CompilerParams(dimension_semantics=semantics, vmem_limit_bytes=VMEM_LIMIT)


def _whole(shape):
    nd = len(shape)
    return pl.BlockSpec(shape, lambda *_: (0,) * nd, pipeline_mode=pl.Buffered(1))


def _norm_matmul_kernel(h_ref, g_ref, w_ref, o_ref, *, q_cols, q_scale):
    u = _rms(h_ref[...], g_ref[...]).astype(BF16)
    for c in range(0, o_ref.shape[-1], 512):
        y = _dot(u, w_ref[:, c:c + 512])
        if c < q_cols:
            y = y * q_scale
        o_ref[:, c:c + 512] = y.astype(o_ref.dtype)


def _norm_matmul(h, g, w, q_cols, q_scale):
    n, d = h.shape
    n_out = w.shape[1]
    return pl.pallas_call(
        functools.partial(_norm_matmul_kernel, q_cols=q_cols, q_scale=q_scale),
        grid=(n // TM,),
        in_specs=[pl.BlockSpec((TM, d), lambda t: (t, 0)), _whole((1, d)), _whole(w.shape)],
        out_specs=pl.BlockSpec((TM, n_out), lambda t: (t, 0)),
        out_shape=jax.ShapeDtypeStruct((n, n_out), BF16),
        compiler_params=_params("arbitrary"),
        name="norm_matmul",
    )(h, g.reshape(1, d), w)


def _head_lanes(shape, hh):
    lane = lax.broadcasted_iota(jnp.int32, shape, 1)
    return (lane >= hh * HEAD_DIM) & (lane < (hh + 1) * HEAD_DIM)


def _spare_lane_base(hh):
    return HEAD_DIM * (1 - hh)


def _normalized(acc, hh):
    base = _spare_lane_base(hh)
    return jnp.where(_head_lanes(acc.shape, hh), acc / acc[:, base:base + 1], 0.0)


_O_CQ = 0
_O_CKV = MLA_Q_LORA
_O_KR = _O_CKV + MLA_KV_LORA
_O_KRR = _O_KR + LANES
_O_MOBA = _O_KRR + LANES
_W_IN_EXT = _O_MOBA + 3 * MOBA_HEADS * HEAD_DIM
_QW = MLA_HEADS * LANES


def _even_in_kernel(h_ref, g_ref, w_in_ref, gq_ref, w_uq_ref, gkv_ref, w_k_ref, w_v_ref,
                    cq_ref, sq_ref, ck_ref, sk_ref, q_ref, k_ref, v_ref, moba_ref):
    u = _rms(h_ref[...], g_ref[...]).astype(BF16)
    c_q = _dot(u, w_in_ref[:, _O_CQ:_O_CKV])
    c_kv = _dot(u, w_in_ref[:, _O_CKV:_O_KR])
    kr_both = _dot(u, w_in_ref[:, _O_KR:_O_MOBA])
    kr, krr = kr_both[:, :LANES], kr_both[:, LANES:]
    hw = MOBA_HEADS * HEAD_DIM
    for c in range(0, 3 * hw, hw):
        y = _dot(u, w_in_ref[:, _O_MOBA + c:_O_MOBA + c + hw])
        if c == 0:
            y = y * LOG2_E
        moba_ref[:, c:c + hw] = y.astype(BF16)

    cqn = _rms(c_q, gq_ref[...]).astype(BF16)
    qa = _dot(cqn, w_uq_ref[...])
    qb = pltpu.roll(qa, _QW - MLA_ROPE, axis=1)
    cq, sq = cq_ref[...], sq_ref[...]
    for h in range(MLA_HEADS):
        s = slice(h * LANES, (h + 1) * LANES)
        q_ref[:, s] = (qa[:, s] * cq + qb[:, s] * sq).astype(BF16)

    ckvn = _rms(c_kv, gkv_ref[...]).astype(BF16)
    kn = _dot(ckvn, w_k_ref[...])
    v = _dot(ckvn, w_v_ref[...])
    lane = lax.broadcasted_iota(jnp.int32, v.shape, 1)
    spare = jnp.where((lane // LANES) % 2 == 0, _spare_lane_base(0), _spare_lane_base(1))
    v_ref[...] = jnp.where(lane % LANES == spare, 1.0, v).astype(BF16)
    k_rope = kr * ck_ref[...] + krr * sk_ref[...]
    for h in range(MLA_HEADS):
        s = slice(h * LANES, (h + 1) * LANES)
        k_ref[:, s] = (kn[:, s] + k_rope).astype(BF16)


def _even_in(h, g, w_in, gq, w_uq, gkv, w_k, w_v, tables, seq):
    n, d = h.shape
    row = lambda t: (t, 0)
    pos = lambda t: (t % (seq // TM), 0)
    tab = pl.BlockSpec((TM, LANES), pos)
    outs = [(_QW, "q"), (_QW, "k"), (_QW, "v"), (3 * MOBA_HEADS * HEAD_DIM, "moba")]
    return pl.pallas_call(
        _even_in_kernel,
        grid=(n // TM,),
        in_specs=[pl.BlockSpec((TM, d), row), _whole((1, d)), _whole(w_in.shape),
                  _whole((1, MLA_Q_LORA)), _whole(w_uq.shape), _whole((1, MLA_KV_LORA)),
                  _whole(w_k.shape), _whole(w_v.shape), tab, tab, tab, tab],
        out_specs=[pl.BlockSpec((TM, w), row) for w, _ in outs],
        out_shape=[jax.ShapeDtypeStruct((n, w), BF16) for w, _ in outs],
        compiler_params=_params("arbitrary"),
        name="even_in",
    )(h, g.reshape(1, d), w_in, gq.reshape(1, -1), w_uq, gkv.reshape(1, -1), w_k, w_v, *tables)


def _two_block_attention(q, k_of, v_of, r0, add_near):
    end = r0 + 2 * TQ
    kd, vd = k_of(r0, end), v_of(r0, end)
    s_a = add_near(_dot_nt(q[:TQ], kd[:TQ]), 0)
    s_b = _dot_nt(q[TQ:], kd)
    s_b = jnp.concatenate([add_near(s_b[:, :TQ], 1), add_near(s_b[:, TQ:], 0)], axis=1)
    if r0 > 0:
        s_off = _dot_nt(q, k_of(0, r0))
        s_a = jnp.concatenate([s_off[:TQ, :r0 - TQ], add_near(s_off[:TQ, r0 - TQ:], 1), s_a], axis=1)
        s_b = jnp.concatenate([s_off[TQ:], s_b], axis=1)
    p_a = jnp.exp2(s_a - jnp.max(s_a, axis=-1, keepdims=True)).astype(BF16)
    p_b = jnp.exp2(s_b - jnp.max(s_b, axis=-1, keepdims=True)).astype(BF16)
    acc = jnp.concatenate([_dot(p_a[:, r0:], vd[:TQ]), _dot(p_b[:, r0:], vd)], axis=0)
    if r0 > 0:
        acc = acc + _dot(jnp.concatenate([p_a[:, :r0], p_b[:, :r0]], axis=0), v_of(0, r0))
    return acc


def _mla_kernel(q_ref, k_ref, v_ref, o_ref):
    s_len = q_ref.shape[1]
    row = lax.broadcasted_iota(jnp.int32, (TQ, TQ), 0)
    col = lax.broadcasted_iota(jnp.int32, (TQ, TQ), 1)
    causal = col <= row
    add_near = lambda s, d: jnp.where(causal, s, -jnp.inf) if d == 0 else s
    for r0 in range(0, s_len, 2 * TQ):
        rows = slice(r0, r0 + 2 * TQ)
        out = jnp.zeros((2 * TQ, LANES), F32)
        for hh in range(2):
            lanes = slice(hh * LANES, (hh + 1) * LANES)
            acc = _two_block_attention(q_ref[0, rows, lanes],
                                       lambda a, b: k_ref[0, a:b, lanes],
                                       lambda a, b: v_ref[0, a:b, lanes], r0, add_near)
            out = out + _normalized(acc, hh)
        o_ref[0, rows, :] = out.astype(o_ref.dtype)


def _mla_attention(q, k, v):
    b, s, _ = q.shape
    pairs = MLA_HEADS // 2
    qkv = pl.BlockSpec((1, s, 2 * LANES), lambda bi, p: (bi, 0, p))
    return pl.pallas_call(
        _mla_kernel,
        grid=(b, pairs),
        in_specs=[qkv, qkv, qkv],
        out_specs=pl.BlockSpec((1, s, LANES), lambda bi, p: (bi, 0, p)),
        out_shape=jax.ShapeDtypeStruct((b, s, MLA_HEADS * MLA_V), BF16),
        compiler_params=_params("arbitrary", "arbitrary"),
        name="mla_attention",
    )(q, k, v)


_SOFTPLUS2_CLAMP = 64.0


def _softplus2(z):
    return jnp.maximum(z, jnp.log2(1.0 + jnp.exp2(jnp.minimum(z, _SOFTPLUS2_CLAMP))))


SB_UNDERFLOW = 160.0
SB_NEAR_BLOCKS = 2
SB_PROBE_STEPS = 16


def _sb_kernel(q_ref, k_ref, v_ref, o_ref, qm_scr, vm_scr, acc_scr, carry_scr, limit_scr, plan_scr):
    s_len = q_ref.shape[1]
    n_blk = s_len // TQ
    q, k, v = q_ref[0], k_ref[0], v_ref[0]
    feat = lax.broadcasted_iota(jnp.int32, (LANES, LANES), 0) // HEAD_DIM
    lane = lax.broadcasted_iota(jnp.int32, (LANES, LANES), 1) // HEAD_DIM
    same_head = jnp.where(feat == lane, 1.0, 0.0).astype(BF16)
    q_max2 = jnp.max(_dot(q * q, same_head), axis=0, keepdims=True)
    k_max2 = jnp.max(_dot(k * k, same_head), axis=0, keepdims=True)
    z_bound = jnp.sqrt(q_max2 * k_max2) * 1.02
    for hh in range(2):
        in_head = _head_lanes((1, LANES), hh)
        lane_mask = jnp.where(in_head, 1.0, 0.0).astype(BF16)
        qm_scr[hh] = q * lane_mask
        vm_scr[hh] = v * lane_mask
        bound = jnp.max(jnp.where(in_head, z_bound, 0.0), axis=1, keepdims=True)
        limit_scr[hh] = jnp.broadcast_to(bound + SB_UNDERFLOW, limit_scr.shape[1:])
    acc_scr[...] = jnp.zeros(acc_scr.shape, F32)

    row = lax.broadcasted_iota(jnp.int32, (TQ, TQ), 0)
    col = lax.broadcasted_iota(jnp.int32, (TQ, TQ), 1)
    strict = col < row
    tri = jnp.where(row >= col, 1.0, 0.0).astype(BF16)

    def tile_rows(j, r0, r1):
        keys = slice(j * TQ, (j + 1) * TQ)
        rows = slice(r0 * TQ, r1 * TQ)
        m = (r1 - r0) * TQ
        first = TQ if r0 == j else 0
        z = _dot_nt(jnp.concatenate([qm_scr[0, rows, :], qm_scr[1, rows, :]], axis=0),
                    k_ref[0, keys, :])
        sp = _softplus2(z)
        if first:
            parts = []
            for base in (0, m):
                parts.append(jnp.where(strict, sp[base:base + TQ], 0.0))
                if m > TQ:
                    parts.append(sp[base + TQ:base + m])
            sp = jnp.concatenate(parts, axis=0)
        c = _dot(sp.astype(BF16), tri)
        tot = jnp.broadcast_to(c[:, 0:1], (2 * m, LANES))
        w = []
        for h, base in enumerate((0, m)):
            w_h = []
            if first:
                d = slice(base, base + TQ)
                w_h.append(jnp.where(strict, jnp.exp2(z[d] - c[d]), 0.0))
                carry_scr[h, keys, :] = tot[d]
            if m > first:
                below = slice(r0 * TQ + first, r1 * TQ)
                o = slice(base + first, base + m)
                carry = carry_scr[h, below, :]
                w_h.append(jnp.exp2(z[o] - c[o] - jnp.concatenate([carry, carry], axis=1)))
                carry_scr[h, below, :] = carry + tot[o]
            w.append(w_h[0] if len(w_h) == 1 else jnp.concatenate(w_h, axis=0))
        vv = jnp.concatenate([vm_scr[0, keys, :], vm_scr[1, keys, :]], axis=0)
        acc_scr[rows, :] += _dot(jnp.concatenate(w, axis=1).astype(BF16), vv)

    def near_tiles(n_near):
        for j in range(n_blk - 1, -1, -1):
            tile_rows(j, j, min(j + n_near, n_blk))

    def live(r0, r1):
        slack = [jnp.min(carry_scr[h, r0 * TQ:r1 * TQ, :], axis=0, keepdims=True) - limit_scr[h, 0:1, :]
                 for h in range(2)]
        return jnp.min(jnp.minimum(slack[0], slack[1])) < 0.0

    def diagonal(dist):
        for i in range(dist, n_blk):
            tile_rows(i - dist, i, i + 1)

    step = pl.program_id(0) * pl.num_programs(1) + pl.program_id(1)

    @pl.when(step == 0)
    def _():
        plan_scr[0] = 0
        plan_scr[1] = 0

    wide = plan_scr[0] == 1

    @pl.when(jnp.logical_not(wide))
    def _():
        near_tiles(SB_NEAR_BLOCKS)
        flags = [live(i, i + 1) for i in range(SB_NEAR_BLOCKS, n_blk)]
        n_live = sum(f.astype(jnp.int32) for f in flags)
        plan_scr[1] += (n_live >= 2).astype(jnp.int32)
        plan_scr[2] = n_live

        @pl.when(n_live >= 2)
        def _():
            diagonal(SB_NEAR_BLOCKS)

        @pl.when(n_live == 1)
        def _():
            for i, f in zip(range(SB_NEAR_BLOCKS, n_blk), flags):
                pl.when(f)(functools.partial(tile_rows, i - SB_NEAR_BLOCKS, i, i + 1))

    @pl.when(wide)
    def _():
        near_tiles(SB_NEAR_BLOCKS + 1)
        plan_scr[2] = 1

    def farther(dist):
        @pl.when(live(dist, n_blk))
        def _():
            diagonal(dist)
            if dist + 1 < n_blk:
                farther(dist + 1)

    if SB_NEAR_BLOCKS + 1 < n_blk:
        pl.when(plan_scr[2] > 0)(functools.partial(farther, SB_NEAR_BLOCKS + 1))

    @pl.when(step == SB_PROBE_STEPS - 1)
    def _():
        plan_scr[0] = (5 * plan_scr[1] >= 3 * SB_PROBE_STEPS).astype(jnp.int32)

    o_ref[0] = acc_scr[...].astype(o_ref.dtype)


def _sb_attention(qkv):
    b, s, _ = qkv.shape
    pairs = SB_HEADS // 2
    return pl.pallas_call(
        _sb_kernel,
        grid=(b, pairs),
        in_specs=[pl.BlockSpec((1, s, LANES), lambda bi, p: (bi, 0, p)),
                  pl.BlockSpec((1, s, LANES), lambda bi, p: (bi, 0, pairs + p)),
                  pl.BlockSpec((1, s, LANES), lambda bi, p: (bi, 0, 2 * pairs + p))],
        out_specs=pl.BlockSpec((1, s, LANES), lambda bi, p: (bi, 0, p)),
        out_shape=jax.ShapeDtypeStruct((b, s, SB_HEADS * HEAD_DIM), BF16),
        scratch_shapes=[pltpu.VMEM((2, s, LANES), BF16), pltpu.VMEM((2, s, LANES), BF16),
                        pltpu.VMEM((s, LANES), F32), pltpu.VMEM((2, s, LANES), F32),
                        pltpu.VMEM((2, 8, LANES), F32), pltpu.SMEM((3,), jnp.int32)],
        compiler_params=_params("arbitrary", "arbitrary"),
        name="sb_attention",
    )(qkv, qkv, qkv)


def _t5_bucket_np(rel):
    max_exact = NUM_BUCKETS // 2
    rel = np.maximum(rel, 0)
    large = max_exact + (np.log(np.maximum(rel, max_exact) / max_exact)
                         / math.log(MAX_DISTANCE / max_exact) * (NUM_BUCKETS - max_exact)).astype(np.int64)
    return np.where(rel < max_exact, rel, np.minimum(large, NUM_BUCKETS - 1)).astype(np.int32)


def _bias_kernel(bucket_ref, rel_bias_ref, o_ref):
    h = pl.program_id(0)
    for plane in range(2):
        bk = bucket_ref[plane]
        acc = jnp.full(bk.shape, -jnp.inf, F32)
        for b in range(NUM_BUCKETS):
            acc = jnp.where(bk == b, rel_bias_ref[b, h] * LOG2_E, acc)
        o_ref[0, plane] = acc
    o_ref[0, 2] = jnp.full((TQ, TQ), rel_bias_ref[NUM_BUCKETS - 1, h] * LOG2_E, F32)


def _moba_bias_tables(rel_bias):
    off = np.arange(TQ)
    rel_own = off[:, None] - off[None, :]
    own = np.where(rel_own >= 0, _t5_bucket_np(rel_own), -1)
    adj = _t5_bucket_np(rel_own + MOBA_BLOCK)
    buckets = jnp.asarray(np.stack([own, adj]).astype(np.int32))
    return pl.pallas_call(
        _bias_kernel,
        grid=(MOBA_HEADS,),
        in_specs=[_whole(buckets.shape), pl.BlockSpec(memory_space=pltpu.SMEM)],
        out_specs=pl.BlockSpec((1, 3, TQ, TQ), lambda h: (h, 0, 0, 0)),
        out_shape=jax.ShapeDtypeStruct((MOBA_HEADS, 3, TQ, TQ), F32),
        compiler_params=_params("arbitrary"),
        name="moba_bias_tables",
    )(buckets, rel_bias)


_N_BLK_LANES = 8


def _moba_constants(nb):
    assert nb <= _N_BLK_LANES
    diff = np.zeros((LANES, LANES), np.float32)
    count = np.zeros((LANES, LANES), np.float32)
    for hh in range(2):
        for a in range(nb):
            for b in range(nb):
                if a != b:
                    pair = hh * HEAD_DIM + a * 8 + b
                    diff[pair, a] += 1.0
                    diff[pair, b] -= 1.0
                    count[pair, _spare_lane_base(hh) + b] = 1.0
                    count[pair, _spare_lane_base(hh) + _N_BLK_LANES + b] = 1.0
    return jnp.asarray(diff, BF16), jnp.asarray(count, BF16)


_T_KIND, _T_FAR_HI, _T_FAR_LO, _T_FUTURE, _T_UNSEL, _N_HEAD_TABLES = 0, 1, 2, 3, 4, 5


def _moba_fill_tables(head_tab, gate_tab, s_len):
    lane = lax.broadcasted_iota(jnp.int32, (s_len, LANES), 1)
    row_blk = lax.broadcasted_iota(jnp.int32, (s_len, LANES), 0) // MOBA_BLOCK
    for hh in range(2):
        base = _spare_lane_base(hh)
        in_hi = (lane >= base) & (lane < base + _N_BLK_LANES)
        in_lo = (lane >= base + _N_BLK_LANES) & (lane < base + 2 * _N_BLK_LANES)
        blk = jnp.where(in_lo, lane - base - _N_BLK_LANES, lane - base)
        far = blk < row_blk - 1
        store = lambda t, x: head_tab.__setitem__((hh, t), x.astype(BF16))
        store(_T_KIND, jnp.where(in_hi | in_lo, jnp.where(blk == row_blk, 1.0, 0.0), -1.0))
        store(_T_FAR_HI, jnp.where(in_hi & far, 1.0, 0.0))
        store(_T_FAR_LO, jnp.where(in_lo & far, 1.0, 0.0))
        store(_T_FUTURE, jnp.where(in_hi & (blk > row_blk), MASK_PENALTY, 0.0))
        store(_T_UNSEL, jnp.where(in_hi & (blk != row_blk), MASK_PENALTY, 0.0))
    pair_a, pair_b = (lane % HEAD_DIM) // 8, lane % 8
    valid = pair_a < row_blk
    gate_tab[0] = jnp.where(valid, 1.0, 0.0)
    gate_tab[1] = jnp.where(valid & (pair_a < pair_b), 1.0, 0.0)


def _moba_kernel(q_ref, k_ref, v_ref, bias_ref, diff_ref, count_ref, o_ref,
                 qaug_scr, kaug_scr, vm_scr, head_tab, gate_tab, *, nb):
    s_len = q_ref.shape[1]

    @pl.when((pl.program_id(0) == 0) & (pl.program_id(1) == 0))
    def _():
        _moba_fill_tables(head_tab, gate_tab, s_len)

    q, k, v = q_ref[0], k_ref[0], v_ref[0]
    kbar = jnp.mean(k.astype(F32).reshape(nb, MOBA_BLOCK, LANES), axis=1)
    kbar = jnp.concatenate([kbar, jnp.zeros((LANES - nb, LANES), F32)], axis=0)
    p0 = kbar.astype(BF16)
    r1 = kbar - p0.astype(F32)
    p1 = r1.astype(BF16)
    p2 = (r1 - p1.astype(F32)).astype(BF16)
    d = diff_ref[...]
    kdiff = _dot(d, p0) + _dot(d, p1) + _dot(d, p2)
    pair_head = lax.broadcasted_iota(jnp.int32, (LANES, LANES), 0) // HEAD_DIM
    feat_head = lax.broadcasted_iota(jnp.int32, (LANES, LANES), 1) // HEAD_DIM
    kdiff = jnp.where(pair_head == feat_head, kdiff, 0.0).astype(BF16)

    g = _dot_nt(q, kdiff)
    beats = jnp.where(g > 0.0, gate_tab[0], jnp.where(g == 0.0, gate_tab[1], 0.0))
    rank = _dot(beats.astype(BF16), count_ref[...]).astype(BF16)
    top = rank < float(MOBA_TOPK)

    lane = lax.broadcasted_iota(jnp.int32, (1, LANES), 1)
    for hh in range(2):
        lane_mask = jnp.where(_head_lanes((1, LANES), hh), 1.0, 0.0).astype(BF16)
        ones_lane = jnp.where(lane == _spare_lane_base(hh), 1.0, 0.0).astype(BF16)
        vm_scr[hh] = v * lane_mask + ones_lane
        kind = head_tab[hh, _T_KIND]
        spare = kind >= 0.0
        kaug_scr[hh] = jnp.where(spare, kind, k)
        far = bias_ref[hh, 2, 0:1, 0:LANES]
        far_hi = far.astype(BF16)
        far_lo = (far - far_hi.astype(F32)).astype(BF16)
        chosen_val = head_tab[hh, _T_FAR_HI] * far_hi + head_tab[hh, _T_FAR_LO] * far_lo + head_tab[hh, _T_FUTURE]
        qaug_scr[hh] = jnp.where(spare, jnp.where(top, chosen_val, head_tab[hh, _T_UNSEL]), q * lane_mask)

    for r0 in range(0, s_len, 2 * TQ):
        rows = slice(r0, r0 + 2 * TQ)
        out = jnp.zeros((2 * TQ, LANES), F32)
        for hh in range(2):
            acc = _two_block_attention(qaug_scr[hh, rows, :],
                                       lambda a, b: kaug_scr[hh, a:b, :],
                                       lambda a, b: vm_scr[hh, a:b, :], r0,
                                       lambda s, d: s + bias_ref[hh, d])
            out = out + _normalized(acc, hh)
        o_ref[0, rows, :] = out.astype(o_ref.dtype)


def _moba_attention(qkv, bias_tables):
    b, s, _ = qkv.shape
    nb = s // MOBA_BLOCK
    pairs = MOBA_HEADS // 2
    diff, count = _moba_constants(nb)
    return pl.pallas_call(
        functools.partial(_moba_kernel, nb=nb),
        grid=(b, pairs),
        in_specs=[pl.BlockSpec((1, s, LANES), lambda bi, p: (bi, 0, p)),
                  pl.BlockSpec((1, s, LANES), lambda bi, p: (bi, 0, pairs + p)),
                  pl.BlockSpec((1, s, LANES), lambda bi, p: (bi, 0, 2 * pairs + p)),
                  pl.BlockSpec((2, 3, TQ, TQ), lambda bi, p: (p, 0, 0, 0)),
                  _whole(diff.shape), _whole(count.shape)],
        out_specs=pl.BlockSpec((1, s, LANES), lambda bi, p: (bi, 0, p)),
        out_shape=jax.ShapeDtypeStruct((b, s, MOBA_HEADS * HEAD_DIM), BF16),
        scratch_shapes=[pltpu.VMEM((2, s, LANES), BF16)] * 3
        + [pltpu.VMEM((2, _N_HEAD_TABLES, s, LANES), BF16), pltpu.VMEM((2, s, LANES), F32)],
        compiler_params=_params("arbitrary", "arbitrary"),
        name="moba_attention",
    )(qkv, qkv, qkv, bias_tables, diff, count)


def _mix_ffn_kernel(*refs, n_in):
    a_refs = refs[:n_in]
    h_ref, w_out_ref, gmix_ref, gpre_ref, wgu_ref, wd_ref, gpost_ref, o_ref, act_scr = refs[n_in:]
    mixed, k0 = None, 0
    for a_ref in a_refs:
        k1 = k0 + a_ref.shape[1]
        part = _dot(a_ref[...], w_out_ref[k0:k1, :])
        mixed, k0 = part if mixed is None else mixed + part, k1
    x = h_ref[...] + _rms(mixed, gmix_ref[...])
    u = _rms(x, gpre_ref[...]).astype(BF16)
    for c in range(0, D_FF, FF_CHUNK):
        gate = _dot(u, wgu_ref[:, c:c + FF_CHUNK])
        up = _dot(u, wgu_ref[:, D_FF + c:D_FF + c + FF_CHUNK])
        act_scr[:, c:c + FF_CHUNK] = (gate * jax.nn.sigmoid(gate) * up).astype(BF16)
    f = _dot(act_scr[...], wd_ref[...])
    o_ref[...] = x + _rms(f, gpost_ref[...])


def _mix_ffn(acts, h, w_out, gmix, gpre, wgu, wd, gpost):
    n, d = h.shape
    row = lambda t: (t, 0)
    vec = _whole((1, d))
    return pl.pallas_call(
        functools.partial(_mix_ffn_kernel, n_in=len(acts)),
        grid=(n // TM_FFN,),
        in_specs=([pl.BlockSpec((TM_FFN, a.shape[1]), row) for a in acts]
                  + [pl.BlockSpec((TM_FFN, d), row), _whole(w_out.shape), vec, vec,
                     _whole(wgu.shape), _whole(wd.shape), vec]),
        out_specs=pl.BlockSpec((TM_FFN, d), row),
        out_shape=jax.ShapeDtypeStruct((n, d), F32),
        scratch_shapes=[pltpu.VMEM((TM_FFN, D_FF), BF16)],
        compiler_params=pltpu.CompilerParams(dimension_semantics=("arbitrary",),
                                             vmem_limit_bytes=VMEM_LIMIT_FFN),
        name="mix_ffn",
    )(*acts, h, w_out, gmix.reshape(1, d), gpre.reshape(1, d), wgu, wd, gpost.reshape(1, d))


def _rot_half_cols(w):
    half = w.shape[-1] // 2
    return jnp.concatenate([-w[..., half:], w[..., :half]], axis=-1)


def _prep_even_weights(w_in, w_uq, w_ukv):
    d = w_in.shape[0]
    o1, o2, o3 = MLA_Q_LORA, MLA_Q_LORA + MLA_KV_LORA, MLA_Q_LORA + MLA_KV_LORA + MLA_ROPE
    hw = MOBA_HEADS * HEAD_DIM
    z = lambda *shape: jnp.zeros(shape, F32)
    kr = w_in[:, o2:o3]
    w_in_ext = jnp.concatenate(
        [w_in[:, :o2],
         z(d, MLA_NOPE), kr, z(d, LANES - MLA_NOPE - MLA_ROPE),
         z(d, MLA_NOPE), _rot_half_cols(kr), z(d, LANES - MLA_NOPE - MLA_ROPE),
         w_in[:, o3:o3 + hw] * (HEAD_DIM ** -0.5), w_in[:, o3 + hw:]], axis=1).astype(BF16)

    wq = w_uq.reshape(MLA_Q_LORA, MLA_HEADS, MLA_NOPE + MLA_ROPE)
    nope, rp = wq[..., :MLA_NOPE], wq[..., MLA_NOPE:]
    assert MLA_NOPE + 2 * MLA_ROPE == LANES
    w_uq_ext = jnp.concatenate([nope, rp, _rot_half_cols(rp)], axis=-1).reshape(MLA_Q_LORA, _QW).astype(BF16)

    wkv = w_ukv.reshape(MLA_KV_LORA, MLA_HEADS, MLA_NOPE + MLA_V)
    k_nope, v = wkv[..., :MLA_NOPE], wkv[..., MLA_NOPE:]
    w_k = jnp.concatenate([k_nope, jnp.zeros_like(k_nope)], axis=-1).reshape(MLA_KV_LORA, _QW).astype(BF16)
    v = v.reshape(MLA_KV_LORA, MLA_HEADS // 2, 2, MLA_V)
    zero = jnp.zeros_like(v[:, :, 0])
    w_v = jnp.stack([jnp.concatenate([v[:, :, 0], zero], axis=-1),
                     jnp.concatenate([zero, v[:, :, 1]], axis=-1)], axis=2)
    w_v = w_v.reshape(MLA_KV_LORA, _QW).astype(BF16)
    return w_in_ext, w_uq_ext, w_k, w_v


def _rope_tables(seq):
    half = MLA_ROPE // 2
    inv_freq = ROPE_THETA ** (-jnp.arange(half, dtype=F32) / half)
    ang = jnp.arange(seq).astype(F32)[:, None] * inv_freq[None, :]
    cos, sin = jnp.cos(ang), jnp.sin(ang)
    ones, zeros = jnp.ones((seq, MLA_NOPE), F32), jnp.zeros((seq, MLA_NOPE), F32)
    tail = jnp.zeros((seq, LANES - MLA_NOPE - MLA_ROPE), F32)
    scale = (MLA_NOPE + MLA_ROPE) ** -0.5 * LOG2_E
    cq = jnp.concatenate([ones, cos, cos, tail], axis=1) * scale
    sq = jnp.concatenate([zeros, sin, sin, tail], axis=1) * scale
    ck = jnp.concatenate([zeros, cos, cos, tail], axis=1)
    sk = jnp.concatenate([zeros, sin, sin, tail], axis=1)
    return cq, sq, ck, sk


def kernel(x, mix_pre_g, mix_post_g, ffn_pre_g, ffn_post_g, ab_w_in, mla_q_norm_g, mla_w_uq,
           mla_kv_norm_g, mla_w_ukv, ab_w_out, rel_bias, sb_w_qkv, sb_w_out, ffn_w_gate_up, ffn_w_down):
    b, s, d = x.shape
    n = b * s
    h = x.reshape(n, d)
    tables = _rope_tables(s)
    bias_tables = _moba_bias_tables(rel_bias)
    for layer in range(DEPTH):
        if layer % 2 == 0:
            e = layer // 2
            w_in, w_uq, w_k, w_v = _prep_even_weights(ab_w_in[e], mla_w_uq[e], mla_w_ukv[e])
            q, k, v, moba_qkv = _even_in(h, mix_pre_g[layer], w_in, mla_q_norm_g[e], w_uq,
                                         mla_kv_norm_g[e], w_k, w_v, tables, s)
            mla_out = _mla_attention(q.reshape(b, s, -1), k.reshape(b, s, -1), v.reshape(b, s, -1))
            moba_out = _moba_attention(moba_qkv.reshape(b, s, -1), bias_tables)
            mixed, w_out = [mla_out.reshape(n, -1), moba_out.reshape(n, -1)], ab_w_out[e]
        else:
            o = layer // 2
            width = SB_HEADS * HEAD_DIM
            qkv = _norm_matmul(h, mix_pre_g[layer], sb_w_qkv[o].astype(BF16),
                               width, HEAD_DIM ** -0.5 * LOG2_E)
            mixed, w_out = [_sb_attention(qkv.reshape(b, s, -1)).reshape(n, -1)], sb_w_out[o]
        h = _mix_ffn(mixed, h, w_out.astype(BF16), mix_post_g[layer], ffn_pre_g[layer],
                     ffn_w_gate_up[layer].astype(BF16), ffn_w_down[layer].astype(BF16),
                     ffn_post_g[layer])
    return h.reshape(b, s, d)
```

```python
import functools
import math

import numpy as np
import jax
import jax.numpy as jnp
from jax import lax
from jax.experimental import pallas as pl
from jax.experimental.pallas import tpu as pltpu

D_MODEL = 1024
DEPTH = 4
HEAD_DIM = 64
MLA_HEADS = 8
MLA_NOPE = 64
MLA_ROPE = 32
MLA_V = 64
MLA_Q_LORA = 384
MLA_KV_LORA = 256
ROPE_THETA = 10000.0
MOBA_HEADS = 8
MOBA_BLOCK = 256
MOBA_TOPK = 3
SB_HEADS = 16
NUM_BUCKETS = 32
MAX_DISTANCE = 128
RMS_EPS = 1e-6
D_FF = 2816

LANES = 128
TQ = 256
TM = 512
TM_FFN = 512
VMEM_LIMIT_FFN = 48 * 1024 * 1024
FF_CHUNK = 256
MASK_PENALTY = -1e30
LOG2_E = 1.0 / math.log(2.0)
VMEM_LIMIT = 48 * 1024 * 1024

F32 = jnp.float32
BF16 = jnp.bfloat16


def _rms(x, g):
    return x * lax.rsqrt(jnp.mean(x * x, axis=-1, keepdims=True) + RMS_EPS) * g


def _dot(a, b):
    return jnp.dot(a, b, preferred_element_type=F32)


def _dot_nt(a, b):
    return lax.dot_general(a, b, (((1,), (1,)), ((), ())), preferred_element_type=F32)


def _params(*semantics):
    return pltpu.CompilerParams(dimension_semantics=semantics, vmem_limit_bytes=VMEM_LIMIT)


def _whole(shape):
    nd = len(shape)
    return pl.BlockSpec(shape, lambda *_: (0,) * nd, pipeline_mode=pl.Buffered(1))


def _norm_matmul_kernel(h_ref, g_ref, w_ref, o_ref, *, q_cols, q_scale):
    u = _rms(h_ref[...], g_ref[...]).astype(BF16)
    for c in range(0, o_ref.shape[-1], 512):
        y = _dot(u, w_ref[:, c:c + 512])
        if c < q_cols:
            y = y * q_scale
        o_ref[:, c:c + 512] = y.astype(o_ref.dtype)


def _norm_matmul(h, g, w, q_cols, q_scale):
    n, d = h.shape
    n_out = w.shape[1]
    return pl.pallas_call(
        functools.partial(_norm_matmul_kernel, q_cols=q_cols, q_scale=q_scale),
        grid=(n // TM,),
        in_specs=[pl.BlockSpec((TM, d), lambda t: (t, 0)), _whole((1, d)), _whole(w.shape)],
        out_specs=pl.BlockSpec((TM, n_out), lambda t: (t, 0)),
        out_shape=jax.ShapeDtypeStruct((n, n_out), BF16),
        compiler_params=_params("arbitrary"),
        name="norm_matmul",
    )(h, g.reshape(1, d), w)


def _head_lanes(shape, hh):
    lane = lax.broadcasted_iota(jnp.int32, shape, 1)
    return (lane >= hh * HEAD_DIM) & (lane < (hh + 1) * HEAD_DIM)


def _spare_lane_base(hh):
    return HEAD_DIM * (1 - hh)


def _normalized(acc, hh):
    base = _spare_lane_base(hh)
    return jnp.where(_head_lanes(acc.shape, hh), acc / acc[:, base:base + 1], 0.0)


_O_CQ = 0
_O_KR = MLA_Q_LORA
_O_CKV = _O_KR + LANES
_O_MOBA = _O_CKV + MLA_KV_LORA
_W_IN_EXT = _O_MOBA + 3 * MOBA_HEADS * HEAD_DIM
_QW = MLA_HEADS * LANES


def _even_in_kernel(h_ref, g_ref, w_in_ref, gq_ref, w_uq_ref, gkv_ref, w_k_ref, w_v_ref,
                    cq_ref, sq_ref, ck_ref, sk_ref, q_ref, k_ref, v_ref, moba_ref):
    u = _rms(h_ref[...], g_ref[...]).astype(BF16)
    cq_kr = _dot(u, w_in_ref[:, _O_CQ:_O_CKV])
    c_q, kr = cq_kr[:, :_O_KR], cq_kr[:, _O_KR:]
    c_kv = _dot(u, w_in_ref[:, _O_CKV:_O_MOBA])
    hw = MOBA_HEADS * HEAD_DIM
    for c in range(0, 3 * hw, hw):
        y = _dot(u, w_in_ref[:, _O_MOBA + c:_O_MOBA + c + hw])
        if c == 0:
            y = y * LOG2_E
        moba_ref[:, c:c + hw] = y.astype(BF16)

    cqn = _rms(c_q, gq_ref[...]).astype(BF16)
    qa = _dot(cqn, w_uq_ref[...])
    qb = pltpu.roll(qa, _QW - MLA_ROPE, axis=1)
    cq, sq = cq_ref[...], sq_ref[...]
    for h in range(MLA_HEADS):
        s = slice(h * LANES, (h + 1) * LANES)
        q_ref[:, s] = (qa[:, s] * cq + qb[:, s] * sq).astype(BF16)

    ckvn = _rms(c_kv, gkv_ref[...]).astype(BF16)
    kn = _dot(ckvn, w_k_ref[...])
    v = _dot(ckvn, w_v_ref[...])
    lane = lax.broadcasted_iota(jnp.int32, v.shape, 1)
    spare = jnp.where((lane // LANES) % 2 == 0, _spare_lane_base(0), _spare_lane_base(1))
    v_ref[...] = jnp.where(lane % LANES == spare, 1.0, v).astype(BF16)
    k_rope = kr * ck_ref[...] + pltpu.roll(kr, LANES - MLA_ROPE, axis=1) * sk_ref[...]
    for h in range(MLA_HEADS):
        s = slice(h * LANES, (h + 1) * LANES)
        k_ref[:, s] = (kn[:, s] + k_rope).astype(BF16)


def _even_in(h, g, w_in, gq, w_uq, gkv, w_k, w_v, tables, seq):
    n, d = h.shape
    row = lambda t: (t, 0)
    pos = lambda t: (t % (seq // TM), 0)
    tab = pl.BlockSpec((TM, LANES), pos)
    outs = [(_QW, "q"), (_QW, "k"), (_QW, "v"), (3 * MOBA_HEADS * HEAD_DIM, "moba")]
    return pl.pallas_call(
        _even_in_kernel,
        grid=(n // TM,),
        in_specs=[pl.BlockSpec((TM, d), row), _whole((1, d)), _whole(w_in.shape),
                  _whole((1, MLA_Q_LORA)), _whole(w_uq.shape), _whole((1, MLA_KV_LORA)),
                  _whole(w_k.shape), _whole(w_v.shape), tab, tab, tab, tab],
        out_specs=[pl.BlockSpec((TM, w), row) for w, _ in outs],
        out_shape=[jax.ShapeDtypeStruct((n, w), BF16) for w, _ in outs],
        compiler_params=_params("arbitrary"),
        name="even_in",
    )(h, g.reshape(1, d), w_in, gq.reshape(1, -1), w_uq, gkv.reshape(1, -1), w_k, w_v, *tables)


def _two_block_attention(q, k_of, v_of, r0, add_near):
    end = r0 + 2 * TQ
    kd, vd = k_of(r0, end), v_of(r0, end)
    s_a = add_near(_dot_nt(q[:TQ], kd[:TQ]), 0)
    s_b = _dot_nt(q[TQ:], kd)
    s_b = jnp.concatenate([add_near(s_b[:, :TQ], 1), add_near(s_b[:, TQ:], 0)], axis=1)
    if r0 > 0:
        s_off = _dot_nt(q, k_of(0, r0))
        s_a = jnp.concatenate([s_off[:TQ, :r0 - TQ], add_near(s_off[:TQ, r0 - TQ:], 1), s_a], axis=1)
        s_b = jnp.concatenate([s_off[TQ:], s_b], axis=1)
    p_a = jnp.exp2(s_a - jnp.max(s_a, axis=-1, keepdims=True)).astype(BF16)
    p_b = jnp.exp2(s_b - jnp.max(s_b, axis=-1, keepdims=True)).astype(BF16)
    acc = jnp.concatenate([_dot(p_a[:, r0:], vd[:TQ]), _dot(p_b[:, r0:], vd)], axis=0)
    if r0 > 0:
        acc = acc + _dot(jnp.concatenate([p_a[:, :r0], p_b[:, :r0]], axis=0), v_of(0, r0))
    return acc


def _mla_kernel(q_ref, k_ref, v_ref, o_ref):
    s_len = q_ref.shape[1]
    row = lax.broadcasted_iota(jnp.int32, (TQ, TQ), 0)
    col = lax.broadcasted_iota(jnp.int32, (TQ, TQ), 1)
    causal = col <= row
    add_near = lambda s, d: jnp.where(causal, s, -jnp.inf) if d == 0 else s
    for r0 in range(0, s_len, 2 * TQ):
        rows = slice(r0, r0 + 2 * TQ)
        out = jnp.zeros((2 * TQ, LANES), F32)
        for hh in range(2):
            lanes = slice(hh * LANES, (hh + 1) * LANES)
            acc = _two_block_attention(q_ref[0, rows, lanes],
                                       lambda a, b: k_ref[0, a:b, lanes],
                                       lambda a, b: v_ref[0, a:b, lanes], r0, add_near)
            out = out + _normalized(acc, hh)
        o_ref[0, rows, :] = out.astype(o_ref.dtype)


def _mla_attention(q, k, v):
    b, s, _ = q.shape
    pairs = MLA_HEADS // 2
    qkv = pl.BlockSpec((1, s, 2 * LANES), lambda bi, p: (bi, 0, p))
    return pl.pallas_call(
        _mla_kernel,
        grid=(b, pairs),
        in_specs=[qkv, qkv, qkv],
        out_specs=pl.BlockSpec((1, s, LANES), lambda bi, p: (bi, 0, p)),
        out_shape=jax.ShapeDtypeStruct((b, s, MLA_HEADS * MLA_V), BF16),
        compiler_params=_params("arbitrary", "arbitrary"),
        name="mla_attention",
    )(q, k, v)


_SOFTPLUS2_CLAMP = 64.0


def _softplus2(z):
    return jnp.maximum(z, jnp.log2(1.0 + jnp.exp2(jnp.minimum(z, _SOFTPLUS2_CLAMP))))


SB_UNDERFLOW = 160.0
SB_NEAR_BLOCKS = 2
SB_PROBE_STEPS = 16


def _sb_kernel(q_ref, k_ref, v_ref, o_ref, qm_scr, vm_scr, acc_scr, carry_scr, limit_scr, plan_scr):
    s_len = q_ref.shape[1]
    n_blk = s_len // TQ
    q, k, v = q_ref[0], k_ref[0], v_ref[0]
    feat = lax.broadcasted_iota(jnp.int32, (LANES, LANES), 0) // HEAD_DIM
    lane = lax.broadcasted_iota(jnp.int32, (LANES, LANES), 1) // HEAD_DIM
    same_head = jnp.where(feat == lane, 1.0, 0.0).astype(BF16)
    q_max2 = jnp.max(_dot(q * q, same_head), axis=0, keepdims=True)
    k_max2 = jnp.max(_dot(k * k, same_head), axis=0, keepdims=True)
    z_bound = jnp.sqrt(q_max2 * k_max2) * 1.02
    for hh in range(2):
        in_head = _head_lanes((1, LANES), hh)
        lane_mask = jnp.where(in_head, 1.0, 0.0).astype(BF16)
        qm_scr[hh] = q * lane_mask
        vm_scr[hh] = v * lane_mask
        bound = jnp.max(jnp.where(in_head, z_bound, 0.0), axis=1, keepdims=True)
        limit_scr[hh] = jnp.broadcast_to(bound + SB_UNDERFLOW, limit_scr.shape[1:])
    acc_scr[...] = jnp.zeros(acc_scr.shape, F32)

    row = lax.broadcasted_iota(jnp.int32, (TQ, TQ), 0)
    col = lax.broadcasted_iota(jnp.int32, (TQ, TQ), 1)
    strict = col < row
    tri = jnp.where(row >= col, 1.0, 0.0).astype(BF16)

    def tile_rows(j, r0, r1):
        keys = slice(j * TQ, (j + 1) * TQ)
        rows = slice(r0 * TQ, r1 * TQ)
        m = (r1 - r0) * TQ
        first = TQ if r0 == j else 0
        z = _dot_nt(jnp.concatenate([qm_scr[0, rows, :], qm_scr[1, rows, :]], axis=0),
                    k_ref[0, keys, :])
        sp = _softplus2(z)
        if first:
            parts = []
            for base in (0, m):
                parts.append(jnp.where(strict, sp[base:base + TQ], 0.0))
                if m > TQ:
                    parts.append(sp[base + TQ:base + m])
            sp = jnp.concatenate(parts, axis=0)
        c = _dot(sp.astype(BF16), tri)
        tot = jnp.broadcast_to(c[:, 0:1], (2 * m, LANES))
        w = []
        for h, base in enumerate((0, m)):
            w_h = []
            if first:
                d = slice(base, base + TQ)
                w_h.append(jnp.where(strict, jnp.exp2(z[d] - c[d]), 0.0))
                carry_scr[h, keys, :] = tot[d]
            if m > first:
                below = slice(r0 * TQ + first, r1 * TQ)
                o = slice(base + first, base + m)
                carry = carry_scr[h, below, :]
                w_h.append(jnp.exp2(z[o] - c[o] - jnp.concatenate([carry, carry], axis=1)))
                carry_scr[h, below, :] = carry + tot[o]
            w.append(w_h[0] if len(w_h) == 1 else jnp.concatenate(w_h, axis=0))
        vv = jnp.concatenate([vm_scr[0, keys, :], vm_scr[1, keys, :]], axis=0)
        acc_scr[rows, :] += _dot(jnp.concatenate(w, axis=1).astype(BF16), vv)

    def near_tiles(n_near):
        for j in range(n_blk - 1, -1, -1):
            tile_rows(j, j, min(j + n_near, n_blk))

    def live(r0, r1):
        slack = [jnp.min(carry_scr[h, r0 * TQ:r1 * TQ, :], axis=0, keepdims=True) - limit_scr[h, 0:1, :]
                 for h in range(2)]
        return jnp.min(jnp.minimum(slack[0], slack[1])) < 0.0

    def diagonal(dist):
        for i in range(dist, n_blk):
            tile_rows(i - dist, i, i + 1)

    step = pl.program_id(0) * pl.num_programs(1) + pl.program_id(1)

    @pl.when(step == 0)
    def _():
        plan_scr[0] = 0
        plan_scr[1] = 0

    wide = plan_scr[0] == 1

    @pl.when(jnp.logical_not(wide))
    def _():
        near_tiles(SB_NEAR_BLOCKS)
        flags = [live(i, i + 1) for i in range(SB_NEAR_BLOCKS, n_blk)]
        n_live = sum(f.astype(jnp.int32) for f in flags)
        plan_scr[1] += (n_live >= 2).astype(jnp.int32)
        plan_scr[2] = n_live

        @pl.when(n_live >= 2)
        def _():
            diagonal(SB_NEAR_BLOCKS)

        @pl.when(n_live == 1)
        def _():
            for i, f in zip(range(SB_NEAR_BLOCKS, n_blk), flags):
                pl.when(f)(functools.partial(tile_rows, i - SB_NEAR_BLOCKS, i, i + 1))

    @pl.when(wide)
    def _():
        near_tiles(SB_NEAR_BLOCKS + 1)
        plan_scr[2] = 1

    def farther(dist):
        @pl.when(live(dist, n_blk))
        def _():
            diagonal(dist)
            if dist + 1 < n_blk:
                farther(dist + 1)

    if SB_NEAR_BLOCKS + 1 < n_blk:
        pl.when(plan_scr[2] > 0)(functools.partial(farther, SB_NEAR_BLOCKS + 1))

    @pl.when(step == SB_PROBE_STEPS - 1)
    def _():
        plan_scr[0] = (5 * plan_scr[1] >= 3 * SB_PROBE_STEPS).astype(jnp.int32)

    o_ref[0] = acc_scr[...].astype(o_ref.dtype)


def _sb_attention(qkv):
    b, s, _ = qkv.shape
    pairs = SB_HEADS // 2
    return pl.pallas_call(
        _sb_kernel,
        grid=(b, pairs),
        in_specs=[pl.BlockSpec((1, s, LANES), lambda bi, p: (bi, 0, p)),
                  pl.BlockSpec((1, s, LANES), lambda bi, p: (bi, 0, pairs + p)),
                  pl.BlockSpec((1, s, LANES), lambda bi, p: (bi, 0, 2 * pairs + p))],
        out_specs=pl.BlockSpec((1, s, LANES), lambda bi, p: (bi, 0, p)),
        out_shape=jax.ShapeDtypeStruct((b, s, SB_HEADS * HEAD_DIM), BF16),
        scratch_shapes=[pltpu.VMEM((2, s, LANES), BF16), pltpu.VMEM((2, s, LANES), BF16),
                        pltpu.VMEM((s, LANES), F32), pltpu.VMEM((2, s, LANES), F32),
                        pltpu.VMEM((2, 8, LANES), F32), pltpu.SMEM((3,), jnp.int32)],
        compiler_params=_params("arbitrary", "arbitrary"),
        name="sb_attention",
    )(qkv, qkv, qkv)


def _t5_bucket_np(rel):
    max_exact = NUM_BUCKETS // 2
    rel = np.maximum(rel, 0)
    large = max_exact + (np.log(np.maximum(rel, max_exact) / max_exact)
                         / math.log(MAX_DISTANCE / max_exact) * (NUM_BUCKETS - max_exact)).astype(np.int64)
    return np.where(rel < max_exact, rel, np.minimum(large, NUM_BUCKETS - 1)).astype(np.int32)


def _bias_kernel(bucket_ref, rel_bias_ref, o_ref):
    h = pl.program_id(0)
    for plane in range(2):
        bk = bucket_ref[plane]
        acc = jnp.full(bk.shape, -jnp.inf, F32)
        for b in range(NUM_BUCKETS):
            acc = jnp.where(bk == b, rel_bias_ref[b, h] * LOG2_E, acc)
        o_ref[0, plane] = acc
    o_ref[0, 2] = jnp.full((TQ, TQ), rel_bias_ref[NUM_BUCKETS - 1, h] * LOG2_E, F32)


def _moba_bias_tables(rel_bias):
    off = np.arange(TQ)
    rel_own = off[:, None] - off[None, :]
    own = np.where(rel_own >= 0, _t5_bucket_np(rel_own), -1)
    adj = _t5_bucket_np(rel_own + MOBA_BLOCK)
    buckets = jnp.asarray(np.stack([own, adj]).astype(np.int32))
    return pl.pallas_call(
        _bias_kernel,
        grid=(MOBA_HEADS,),
        in_specs=[_whole(buckets.shape), pl.BlockSpec(memory_space=pltpu.SMEM)],
        out_specs=pl.BlockSpec((1, 3, TQ, TQ), lambda h: (h, 0, 0, 0)),
        out_shape=jax.ShapeDtypeStruct((MOBA_HEADS, 3, TQ, TQ), F32),
        compiler_params=_params("arbitrary"),
        name="moba_bias_tables",
    )(buckets, rel_bias)


_N_BLK_LANES = 8


def _moba_constants(nb):
    assert nb <= _N_BLK_LANES
    diff = np.zeros((LANES, LANES), np.float32)
    count = np.zeros((LANES, LANES), np.float32)
    for hh in range(2):
        for a in range(nb):
            for b in range(nb):
                if a != b:
                    pair = hh * HEAD_DIM + a * 8 + b
                    diff[pair, a] += 1.0
                    diff[pair, b] -= 1.0
                    count[pair, _spare_lane_base(hh) + b] = 1.0
                    count[pair, _spare_lane_base(hh) + _N_BLK_LANES + b] = 1.0
    return jnp.asarray(diff, BF16), jnp.asarray(count, BF16)


_T_KIND, _T_FAR_HI, _T_FAR_LO, _T_FUTURE, _T_UNSEL, _N_HEAD_TABLES = 0, 1, 2, 3, 4, 5


def _moba_fill_tables(head_tab, gate_tab, s_len):
    lane = lax.broadcasted_iota(jnp.int32, (s_len, LANES), 1)
    row_blk = lax.broadcasted_iota(jnp.int32, (s_len, LANES), 0) // MOBA_BLOCK
    for hh in range(2):
        base = _spare_lane_base(hh)
        in_hi = (lane >= base) & (lane < base + _N_BLK_LANES)
        in_lo = (lane >= base + _N_BLK_LANES) & (lane < base + 2 * _N_BLK_LANES)
        blk = jnp.where(in_lo, lane - base - _N_BLK_LANES, lane - base)
        far = blk < row_blk - 1
        store = lambda t, x: head_tab.__setitem__((hh, t), x.astype(BF16))
        store(_T_KIND, jnp.where(in_hi | in_lo, jnp.where(blk == row_blk, 1.0, 0.0), -1.0))
        store(_T_FAR_HI, jnp.where(in_hi & far, 1.0, 0.0))
        store(_T_FAR_LO, jnp.where(in_lo & far, 1.0, 0.0))
        store(_T_FUTURE, jnp.where(in_hi & (blk > row_blk), MASK_PENALTY, 0.0))
        store(_T_UNSEL, jnp.where(in_hi & (blk != row_blk), MASK_PENALTY, 0.0))
    pair_a, pair_b = (lane % HEAD_DIM) // 8, lane % 8
    valid = pair_a < row_blk
    gate_tab[0] = jnp.where(valid, 1.0, 0.0)
    gate_tab[1] = jnp.where(valid & (pair_a < pair_b), 1.0, 0.0)


def _moba_kernel(q_ref, k_ref, v_ref, bias_ref, diff_ref, count_ref, o_ref,
                 qaug_scr, kaug_scr, vm_scr, head_tab, gate_tab, *, nb):
    s_len = q_ref.shape[1]

    @pl.when((pl.program_id(0) == 0) & (pl.program_id(1) == 0))
    def _():
        _moba_fill_tables(head_tab, gate_tab, s_len)

    q, k, v = q_ref[0], k_ref[0], v_ref[0]
    kbar = jnp.mean(k.astype(F32).reshape(nb, MOBA_BLOCK, LANES), axis=1)
    kbar = jnp.concatenate([kbar, jnp.zeros((LANES - nb, LANES), F32)], axis=0)
    p0 = kbar.astype(BF16)
    r1 = kbar - p0.astype(F32)
    p1 = r1.astype(BF16)
    p2 = (r1 - p1.astype(F32)).astype(BF16)
    d = diff_ref[...]
    kdiff = _dot(d, p0) + _dot(d, p1) + _dot(d, p2)
    pair_head = lax.broadcasted_iota(jnp.int32, (LANES, LANES), 0) // HEAD_DIM
    feat_head = lax.broadcasted_iota(jnp.int32, (LANES, LANES), 1) // HEAD_DIM
    kdiff = jnp.where(pair_head == feat_head, kdiff, 0.0).astype(BF16)

    g = _dot_nt(q, kdiff)
    beats = jnp.where(g > 0.0, gate_tab[0], jnp.where(g == 0.0, gate_tab[1], 0.0))
    rank = _dot(beats.astype(BF16), count_ref[...]).astype(BF16)
    top = rank < float(MOBA_TOPK)

    lane = lax.broadcasted_iota(jnp.int32, (1, LANES), 1)
    for hh in range(2):
        lane_mask = jnp.where(_head_lanes((1, LANES), hh), 1.0, 0.0).astype(BF16)
        ones_lane = jnp.where(lane == _spare_lane_base(hh), 1.0, 0.0).astype(BF16)
        vm_scr[hh] = v * lane_mask + ones_lane
        kind = head_tab[hh, _T_KIND]
        spare = kind >= 0.0
        kaug_scr[hh] = jnp.where(spare, kind, k)
        far = bias_ref[hh, 2, 0:1, 0:LANES]
        far_hi = far.astype(BF16)
        far_lo = (far - far_hi.astype(F32)).astype(BF16)
        chosen_val = head_tab[hh, _T_FAR_HI] * far_hi + head_tab[hh, _T_FAR_LO] * far_lo + head_tab[hh, _T_FUTURE]
        qaug_scr[hh] = jnp.where(spare, jnp.where(top, chosen_val, head_tab[hh, _T_UNSEL]), q * lane_mask)

    for r0 in range(0, s_len, 2 * TQ):
        rows = slice(r0, r0 + 2 * TQ)
        out = jnp.zeros((2 * TQ, LANES), F32)
        for hh in range(2):
            acc = _two_block_attention(qaug_scr[hh, rows, :],
                                       lambda a, b: kaug_scr[hh, a:b, :],
                                       lambda a, b: vm_scr[hh, a:b, :], r0,
                                       lambda s, d: s + bias_ref[hh, d])
            out = out + _normalized(acc, hh)
        o_ref[0, rows, :] = out.astype(o_ref.dtype)


def _moba_attention(qkv, bias_tables):
    b, s, _ = qkv.shape
    nb = s // MOBA_BLOCK
    pairs = MOBA_HEADS // 2
    diff, count = _moba_constants(nb)
    return pl.pallas_call(
        functools.partial(_moba_kernel, nb=nb),
        grid=(pairs, b),
        in_specs=[pl.BlockSpec((1, s, LANES), lambda p, bi: (bi, 0, p)),
                  pl.BlockSpec((1, s, LANES), lambda p, bi: (bi, 0, pairs + p)),
                  pl.BlockSpec((1, s, LANES), lambda p, bi: (bi, 0, 2 * pairs + p)),
                  pl.BlockSpec((2, 3, TQ, TQ), lambda p, bi: (p, 0, 0, 0)),
                  _whole(diff.shape), _whole(count.shape)],
        out_specs=pl.BlockSpec((1, s, LANES), lambda p, bi: (bi, 0, p)),
        out_shape=jax.ShapeDtypeStruct((b, s, MOBA_HEADS * HEAD_DIM), BF16),
        scratch_shapes=[pltpu.VMEM((2, s, LANES), BF16)] * 3
        + [pltpu.VMEM((2, _N_HEAD_TABLES, s, LANES), BF16), pltpu.VMEM((2, s, LANES), F32)],
        compiler_params=_params("arbitrary", "arbitrary"),
        name="moba_attention",
    )(qkv, qkv, qkv, bias_tables, diff, count)


def _mix_ffn_kernel(*refs, n_in):
    a_refs = refs[:n_in]
    h_ref, w_out_ref, gmix_ref, gpre_ref, wgu_ref, wd_ref, gpost_ref, o_ref, act_scr = refs[n_in:]
    mixed, k0 = None, 0
    for a_ref in a_refs:
        k1 = k0 + a_ref.shape[1]
        part = _dot(a_ref[...], w_out_ref[k0:k1, :])
        mixed, k0 = part if mixed is None else mixed + part, k1
    x = h_ref[...] + _rms(mixed, gmix_ref[...])
    u = _rms(x, gpre_ref[...]).astype(BF16)
    for c in range(0, D_FF, FF_CHUNK):
        gate = _dot(u, wgu_ref[:, c:c + FF_CHUNK])
        up = _dot(u, wgu_ref[:, D_FF + c:D_FF + c + FF_CHUNK])
        act_scr[:, c:c + FF_CHUNK] = (gate * jax.nn.sigmoid(gate) * up).astype(BF16)
    f = _dot(act_scr[...], wd_ref[...])
    o_ref[...] = x + _rms(f, gpost_ref[...])


def _mix_ffn(acts, h, w_out, gmix, gpre, wgu, wd, gpost):
    n, d = h.shape
    row = lambda t: (t, 0)
    vec = _whole((1, d))
    return pl.pallas_call(
        functools.partial(_mix_ffn_kernel, n_in=len(acts)),
        grid=(n // TM_FFN,),
        in_specs=([pl.BlockSpec((TM_FFN, a.shape[1]), row) for a in acts]
                  + [pl.BlockSpec((TM_FFN, d), row), _whole(w_out.shape), vec, vec,
                     _whole(wgu.shape), _whole(wd.shape), vec]),
        out_specs=pl.BlockSpec((TM_FFN, d), row),
        out_shape=jax.ShapeDtypeStruct((n, d), F32),
        scratch_shapes=[pltpu.VMEM((TM_FFN, D_FF), BF16)],
        compiler_params=pltpu.CompilerParams(dimension_semantics=("arbitrary",),
                                             vmem_limit_bytes=VMEM_LIMIT_FFN),
        name="mix_ffn",
    )(*acts, h, w_out, gmix.reshape(1, d), gpre.reshape(1, d), wgu, wd, gpost.reshape(1, d))


def _rot_half_cols(w):
    half = w.shape[-1] // 2
    return jnp.concatenate([-w[..., half:], w[..., :half]], axis=-1)


def _prep_even_weights(w_in, w_uq, w_ukv):
    d = w_in.shape[0]
    o1, o2, o3 = MLA_Q_LORA, MLA_Q_LORA + MLA_KV_LORA, MLA_Q_LORA + MLA_KV_LORA + MLA_ROPE
    hw = MOBA_HEADS * HEAD_DIM
    z = lambda *shape: jnp.zeros(shape, F32)
    kr = w_in[:, o2:o3]
    w_in_ext = jnp.concatenate(
        [w_in[:, :o1], z(d, MLA_NOPE), kr, _rot_half_cols(kr), w_in[:, o1:o2],
         w_in[:, o3:o3 + hw] * (HEAD_DIM ** -0.5), w_in[:, o3 + hw:]], axis=1).astype(BF16)

    wq = w_uq.reshape(MLA_Q_LORA, MLA_HEADS, MLA_NOPE + MLA_ROPE)
    nope, rp = wq[..., :MLA_NOPE], wq[..., MLA_NOPE:]
    assert MLA_NOPE + 2 * MLA_ROPE == LANES
    w_uq_ext = jnp.concatenate([nope, rp, _rot_half_cols(rp)], axis=-1).reshape(MLA_Q_LORA, _QW).astype(BF16)

    wkv = w_ukv.reshape(MLA_KV_LORA, MLA_HEADS, MLA_NOPE + MLA_V)
    k_nope, v = wkv[..., :MLA_NOPE], wkv[..., MLA_NOPE:]
    w_k = jnp.concatenate([k_nope, jnp.zeros_like(k_nope)], axis=-1).reshape(MLA_KV_LORA, _QW).astype(BF16)
    v = v.reshape(MLA_KV_LORA, MLA_HEADS // 2, 2, MLA_V)
    zero = jnp.zeros_like(v[:, :, 0])
    w_v = jnp.stack([jnp.concatenate([v[:, :, 0], zero], axis=-1),
                     jnp.concatenate([zero, v[:, :, 1]], axis=-1)], axis=2)
    w_v = w_v.reshape(MLA_KV_LORA, _QW).astype(BF16)
    return w_in_ext, w_uq_ext, w_k, w_v


def _rope_tables(seq):
    half = MLA_ROPE // 2
    inv_freq = ROPE_THETA ** (-jnp.arange(half, dtype=F32) / half)
    ang = jnp.arange(seq).astype(F32)[:, None] * inv_freq[None, :]
    cos, sin = jnp.cos(ang), jnp.sin(ang)
    ones, zeros = jnp.ones((seq, MLA_NOPE), F32), jnp.zeros((seq, MLA_NOPE), F32)
    tail = jnp.zeros((seq, LANES - MLA_NOPE - MLA_ROPE), F32)
    scale = (MLA_NOPE + MLA_ROPE) ** -0.5 * LOG2_E
    cq = jnp.concatenate([ones, cos, cos, tail], axis=1) * scale
    sq = jnp.concatenate([zeros, sin, sin, tail], axis=1) * scale
    ck = jnp.concatenate([zeros, cos, cos, tail], axis=1)
    sk = jnp.concatenate([zeros, sin, sin, tail], axis=1)
    return cq, sq, ck, sk


def kernel(x, mix_pre_g, mix_post_g, ffn_pre_g, ffn_post_g, ab_w_in, mla_q_norm_g, mla_w_uq,
           mla_kv_norm_g, mla_w_ukv, ab_w_out, rel_bias, sb_w_qkv, sb_w_out, ffn_w_gate_up, ffn_w_down):
    b, s, d = x.shape
    n = b * s
    h = x.reshape(n, d)
    tables = _rope_tables(s)
    bias_tables = _moba_bias_tables(rel_bias)
    for layer in range(DEPTH):
        if layer % 2 == 0:
            e = layer // 2
            w_in, w_uq, w_k, w_v = _prep_even_weights(ab_w_in[e], mla_w_uq[e], mla_w_ukv[e])
            q, k, v, moba_qkv = _even_in(h, mix_pre_g[layer], w_in, mla_q_norm_g[e], w_uq,
                                         mla_kv_norm_g[e], w_k, w_v, tables, s)
            mla_out = _mla_attention(q.reshape(b, s, -1), k.reshape(b, s, -1), v.reshape(b, s, -1))
            moba_out = _moba_attention(moba_qkv.reshape(b, s, -1), bias_tables)
            mixed, w_out = [mla_out.reshape(n, -1), moba_out.reshape(n, -1)], ab_w_out[e]
        else:
            o = layer // 2
            width = SB_HEADS * HEAD_DIM
            qkv = _norm_matmul(h, mix_pre_g[layer], sb_w_qkv[o].astype(BF16),
                               width, HEAD_DIM ** -0.5 * LOG2_E)
            mixed, w_out = [_sb_attention(qkv.reshape(b, s, -1)).reshape(n, -1)], sb_w_out[o]
        h = _mix_ffn(mixed, h, w_out.astype(BF16), mix_post_g[layer], ffn_pre_g[layer],
                     ffn_w_gate_up[layer].astype(BF16), ffn_w_down[layer].astype(BF16),
                     ffn_post_g[layer])
    return h.reshape(b, s, d)
```

```python
import functools
import math

import numpy as np
import jax
import jax.numpy as jnp
from jax import lax
from jax.experimental import pallas as pl
from jax.experimental.pallas import tpu as pltpu

D_MODEL = 1024
DEPTH = 4
HEAD_DIM = 64
MLA_HEADS = 8
MLA_NOPE = 64
MLA_ROPE = 32
MLA_V = 64
MLA_Q_LORA = 384
MLA_KV_LORA = 256
ROPE_THETA = 10000.0
MOBA_HEADS = 8
MOBA_BLOCK = 256
MOBA_TOPK = 3
SB_HEADS = 16
NUM_BUCKETS = 32
MAX_DISTANCE = 128
RMS_EPS = 1e-6
D_FF = 2816

LANES = 128
TQ = 256
TM = 512
TM_FFN = 512
VMEM_LIMIT_FFN = 48 * 1024 * 1024
FF_CHUNK = 256
MASK_PENALTY = -1e30
LOG2_E = 1.0 / math.log(2.0)
VMEM_LIMIT = 48 * 1024 * 1024

F32 = jnp.float32
BF16 = jnp.bfloat16


def _rms(x, g):
    return x * lax.rsqrt(jnp.mean(x * x, axis=-1, keepdims=True) + RMS_EPS) * g


def _dot(a, b):
    return jnp.dot(a, b, preferred_element_type=F32)


def _dot_nt(a, b):
    return lax.dot_general(a, b, (((1,), (1,)), ((), ())), preferred_element_type=F32)


def _params(*semantics):
    return pltpu.CompilerParams(dimension_semantics=semantics, vmem_limit_bytes=VMEM_LIMIT)


def _whole(shape):
    nd = len(shape)
    return pl.BlockSpec(shape, lambda *_: (0,) * nd, pipeline_mode=pl.Buffered(1))


def _norm_matmul_kernel(h_ref, g_ref, w_ref, o_ref, *, q_cols, q_scale):
    u = _rms(h_ref[...], g_ref[...]).astype(BF16)
    for c in range(0, o_ref.shape[-1], 512):
        y = _dot(u, w_ref[:, c:c + 512])
        if c < q_cols:
            y = y * q_scale
        o_ref[:, c:c + 512] = y.astype(o_ref.dtype)


def _norm_matmul(h, g, w, q_cols, q_scale):
    n, d = h.shape
    n_out = w.shape[1]
    return pl.pallas_call(
        functools.partial(_norm_matmul_kernel, q_cols=q_cols, q_scale=q_scale),
        grid=(n // TM,),
        in_specs=[pl.BlockSpec((TM, d), lambda t: (t, 0)), _whole((1, d)), _whole(w.shape)],
        out_specs=pl.BlockSpec((TM, n_out), lambda t: (t, 0)),
        out_shape=jax.ShapeDtypeStruct((n, n_out), BF16),
        compiler_params=_params("arbitrary"),
        name="norm_matmul",
    )(h, g.reshape(1, d), w)


def _head_lanes(shape, hh):
    lane = lax.broadcasted_iota(jnp.int32, shape, 1)
    return (lane >= hh * HEAD_DIM) & (lane < (hh + 1) * HEAD_DIM)


def _spare_lane_base(hh):
    return HEAD_DIM * (1 - hh)


def _normalized(acc, hh):
    base = _spare_lane_base(hh)
    return jnp.where(_head_lanes(acc.shape, hh), acc / acc[:, base:base + 1], 0.0)


_O_CQ = 0
_O_KR = MLA_Q_LORA
_O_CKV = _O_KR + LANES
_O_MOBA = _O_CKV + MLA_KV_LORA
_W_IN_EXT = _O_MOBA + 3 * MOBA_HEADS * HEAD_DIM
_QW = MLA_HEADS * LANES


def _even_in_kernel(h_ref, g_ref, w_in_ref, gq_ref, w_uq_ref, gkv_ref, w_k_ref, w_v_ref,
                    cq_ref, sq_ref, ck_ref, sk_ref, q_ref, k_ref, v_ref, moba_ref):
    u = _rms(h_ref[...], g_ref[...]).astype(BF16)
    cq_kr = _dot(u, w_in_ref[:, _O_CQ:_O_CKV])
    c_q, kr = cq_kr[:, :_O_KR], cq_kr[:, _O_KR:]
    c_kv = _dot(u, w_in_ref[:, _O_CKV:_O_MOBA])
    hw = MOBA_HEADS * HEAD_DIM
    for c in range(0, 3 * hw, hw):
        y = _dot(u, w_in_ref[:, _O_MOBA + c:_O_MOBA + c + hw])
        if c == 0:
            y = y * LOG2_E
        moba_ref[:, c:c + hw] = y.astype(BF16)

    cqn = _rms(c_q, gq_ref[...]).astype(BF16)
    qa = _dot(cqn, w_uq_ref[...])
    qb = pltpu.roll(qa, _QW - MLA_ROPE, axis=1)
    cq, sq = cq_ref[...], sq_ref[...]
    for h in range(MLA_HEADS):
        s = slice(h * LANES, (h + 1) * LANES)
        q_ref[:, s] = (qa[:, s] * cq + qb[:, s] * sq).astype(BF16)

    ckvn = _rms(c_kv, gkv_ref[...]).astype(BF16)
    kn = _dot(ckvn, w_k_ref[...])
    v = _dot(ckvn, w_v_ref[...])
    lane = lax.broadcasted_iota(jnp.int32, v.shape, 1)
    spare = jnp.where((lane // LANES) % 2 == 0, _spare_lane_base(0), _spare_lane_base(1))
    v_ref[...] = jnp.where(lane % LANES == spare, 1.0, v).astype(BF16)
    k_rope = kr * ck_ref[...] + pltpu.roll(kr, LANES - MLA_ROPE, axis=1) * sk_ref[...]
    for h in range(MLA_HEADS):
        s = slice(h * LANES, (h + 1) * LANES)
        k_ref[:, s] = (kn[:, s] + k_rope).astype(BF16)


def _even_in(h, g, w_in, gq, w_uq, gkv, w_k, w_v, tables, seq):
    n, d = h.shape
    row = lambda t: (t, 0)
    pos = lambda t: (t % (seq // TM), 0)
    tab = pl.BlockSpec((TM, LANES), pos)
    outs = [(_QW, "q"), (_QW, "k"), (_QW, "v"), (3 * MOBA_HEADS * HEAD_DIM, "moba")]
    return pl.pallas_call(
        _even_in_kernel,
        grid=(n // TM,),
        in_specs=[pl.BlockSpec((TM, d), row), _whole((1, d)), _whole(w_in.shape),
                  _whole((1, MLA_Q_LORA)), _whole(w_uq.shape), _whole((1, MLA_KV_LORA)),
                  _whole(w_k.shape), _whole(w_v.shape), tab, tab, tab, tab],
        out_specs=[pl.BlockSpec((TM, w), row) for w, _ in outs],
        out_shape=[jax.ShapeDtypeStruct((n, w), BF16) for w, _ in outs],
        compiler_params=_params("arbitrary"),
        name="even_in",
    )(h, g.reshape(1, d), w_in, gq.reshape(1, -1), w_uq, gkv.reshape(1, -1), w_k, w_v, *tables)


def _two_block_attention(q, k_of, v_of, r0, add_near):
    end = r0 + 2 * TQ
    kd, vd = k_of(r0, end), v_of(r0, end)
    s_a = add_near(_dot_nt(q[:TQ], kd[:TQ]), 0)
    s_b = _dot_nt(q[TQ:], kd)
    s_b = jnp.concatenate([add_near(s_b[:, :TQ], 1), add_near(s_b[:, TQ:], 0)], axis=1)
    if r0 > 0:
        s_off = _dot_nt(q, k_of(0, r0))
        s_a = jnp.concatenate([s_off[:TQ, :r0 - TQ], add_near(s_off[:TQ, r0 - TQ:], 1), s_a], axis=1)
        s_b = jnp.concatenate([s_off[TQ:], s_b], axis=1)
    p_a = jnp.exp2(s_a - jnp.max(s_a, axis=-1, keepdims=True)).astype(BF16)
    p_b = jnp.exp2(s_b - jnp.max(s_b, axis=-1, keepdims=True)).astype(BF16)
    acc = jnp.concatenate([_dot(p_a[:, r0:], vd[:TQ]), _dot(p_b[:, r0:], vd)], axis=0)
    if r0 > 0:
        acc = acc + _dot(jnp.concatenate([p_a[:, :r0], p_b[:, :r0]], axis=0), v_of(0, r0))
    return acc


def _mla_kernel(q_ref, k_ref, v_ref, o_ref):
    s_len = q_ref.shape[1]
    row = lax.broadcasted_iota(jnp.int32, (TQ, TQ), 0)
    col = lax.broadcasted_iota(jnp.int32, (TQ, TQ), 1)
    causal = col <= row
    add_near = lambda s, d: jnp.where(causal, s, -jnp.inf) if d == 0 else s
    for r0 in range(0, s_len, 2 * TQ):
        rows = slice(r0, r0 + 2 * TQ)
        out = jnp.zeros((2 * TQ, LANES), F32)
        for hh in range(2):
            lanes = slice(hh * LANES, (hh + 1) * LANES)
            acc = _two_block_attention(q_ref[0, rows, lanes],
                                       lambda a, b: k_ref[0, a:b, lanes],
                                       lambda a, b: v_ref[0, a:b, lanes], r0, add_near)
            out = out + _normalized(acc, hh)
        o_ref[0, rows, :] = out.astype(o_ref.dtype)


def _mla_attention(q, k, v):
    b, s, _ = q.shape
    pairs = MLA_HEADS // 2
    qkv = pl.BlockSpec((1, s, 2 * LANES), lambda bi, p: (bi, 0, p))
    return pl.pallas_call(
        _mla_kernel,
        grid=(b, pairs),
        in_specs=[qkv, qkv, qkv],
        out_specs=pl.BlockSpec((1, s, LANES), lambda bi, p: (bi, 0, p)),
        out_shape=jax.ShapeDtypeStruct((b, s, MLA_HEADS * MLA_V), BF16),
        compiler_params=_params("arbitrary", "arbitrary"),
        name="mla_attention",
    )(q, k, v)


_SOFTPLUS2_CLAMP = 64.0


def _softplus2(z):
    return jnp.maximum(z, jnp.log2(1.0 + jnp.exp2(jnp.minimum(z, _SOFTPLUS2_CLAMP))))


SB_UNDERFLOW = 160.0
SB_NEAR_BLOCKS = 2
SB_PROBE_STEPS = 16


def _sb_kernel(q_ref, k_ref, v_ref, o_ref, qm_scr, vm_scr, acc_scr, carry_scr, limit_scr, plan_scr):
    s_len = q_ref.shape[1]
    n_blk = s_len // TQ
    def prepare():
        q, v = q_ref[0], v_ref[0]
        for hh in range(2):
            lane_mask = jnp.where(_head_lanes((1, LANES), hh), 1.0, 0.0).astype(BF16)
            qm_scr[hh] = q * lane_mask
            vm_scr[hh] = v * lane_mask
        acc_scr[...] = jnp.zeros(acc_scr.shape, F32)

    def set_limits():
        q, k = q_ref[0], k_ref[0]
        feat = lax.broadcasted_iota(jnp.int32, (LANES, LANES), 0) // HEAD_DIM
        lane = lax.broadcasted_iota(jnp.int32, (LANES, LANES), 1) // HEAD_DIM
        same_head = jnp.where(feat == lane, 1.0, 0.0).astype(BF16)
        q_max2 = jnp.max(_dot(q * q, same_head), axis=0, keepdims=True)
        k_max2 = jnp.max(_dot(k * k, same_head), axis=0, keepdims=True)
        z_bound = jnp.sqrt(q_max2 * k_max2) * 1.02
        for hh in range(2):
            bound = jnp.max(jnp.where(_head_lanes((1, LANES), hh), z_bound, 0.0), axis=1, keepdims=True)
            limit_scr[hh] = jnp.broadcast_to(bound + SB_UNDERFLOW, limit_scr.shape[1:])

    row = lax.broadcasted_iota(jnp.int32, (TQ, TQ), 0)
    col = lax.broadcasted_iota(jnp.int32, (TQ, TQ), 1)
    strict = col < row
    tri = jnp.where(row >= col, 1.0, 0.0).astype(BF16)

    def tile_rows(j, r0, r1):
        keys = slice(j * TQ, (j + 1) * TQ)
        rows = slice(r0 * TQ, r1 * TQ)
        m = (r1 - r0) * TQ
        first = TQ if r0 == j else 0
        z = _dot_nt(jnp.concatenate([qm_scr[0, rows, :], qm_scr[1, rows, :]], axis=0),
                    k_ref[0, keys, :])
        sp = _softplus2(z)
        if first:
            parts = []
            for base in (0, m):
                parts.append(jnp.where(strict, sp[base:base + TQ], 0.0))
                if m > TQ:
                    parts.append(sp[base + TQ:base + m])
            sp = jnp.concatenate(parts, axis=0)
        c = _dot(sp.astype(BF16), tri)
        tot = jnp.broadcast_to(c[:, 0:1], (2 * m, LANES))
        w = []
        for h, base in enumerate((0, m)):
            w_h = []
            if first:
                d = slice(base, base + TQ)
                w_h.append(jnp.where(strict, jnp.exp2(z[d] - c[d]), 0.0))
                carry_scr[h, keys, :] = tot[d]
            if m > first:
                below = slice(r0 * TQ + first, r1 * TQ)
                o = slice(base + first, base + m)
                carry = carry_scr[h, below, :]
                w_h.append(jnp.exp2(z[o] - c[o] - jnp.concatenate([carry, carry], axis=1)))
                carry_scr[h, below, :] = carry + tot[o]
            w.append(w_h[0] if len(w_h) == 1 else jnp.concatenate(w_h, axis=0))
        vv = jnp.concatenate([vm_scr[0, keys, :], vm_scr[1, keys, :]], axis=0)
        acc_scr[rows, :] += _dot(jnp.concatenate(w, axis=1).astype(BF16), vv)

    def near_tiles(n_near):
        for j in range(n_blk - 1, -1, -1):
            tile_rows(j, j, min(j + n_near, n_blk))

    def live(r0, r1):
        slack = [jnp.min(carry_scr[h, r0 * TQ:r1 * TQ, :], axis=0, keepdims=True) - limit_scr[h, 0:1, :]
                 for h in range(2)]
        return jnp.min(jnp.minimum(slack[0], slack[1])) < 0.0

    def diagonal(dist):
        for i in range(dist, n_blk):
            tile_rows(i - dist, i, i + 1)

    step = pl.program_id(0) * pl.num_programs(1) + pl.program_id(1)

    @pl.when(step == 0)
    def _():
        plan_scr[0] = 0
        plan_scr[1] = 0

    wide = plan_scr[0] == 1

    @pl.when(jnp.logical_not(wide))
    def _():
        prepare()
        set_limits()
        near_tiles(SB_NEAR_BLOCKS)
        flags = [live(i, i + 1) for i in range(SB_NEAR_BLOCKS, n_blk)]
        n_live = sum(f.astype(jnp.int32) for f in flags)
        plan_scr[1] += (n_live >= 2).astype(jnp.int32)
        plan_scr[2] = n_live

        @pl.when(n_live >= 2)
        def _():
            diagonal(SB_NEAR_BLOCKS)

        @pl.when(n_live == 1)
        def _():
            for i, f in zip(range(SB_NEAR_BLOCKS, n_blk), flags):
                pl.when(f)(functools.partial(tile_rows, i - SB_NEAR_BLOCKS, i, i + 1))

    @pl.when(wide)
    def _():
        prepare()
        set_limits()
        near_tiles(SB_NEAR_BLOCKS + 1)
        plan_scr[2] = 1

    def farther(dist):
        @pl.when(live(dist, n_blk))
        def _():
            diagonal(dist)
            if dist + 1 < n_blk:
                farther(dist + 1)

    if SB_NEAR_BLOCKS + 1 < n_blk:
        pl.when(plan_scr[2] > 0)(functools.partial(farther, SB_NEAR_BLOCKS + 1))

    @pl.when(step == SB_PROBE_STEPS - 1)
    def _():
        plan_scr[0] = (5 * plan_scr[1] >= 3 * SB_PROBE_STEPS).astype(jnp.int32)

    o_ref[0] = acc_scr[...].astype(o_ref.dtype)


def _sb_attention(qkv):
    b, s, _ = qkv.shape
    pairs = SB_HEADS // 2
    return pl.pallas_call(
        _sb_kernel,
        grid=(b, pairs),
        in_specs=[pl.BlockSpec((1, s, LANES), lambda bi, p: (bi, 0, p)),
                  pl.BlockSpec((1, s, LANES), lambda bi, p: (bi, 0, pairs + p)),
                  pl.BlockSpec((1, s, LANES), lambda bi, p: (bi, 0, 2 * pairs + p))],
        out_specs=pl.BlockSpec((1, s, LANES), lambda bi, p: (bi, 0, p)),
        out_shape=jax.ShapeDtypeStruct((b, s, SB_HEADS * HEAD_DIM), BF16),
        scratch_shapes=[pltpu.VMEM((2, s, LANES), BF16), pltpu.VMEM((2, s, LANES), BF16),
                        pltpu.VMEM((s, LANES), F32), pltpu.VMEM((2, s, LANES), F32),
                        pltpu.VMEM((2, 8, LANES), F32), pltpu.SMEM((3,), jnp.int32)],
        compiler_params=_params("arbitrary", "arbitrary"),
        name="sb_attention",
    )(qkv, qkv, qkv)


def _t5_bucket_np(rel):
    max_exact = NUM_BUCKETS // 2
    rel = np.maximum(rel, 0)
    large = max_exact + (np.log(np.maximum(rel, max_exact) / max_exact)
                         / math.log(MAX_DISTANCE / max_exact) * (NUM_BUCKETS - max_exact)).astype(np.int64)
    return np.where(rel < max_exact, rel, np.minimum(large, NUM_BUCKETS - 1)).astype(np.int32)


def _bias_kernel(bucket_ref, rel_bias_ref, o_ref):
    h = pl.program_id(0)
    for plane in range(2):
        bk = bucket_ref[plane]
        acc = jnp.full(bk.shape, -jnp.inf, F32)
        for b in range(NUM_BUCKETS):
            acc = jnp.where(bk == b, rel_bias_ref[b, h] * LOG2_E, acc)
        o_ref[0, plane] = acc
    o_ref[0, 2] = jnp.full((TQ, TQ), rel_bias_ref[NUM_BUCKETS - 1, h] * LOG2_E, F32)


def _moba_bias_tables(rel_bias):
    off = np.arange(TQ)
    rel_own = off[:, None] - off[None, :]
    own = np.where(rel_own >= 0, _t5_bucket_np(rel_own), -1)
    adj = _t5_bucket_np(rel_own + MOBA_BLOCK)
    buckets = jnp.asarray(np.stack([own, adj]).astype(np.int32))
    return pl.pallas_call(
        _bias_kernel,
        grid=(MOBA_HEADS,),
        in_specs=[_whole(buckets.shape), pl.BlockSpec(memory_space=pltpu.SMEM)],
        out_specs=pl.BlockSpec((1, 3, TQ, TQ), lambda h: (h, 0, 0, 0)),
        out_shape=jax.ShapeDtypeStruct((MOBA_HEADS, 3, TQ, TQ), F32),
        compiler_params=_params("arbitrary"),
        name="moba_bias_tables",
    )(buckets, rel_bias)


_N_BLK_LANES = 8


def _moba_constants(nb):
    assert nb <= _N_BLK_LANES
    diff = np.zeros((LANES, LANES), np.float32)
    count = np.zeros((LANES, LANES), np.float32)
    for hh in range(2):
        for a in range(nb):
            for b in range(nb):
                if a != b:
                    pair = hh * HEAD_DIM + a * 8 + b
                    diff[pair, a] += 1.0
                    diff[pair, b] -= 1.0
                    count[pair, _spare_lane_base(hh) + b] = 1.0
                    count[pair, _spare_lane_base(hh) + _N_BLK_LANES + b] = 1.0
    return jnp.asarray(diff, BF16), jnp.asarray(count, BF16)


_T_KIND, _T_FAR_HI, _T_FAR_LO, _T_FUTURE, _T_UNSEL, _N_HEAD_TABLES = 0, 1, 2, 3, 4, 5


def _moba_fill_tables(head_tab, gate_tab, s_len):
    lane = lax.broadcasted_iota(jnp.int32, (s_len, LANES), 1)
    row_blk = lax.broadcasted_iota(jnp.int32, (s_len, LANES), 0) // MOBA_BLOCK
    for hh in range(2):
        base = _spare_lane_base(hh)
        in_hi = (lane >= base) & (lane < base + _N_BLK_LANES)
        in_lo = (lane >= base + _N_BLK_LANES) & (lane < base + 2 * _N_BLK_LANES)
        blk = jnp.where(in_lo, lane - base - _N_BLK_LANES, lane - base)
        far = blk < row_blk - 1
        store = lambda t, x: head_tab.__setitem__((hh, t), x.astype(BF16))
        store(_T_KIND, jnp.where(in_hi | in_lo, jnp.where(blk == row_blk, 1.0, 0.0), -1.0))
        store(_T_FAR_HI, jnp.where(in_hi & far, 1.0, 0.0))
        store(_T_FAR_LO, jnp.where(in_lo & far, 1.0, 0.0))
        store(_T_FUTURE, jnp.where(in_hi & (blk > row_blk), MASK_PENALTY, 0.0))
        store(_T_UNSEL, jnp.where(in_hi & (blk != row_blk), MASK_PENALTY, 0.0))
    pair_a, pair_b = (lane % HEAD_DIM) // 8, lane % 8
    valid = pair_a < row_blk
    gate_tab[0] = jnp.where(valid, 1.0, 0.0)
    gate_tab[1] = jnp.where(valid & (pair_a < pair_b), 1.0, 0.0)


def _moba_kernel(q_ref, k_ref, v_ref, bias_ref, diff_ref, count_ref, o_ref,
                 qaug_scr, kaug_scr, vm_scr, head_tab, gate_tab, *, nb):
    s_len = q_ref.shape[1]

    @pl.when((pl.program_id(0) == 0) & (pl.program_id(1) == 0))
    def _():
        _moba_fill_tables(head_tab, gate_tab, s_len)

    q, k, v = q_ref[0], k_ref[0], v_ref[0]
    kbar = jnp.mean(k.astype(F32).reshape(nb, MOBA_BLOCK, LANES), axis=1)
    kbar = jnp.concatenate([kbar, jnp.zeros((LANES - nb, LANES), F32)], axis=0)
    p0 = kbar.astype(BF16)
    r1 = kbar - p0.astype(F32)
    p1 = r1.astype(BF16)
    p2 = (r1 - p1.astype(F32)).astype(BF16)
    d = diff_ref[...]
    kdiff = _dot(d, p0) + _dot(d, p1) + _dot(d, p2)
    pair_head = lax.broadcasted_iota(jnp.int32, (LANES, LANES), 0) // HEAD_DIM
    feat_head = lax.broadcasted_iota(jnp.int32, (LANES, LANES), 1) // HEAD_DIM
    kdiff = jnp.where(pair_head == feat_head, kdiff, 0.0).astype(BF16)

    g = _dot_nt(q, kdiff)
    beats = jnp.where(g > 0.0, gate_tab[0], jnp.where(g == 0.0, gate_tab[1], 0.0))
    rank = _dot(beats.astype(BF16), count_ref[...]).astype(BF16)
    top = rank < float(MOBA_TOPK)

    lane = lax.broadcasted_iota(jnp.int32, (1, LANES), 1)
    for hh in range(2):
        lane_mask = jnp.where(_head_lanes((1, LANES), hh), 1.0, 0.0).astype(BF16)
        ones_lane = jnp.where(lane == _spare_lane_base(hh), 1.0, 0.0).astype(BF16)
        vm_scr[hh] = v * lane_mask + ones_lane
        kind = head_tab[hh, _T_KIND]
        spare = kind >= 0.0
        kaug_scr[hh] = jnp.where(spare, kind, k)
        far = bias_ref[hh, 2, 0:1, 0:LANES]
        far_hi = far.astype(BF16)
        far_lo = (far - far_hi.astype(F32)).astype(BF16)
        chosen_val = head_tab[hh, _T_FAR_HI] * far_hi + head_tab[hh, _T_FAR_LO] * far_lo + head_tab[hh, _T_FUTURE]
        qaug_scr[hh] = jnp.where(spare, jnp.where(top, chosen_val, head_tab[hh, _T_UNSEL]), q * lane_mask)

    for r0 in range(0, s_len, 2 * TQ):
        rows = slice(r0, r0 + 2 * TQ)
        out = jnp.zeros((2 * TQ, LANES), F32)
        for hh in range(2):
            acc = _two_block_attention(qaug_scr[hh, rows, :],
                                       lambda a, b: kaug_scr[hh, a:b, :],
                                       lambda a, b: vm_scr[hh, a:b, :], r0,
                                       lambda s, d: s + bias_ref[hh, d])
            out = out + _normalized(acc, hh)
        o_ref[0, rows, :] = out.astype(o_ref.dtype)


def _moba_attention(qkv, bias_tables):
    b, s, _ = qkv.shape
    nb = s // MOBA_BLOCK
    pairs = MOBA_HEADS // 2
    diff, count = _moba_constants(nb)
    return pl.pallas_call(
        functools.partial(_moba_kernel, nb=nb),
        grid=(pairs, b),
        in_specs=[pl.BlockSpec((1, s, LANES), lambda p, bi: (bi, 0, p)),
                  pl.BlockSpec((1, s, LANES), lambda p, bi: (bi, 0, pairs + p)),
                  pl.BlockSpec((1, s, LANES), lambda p, bi: (bi, 0, 2 * pairs + p)),
                  pl.BlockSpec((2, 3, TQ, TQ), lambda p, bi: (p, 0, 0, 0)),
                  _whole(diff.shape), _whole(count.shape)],
        out_specs=pl.BlockSpec((1, s, LANES), lambda p, bi: (bi, 0, p)),
        out_shape=jax.ShapeDtypeStruct((b, s, MOBA_HEADS * HEAD_DIM), BF16),
        scratch_shapes=[pltpu.VMEM((2, s, LANES), BF16)] * 3
        + [pltpu.VMEM((2, _N_HEAD_TABLES, s, LANES), BF16), pltpu.VMEM((2, s, LANES), F32)],
        compiler_params=_params("arbitrary", "arbitrary"),
        name="moba_attention",
    )(qkv, qkv, qkv, bias_tables, diff, count)


def _mix_ffn_kernel(*refs, n_in):
    a_refs = refs[:n_in]
    h_ref, w_out_ref, gmix_ref, gpre_ref, wgu_ref, wd_ref, gpost_ref, o_ref, act_scr = refs[n_in:]
    mixed, k0 = None, 0
    for a_ref in a_refs:
        k1 = k0 + a_ref.shape[1]
        part = _dot(a_ref[...], w_out_ref[k0:k1, :])
        mixed, k0 = part if mixed is None else mixed + part, k1
    x = h_ref[...] + _rms(mixed, gmix_ref[...])
    u = _rms(x, gpre_ref[...]).astype(BF16)
    for c in range(0, D_FF, FF_CHUNK):
        gate = _dot(u, wgu_ref[:, c:c + FF_CHUNK])
        up = _dot(u, wgu_ref[:, D_FF + c:D_FF + c + FF_CHUNK])
        act_scr[:, c:c + FF_CHUNK] = (gate * jax.nn.sigmoid(gate) * up).astype(BF16)
    f = _dot(act_scr[...], wd_ref[...])
    o_ref[...] = x + _rms(f, gpost_ref[...])


def _mix_ffn(acts, h, w_out, gmix, gpre, wgu, wd, gpost):
    n, d = h.shape
    row = lambda t: (t, 0)
    vec = _whole((1, d))
    return pl.pallas_call(
        functools.partial(_mix_ffn_kernel, n_in=len(acts)),
        grid=(n // TM_FFN,),
        in_specs=([pl.BlockSpec((TM_FFN, a.shape[1]), row) for a in acts]
                  + [pl.BlockSpec((TM_FFN, d), row), _whole(w_out.shape), vec, vec,
                     _whole(wgu.shape), _whole(wd.shape), vec]),
        out_specs=pl.BlockSpec((TM_FFN, d), row),
        out_shape=jax.ShapeDtypeStruct((n, d), F32),
        scratch_shapes=[pltpu.VMEM((TM_FFN, D_FF), BF16)],
        compiler_params=pltpu.CompilerParams(dimension_semantics=("arbitrary",),
                                             vmem_limit_bytes=VMEM_LIMIT_FFN),
        name="mix_ffn",
    )(*acts, h, w_out, gmix.reshape(1, d), gpre.reshape(1, d), wgu, wd, gpost.reshape(1, d))


def _rot_half_cols(w):
    half = w.shape[-1] // 2
    return jnp.concatenate([-w[..., half:], w[..., :half]], axis=-1)


def _prep_even_weights(w_in, w_uq, w_ukv):
    d = w_in.shape[0]
    o1, o2, o3 = MLA_Q_LORA, MLA_Q_LORA + MLA_KV_LORA, MLA_Q_LORA + MLA_KV_LORA + MLA_ROPE
    hw = MOBA_HEADS * HEAD_DIM
    z = lambda *shape: jnp.zeros(shape, F32)
    kr = w_in[:, o2:o3]
    w_in_ext = jnp.concatenate(
        [w_in[:, :o1], z(d, MLA_NOPE), kr, _rot_half_cols(kr), w_in[:, o1:o2],
         w_in[:, o3:o3 + hw] * (HEAD_DIM ** -0.5), w_in[:, o3 + hw:]], axis=1).astype(BF16)

    wq = w_uq.reshape(MLA_Q_LORA, MLA_HEADS, MLA_NOPE + MLA_ROPE)
    nope, rp = wq[..., :MLA_NOPE], wq[..., MLA_NOPE:]
    assert MLA_NOPE + 2 * MLA_ROPE == LANES
    w_uq_ext = jnp.concatenate([nope, rp, _rot_half_cols(rp)], axis=-1).reshape(MLA_Q_LORA, _QW).astype(BF16)

    wkv = w_ukv.reshape(MLA_KV_LORA, MLA_HEADS, MLA_NOPE + MLA_V)
    k_nope, v = wkv[..., :MLA_NOPE], wkv[..., MLA_NOPE:]
    w_k = jnp.concatenate([k_nope, jnp.zeros_like(k_nope)], axis=-1).reshape(MLA_KV_LORA, _QW).astype(BF16)
    v = v.reshape(MLA_KV_LORA, MLA_HEADS // 2, 2, MLA_V)
    zero = jnp.zeros_like(v[:, :, 0])
    w_v = jnp.stack([jnp.concatenate([v[:, :, 0], zero], axis=-1),
                     jnp.concatenate([zero, v[:, :, 1]], axis=-1)], axis=2)
    w_v = w_v.reshape(MLA_KV_LORA, _QW).astype(BF16)
    return w_in_ext, w_uq_ext, w_k, w_v


def _rope_tables(seq):
    half = MLA_ROPE // 2
    inv_freq = ROPE_THETA ** (-jnp.arange(half, dtype=F32) / half)
    ang = jnp.arange(seq).astype(F32)[:, None] * inv_freq[None, :]
    cos, sin = jnp.cos(ang), jnp.sin(ang)
    ones, zeros = jnp.ones((seq, MLA_NOPE), F32), jnp.zeros((seq, MLA_NOPE), F32)
    tail = jnp.zeros((seq, LANES - MLA_NOPE - MLA_ROPE), F32)
    scale = (MLA_NOPE + MLA_ROPE) ** -0.5 * LOG2_E
    cq = jnp.concatenate([ones, cos, cos, tail], axis=1) * scale
    sq = jnp.concatenate([zeros, sin, sin, tail], axis=1) * scale
    ck = jnp.concatenate([zeros, cos, cos, tail], axis=1)
    sk = jnp.concatenate([zeros, sin, sin, tail], axis=1)
    return cq, sq, ck, sk


def kernel(x, mix_pre_g, mix_post_g, ffn_pre_g, ffn_post_g, ab_w_in, mla_q_norm_g, mla_w_uq,
           mla_kv_norm_g, mla_w_ukv, ab_w_out, rel_bias, sb_w_qkv, sb_w_out, ffn_w_gate_up, ffn_w_down):
    b, s, d = x.shape
    n = b * s
    h = x.reshape(n, d)
    tables = _rope_tables(s)
    bias_tables = _moba_bias_tables(rel_bias)
    for layer in range(DEPTH):
        if layer % 2 == 0:
            e = layer // 2
            w_in, w_uq, w_k, w_v = _prep_even_weights(ab_w_in[e], mla_w_uq[e], mla_w_ukv[e])
            q, k, v, moba_qkv = _even_in(h, mix_pre_g[layer], w_in, mla_q_norm_g[e], w_uq,
                                         mla_kv_norm_g[e], w_k, w_v, tables, s)
            mla_out = _mla_attention(q.reshape(b, s, -1), k.reshape(b, s, -1), v.reshape(b, s, -1))
            moba_out = _moba_attention(moba_qkv.reshape(b, s, -1), bias_tables)
            mixed, w_out = [mla_out.reshape(n, -1), moba_out.reshape(n, -1)], ab_w_out[e]
        else:
            o = layer // 2
            width = SB_HEADS * HEAD_DIM
            qkv = _norm_matmul(h, mix_pre_g[layer], sb_w_qkv[o].astype(BF16),
                               width, HEAD_DIM ** -0.5 * LOG2_E)
            mixed, w_out = [_sb_attention(qkv.reshape(b, s, -1)).reshape(n, -1)], sb_w_out[o]
        h = _mix_ffn(mixed, h, w_out.astype(BF16), mix_post_g[layer], ffn_pre_g[layer],
                     ffn_w_gate_up[layer].astype(BF16), ffn_w_down[layer].astype(BF16),
                     ffn_post_g[layer])
    return h.reshape(b, s, d)
```

```python
import functools
import math

import numpy as np
import jax
import jax.numpy as jnp
from jax import lax
from jax.experimental import pallas as pl
from jax.experimental.pallas import tpu as pltpu

D_MODEL = 1024
DEPTH = 4
HEAD_DIM = 64
MLA_HEADS = 8
MLA_NOPE = 64
MLA_ROPE = 32
MLA_V = 64
MLA_Q_LORA = 384
MLA_KV_LORA = 256
ROPE_THETA = 10000.0
MOBA_HEADS = 8
MOBA_BLOCK = 256
MOBA_TOPK = 3
SB_HEADS = 16
NUM_BUCKETS = 32
MAX_DISTANCE = 128
RMS_EPS = 1e-6
D_FF = 2816

LANES = 128
TQ = 256
TM = 512
TM_FFN = 512
VMEM_LIMIT_FFN = 48 * 1024 * 1024
FF_CHUNK = 256
MASK_PENALTY = -1e30
LOG2_E = 1.0 / math.log(2.0)
VMEM_LIMIT = 48 * 1024 * 1024

F32 = jnp.float32
BF16 = jnp.bfloat16


def _rms(x, g):
    return x * lax.rsqrt(jnp.mean(x * x, axis=-1, keepdims=True) + RMS_EPS) * g


def _dot(a, b):
    return jnp.dot(a, b, preferred_element_type=F32)


def _dot_nt(a, b):
    return lax.dot_general(a, b, (((1,), (1,)), ((), ())), preferred_element_type=F32)


def _params(*semantics):
    return pltpu.CompilerParams(dimension_semantics=semantics, vmem_limit_bytes=VMEM_LIMIT)


def _whole(shape):
    nd = len(shape)
    return pl.BlockSpec(shape, lambda *_: (0,) * nd, pipeline_mode=pl.Buffered(1))


def _one_layer(stacked_shape, layer):
    return pl.BlockSpec((None,) + tuple(stacked_shape[1:]), lambda *_: (layer, 0, 0),
                        pipeline_mode=pl.Buffered(1))


def _norm_matmul_kernel(h_ref, g_ref, w_ref, o_ref, *, q_cols, q_scale):
    u = _rms(h_ref[...], g_ref[...]).astype(BF16)
    for c in range(0, o_ref.shape[-1], 512):
        y = _dot(u, w_ref[:, c:c + 512])
        if c < q_cols:
            y = y * q_scale
        o_ref[:, c:c + 512] = y.astype(o_ref.dtype)


def _norm_matmul(h, g, w_stack, layer, q_cols, q_scale):
    n, d = h.shape
    n_out = w_stack.shape[2]
    return pl.pallas_call(
        functools.partial(_norm_matmul_kernel, q_cols=q_cols, q_scale=q_scale),
        grid=(n // TM,),
        in_specs=[pl.BlockSpec((TM, d), lambda t: (t, 0)), _whole((1, d)),
                  _one_layer(w_stack.shape, layer)],
        out_specs=pl.BlockSpec((TM, n_out), lambda t: (t, 0)),
        out_shape=jax.ShapeDtypeStruct((n, n_out), BF16),
        compiler_params=_params("arbitrary"),
        name="norm_matmul",
    )(h, g.reshape(1, d), w_stack)


def _head_lanes(shape, hh):
    lane = lax.broadcasted_iota(jnp.int32, shape, 1)
    return (lane >= hh * HEAD_DIM) & (lane < (hh + 1) * HEAD_DIM)


def _spare_lane_base(hh):
    return HEAD_DIM * (1 - hh)


def _normalized(acc, hh):
    base = _spare_lane_base(hh)
    return jnp.where(_head_lanes(acc.shape, hh), acc / acc[:, base:base + 1], 0.0)


_O_CQ = 0
_O_KR = MLA_Q_LORA
_O_CKV = _O_KR + LANES
_O_MOBA = _O_CKV + MLA_KV_LORA
_W_IN_EXT = _O_MOBA + 3 * MOBA_HEADS * HEAD_DIM
_QW = MLA_HEADS * LANES


def _even_in_kernel(h_ref, g_ref, w_in_ref, gq_ref, w_uq_ref, gkv_ref, w_k_ref, w_v_ref,
                    cq_ref, sq_ref, ck_ref, sk_ref, q_ref, k_ref, v_ref, moba_ref):
    u = _rms(h_ref[...], g_ref[...]).astype(BF16)
    cq_kr = _dot(u, w_in_ref[:, _O_CQ:_O_CKV])
    c_q, kr = cq_kr[:, :_O_KR], cq_kr[:, _O_KR:]
    c_kv = _dot(u, w_in_ref[:, _O_CKV:_O_MOBA])
    hw = MOBA_HEADS * HEAD_DIM
    for c in range(0, 3 * hw, hw):
        y = _dot(u, w_in_ref[:, _O_MOBA + c:_O_MOBA + c + hw])
        if c == 0:
            y = y * LOG2_E
        moba_ref[:, c:c + hw] = y.astype(BF16)

    cqn = _rms(c_q, gq_ref[...]).astype(BF16)
    qa = _dot(cqn, w_uq_ref[...])
    qb = pltpu.roll(qa, _QW - MLA_ROPE, axis=1)
    cq, sq = cq_ref[...], sq_ref[...]
    for h in range(MLA_HEADS):
        s = slice(h * LANES, (h + 1) * LANES)
        q_ref[:, s] = (qa[:, s] * cq + qb[:, s] * sq).astype(BF16)

    ckvn = _rms(c_kv, gkv_ref[...]).astype(BF16)
    kn = _dot(ckvn, w_k_ref[...])
    v = _dot(ckvn, w_v_ref[...])
    lane = lax.broadcasted_iota(jnp.int32, v.shape, 1)
    spare = jnp.where((lane // LANES) % 2 == 0, _spare_lane_base(0), _spare_lane_base(1))
    v_ref[...] = jnp.where(lane % LANES == spare, 1.0, v).astype(BF16)
    k_rope = kr * ck_ref[...] + pltpu.roll(kr, LANES - MLA_ROPE, axis=1) * sk_ref[...]
    for h in range(MLA_HEADS):
        s = slice(h * LANES, (h + 1) * LANES)
        k_ref[:, s] = (kn[:, s] + k_rope).astype(BF16)


def _even_in(h, g, w_in, gq, w_uq, gkv, w_k, w_v, tables, seq):
    n, d = h.shape
    row = lambda t: (t, 0)
    pos = lambda t: (t % (seq // TM), 0)
    tab = pl.BlockSpec((TM, LANES), pos)
    outs = [(_QW, "q"), (_QW, "k"), (_QW, "v"), (3 * MOBA_HEADS * HEAD_DIM, "moba")]
    return pl.pallas_call(
        _even_in_kernel,
        grid=(n // TM,),
        in_specs=[pl.BlockSpec((TM, d), row), _whole((1, d)), _whole(w_in.shape),
                  _whole((1, MLA_Q_LORA)), _whole(w_uq.shape), _whole((1, MLA_KV_LORA)),
                  _whole(w_k.shape), _whole(w_v.shape), tab, tab, tab, tab],
        out_specs=[pl.BlockSpec((TM, w), row) for w, _ in outs],
        out_shape=[jax.ShapeDtypeStruct((n, w), BF16) for w, _ in outs],
        compiler_params=_params("arbitrary"),
        name="even_in",
    )(h, g.reshape(1, d), w_in, gq.reshape(1, -1), w_uq, gkv.reshape(1, -1), w_k, w_v, *tables)


def _two_block_attention(q, k_of, v_of, r0, add_near):
    end = r0 + 2 * TQ
    kd, vd = k_of(r0, end), v_of(r0, end)
    s_a = add_near(_dot_nt(q[:TQ], kd[:TQ]), 0)
    s_b = _dot_nt(q[TQ:], kd)
    s_b = jnp.concatenate([add_near(s_b[:, :TQ], 1), add_near(s_b[:, TQ:], 0)], axis=1)
    if r0 > 0:
        s_off = _dot_nt(q, k_of(0, r0))
        s_a = jnp.concatenate([s_off[:TQ, :r0 - TQ], add_near(s_off[:TQ, r0 - TQ:], 1), s_a], axis=1)
        s_b = jnp.concatenate([s_off[TQ:], s_b], axis=1)
    p_a = jnp.exp2(s_a - jnp.max(s_a, axis=-1, keepdims=True)).astype(BF16)
    p_b = jnp.exp2(s_b - jnp.max(s_b, axis=-1, keepdims=True)).astype(BF16)
    acc = jnp.concatenate([_dot(p_a[:, r0:], vd[:TQ]), _dot(p_b[:, r0:], vd)], axis=0)
    if r0 > 0:
        acc = acc + _dot(jnp.concatenate([p_a[:, :r0], p_b[:, :r0]], axis=0), v_of(0, r0))
    return acc


def _mla_kernel(q_ref, k_ref, v_ref, o_ref):
    s_len = q_ref.shape[1]
    row = lax.broadcasted_iota(jnp.int32, (TQ, TQ), 0)
    col = lax.broadcasted_iota(jnp.int32, (TQ, TQ), 1)
    causal = col <= row
    add_near = lambda s, d: jnp.where(causal, s, -jnp.inf) if d == 0 else s
    for r0 in range(0, s_len, 2 * TQ):
        rows = slice(r0, r0 + 2 * TQ)
        out = jnp.zeros((2 * TQ, LANES), F32)
        for hh in range(2):
            lanes = slice(hh * LANES, (hh + 1) * LANES)
            acc = _two_block_attention(q_ref[0, rows, lanes],
                                       lambda a, b: k_ref[0, a:b, lanes],
                                       lambda a, b: v_ref[0, a:b, lanes], r0, add_near)
            out = out + _normalized(acc, hh)
        o_ref[0, rows, :] = out.astype(o_ref.dtype)


def _mla_attention(q, k, v):
    b, s, _ = q.shape
    pairs = MLA_HEADS // 2
    qkv = pl.BlockSpec((1, s, 2 * LANES), lambda bi, p: (bi, 0, p))
    return pl.pallas_call(
        _mla_kernel,
        grid=(b, pairs),
        in_specs=[qkv, qkv, qkv],
        out_specs=pl.BlockSpec((1, s, LANES), lambda bi, p: (bi, 0, p)),
        out_shape=jax.ShapeDtypeStruct((b, s, MLA_HEADS * MLA_V), BF16),
        compiler_params=_params("arbitrary", "arbitrary"),
        name="mla_attention",
    )(q, k, v)


_SOFTPLUS2_CLAMP = 64.0


def _softplus2(z):
    return jnp.maximum(z, jnp.log2(1.0 + jnp.exp2(jnp.minimum(z, _SOFTPLUS2_CLAMP))))


SB_UNDERFLOW = 160.0
SB_NEAR_BLOCKS = 2
SB_PROBE_STEPS = 16


def _sb_kernel(q_ref, k_ref, v_ref, o_ref, qm_scr, vm_scr, acc_scr, carry_scr, limit_scr, plan_scr):
    s_len = q_ref.shape[1]
    n_blk = s_len // TQ
    def prepare():
        q, v = q_ref[0], v_ref[0]
        for hh in range(2):
            lane_mask = jnp.where(_head_lanes((1, LANES), hh), 1.0, 0.0).astype(BF16)
            qm_scr[hh] = q * lane_mask
            vm_scr[hh] = v * lane_mask
        acc_scr[...] = jnp.zeros(acc_scr.shape, F32)

    def set_limits():
        q, k = q_ref[0], k_ref[0]
        feat = lax.broadcasted_iota(jnp.int32, (LANES, LANES), 0) // HEAD_DIM
        lane = lax.broadcasted_iota(jnp.int32, (LANES, LANES), 1) // HEAD_DIM
        same_head = jnp.where(feat == lane, 1.0, 0.0).astype(BF16)
        q_max2 = jnp.max(_dot(q * q, same_head), axis=0, keepdims=True)
        k_max2 = jnp.max(_dot(k * k, same_head), axis=0, keepdims=True)
        z_bound = jnp.sqrt(q_max2 * k_max2) * 1.02
        for hh in range(2):
            bound = jnp.max(jnp.where(_head_lanes((1, LANES), hh), z_bound, 0.0), axis=1, keepdims=True)
            limit_scr[hh] = jnp.broadcast_to(bound + SB_UNDERFLOW, limit_scr.shape[1:])

    row = lax.broadcasted_iota(jnp.int32, (TQ, TQ), 0)
    col = lax.broadcasted_iota(jnp.int32, (TQ, TQ), 1)
    strict = col < row
    tri = jnp.where(row >= col, 1.0, 0.0).astype(BF16)

    def tile_rows(j, r0, r1):
        keys = slice(j * TQ, (j + 1) * TQ)
        rows = slice(r0 * TQ, r1 * TQ)
        m = (r1 - r0) * TQ
        first = TQ if r0 == j else 0
        z = _dot_nt(jnp.concatenate([qm_scr[0, rows, :], qm_scr[1, rows, :]], axis=0),
                    k_ref[0, keys, :])
        sp = _softplus2(z)
        if first:
            parts = []
            for base in (0, m):
                parts.append(jnp.where(strict, sp[base:base + TQ], 0.0))
                if m > TQ:
                    parts.append(sp[base + TQ:base + m])
            sp = jnp.concatenate(parts, axis=0)
        c = _dot(sp.astype(BF16), tri)
        tot = jnp.broadcast_to(c[:, 0:1], (2 * m, LANES))
        w = []
        for h, base in enumerate((0, m)):
            w_h = []
            if first:
                d = slice(base, base + TQ)
                w_h.append(jnp.where(strict, jnp.exp2(z[d] - c[d]), 0.0))
                carry_scr[h, keys, :] = tot[d]
            if m > first:
                below = slice(r0 * TQ + first, r1 * TQ)
                o = slice(base + first, base + m)
                carry = carry_scr[h, below, :]
                w_h.append(jnp.exp2(z[o] - c[o] - jnp.concatenate([carry, carry], axis=1)))
                carry_scr[h, below, :] = carry + tot[o]
            w.append(w_h[0] if len(w_h) == 1 else jnp.concatenate(w_h, axis=0))
        vv = jnp.concatenate([vm_scr[0, keys, :], vm_scr[1, keys, :]], axis=0)
        acc_scr[rows, :] += _dot(jnp.concatenate(w, axis=1).astype(BF16), vv)

    def near_tiles(n_near):
        for j in range(n_blk - 1, -1, -1):
            tile_rows(j, j, min(j + n_near, n_blk))

    def live(r0, r1):
        slack = [jnp.min(carry_scr[h, r0 * TQ:r1 * TQ, :], axis=0, keepdims=True) - limit_scr[h, 0:1, :]
                 for h in range(2)]
        return jnp.min(jnp.minimum(slack[0], slack[1])) < 0.0

    def diagonal(dist):
        for i in range(dist, n_blk):
            tile_rows(i - dist, i, i + 1)

    step = pl.program_id(0) * pl.num_programs(1) + pl.program_id(1)

    @pl.when(step == 0)
    def _():
        plan_scr[0] = 0
        plan_scr[1] = 0

    wide = plan_scr[0] == 1

    @pl.when(jnp.logical_not(wide))
    def _():
        prepare()
        set_limits()
        near_tiles(SB_NEAR_BLOCKS)
        flags = [live(i, i + 1) for i in range(SB_NEAR_BLOCKS, n_blk)]
        n_live = sum(f.astype(jnp.int32) for f in flags)
        plan_scr[1] += (n_live >= 2).astype(jnp.int32)
        plan_scr[2] = n_live

        @pl.when(n_live >= 2)
        def _():
            diagonal(SB_NEAR_BLOCKS)

        @pl.when(n_live == 1)
        def _():
            for i, f in zip(range(SB_NEAR_BLOCKS, n_blk), flags):
                pl.when(f)(functools.partial(tile_rows, i - SB_NEAR_BLOCKS, i, i + 1))

    @pl.when(wide)
    def _():
        prepare()
        set_limits()
        near_tiles(SB_NEAR_BLOCKS + 1)
        plan_scr[2] = 1

    def farther(dist):
        @pl.when(live(dist, n_blk))
        def _():
            diagonal(dist)
            if dist + 1 < n_blk:
                farther(dist + 1)

    if SB_NEAR_BLOCKS + 1 < n_blk:
        pl.when(plan_scr[2] > 0)(functools.partial(farther, SB_NEAR_BLOCKS + 1))

    @pl.when(step == SB_PROBE_STEPS - 1)
    def _():
        plan_scr[0] = (5 * plan_scr[1] >= 3 * SB_PROBE_STEPS).astype(jnp.int32)

    o_ref[0] = acc_scr[...].astype(o_ref.dtype)


def _sb_attention(qkv):
    b, s, _ = qkv.shape
    pairs = SB_HEADS // 2
    return pl.pallas_call(
        _sb_kernel,
        grid=(b, pairs),
        in_specs=[pl.BlockSpec((1, s, LANES), lambda bi, p: (bi, 0, p)),
                  pl.BlockSpec((1, s, LANES), lambda bi, p: (bi, 0, pairs + p)),
                  pl.BlockSpec((1, s, LANES), lambda bi, p: (bi, 0, 2 * pairs + p))],
        out_specs=pl.BlockSpec((1, s, LANES), lambda bi, p: (bi, 0, p)),
        out_shape=jax.ShapeDtypeStruct((b, s, SB_HEADS * HEAD_DIM), BF16),
        scratch_shapes=[pltpu.VMEM((2, s, LANES), BF16), pltpu.VMEM((2, s, LANES), BF16),
                        pltpu.VMEM((s, LANES), F32), pltpu.VMEM((2, s, LANES), F32),
                        pltpu.VMEM((2, 8, LANES), F32), pltpu.SMEM((3,), jnp.int32)],
        compiler_params=_params("arbitrary", "arbitrary"),
        name="sb_attention",
    )(qkv, qkv, qkv)


def _t5_bucket_np(rel):
    max_exact = NUM_BUCKETS // 2
    rel = np.maximum(rel, 0)
    large = max_exact + (np.log(np.maximum(rel, max_exact) / max_exact)
                         / math.log(MAX_DISTANCE / max_exact) * (NUM_BUCKETS - max_exact)).astype(np.int64)
    return np.where(rel < max_exact, rel, np.minimum(large, NUM_BUCKETS - 1)).astype(np.int32)


def _bias_kernel(bucket_ref, rel_bias_ref, o_ref):
    h = pl.program_id(0)
    for plane in range(2):
        bk = bucket_ref[plane]
        acc = jnp.full(bk.shape, -jnp.inf, F32)
        for b in range(NUM_BUCKETS):
            acc = jnp.where(bk == b, rel_bias_ref[b, h] * LOG2_E, acc)
        o_ref[0, plane] = acc
    o_ref[0, 2] = jnp.full((TQ, TQ), rel_bias_ref[NUM_BUCKETS - 1, h] * LOG2_E, F32)


def _moba_bias_tables(rel_bias):
    off = np.arange(TQ)
    rel_own = off[:, None] - off[None, :]
    own = np.where(rel_own >= 0, _t5_bucket_np(rel_own), -1)
    adj = _t5_bucket_np(rel_own + MOBA_BLOCK)
    buckets = jnp.asarray(np.stack([own, adj]).astype(np.int32))
    return pl.pallas_call(
        _bias_kernel,
        grid=(MOBA_HEADS,),
        in_specs=[_whole(buckets.shape), pl.BlockSpec(memory_space=pltpu.SMEM)],
        out_specs=pl.BlockSpec((1, 3, TQ, TQ), lambda h: (h, 0, 0, 0)),
        out_shape=jax.ShapeDtypeStruct((MOBA_HEADS, 3, TQ, TQ), F32),
        compiler_params=_params("arbitrary"),
        name="moba_bias_tables",
    )(buckets, rel_bias)


_N_BLK_LANES = 8


def _moba_constants(nb):
    assert nb <= _N_BLK_LANES
    diff = np.zeros((LANES, LANES), np.float32)
    count = np.zeros((LANES, LANES), np.float32)
    for hh in range(2):
        for a in range(nb):
            for b in range(nb):
                if a != b:
                    pair = hh * HEAD_DIM + a * 8 + b
                    diff[pair, a] += 1.0
                    diff[pair, b] -= 1.0
                    count[pair, _spare_lane_base(hh) + b] = 1.0
                    count[pair, _spare_lane_base(hh) + _N_BLK_LANES + b] = 1.0
    return jnp.asarray(diff, BF16), jnp.asarray(count, BF16)


_T_KIND, _T_FAR_HI, _T_FAR_LO, _T_FUTURE, _T_UNSEL, _N_HEAD_TABLES = 0, 1, 2, 3, 4, 5


def _moba_fill_tables(head_tab, gate_tab, s_len):
    lane = lax.broadcasted_iota(jnp.int32, (s_len, LANES), 1)
    row_blk = lax.broadcasted_iota(jnp.int32, (s_len, LANES), 0) // MOBA_BLOCK
    for hh in range(2):
        base = _spare_lane_base(hh)
        in_hi = (lane >= base) & (lane < base + _N_BLK_LANES)
        in_lo = (lane >= base + _N_BLK_LANES) & (lane < base + 2 * _N_BLK_LANES)
        blk = jnp.where(in_lo, lane - base - _N_BLK_LANES, lane - base)
        far = blk < row_blk - 1
        store = lambda t, x: head_tab.__setitem__((hh, t), x.astype(BF16))
        store(_T_KIND, jnp.where(in_hi | in_lo, jnp.where(blk == row_blk, 1.0, 0.0), -1.0))
        store(_T_FAR_HI, jnp.where(in_hi & far, 1.0, 0.0))
        store(_T_FAR_LO, jnp.where(in_lo & far, 1.0, 0.0))
        store(_T_FUTURE, jnp.where(in_hi & (blk > row_blk), MASK_PENALTY, 0.0))
        store(_T_UNSEL, jnp.where(in_hi & (blk != row_blk), MASK_PENALTY, 0.0))
    pair_a, pair_b = (lane % HEAD_DIM) // 8, lane % 8
    valid = pair_a < row_blk
    gate_tab[0] = jnp.where(valid, 1.0, 0.0)
    gate_tab[1] = jnp.where(valid & (pair_a < pair_b), 1.0, 0.0)


def _moba_kernel(q_ref, k_ref, v_ref, bias_ref, diff_ref, count_ref, o_ref,
                 qaug_scr, kaug_scr, vm_scr, head_tab, gate_tab, *, nb):
    s_len = q_ref.shape[1]

    @pl.when((pl.program_id(0) == 0) & (pl.program_id(1) == 0))
    def _():
        _moba_fill_tables(head_tab, gate_tab, s_len)

    q, k, v = q_ref[0], k_ref[0], v_ref[0]
    kbar = jnp.mean(k.astype(F32).reshape(nb, MOBA_BLOCK, LANES), axis=1)
    kbar = jnp.concatenate([kbar, jnp.zeros((LANES - nb, LANES), F32)], axis=0)
    p0 = kbar.astype(BF16)
    r1 = kbar - p0.astype(F32)
    p1 = r1.astype(BF16)
    p2 = (r1 - p1.astype(F32)).astype(BF16)
    d = diff_ref[...]
    kdiff = _dot(d, p0) + _dot(d, p1) + _dot(d, p2)
    pair_head = lax.broadcasted_iota(jnp.int32, (LANES, LANES), 0) // HEAD_DIM
    feat_head = lax.broadcasted_iota(jnp.int32, (LANES, LANES), 1) // HEAD_DIM
    kdiff = jnp.where(pair_head == feat_head, kdiff, 0.0).astype(BF16)

    g = _dot_nt(q, kdiff)
    beats = jnp.where(g > 0.0, gate_tab[0], jnp.where(g == 0.0, gate_tab[1], 0.0))
    rank = _dot(beats.astype(BF16), count_ref[...]).astype(BF16)
    top = rank < float(MOBA_TOPK)

    lane = lax.broadcasted_iota(jnp.int32, (1, LANES), 1)
    for hh in range(2):
        lane_mask = jnp.where(_head_lanes((1, LANES), hh), 1.0, 0.0).astype(BF16)
        ones_lane = jnp.where(lane == _spare_lane_base(hh), 1.0, 0.0).astype(BF16)
        vm_scr[hh] = v * lane_mask + ones_lane
        kind = head_tab[hh, _T_KIND]
        spare = kind >= 0.0
        kaug_scr[hh] = jnp.where(spare, kind, k)
        far = bias_ref[hh, 2, 0:1, 0:LANES]
        far_hi = far.astype(BF16)
        far_lo = (far - far_hi.astype(F32)).astype(BF16)
        chosen_val = head_tab[hh, _T_FAR_HI] * far_hi + head_tab[hh, _T_FAR_LO] * far_lo + head_tab[hh, _T_FUTURE]
        qaug_scr[hh] = jnp.where(spare, jnp.where(top, chosen_val, head_tab[hh, _T_UNSEL]), q * lane_mask)

    for r0 in range(0, s_len, 2 * TQ):
        rows = slice(r0, r0 + 2 * TQ)
        out = jnp.zeros((2 * TQ, LANES), F32)
        for hh in range(2):
            acc = _two_block_attention(qaug_scr[hh, rows, :],
                                       lambda a, b: kaug_scr[hh, a:b, :],
                                       lambda a, b: vm_scr[hh, a:b, :], r0,
                                       lambda s, d: s + bias_ref[hh, d])
            out = out + _normalized(acc, hh)
        o_ref[0, rows, :] = out.astype(o_ref.dtype)


def _moba_attention(qkv, bias_tables):
    b, s, _ = qkv.shape
    nb = s // MOBA_BLOCK
    pairs = MOBA_HEADS // 2
    diff, count = _moba_constants(nb)
    return pl.pallas_call(
        functools.partial(_moba_kernel, nb=nb),
        grid=(pairs, b),
        in_specs=[pl.BlockSpec((1, s, LANES), lambda p, bi: (bi, 0, p)),
                  pl.BlockSpec((1, s, LANES), lambda p, bi: (bi, 0, pairs + p)),
                  pl.BlockSpec((1, s, LANES), lambda p, bi: (bi, 0, 2 * pairs + p)),
                  pl.BlockSpec((2, 3, TQ, TQ), lambda p, bi: (p, 0, 0, 0)),
                  _whole(diff.shape), _whole(count.shape)],
        out_specs=pl.BlockSpec((1, s, LANES), lambda p, bi: (bi, 0, p)),
        out_shape=jax.ShapeDtypeStruct((b, s, MOBA_HEADS * HEAD_DIM), BF16),
        scratch_shapes=[pltpu.VMEM((2, s, LANES), BF16)] * 3
        + [pltpu.VMEM((2, _N_HEAD_TABLES, s, LANES), BF16), pltpu.VMEM((2, s, LANES), F32)],
        compiler_params=_params("arbitrary", "arbitrary"),
        name="moba_attention",
    )(qkv, qkv, qkv, bias_tables, diff, count)


def _mix_ffn_kernel(*refs, n_in):
    a_refs = refs[:n_in]
    h_ref, w_out_ref, gmix_ref, gpre_ref, wgu_ref, wd_ref, gpost_ref, o_ref, act_scr = refs[n_in:]
    mixed, k0 = None, 0
    for a_ref in a_refs:
        k1 = k0 + a_ref.shape[1]
        part = _dot(a_ref[...], w_out_ref[k0:k1, :])
        mixed, k0 = part if mixed is None else mixed + part, k1
    x = h_ref[...] + _rms(mixed, gmix_ref[...])
    u = _rms(x, gpre_ref[...]).astype(BF16)
    for c in range(0, D_FF, FF_CHUNK):
        gate = _dot(u, wgu_ref[:, c:c + FF_CHUNK])
        up = _dot(u, wgu_ref[:, D_FF + c:D_FF + c + FF_CHUNK])
        act_scr[:, c:c + FF_CHUNK] = (gate * jax.nn.sigmoid(gate) * up).astype(BF16)
    f = _dot(act_scr[...], wd_ref[...])
    o_ref[...] = x + _rms(f, gpost_ref[...])


def _mix_ffn(acts, h, w_out, gmix, gpre, wgu, wd, gpost):
    n, d = h.shape
    row = lambda t: (t, 0)
    vec = _whole((1, d))
    return pl.pallas_call(
        functools.partial(_mix_ffn_kernel, n_in=len(acts)),
        grid=(n // TM_FFN,),
        in_specs=([pl.BlockSpec((TM_FFN, a.shape[1]), row) for a in acts]
                  + [pl.BlockSpec((TM_FFN, d), row), _one_layer(w_out[0].shape, w_out[1]), vec, vec,
                     _one_layer(wgu[0].shape, wgu[1]), _one_layer(wd[0].shape, wd[1]), vec]),
        out_specs=pl.BlockSpec((TM_FFN, d), row),
        out_shape=jax.ShapeDtypeStruct((n, d), F32),
        scratch_shapes=[pltpu.VMEM((TM_FFN, D_FF), BF16)],
        compiler_params=pltpu.CompilerParams(dimension_semantics=("arbitrary",),
                                             vmem_limit_bytes=VMEM_LIMIT_FFN),
        name="mix_ffn",
    )(*acts, h, w_out[0], gmix.reshape(1, d), gpre.reshape(1, d), wgu[0], wd[0], gpost.reshape(1, d))


def _rot_half_cols(w):
    half = w.shape[-1] // 2
    return jnp.concatenate([-w[..., half:], w[..., :half]], axis=-1)


def _prep_even_weights(w_in, w_uq, w_ukv):
    d = w_in.shape[0]
    o1, o2, o3 = MLA_Q_LORA, MLA_Q_LORA + MLA_KV_LORA, MLA_Q_LORA + MLA_KV_LORA + MLA_ROPE
    hw = MOBA_HEADS * HEAD_DIM
    z = lambda *shape: jnp.zeros(shape, F32)
    kr = w_in[:, o2:o3]
    w_in_ext = jnp.concatenate(
        [w_in[:, :o1], z(d, MLA_NOPE), kr, _rot_half_cols(kr), w_in[:, o1:o2],
         w_in[:, o3:o3 + hw] * (HEAD_DIM ** -0.5), w_in[:, o3 + hw:]], axis=1).astype(BF16)

    wq = w_uq.reshape(MLA_Q_LORA, MLA_HEADS, MLA_NOPE + MLA_ROPE)
    nope, rp = wq[..., :MLA_NOPE], wq[..., MLA_NOPE:]
    assert MLA_NOPE + 2 * MLA_ROPE == LANES
    w_uq_ext = jnp.concatenate([nope, rp, _rot_half_cols(rp)], axis=-1).reshape(MLA_Q_LORA, _QW).astype(BF16)

    wkv = w_ukv.reshape(MLA_KV_LORA, MLA_HEADS, MLA_NOPE + MLA_V)
    k_nope, v = wkv[..., :MLA_NOPE], wkv[..., MLA_NOPE:]
    w_k = jnp.concatenate([k_nope, jnp.zeros_like(k_nope)], axis=-1).reshape(MLA_KV_LORA, _QW).astype(BF16)
    v = v.reshape(MLA_KV_LORA, MLA_HEADS // 2, 2, MLA_V)
    zero = jnp.zeros_like(v[:, :, 0])
    w_v = jnp.stack([jnp.concatenate([v[:, :, 0], zero], axis=-1),
                     jnp.concatenate([zero, v[:, :, 1]], axis=-1)], axis=2)
    w_v = w_v.reshape(MLA_KV_LORA, _QW).astype(BF16)
    return w_in_ext, w_uq_ext, w_k, w_v


def _rope_tables(seq):
    half = MLA_ROPE // 2
    inv_freq = ROPE_THETA ** (-jnp.arange(half, dtype=F32) / half)
    ang = jnp.arange(seq).astype(F32)[:, None] * inv_freq[None, :]
    cos, sin = jnp.cos(ang), jnp.sin(ang)
    ones, zeros = jnp.ones((seq, MLA_NOPE), F32), jnp.zeros((seq, MLA_NOPE), F32)
    tail = jnp.zeros((seq, LANES - MLA_NOPE - MLA_ROPE), F32)
    scale = (MLA_NOPE + MLA_ROPE) ** -0.5 * LOG2_E
    cq = jnp.concatenate([ones, cos, cos, tail], axis=1) * scale
    sq = jnp.concatenate([zeros, sin, sin, tail], axis=1) * scale
    ck = jnp.concatenate([zeros, cos, cos, tail], axis=1)
    sk = jnp.concatenate([zeros, sin, sin, tail], axis=1)
    return cq, sq, ck, sk


def kernel(x, mix_pre_g, mix_post_g, ffn_pre_g, ffn_post_g, ab_w_in, mla_q_norm_g, mla_w_uq,
           mla_kv_norm_g, mla_w_ukv, ab_w_out, rel_bias, sb_w_qkv, sb_w_out, ffn_w_gate_up, ffn_w_down):
    b, s, d = x.shape
    n = b * s
    h = x.reshape(n, d)
    tables = _rope_tables(s)
    bias_tables = _moba_bias_tables(rel_bias)
    ab_w_out, sb_w_qkv, sb_w_out, ffn_w_gate_up, ffn_w_down = (
        w.astype(BF16) for w in (ab_w_out, sb_w_qkv, sb_w_out, ffn_w_gate_up, ffn_w_down))
    for layer in range(DEPTH):
        if layer % 2 == 0:
            e = layer // 2
            w_in, w_uq, w_k, w_v = _prep_even_weights(ab_w_in[e], mla_w_uq[e], mla_w_ukv[e])
            q, k, v, moba_qkv = _even_in(h, mix_pre_g[layer], w_in, mla_q_norm_g[e], w_uq,
                                         mla_kv_norm_g[e], w_k, w_v, tables, s)
            mla_out = _mla_attention(q.reshape(b, s, -1), k.reshape(b, s, -1), v.reshape(b, s, -1))
            moba_out = _moba_attention(moba_qkv.reshape(b, s, -1), bias_tables)
            mixed, w_out = [mla_out.reshape(n, -1), moba_out.reshape(n, -1)], (ab_w_out, e)
        else:
            o = layer // 2
            qkv = _norm_matmul(h, mix_pre_g[layer], sb_w_qkv, o,
                               SB_HEADS * HEAD_DIM, HEAD_DIM ** -0.5 * LOG2_E)
            mixed, w_out = [_sb_attention(qkv.reshape(b, s, -1)).reshape(n, -1)], (sb_w_out, o)
        h = _mix_ffn(mixed, h, w_out, mix_post_g[layer], ffn_pre_g[layer],
                     (ffn_w_gate_up, layer), (ffn_w_down, layer), ffn_post_g[layer])
    return h.reshape(b, s, d)
```

```python
import functools
import math

import numpy as np
import jax
import jax.numpy as jnp
from jax import lax
from jax.experimental import pallas as pl
from jax.experimental.pallas import tpu as pltpu

D_MODEL = 1024
DEPTH = 4
HEAD_DIM = 64
MLA_HEADS = 8
MLA_NOPE = 64
MLA_ROPE = 32
MLA_V = 64
MLA_Q_LORA = 384
MLA_KV_LORA = 256
ROPE_THETA = 10000.0
MOBA_HEADS = 8
MOBA_BLOCK = 256
MOBA_TOPK = 3
SB_HEADS = 16
NUM_BUCKETS = 32
MAX_DISTANCE = 128
RMS_EPS = 1e-6
D_FF = 2816

LANES = 128
TQ = 256
TM = 512
FF_CHUNK = 256
MASK_PENALTY = -1e30
LOG2_E = 1.0 / math.log(2.0)
VMEM_LIMIT = 48 * 1024 * 1024

F32 = jnp.float32
BF16 = jnp.bfloat16


def _rms(x, g):
    return x * lax.rsqrt(jnp.mean(x * x, axis=-1, keepdims=True) + RMS_EPS) * g


def _dot(a, b):
    return jnp.dot(a, b, preferred_element_type=F32)


def _dot_nt(a, b):
    return lax.dot_general(a, b, (((1,), (1,)), ((), ())), preferred_element_type=F32)


def _params(*semantics):
    return pltpu.CompilerParams(dimension_semantics=semantics, vmem_limit_bytes=VMEM_LIMIT)


def _whole(shape):
    nd = len(shape)
    return pl.BlockSpec(shape, lambda *_: (0,) * nd, pipeline_mode=pl.Buffered(1))


def _one_layer(stacked_shape, layer):
    return pl.BlockSpec((None,) + tuple(stacked_shape[1:]), lambda *_: (layer, 0, 0),
                        pipeline_mode=pl.Buffered(1))


def _norm_matmul_kernel(h_ref, g_ref, w_ref, o_ref, *, q_cols, q_scale):
    u = _rms(h_ref[...], g_ref[...]).astype(BF16)
    for c in range(0, o_ref.shape[-1], 512):
        y = _dot(u, w_ref[:, c:c + 512])
        if c < q_cols:
            y = y * q_scale
        o_ref[:, c:c + 512] = y.astype(o_ref.dtype)


def _norm_matmul(h, g, w_stack, layer, q_cols, q_scale):
    n, d = h.shape
    n_out = w_stack.shape[2]
    return pl.pallas_call(
        functools.partial(_norm_matmul_kernel, q_cols=q_cols, q_scale=q_scale),
        grid=(n // TM,),
        in_specs=[pl.BlockSpec((TM, d), lambda t: (t, 0)), _whole((1, d)),
                  _one_layer(w_stack.shape, layer)],
        out_specs=pl.BlockSpec((TM, n_out), lambda t: (t, 0)),
        out_shape=jax.ShapeDtypeStruct((n, n_out), BF16),
        compiler_params=_params("arbitrary"),
        name="norm_matmul",
    )(h, g.reshape(1, d), w_stack)


def _head_lanes(shape, hh):
    lane = lax.broadcasted_iota(jnp.int32, shape, 1)
    return (lane >= hh * HEAD_DIM) & (lane < (hh + 1) * HEAD_DIM)


def _spare_lane_base(hh):
    return HEAD_DIM * (1 - hh)


def _normalized(acc, hh):
    base = _spare_lane_base(hh)
    return jnp.where(_head_lanes(acc.shape, hh), acc / acc[:, base:base + 1], 0.0)


_O_CQ = 0
_O_KR = MLA_Q_LORA
_O_CKV = _O_KR + LANES
_O_MOBA = _O_CKV + MLA_KV_LORA
_W_IN_EXT = _O_MOBA + 3 * MOBA_HEADS * HEAD_DIM
_QW = MLA_HEADS * LANES


def _even_in_kernel(h_ref, g_ref, w_in_ref, gq_ref, w_uq_ref, gkv_ref, w_k_ref, w_v_ref,
                    cq_ref, sq_ref, ck_ref, sk_ref, q_ref, k_ref, v_ref, moba_ref):
    u = _rms(h_ref[...], g_ref[...]).astype(BF16)
    cq_kr = _dot(u, w_in_ref[:, _O_CQ:_O_CKV])
    c_q, kr = cq_kr[:, :_O_KR], cq_kr[:, _O_KR:]
    c_kv = _dot(u, w_in_ref[:, _O_CKV:_O_MOBA])
    hw = MOBA_HEADS * HEAD_DIM
    for c in range(0, 3 * hw, hw):
        y = _dot(u, w_in_ref[:, _O_MOBA + c:_O_MOBA + c + hw])
        if c == 0:
            y = y * LOG2_E
        moba_ref[:, c:c + hw] = y.astype(BF16)

    cqn = _rms(c_q, gq_ref[...]).astype(BF16)
    qa = _dot(cqn, w_uq_ref[...])
    qb = pltpu.roll(qa, _QW - MLA_ROPE, axis=1)
    cq, sq = cq_ref[...], sq_ref[...]
    for h in range(MLA_HEADS):
        s = slice(h * LANES, (h + 1) * LANES)
        q_ref[:, s] = (qa[:, s] * cq + qb[:, s] * sq).astype(BF16)

    ckvn = _rms(c_kv, gkv_ref[...]).astype(BF16)
    kn = _dot(ckvn, w_k_ref[...])
    v = _dot(ckvn, w_v_ref[...])
    lane = lax.broadcasted_iota(jnp.int32, v.shape, 1)
    spare = jnp.where((lane // LANES) % 2 == 0, _spare_lane_base(0), _spare_lane_base(1))
    v_ref[...] = jnp.where(lane % LANES == spare, 1.0, v).astype(BF16)
    k_rope = kr * ck_ref[...] + pltpu.roll(kr, LANES - MLA_ROPE, axis=1) * sk_ref[...]
    for h in range(MLA_HEADS):
        s = slice(h * LANES, (h + 1) * LANES)
        k_ref[:, s] = (kn[:, s] + k_rope).astype(BF16)


def _even_in(h, g, w_in, gq, w_uq, gkv, w_k, w_v, tables, seq):
    n, d = h.shape
    row = lambda t: (t, 0)
    pos = lambda t: (t % (seq // TM), 0)
    tab = pl.BlockSpec((TM, LANES), pos)
    outs = [(_QW, "q"), (_QW, "k"), (_QW, "v"), (3 * MOBA_HEADS * HEAD_DIM, "moba")]
    return pl.pallas_call(
        _even_in_kernel,
        grid=(n // TM,),
        in_specs=[pl.BlockSpec((TM, d), row), _whole((1, d)), _whole(w_in.shape),
                  _whole((1, MLA_Q_LORA)), _whole(w_uq.shape), _whole((1, MLA_KV_LORA)),
                  _whole(w_k.shape), _whole(w_v.shape), tab, tab, tab, tab],
        out_specs=[pl.BlockSpec((TM, w), row) for w, _ in outs],
        out_shape=[jax.ShapeDtypeStruct((n, w), BF16) for w, _ in outs],
        compiler_params=_params("arbitrary"),
        name="even_in",
    )(h, g.reshape(1, d), w_in, gq.reshape(1, -1), w_uq, gkv.reshape(1, -1), w_k, w_v, *tables)


def _two_block_attention(qs, k_of, v_ofs, r0, add_nears):
    end, n_h = r0 + 2 * TQ, len(qs)
    stack = lambda parts: parts[0] if n_h == 1 else jnp.concatenate(parts, axis=0)
    kd = k_of(r0, end)
    s_a_all = _dot_nt(stack([q[:TQ] for q in qs]), kd[:TQ])
    s_b_all = _dot_nt(stack([q[TQ:] for q in qs]), kd)
    s_off_all = _dot_nt(stack(list(qs)), k_of(0, r0)) if r0 > 0 else None
    accs = []
    for h, add_near in enumerate(add_nears):
        s_a = add_near(s_a_all[h * TQ:(h + 1) * TQ], 0)
        s_b = s_b_all[h * TQ:(h + 1) * TQ]
        s_b = jnp.concatenate([add_near(s_b[:, :TQ], 1), add_near(s_b[:, TQ:], 0)], axis=1)
        if r0 > 0:
            s_off = s_off_all[h * 2 * TQ:(h + 1) * 2 * TQ]
            s_a = jnp.concatenate([s_off[:TQ, :r0 - TQ], add_near(s_off[:TQ, r0 - TQ:], 1), s_a], axis=1)
            s_b = jnp.concatenate([s_off[TQ:], s_b], axis=1)
        p_a = jnp.exp2(s_a - jnp.max(s_a, axis=-1, keepdims=True)).astype(BF16)
        p_b = jnp.exp2(s_b - jnp.max(s_b, axis=-1, keepdims=True)).astype(BF16)
        vd = v_ofs[h](r0, end)
        acc = jnp.concatenate([_dot(p_a[:, r0:], vd[:TQ]), _dot(p_b[:, r0:], vd)], axis=0)
        if r0 > 0:
            acc = acc + _dot(jnp.concatenate([p_a[:, :r0], p_b[:, :r0]], axis=0), v_ofs[h](0, r0))
        accs.append(acc)
    return accs


def _mla_kernel(q_ref, k_ref, v_ref, o_ref):
    s_len = q_ref.shape[1]
    row = lax.broadcasted_iota(jnp.int32, (TQ, TQ), 0)
    col = lax.broadcasted_iota(jnp.int32, (TQ, TQ), 1)
    causal = col <= row
    add_near = lambda s, d: jnp.where(causal, s, -jnp.inf) if d == 0 else s
    zeros = jnp.zeros((2 * TQ, LANES), BF16)
    for r0 in range(0, s_len, 2 * TQ):
        rows = slice(r0, r0 + 2 * TQ)
        qs = [jnp.concatenate([q_ref[0, rows, :LANES], zeros], axis=1),
              jnp.concatenate([zeros, q_ref[0, rows, LANES:]], axis=1)]
        v_ofs = [lambda a, b, lanes=slice(hh * LANES, (hh + 1) * LANES): v_ref[0, a:b, lanes]
                 for hh in range(2)]
        accs = _two_block_attention(qs, lambda a, b: k_ref[0, a:b, :], v_ofs, r0, [add_near] * 2)
        out = _normalized(accs[0], 0) + _normalized(accs[1], 1)
        o_ref[0, rows, :] = out.astype(o_ref.dtype)


def _mla_attention(q, k, v):
    b, s, _ = q.shape
    pairs = MLA_HEADS // 2
    qkv = pl.BlockSpec((1, s, 2 * LANES), lambda bi, p: (bi, 0, p))
    return pl.pallas_call(
        _mla_kernel,
        grid=(b, pairs),
        in_specs=[qkv, qkv, qkv],
        out_specs=pl.BlockSpec((1, s, LANES), lambda bi, p: (bi, 0, p)),
        out_shape=jax.ShapeDtypeStruct((b, s, MLA_HEADS * MLA_V), BF16),
        compiler_params=_params("arbitrary", "arbitrary"),
        name="mla_attention",
    )(q, k, v)


_SOFTPLUS2_CLAMP = 64.0


def _softplus2(z):
    return jnp.maximum(z, jnp.log2(1.0 + jnp.exp2(jnp.minimum(z, _SOFTPLUS2_CLAMP))))


SB_UNDERFLOW = 160.0
SB_NEAR_BLOCKS = 2
SB_PROBE_STEPS = 16


def _sb_kernel(q_ref, k_ref, v_ref, o_ref, qm_scr, vm_scr, acc_scr, carry_scr, limit_scr, plan_scr):
    s_len = q_ref.shape[1]
    n_blk = s_len // TQ
    def prepare():
        q, v = q_ref[0], v_ref[0]
        for hh in range(2):
            lane_mask = jnp.where(_head_lanes((1, LANES), hh), 1.0, 0.0).astype(BF16)
            qm_scr[hh] = q * lane_mask
            vm_scr[hh] = v * lane_mask
        acc_scr[...] = jnp.zeros(acc_scr.shape, F32)

    def set_limits():
        q, k = q_ref[0], k_ref[0]
        feat = lax.broadcasted_iota(jnp.int32, (LANES, LANES), 0) // HEAD_DIM
        lane = lax.broadcasted_iota(jnp.int32, (LANES, LANES), 1) // HEAD_DIM
        same_head = jnp.where(feat == lane, 1.0, 0.0).astype(BF16)
        q_max2 = jnp.max(_dot(q * q, same_head), axis=0, keepdims=True)
        k_max2 = jnp.max(_dot(k * k, same_head), axis=0, keepdims=True)
        z_bound = jnp.sqrt(q_max2 * k_max2) * 1.02
        for hh in range(2):
            bound = jnp.max(jnp.where(_head_lanes((1, LANES), hh), z_bound, 0.0), axis=1, keepdims=True)
            limit_scr[hh] = jnp.broadcast_to(bound + SB_UNDERFLOW, limit_scr.shape[1:])

    row = lax.broadcasted_iota(jnp.int32, (TQ, TQ), 0)
    col = lax.broadcasted_iota(jnp.int32, (TQ, TQ), 1)
    strict = col < row
    tri = jnp.where(row >= col, 1.0, 0.0).astype(BF16)

    def tile_rows(j, r0, r1):
        keys = slice(j * TQ, (j + 1) * TQ)
        rows = slice(r0 * TQ, r1 * TQ)
        m = (r1 - r0) * TQ
        first = TQ if r0 == j else 0
        z = _dot_nt(jnp.concatenate([qm_scr[0, rows, :], qm_scr[1, rows, :]], axis=0),
                    k_ref[0, keys, :])
        sp = _softplus2(z)
        if first:
            parts = []
            for base in (0, m):
                parts.append(jnp.where(strict, sp[base:base + TQ], 0.0))
                if m > TQ:
                    parts.append(sp[base + TQ:base + m])
            sp = jnp.concatenate(parts, axis=0)
        c = _dot(sp.astype(BF16), tri)
        tot = jnp.broadcast_to(c[:, 0:1], (2 * m, LANES))
        w = []
        for h, base in enumerate((0, m)):
            w_h = []
            if first:
                d = slice(base, base + TQ)
                w_h.append(jnp.where(strict, jnp.exp2(z[d] - c[d]), 0.0))
                carry_scr[h, keys, :] = tot[d]
            if m > first:
                below = slice(r0 * TQ + first, r1 * TQ)
                o = slice(base + first, base + m)
                carry = carry_scr[h, below, :]
                w_h.append(jnp.exp2(z[o] - c[o] - jnp.concatenate([carry, carry], axis=1)))
                carry_scr[h, below, :] = carry + tot[o]
            w.append(w_h[0] if len(w_h) == 1 else jnp.concatenate(w_h, axis=0))
        vv = jnp.concatenate([vm_scr[0, keys, :], vm_scr[1, keys, :]], axis=0)
        acc_scr[rows, :] += _dot(jnp.concatenate(w, axis=1).astype(BF16), vv)

    def near_tiles(n_near):
        for j in range(n_blk - 1, -1, -1):
            tile_rows(j, j, min(j + n_near, n_blk))

    def live(r0, r1):
        slack = [jnp.min(carry_scr[h, r0 * TQ:r1 * TQ, :], axis=0, keepdims=True) - limit_scr[h, 0:1, :]
                 for h in range(2)]
        return jnp.min(jnp.minimum(slack[0], slack[1])) < 0.0

    def diagonal(dist):
        for i in range(dist, n_blk):
            tile_rows(i - dist, i, i + 1)

    step = pl.program_id(0) * pl.num_programs(1) + pl.program_id(1)

    @pl.when(step == 0)
    def _():
        plan_scr[0] = 0
        plan_scr[1] = 0

    wide = plan_scr[0] == 1

    @pl.when(jnp.logical_not(wide))
    def _():
        prepare()
        set_limits()
        near_tiles(SB_NEAR_BLOCKS)
        flags = [live(i, i + 1) for i in range(SB_NEAR_BLOCKS, n_blk)]
        n_live = sum(f.astype(jnp.int32) for f in flags)
        plan_scr[1] += (n_live >= 2).astype(jnp.int32)
        plan_scr[2] = n_live

        @pl.when(n_live >= 2)
        def _():
            diagonal(SB_NEAR_BLOCKS)

        @pl.when(n_live == 1)
        def _():
            for i, f in zip(range(SB_NEAR_BLOCKS, n_blk), flags):
                pl.when(f)(functools.partial(tile_rows, i - SB_NEAR_BLOCKS, i, i + 1))

    @pl.when(wide)
    def _():
        prepare()
        set_limits()
        near_tiles(SB_NEAR_BLOCKS + 1)
        plan_scr[2] = 1

    def farther(dist):
        @pl.when(live(dist, n_blk))
        def _():
            diagonal(dist)
            if dist + 1 < n_blk:
                farther(dist + 1)

    if SB_NEAR_BLOCKS + 1 < n_blk:
        pl.when(plan_scr[2] > 0)(functools.partial(farther, SB_NEAR_BLOCKS + 1))

    @pl.when(step == SB_PROBE_STEPS - 1)
    def _():
        plan_scr[0] = (5 * plan_scr[1] >= 3 * SB_PROBE_STEPS).astype(jnp.int32)

    o_ref[0] = acc_scr[...].astype(o_ref.dtype)


def _sb_attention(qkv):
    b, s, _ = qkv.shape
    pairs = SB_HEADS // 2
    return pl.pallas_call(
        _sb_kernel,
        grid=(b, pairs),
        in_specs=[pl.BlockSpec((1, s, LANES), lambda bi, p: (bi, 0, p)),
                  pl.BlockSpec((1, s, LANES), lambda bi, p: (bi, 0, pairs + p)),
                  pl.BlockSpec((1, s, LANES), lambda bi, p: (bi, 0, 2 * pairs + p))],
        out_specs=pl.BlockSpec((1, s, LANES), lambda bi, p: (bi, 0, p)),
        out_shape=jax.ShapeDtypeStruct((b, s, SB_HEADS * HEAD_DIM), BF16),
        scratch_shapes=[pltpu.VMEM((2, s, LANES), BF16), pltpu.VMEM((2, s, LANES), BF16),
                        pltpu.VMEM((s, LANES), F32), pltpu.VMEM((2, s, LANES), F32),
                        pltpu.VMEM((2, 8, LANES), F32), pltpu.SMEM((3,), jnp.int32)],
        compiler_params=_params("arbitrary", "arbitrary"),
        name="sb_attention",
    )(qkv, qkv, qkv)


def _t5_bucket_np(rel):
    max_exact = NUM_BUCKETS // 2
    rel = np.maximum(rel, 0)
    large = max_exact + (np.log(np.maximum(rel, max_exact) / max_exact)
                         / math.log(MAX_DISTANCE / max_exact) * (NUM_BUCKETS - max_exact)).astype(np.int64)
    return np.where(rel < max_exact, rel, np.minimum(large, NUM_BUCKETS - 1)).astype(np.int32)


def _bias_kernel(bucket_ref, rel_bias_ref, o_ref):
    h = pl.program_id(0)
    for plane in range(2):
        bk = bucket_ref[plane]
        acc = jnp.full(bk.shape, -jnp.inf, F32)
        for b in range(NUM_BUCKETS):
            acc = jnp.where(bk == b, rel_bias_ref[b, h] * LOG2_E, acc)
        o_ref[0, plane] = acc
    o_ref[0, 2] = jnp.full((TQ, TQ), rel_bias_ref[NUM_BUCKETS - 1, h] * LOG2_E, F32)


def _moba_bias_tables(rel_bias):
    off = np.arange(TQ)
    rel_own = off[:, None] - off[None, :]
    own = np.where(rel_own >= 0, _t5_bucket_np(rel_own), -1)
    adj = _t5_bucket_np(rel_own + MOBA_BLOCK)
    buckets = jnp.asarray(np.stack([own, adj]).astype(np.int32))
    return pl.pallas_call(
        _bias_kernel,
        grid=(MOBA_HEADS,),
        in_specs=[_whole(buckets.shape), pl.BlockSpec(memory_space=pltpu.SMEM)],
        out_specs=pl.BlockSpec((1, 3, TQ, TQ), lambda h: (h, 0, 0, 0)),
        out_shape=jax.ShapeDtypeStruct((MOBA_HEADS, 3, TQ, TQ), F32),
        compiler_params=_params("arbitrary"),
        name="moba_bias_tables",
    )(buckets, rel_bias)


_N_BLK_LANES = 8


def _moba_constants(nb):
    assert nb <= _N_BLK_LANES
    diff = np.zeros((LANES, LANES), np.float32)
    count = np.zeros((LANES, LANES), np.float32)
    for hh in range(2):
        for a in range(nb):
            for b in range(nb):
                if a != b:
                    pair = hh * HEAD_DIM + a * 8 + b
                    diff[pair, a] += 1.0
                    diff[pair, b] -= 1.0
                    for part in range(2):
                        count[pair, (2 * hh + part) * _N_BLK_LANES + b] = 1.0
    return jnp.asarray(diff, BF16), jnp.asarray(count, BF16)


_T_FAR_HI, _T_FAR_LO, _T_FUTURE, _T_UNSEL, _N_HEAD_TABLES = 0, 1, 2, 3, 4


def _moba_fill_tables(head_tab, key_tab, gate_tab, s_len):
    lane = lax.broadcasted_iota(jnp.int32, (s_len, LANES), 1)
    row_blk = lax.broadcasted_iota(jnp.int32, (s_len, LANES), 0) // MOBA_BLOCK
    group, blk = lane // _N_BLK_LANES, lane % _N_BLK_LANES
    key_tab[...] = jnp.where((group < 4) & (blk == row_blk), 1.0, 0.0).astype(BF16)
    far = blk < row_blk - 1
    for hh in range(2):
        in_hi, in_lo = group == 2 * hh, group == 2 * hh + 1
        store = lambda t, x: head_tab.__setitem__((hh, t), x.astype(BF16))
        store(_T_FAR_HI, jnp.where(in_hi & far, 1.0, 0.0))
        store(_T_FAR_LO, jnp.where(in_lo & far, 1.0, 0.0))
        store(_T_FUTURE, jnp.where(in_hi & (blk > row_blk), MASK_PENALTY, 0.0))
        store(_T_UNSEL, jnp.where(in_hi & (blk != row_blk), MASK_PENALTY, 0.0))
    pair_a, pair_b = (lane % HEAD_DIM) // 8, lane % 8
    valid = pair_a < row_blk
    gate_tab[0] = jnp.where(valid, 1.0, 0.0)
    gate_tab[1] = jnp.where(valid & (pair_a < pair_b), 1.0, 0.0)


def _moba_kernel(q_ref, k_ref, v_ref, bias_ref, diff_ref, count_ref, o_ref,
                 qaug_scr, vm_scr, head_tab, key_tab, gate_tab, *, nb):
    s_len = q_ref.shape[1]

    @pl.when((pl.program_id(0) == 0) & (pl.program_id(1) == 0))
    def _():
        _moba_fill_tables(head_tab, key_tab, gate_tab, s_len)

    q, k, v = q_ref[0], k_ref[0], v_ref[0]
    kbar = jnp.mean(k.astype(F32).reshape(nb, MOBA_BLOCK, LANES), axis=1)
    kbar = jnp.concatenate([kbar, jnp.zeros((LANES - nb, LANES), F32)], axis=0)
    p0 = kbar.astype(BF16)
    r1 = kbar - p0.astype(F32)
    p1 = r1.astype(BF16)
    p2 = (r1 - p1.astype(F32)).astype(BF16)
    d = diff_ref[...]
    kdiff = _dot(d, p0) + _dot(d, p1) + _dot(d, p2)
    pair_head = lax.broadcasted_iota(jnp.int32, (LANES, LANES), 0) // HEAD_DIM
    feat_head = lax.broadcasted_iota(jnp.int32, (LANES, LANES), 1) // HEAD_DIM
    kdiff = jnp.where(pair_head == feat_head, kdiff, 0.0).astype(BF16)

    g = _dot_nt(q, kdiff)
    beats = jnp.where(g > 0.0, gate_tab[0], jnp.where(g == 0.0, gate_tab[1], 0.0))
    rank = _dot(beats.astype(BF16), count_ref[...]).astype(BF16)
    top = rank < float(MOBA_TOPK)

    lane = lax.broadcasted_iota(jnp.int32, (1, LANES), 1)
    for hh in range(2):
        lane_mask = jnp.where(_head_lanes((1, LANES), hh), 1.0, 0.0).astype(BF16)
        ones_lane = jnp.where(lane == _spare_lane_base(hh), 1.0, 0.0).astype(BF16)
        vm_scr[hh] = v * lane_mask + ones_lane
        far = bias_ref[hh, 2, 0:1, 0:LANES]
        far_hi = far.astype(BF16)
        far_lo = (far - far_hi.astype(F32)).astype(BF16)
        chosen_val = head_tab[hh, _T_FAR_HI] * far_hi + head_tab[hh, _T_FAR_LO] * far_lo + head_tab[hh, _T_FUTURE]
        qaug_scr[hh, :, :LANES] = q * lane_mask
        qaug_scr[hh, :, LANES:] = jnp.where(top, chosen_val, head_tab[hh, _T_UNSEL])

    keys_of = lambda a, b: jnp.concatenate([k_ref[0, a:b, :], key_tab[a:b, :]], axis=1)
    for r0 in range(0, s_len, 2 * TQ):
        rows = slice(r0, r0 + 2 * TQ)
        accs = _two_block_attention(
            [qaug_scr[hh, rows, :] for hh in range(2)], keys_of,
            [lambda a, b, hh=hh: vm_scr[hh, a:b, :] for hh in range(2)], r0,
            [lambda s, d, hh=hh: s + bias_ref[hh, d] for hh in range(2)])
        out = _normalized(accs[0], 0) + _normalized(accs[1], 1)
        o_ref[0, rows, :] = out.astype(o_ref.dtype)


def _moba_attention(qkv, bias_tables):
    b, s, _ = qkv.shape
    nb = s // MOBA_BLOCK
    pairs = MOBA_HEADS // 2
    diff, count = _moba_constants(nb)
    return pl.pallas_call(
        functools.partial(_moba_kernel, nb=nb),
        grid=(pairs, b),
        in_specs=[pl.BlockSpec((1, s, LANES), lambda p, bi: (bi, 0, p)),
                  pl.BlockSpec((1, s, LANES), lambda p, bi: (bi, 0, pairs + p)),
                  pl.BlockSpec((1, s, LANES), lambda p, bi: (bi, 0, 2 * pairs + p)),
                  pl.BlockSpec((2, 3, TQ, TQ), lambda p, bi: (p, 0, 0, 0)),
                  _whole(diff.shape), _whole(count.shape)],
        out_specs=pl.BlockSpec((1, s, LANES), lambda p, bi: (bi, 0, p)),
        out_shape=jax.ShapeDtypeStruct((b, s, MOBA_HEADS * HEAD_DIM), BF16),
        scratch_shapes=[pltpu.VMEM((2, s, 2 * LANES), BF16), pltpu.VMEM((2, s, LANES), BF16),
                        pltpu.VMEM((2, _N_HEAD_TABLES, s, LANES), BF16),
                        pltpu.VMEM((s, LANES), BF16), pltpu.VMEM((2, s, LANES), F32)],
        compiler_params=_params("arbitrary", "arbitrary"),
        name="moba_attention",
    )(qkv, qkv, qkv, bias_tables, diff, count)


def _mix_ffn_kernel(*refs, n_in):
    a_refs = refs[:n_in]
    h_ref, w_out_ref, gmix_ref, gpre_ref, wgu_ref, wd_ref, gpost_ref, o_ref, act_scr = refs[n_in:]
    mixed, k0 = None, 0
    for a_ref in a_refs:
        k1 = k0 + a_ref.shape[1]
        part = _dot(a_ref[...], w_out_ref[k0:k1, :])
        mixed, k0 = part if mixed is None else mixed + part, k1
    x = h_ref[...] + _rms(mixed, gmix_ref[...])
    u = _rms(x, gpre_ref[...]).astype(BF16)
    for c in range(0, D_FF, FF_CHUNK):
        gate = _dot(u, wgu_ref[:, c:c + FF_CHUNK])
        up = _dot(u, wgu_ref[:, D_FF + c:D_FF + c + FF_CHUNK])
        act_scr[:, c:c + FF_CHUNK] = (gate * jax.nn.sigmoid(gate) * up).astype(BF16)
    f = _dot(act_scr[...], wd_ref[...])
    o_ref[...] = x + _rms(f, gpost_ref[...])


def _mix_ffn(acts, h, w_out, gmix, gpre, wgu, wd, gpost):
    n, d = h.shape
    row = lambda t: (t, 0)
    vec = _whole((1, d))
    return pl.pallas_call(
        functools.partial(_mix_ffn_kernel, n_in=len(acts)),
        grid=(n // TM,),
        in_specs=([pl.BlockSpec((TM, a.shape[1]), row) for a in acts]
                  + [pl.BlockSpec((TM, d), row), _one_layer(w_out[0].shape, w_out[1]), vec, vec,
                     _one_layer(wgu[0].shape, wgu[1]), _one_layer(wd[0].shape, wd[1]), vec]),
        out_specs=pl.BlockSpec((TM, d), row),
        out_shape=jax.ShapeDtypeStruct((n, d), F32),
        scratch_shapes=[pltpu.VMEM((TM, D_FF), BF16)],
        compiler_params=_params("arbitrary"),
        name="mix_ffn",
    )(*acts, h, w_out[0], gmix.reshape(1, d), gpre.reshape(1, d), wgu[0], wd[0], gpost.reshape(1, d))


def _rot_half_cols(w):
    half = w.shape[-1] // 2
    return jnp.concatenate([-w[..., half:], w[..., :half]], axis=-1)


def _prep_even_weights(w_in, w_uq, w_ukv):
    d = w_in.shape[0]
    o1, o2, o3 = MLA_Q_LORA, MLA_Q_LORA + MLA_KV_LORA, MLA_Q_LORA + MLA_KV_LORA + MLA_ROPE
    hw = MOBA_HEADS * HEAD_DIM
    z = lambda *shape: jnp.zeros(shape, F32)
    kr = w_in[:, o2:o3]
    w_in_ext = jnp.concatenate(
        [w_in[:, :o1], z(d, MLA_NOPE), kr, _rot_half_cols(kr), w_in[:, o1:o2],
         w_in[:, o3:o3 + hw] * (HEAD_DIM ** -0.5), w_in[:, o3 + hw:]], axis=1).astype(BF16)

    wq = w_uq.reshape(MLA_Q_LORA, MLA_HEADS, MLA_NOPE + MLA_ROPE)
    nope, rp = wq[..., :MLA_NOPE], wq[..., MLA_NOPE:]
    assert MLA_NOPE + 2 * MLA_ROPE == LANES
    w_uq_ext = jnp.concatenate([nope, rp, _rot_half_cols(rp)], axis=-1).reshape(MLA_Q_LORA, _QW).astype(BF16)

    wkv = w_ukv.reshape(MLA_KV_LORA, MLA_HEADS, MLA_NOPE + MLA_V)
    k_nope, v = wkv[..., :MLA_NOPE], wkv[..., MLA_NOPE:]
    w_k = jnp.concatenate([k_nope, jnp.zeros_like(k_nope)], axis=-1).reshape(MLA_KV_LORA, _QW).astype(BF16)
    v = v.reshape(MLA_KV_LORA, MLA_HEADS // 2, 2, MLA_V)
    zero = jnp.zeros_like(v[:, :, 0])
    w_v = jnp.stack([jnp.concatenate([v[:, :, 0], zero], axis=-1),
                     jnp.concatenate([zero, v[:, :, 1]], axis=-1)], axis=2)
    w_v = w_v.reshape(MLA_KV_LORA, _QW).astype(BF16)
    return w_in_ext, w_uq_ext, w_k, w_v


def _rope_tables(seq):
    half = MLA_ROPE // 2
    inv_freq = ROPE_THETA ** (-jnp.arange(half, dtype=F32) / half)
    ang = jnp.arange(seq).astype(F32)[:, None] * inv_freq[None, :]
    cos, sin = jnp.cos(ang), jnp.sin(ang)
    ones, zeros = jnp.ones((seq, MLA_NOPE), F32), jnp.zeros((seq, MLA_NOPE), F32)
    tail = jnp.zeros((seq, LANES - MLA_NOPE - MLA_ROPE), F32)
    scale = (MLA_NOPE + MLA_ROPE) ** -0.5 * LOG2_E
    cq = jnp.concatenate([ones, cos, cos, tail], axis=1) * scale
    sq = jnp.concatenate([zeros, sin, sin, tail], axis=1) * scale
    ck = jnp.concatenate([zeros, cos, cos, tail], axis=1)
    sk = jnp.concatenate([zeros, sin, sin, tail], axis=1)
    return cq, sq, ck, sk


def kernel(x, mix_pre_g, mix_post_g, ffn_pre_g, ffn_post_g, ab_w_in, mla_q_norm_g, mla_w_uq,
           mla_kv_norm_g, mla_w_ukv, ab_w_out, rel_bias, sb_w_qkv, sb_w_out, ffn_w_gate_up, ffn_w_down):
    b, s, d = x.shape
    n = b * s
    h = x.reshape(n, d)
    tables = _rope_tables(s)
    bias_tables = _moba_bias_tables(rel_bias)
    ab_w_out, sb_w_qkv, sb_w_out, ffn_w_gate_up, ffn_w_down = (
        w.astype(BF16) for w in (ab_w_out, sb_w_qkv, sb_w_out, ffn_w_gate_up, ffn_w_down))
    for layer in range(DEPTH):
        if layer % 2 == 0:
            e = layer // 2
            w_in, w_uq, w_k, w_v = _prep_even_weights(ab_w_in[e], mla_w_uq[e], mla_w_ukv[e])
            q, k, v, moba_qkv = _even_in(h, mix_pre_g[layer], w_in, mla_q_norm_g[e], w_uq,
                                         mla_kv_norm_g[e], w_k, w_v, tables, s)
            mla_out = _mla_attention(q.reshape(b, s, -1), k.reshape(b, s, -1), v.reshape(b, s, -1))
            moba_out = _moba_attention(moba_qkv.reshape(b, s, -1), bias_tables)
            mixed, w_out = [mla_out.reshape(n, -1), moba_out.reshape(n, -1)], (ab_w_out, e)
        else:
            o = layer // 2
            qkv = _norm_matmul(h, mix_pre_g[layer], sb_w_qkv, o,
                               SB_HEADS * HEAD_DIM, HEAD_DIM ** -0.5 * LOG2_E)
            mixed, w_out = [_sb_attention(qkv.reshape(b, s, -1)).reshape(n, -1)], (sb_w_out, o)
        h = _mix_ffn(mixed, h, w_out, mix_post_g[layer], ffn_pre_g[layer],
                     (ffn_w_gate_up, layer), (ffn_w_down, layer), ffn_post_g[layer])
    return h.reshape(b, s, d)
```

```python
import functools
import math

import numpy as np
import jax
import jax.numpy as jnp
from jax import lax
from jax.experimental import pallas as pl
from jax.experimental.pallas import tpu as pltpu

D_MODEL = 1024
DEPTH = 4
HEAD_DIM = 64
MLA_HEADS = 8
MLA_NOPE = 64
MLA_ROPE = 32
MLA_V = 64
MLA_Q_LORA = 384
MLA_KV_LORA = 256
ROPE_THETA = 10000.0
MOBA_HEADS = 8
MOBA_BLOCK = 256
MOBA_TOPK = 3
SB_HEADS = 16
NUM_BUCKETS = 32
MAX_DISTANCE = 128
RMS_EPS = 1e-6
D_FF = 2816

LANES = 128
TQ = 256
TM = 512
FF_CHUNK = 256
MASK_PENALTY = -1e30
LOG2_E = 1.0 / math.log(2.0)
VMEM_LIMIT = 48 * 1024 * 1024

F32 = jnp.float32
BF16 = jnp.bfloat16


def _rms(x, g):
    return x * lax.rsqrt(jnp.mean(x * x, axis=-1, keepdims=True) + RMS_EPS) * g


def _dot(a, b):
    return jnp.dot(a, b, preferred_element_type=F32)


def _dot_nt(a, b):
    return lax.dot_general(a, b, (((1,), (1,)), ((), ())), preferred_element_type=F32)


def _params(*semantics):
    return pltpu.CompilerParams(dimension_semantics=semantics, vmem_limit_bytes=VMEM_LIMIT)


def _whole(shape):
    nd = len(shape)
    return pl.BlockSpec(shape, lambda *_: (0,) * nd, pipeline_mode=pl.Buffered(1))


def _one_layer(stacked_shape, layer):
    return pl.BlockSpec((None,) + tuple(stacked_shape[1:]), lambda *_: (layer, 0, 0),
                        pipeline_mode=pl.Buffered(1))


def _norm_matmul_kernel(h_ref, g_ref, w_ref, o_ref, *, q_cols, q_scale):
    u = _rms(h_ref[...], g_ref[...]).astype(BF16)
    for c in range(0, o_ref.shape[-1], 512):
        y = _dot(u, w_ref[:, c:c + 512])
        if c < q_cols:
            y = y * q_scale
        o_ref[:, c:c + 512] = y.astype(o_ref.dtype)


def _norm_matmul(h, g, w_stack, layer, q_cols, q_scale):
    n, d = h.shape
    n_out = w_stack.shape[2]
    return pl.pallas_call(
        functools.partial(_norm_matmul_kernel, q_cols=q_cols, q_scale=q_scale),
        grid=(n // TM,),
        in_specs=[pl.BlockSpec((TM, d), lambda t: (t, 0)), _whole((1, d)),
                  _one_layer(w_stack.shape, layer)],
        out_specs=pl.BlockSpec((TM, n_out), lambda t: (t, 0)),
        out_shape=jax.ShapeDtypeStruct((n, n_out), BF16),
        compiler_params=_params("arbitrary"),
        name="norm_matmul",
    )(h, g.reshape(1, d), w_stack)


def _head_lanes(shape, hh):
    lane = lax.broadcasted_iota(jnp.int32, shape, 1)
    return (lane >= hh * HEAD_DIM) & (lane < (hh + 1) * HEAD_DIM)


def _spare_lane_base(hh):
    return HEAD_DIM * (1 - hh)


def _normalized(acc, hh):
    base = _spare_lane_base(hh)
    return jnp.where(_head_lanes(acc.shape, hh), acc / acc[:, base:base + 1], 0.0)


_O_CQ = 0
_O_KR = MLA_Q_LORA
_O_CKV = _O_KR + LANES
_O_MOBA = _O_CKV + MLA_KV_LORA
_W_IN_EXT = _O_MOBA + 3 * MOBA_HEADS * HEAD_DIM
_QW = MLA_HEADS * LANES


def _even_in_kernel(h_ref, g_ref, w_in_ref, gq_ref, w_uq_ref, gkv_ref, w_k_ref, w_v_ref,
                    cq_ref, sq_ref, ck_ref, sk_ref, q_ref, k_ref, v_ref, moba_ref):
    u = _rms(h_ref[...], g_ref[...]).astype(BF16)
    cq_kr = _dot(u, w_in_ref[:, _O_CQ:_O_CKV])
    c_q, kr = cq_kr[:, :_O_KR], cq_kr[:, _O_KR:]
    c_kv = _dot(u, w_in_ref[:, _O_CKV:_O_MOBA])
    hw = MOBA_HEADS * HEAD_DIM
    for c in range(0, 3 * hw, hw):
        y = _dot(u, w_in_ref[:, _O_MOBA + c:_O_MOBA + c + hw])
        if c == 0:
            y = y * LOG2_E
        moba_ref[:, c:c + hw] = y.astype(BF16)

    cqn = _rms(c_q, gq_ref[...]).astype(BF16)
    qa = _dot(cqn, w_uq_ref[...])
    qb = pltpu.roll(qa, _QW - MLA_ROPE, axis=1)
    cq, sq = cq_ref[...], sq_ref[...]
    for h in range(MLA_HEADS):
        s = slice(h * LANES, (h + 1) * LANES)
        q_ref[:, s] = (qa[:, s] * cq + qb[:, s] * sq).astype(BF16)

    ckvn = _rms(c_kv, gkv_ref[...]).astype(BF16)
    kn = _dot(ckvn, w_k_ref[...])
    v = _dot(ckvn, w_v_ref[...])
    lane = lax.broadcasted_iota(jnp.int32, v.shape, 1)
    spare = jnp.where((lane // LANES) % 2 == 0, _spare_lane_base(0), _spare_lane_base(1))
    v_ref[...] = jnp.where(lane % LANES == spare, 1.0, v).astype(BF16)
    k_rope = kr * ck_ref[...] + pltpu.roll(kr, LANES - MLA_ROPE, axis=1) * sk_ref[...]
    for h in range(MLA_HEADS):
        s = slice(h * LANES, (h + 1) * LANES)
        k_ref[:, s] = (kn[:, s] + k_rope).astype(BF16)


def _even_in(h, g, w_in, gq, w_uq, gkv, w_k, w_v, tables, seq):
    n, d = h.shape
    row = lambda t: (t, 0)
    pos = lambda t: (t % (seq // TM), 0)
    tab = pl.BlockSpec((TM, LANES), pos)
    outs = [(_QW, "q"), (_QW, "k"), (_QW, "v"), (3 * MOBA_HEADS * HEAD_DIM, "moba")]
    return pl.pallas_call(
        _even_in_kernel,
        grid=(n // TM,),
        in_specs=[pl.BlockSpec((TM, d), row), _whole((1, d)), _whole(w_in.shape),
                  _whole((1, MLA_Q_LORA)), _whole(w_uq.shape), _whole((1, MLA_KV_LORA)),
                  _whole(w_k.shape), _whole(w_v.shape), tab, tab, tab, tab],
        out_specs=[pl.BlockSpec((TM, w), row) for w, _ in outs],
        out_shape=[jax.ShapeDtypeStruct((n, w), BF16) for w, _ in outs],
        compiler_params=_params("arbitrary"),
        name="even_in",
    )(h, g.reshape(1, d), w_in, gq.reshape(1, -1), w_uq, gkv.reshape(1, -1), w_k, w_v, *tables)


def _two_block_attention(qs, k_of, v_ofs, r0, add_nears):
    mid, end = r0 + TQ, r0 + 2 * TQ
    stack = lambda parts: jnp.concatenate(parts, axis=0)
    s_main_all = _dot_nt(stack(list(qs)), k_of(0, mid))
    s_last_all = _dot_nt(stack([q[TQ:] for q in qs]), k_of(mid, end))
    accs = []
    for h, add_near in enumerate(add_nears):
        s_main = s_main_all[h * 2 * TQ:(h + 1) * 2 * TQ]
        a_near = [add_near(s_main[:TQ, r0 - TQ:r0], 1)] if r0 > 0 else []
        a_far = [s_main[:TQ, :r0 - TQ]] if r0 > TQ else []
        s_a = jnp.concatenate(a_far + a_near + [add_near(s_main[:TQ, r0:], 0)], axis=1)
        b_far = [s_main[TQ:, :r0]] if r0 > 0 else []
        s_b = jnp.concatenate(b_far + [add_near(s_main[TQ:, r0:], 1),
                                       add_near(s_last_all[h * TQ:(h + 1) * TQ], 0)], axis=1)
        p_a = jnp.exp2(s_a - jnp.max(s_a, axis=-1, keepdims=True)).astype(BF16)
        p_b = jnp.exp2(s_b - jnp.max(s_b, axis=-1, keepdims=True)).astype(BF16)
        acc = _dot(stack([p_a, p_b[:, :mid]]), v_ofs[h](0, mid))
        accs.append(stack([acc[:TQ], acc[TQ:] + _dot(p_b[:, mid:], v_ofs[h](mid, end))]))
    return accs


def _mla_kernel(q_ref, k_ref, v_ref, o_ref):
    s_len = q_ref.shape[1]
    row = lax.broadcasted_iota(jnp.int32, (TQ, TQ), 0)
    col = lax.broadcasted_iota(jnp.int32, (TQ, TQ), 1)
    causal = col <= row
    add_near = lambda s, d: jnp.where(causal, s, -jnp.inf) if d == 0 else s
    zeros = jnp.zeros((2 * TQ, LANES), BF16)
    for r0 in range(0, s_len, 2 * TQ):
        rows = slice(r0, r0 + 2 * TQ)
        qs = [jnp.concatenate([q_ref[0, rows, :LANES], zeros], axis=1),
              jnp.concatenate([zeros, q_ref[0, rows, LANES:]], axis=1)]
        v_ofs = [lambda a, b, lanes=slice(hh * LANES, (hh + 1) * LANES): v_ref[0, a:b, lanes]
                 for hh in range(2)]
        accs = _two_block_attention(qs, lambda a, b: k_ref[0, a:b, :], v_ofs, r0, [add_near] * 2)
        out = _normalized(accs[0], 0) + _normalized(accs[1], 1)
        o_ref[0, rows, :] = out.astype(o_ref.dtype)


def _mla_attention(q, k, v):
    b, s, _ = q.shape
    pairs = MLA_HEADS // 2
    qkv = pl.BlockSpec((1, s, 2 * LANES), lambda bi, p: (bi, 0, p))
    return pl.pallas_call(
        _mla_kernel,
        grid=(b, pairs),
        in_specs=[qkv, qkv, qkv],
        out_specs=pl.BlockSpec((1, s, LANES), lambda bi, p: (bi, 0, p)),
        out_shape=jax.ShapeDtypeStruct((b, s, MLA_HEADS * MLA_V), BF16),
        compiler_params=_params("arbitrary", "arbitrary"),
        name="mla_attention",
    )(q, k, v)


_SOFTPLUS2_CLAMP = 64.0


def _softplus2(z):
    return jnp.maximum(z, jnp.log2(1.0 + jnp.exp2(jnp.minimum(z, _SOFTPLUS2_CLAMP))))


SB_UNDERFLOW = 160.0
SB_NEAR_BLOCKS = 2
SB_PROBE_STEPS = 16


def _sb_kernel(q_ref, k_ref, v_ref, o_ref, qm_scr, vm_scr, acc_scr, carry_scr, limit_scr, plan_scr):
    s_len = q_ref.shape[1]
    n_blk = s_len // TQ
    def prepare():
        q, v = q_ref[0], v_ref[0]
        for hh in range(2):
            lane_mask = jnp.where(_head_lanes((1, LANES), hh), 1.0, 0.0).astype(BF16)
            qm_scr[hh] = q * lane_mask
            vm_scr[hh] = v * lane_mask
        acc_scr[...] = jnp.zeros(acc_scr.shape, F32)

    def set_limits():
        q, k = q_ref[0], k_ref[0]
        feat = lax.broadcasted_iota(jnp.int32, (LANES, LANES), 0) // HEAD_DIM
        lane = lax.broadcasted_iota(jnp.int32, (LANES, LANES), 1) // HEAD_DIM
        same_head = jnp.where(feat == lane, 1.0, 0.0).astype(BF16)
        q_max2 = jnp.max(_dot(q * q, same_head), axis=0, keepdims=True)
        k_max2 = jnp.max(_dot(k * k, same_head), axis=0, keepdims=True)
        z_bound = jnp.sqrt(q_max2 * k_max2) * 1.02
        for hh in range(2):
            bound = jnp.max(jnp.where(_head_lanes((1, LANES), hh), z_bound, 0.0), axis=1, keepdims=True)
            limit_scr[hh] = jnp.broadcast_to(bound + SB_UNDERFLOW, limit_scr.shape[1:])

    row = lax.broadcasted_iota(jnp.int32, (TQ, TQ), 0)
    col = lax.broadcasted_iota(jnp.int32, (TQ, TQ), 1)
    strict = col < row
    tri = jnp.where(row >= col, 1.0, 0.0).astype(BF16)

    def tile_rows(j, r0, r1):
        keys = slice(j * TQ, (j + 1) * TQ)
        rows = slice(r0 * TQ, r1 * TQ)
        m = (r1 - r0) * TQ
        first = TQ if r0 == j else 0
        z = _dot_nt(jnp.concatenate([qm_scr[0, rows, :], qm_scr[1, rows, :]], axis=0),
                    k_ref[0, keys, :])
        sp = _softplus2(z)
        if first:
            parts = []
            for base in (0, m):
                parts.append(jnp.where(strict, sp[base:base + TQ], 0.0))
                if m > TQ:
                    parts.append(sp[base + TQ:base + m])
            sp = jnp.concatenate(parts, axis=0)
        c = _dot(sp.astype(BF16), tri)
        tot = jnp.broadcast_to(c[:, 0:1], (2 * m, LANES))
        w = []
        for h, base in enumerate((0, m)):
            w_h = []
            if first:
                d = slice(base, base + TQ)
                w_h.append(jnp.where(strict, jnp.exp2(z[d] - c[d]), 0.0))
                carry_scr[h, keys, :] = tot[d]
            if m > first:
                below = slice(r0 * TQ + first, r1 * TQ)
                o = slice(base + first, base + m)
                carry = carry_scr[h, below, :]
                w_h.append(jnp.exp2(z[o] - c[o] - jnp.concatenate([carry, carry], axis=1)))
                carry_scr[h, below, :] = carry + tot[o]
            w.append(w_h[0] if len(w_h) == 1 else jnp.concatenate(w_h, axis=0))
        vv = jnp.concatenate([vm_scr[0, keys, :], vm_scr[1, keys, :]], axis=0)
        acc_scr[rows, :] += _dot(jnp.concatenate(w, axis=1).astype(BF16), vv)

    def near_tiles(n_near):
        for j in range(n_blk - 1, -1, -1):
            tile_rows(j, j, min(j + n_near, n_blk))

    def live(r0, r1):
        slack = [jnp.min(carry_scr[h, r0 * TQ:r1 * TQ, :], axis=0, keepdims=True) - limit_scr[h, 0:1, :]
                 for h in range(2)]
        return jnp.min(jnp.minimum(slack[0], slack[1])) < 0.0

    def diagonal(dist):
        for i in range(dist, n_blk):
            tile_rows(i - dist, i, i + 1)

    step = pl.program_id(0) * pl.num_programs(1) + pl.program_id(1)

    @pl.when(step == 0)
    def _():
        plan_scr[0] = 0
        plan_scr[1] = 0

    wide = plan_scr[0] == 1

    @pl.when(jnp.logical_not(wide))
    def _():
        prepare()
        set_limits()
        near_tiles(SB_NEAR_BLOCKS)
        flags = [live(i, i + 1) for i in range(SB_NEAR_BLOCKS, n_blk)]
        n_live = sum(f.astype(jnp.int32) for f in flags)
        plan_scr[1] += (n_live >= 2).astype(jnp.int32)
        plan_scr[2] = n_live

        @pl.when(n_live >= 2)
        def _():
            diagonal(SB_NEAR_BLOCKS)

        @pl.when(n_live == 1)
        def _():
            for i, f in zip(range(SB_NEAR_BLOCKS, n_blk), flags):
                pl.when(f)(functools.partial(tile_rows, i - SB_NEAR_BLOCKS, i, i + 1))

    @pl.when(wide)
    def _():
        prepare()
        set_limits()
        near_tiles(SB_NEAR_BLOCKS + 1)
        plan_scr[2] = 1

    def farther(dist):
        @pl.when(live(dist, n_blk))
        def _():
            diagonal(dist)
            if dist + 1 < n_blk:
                farther(dist + 1)

    if SB_NEAR_BLOCKS + 1 < n_blk:
        pl.when(plan_scr[2] > 0)(functools.partial(farther, SB_NEAR_BLOCKS + 1))

    @pl.when(step == SB_PROBE_STEPS - 1)
    def _():
        plan_scr[0] = (5 * plan_scr[1] >= 3 * SB_PROBE_STEPS).astype(jnp.int32)

    o_ref[0] = acc_scr[...].astype(o_ref.dtype)


def _sb_attention(qkv):
    b, s, _ = qkv.shape
    pairs = SB_HEADS // 2
    return pl.pallas_call(
        _sb_kernel,
        grid=(b, pairs),
        in_specs=[pl.BlockSpec((1, s, LANES), lambda bi, p: (bi, 0, p)),
                  pl.BlockSpec((1, s, LANES), lambda bi, p: (bi, 0, pairs + p)),
                  pl.BlockSpec((1, s, LANES), lambda bi, p: (bi, 0, 2 * pairs + p))],
        out_specs=pl.BlockSpec((1, s, LANES), lambda bi, p: (bi, 0, p)),
        out_shape=jax.ShapeDtypeStruct((b, s, SB_HEADS * HEAD_DIM), BF16),
        scratch_shapes=[pltpu.VMEM((2, s, LANES), BF16), pltpu.VMEM((2, s, LANES), BF16),
                        pltpu.VMEM((s, LANES), F32), pltpu.VMEM((2, s, LANES), F32),
                        pltpu.VMEM((2, 8, LANES), F32), pltpu.SMEM((3,), jnp.int32)],
        compiler_params=_params("arbitrary", "arbitrary"),
        name="sb_attention",
    )(qkv, qkv, qkv)


def _t5_bucket_np(rel):
    max_exact = NUM_BUCKETS // 2
    rel = np.maximum(rel, 0)
    large = max_exact + (np.log(np.maximum(rel, max_exact) / max_exact)
                         / math.log(MAX_DISTANCE / max_exact) * (NUM_BUCKETS - max_exact)).astype(np.int64)
    return np.where(rel < max_exact, rel, np.minimum(large, NUM_BUCKETS - 1)).astype(np.int32)


def _bias_kernel(bucket_ref, rel_bias_ref, o_ref):
    h = pl.program_id(0)
    for plane in range(2):
        bk = bucket_ref[plane]
        acc = jnp.full(bk.shape, -jnp.inf, F32)
        for b in range(NUM_BUCKETS):
            acc = jnp.where(bk == b, rel_bias_ref[b, h] * LOG2_E, acc)
        o_ref[0, plane] = acc
    o_ref[0, 2] = jnp.full((TQ, TQ), rel_bias_ref[NUM_BUCKETS - 1, h] * LOG2_E, F32)


def _moba_bias_tables(rel_bias):
    off = np.arange(TQ)
    rel_own = off[:, None] - off[None, :]
    own = np.where(rel_own >= 0, _t5_bucket_np(rel_own), -1)
    adj = _t5_bucket_np(rel_own + MOBA_BLOCK)
    buckets = jnp.asarray(np.stack([own, adj]).astype(np.int32))
    return pl.pallas_call(
        _bias_kernel,
        grid=(MOBA_HEADS,),
        in_specs=[_whole(buckets.shape), pl.BlockSpec(memory_space=pltpu.SMEM)],
        out_specs=pl.BlockSpec((1, 3, TQ, TQ), lambda h: (h, 0, 0, 0)),
        out_shape=jax.ShapeDtypeStruct((MOBA_HEADS, 3, TQ, TQ), F32),
        compiler_params=_params("arbitrary"),
        name="moba_bias_tables",
    )(buckets, rel_bias)


_N_BLK_LANES = 8


def _moba_constants(nb):
    assert nb <= _N_BLK_LANES
    diff = np.zeros((LANES, LANES), np.float32)
    count = np.zeros((LANES, LANES), np.float32)
    for hh in range(2):
        for a in range(nb):
            for b in range(nb):
                if a != b:
                    pair = hh * HEAD_DIM + a * 8 + b
                    diff[pair, a] += 1.0
                    diff[pair, b] -= 1.0
                    for part in range(2):
                        count[pair, (2 * hh + part) * _N_BLK_LANES + b] = 1.0
    return jnp.asarray(diff, BF16), jnp.asarray(count, BF16)


_T_FAR_HI, _T_FAR_LO, _T_FUTURE, _T_UNSEL, _N_HEAD_TABLES = 0, 1, 2, 3, 4


def _moba_fill_tables(head_tab, key_tab, gate_tab, s_len):
    lane = lax.broadcasted_iota(jnp.int32, (s_len, LANES), 1)
    row_blk = lax.broadcasted_iota(jnp.int32, (s_len, LANES), 0) // MOBA_BLOCK
    group, blk = lane // _N_BLK_LANES, lane % _N_BLK_LANES
    key_tab[...] = jnp.where((group < 4) & (blk == row_blk), 1.0, 0.0).astype(BF16)
    far = blk < row_blk - 1
    for hh in range(2):
        in_hi, in_lo = group == 2 * hh, group == 2 * hh + 1
        store = lambda t, x: head_tab.__setitem__((hh, t), x.astype(BF16))
        store(_T_FAR_HI, jnp.where(in_hi & far, 1.0, 0.0))
        store(_T_FAR_LO, jnp.where(in_lo & far, 1.0, 0.0))
        store(_T_FUTURE, jnp.where(in_hi & (blk > row_blk), MASK_PENALTY, 0.0))
        store(_T_UNSEL, jnp.where(in_hi & (blk != row_blk), MASK_PENALTY, 0.0))
    pair_a, pair_b = (lane % HEAD_DIM) // 8, lane % 8
    valid = pair_a < row_blk
    gate_tab[0] = jnp.where(valid, 1.0, 0.0)
    gate_tab[1] = jnp.where(valid & (pair_a < pair_b), 1.0, 0.0)


def _moba_kernel(q_ref, k_ref, v_ref, bias_ref, diff_ref, count_ref, o_ref,
                 qaug_scr, vm_scr, head_tab, key_tab, gate_tab, *, nb):
    s_len = q_ref.shape[1]

    @pl.when((pl.program_id(0) == 0) & (pl.program_id(1) == 0))
    def _():
        _moba_fill_tables(head_tab, key_tab, gate_tab, s_len)

    q, k, v = q_ref[0], k_ref[0], v_ref[0]
    kbar = jnp.mean(k.astype(F32).reshape(nb, MOBA_BLOCK, LANES), axis=1)
    kbar = jnp.concatenate([kbar, jnp.zeros((LANES - nb, LANES), F32)], axis=0)
    p0 = kbar.astype(BF16)
    r1 = kbar - p0.astype(F32)
    p1 = r1.astype(BF16)
    p2 = (r1 - p1.astype(F32)).astype(BF16)
    d = diff_ref[...]
    kdiff = _dot(d, p0) + _dot(d, p1) + _dot(d, p2)
    pair_head = lax.broadcasted_iota(jnp.int32, (LANES, LANES), 0) // HEAD_DIM
    feat_head = lax.broadcasted_iota(jnp.int32, (LANES, LANES), 1) // HEAD_DIM
    kdiff = jnp.where(pair_head == feat_head, kdiff, 0.0).astype(BF16)

    g = _dot_nt(q, kdiff)
    beats = jnp.where(g > 0.0, gate_tab[0], jnp.where(g == 0.0, gate_tab[1], 0.0))
    rank = _dot(beats.astype(BF16), count_ref[...]).astype(BF16)
    top = rank < float(MOBA_TOPK)

    lane = lax.broadcasted_iota(jnp.int32, (1, LANES), 1)
    for hh in range(2):
        lane_mask = jnp.where(_head_lanes((1, LANES), hh), 1.0, 0.0).astype(BF16)
        ones_lane = jnp.where(lane == _spare_lane_base(hh), 1.0, 0.0).astype(BF16)
        vm_scr[hh] = v * lane_mask + ones_lane
        far = bias_ref[hh, 2, 0:1, 0:LANES]
        far_hi = far.astype(BF16)
        far_lo = (far - far_hi.astype(F32)).astype(BF16)
        chosen_val = head_tab[hh, _T_FAR_HI] * far_hi + head_tab[hh, _T_FAR_LO] * far_lo + head_tab[hh, _T_FUTURE]
        qaug_scr[hh, :, :LANES] = q * lane_mask
        qaug_scr[hh, :, LANES:] = jnp.where(top, chosen_val, head_tab[hh, _T_UNSEL])

    keys_of = lambda a, b: jnp.concatenate([k_ref[0, a:b, :], key_tab[a:b, :]], axis=1)
    for r0 in range(0, s_len, 2 * TQ):
        rows = slice(r0, r0 + 2 * TQ)
        accs = _two_block_attention(
            [qaug_scr[hh, rows, :] for hh in range(2)], keys_of,
            [lambda a, b, hh=hh: vm_scr[hh, a:b, :] for hh in range(2)], r0,
            [lambda s, d, hh=hh: s + bias_ref[hh, d] for hh in range(2)])
        out = _normalized(accs[0], 0) + _normalized(accs[1], 1)
        o_ref[0, rows, :] = out.astype(o_ref.dtype)


def _moba_attention(qkv, bias_tables):
    b, s, _ = qkv.shape
    nb = s // MOBA_BLOCK
    pairs = MOBA_HEADS // 2
    diff, count = _moba_constants(nb)
    return pl.pallas_call(
        functools.partial(_moba_kernel, nb=nb),
        grid=(pairs, b),
        in_specs=[pl.BlockSpec((1, s, LANES), lambda p, bi: (bi, 0, p)),
                  pl.BlockSpec((1, s, LANES), lambda p, bi: (bi, 0, pairs + p)),
                  pl.BlockSpec((1, s, LANES), lambda p, bi: (bi, 0, 2 * pairs + p)),
                  pl.BlockSpec((2, 3, TQ, TQ), lambda p, bi: (p, 0, 0, 0)),
                  _whole(diff.shape), _whole(count.shape)],
        out_specs=pl.BlockSpec((1, s, LANES), lambda p, bi: (bi, 0, p)),
        out_shape=jax.ShapeDtypeStruct((b, s, MOBA_HEADS * HEAD_DIM), BF16),
        scratch_shapes=[pltpu.VMEM((2, s, 2 * LANES), BF16), pltpu.VMEM((2, s, LANES), BF16),
                        pltpu.VMEM((2, _N_HEAD_TABLES, s, LANES), BF16),
                        pltpu.VMEM((s, LANES), BF16), pltpu.VMEM((2, s, LANES), F32)],
        compiler_params=_params("arbitrary", "arbitrary"),
        name="moba_attention",
    )(qkv, qkv, qkv, bias_tables, diff, count)


def _mix_ffn_kernel(*refs, n_in):
    a_refs = refs[:n_in]
    h_ref, w_out_ref, gmix_ref, gpre_ref, wgu_ref, wd_ref, gpost_ref, o_ref, act_scr = refs[n_in:]
    mixed, k0 = None, 0
    for a_ref in a_refs:
        k1 = k0 + a_ref.shape[1]
        part = _dot(a_ref[...], w_out_ref[k0:k1, :])
        mixed, k0 = part if mixed is None else mixed + part, k1
    x = h_ref[...] + _rms(mixed, gmix_ref[...])
    u = _rms(x, gpre_ref[...]).astype(BF16)
    for c in range(0, D_FF, FF_CHUNK):
        gate = _dot(u, wgu_ref[:, c:c + FF_CHUNK])
        up = _dot(u, wgu_ref[:, D_FF + c:D_FF + c + FF_CHUNK])
        act_scr[:, c:c + FF_CHUNK] = (gate * jax.nn.sigmoid(gate) * up).astype(BF16)
    f = _dot(act_scr[...], wd_ref[...])
    o_ref[...] = x + _rms(f, gpost_ref[...])


def _mix_ffn(acts, h, w_out, gmix, gpre, wgu, wd, gpost):
    n, d = h.shape
    row = lambda t: (t, 0)
    vec = _whole((1, d))
    return pl.pallas_call(
        functools.partial(_mix_ffn_kernel, n_in=len(acts)),
        grid=(n // TM,),
        in_specs=([pl.BlockSpec((TM, a.shape[1]), row) for a in acts]
                  + [pl.BlockSpec((TM, d), row), _one_layer(w_out[0].shape, w_out[1]), vec, vec,
                     _one_layer(wgu[0].shape, wgu[1]), _one_layer(wd[0].shape, wd[1]), vec]),
        out_specs=pl.BlockSpec((TM, d), row),
        out_shape=jax.ShapeDtypeStruct((n, d), F32),
        scratch_shapes=[pltpu.VMEM((TM, D_FF), BF16)],
        compiler_params=_params("arbitrary"),
        name="mix_ffn",
    )(*acts, h, w_out[0], gmix.reshape(1, d), gpre.reshape(1, d), wgu[0], wd[0], gpost.reshape(1, d))


def _rot_half_cols(w):
    half = w.shape[-1] // 2
    return jnp.concatenate([-w[..., half:], w[..., :half]], axis=-1)


def _prep_even_weights(w_in, w_uq, w_ukv):
    d = w_in.shape[0]
    o1, o2, o3 = MLA_Q_LORA, MLA_Q_LORA + MLA_KV_LORA, MLA_Q_LORA + MLA_KV_LORA + MLA_ROPE
    hw = MOBA_HEADS * HEAD_DIM
    z = lambda *shape: jnp.zeros(shape, F32)
    kr = w_in[:, o2:o3]
    w_in_ext = jnp.concatenate(
        [w_in[:, :o1], z(d, MLA_NOPE), kr, _rot_half_cols(kr), w_in[:, o1:o2],
         w_in[:, o3:o3 + hw] * (HEAD_DIM ** -0.5), w_in[:, o3 + hw:]], axis=1).astype(BF16)

    wq = w_uq.reshape(MLA_Q_LORA, MLA_HEADS, MLA_NOPE + MLA_ROPE)
    nope, rp = wq[..., :MLA_NOPE], wq[..., MLA_NOPE:]
    assert MLA_NOPE + 2 * MLA_ROPE == LANES
    w_uq_ext = jnp.concatenate([nope, rp, _rot_half_cols(rp)], axis=-1).reshape(MLA_Q_LORA, _QW).astype(BF16)

    wkv = w_ukv.reshape(MLA_KV_LORA, MLA_HEADS, MLA_NOPE + MLA_V)
    k_nope, v = wkv[..., :MLA_NOPE], wkv[..., MLA_NOPE:]
    w_k = jnp.concatenate([k_nope, jnp.zeros_like(k_nope)], axis=-1).reshape(MLA_KV_LORA, _QW).astype(BF16)
    v = v.reshape(MLA_KV_LORA, MLA_HEADS // 2, 2, MLA_V)
    zero = jnp.zeros_like(v[:, :, 0])
    w_v = jnp.stack([jnp.concatenate([v[:, :, 0], zero], axis=-1),
                     jnp.concatenate([zero, v[:, :, 1]], axis=-1)], axis=2)
    w_v = w_v.reshape(MLA_KV_LORA, _QW).astype(BF16)
    return w_in_ext, w_uq_ext, w_k, w_v


def _rope_tables(seq):
    half = MLA_ROPE // 2
    inv_freq = ROPE_THETA ** (-jnp.arange(half, dtype=F32) / half)
    ang = jnp.arange(seq).astype(F32)[:, None] * inv_freq[None, :]
    cos, sin = jnp.cos(ang), jnp.sin(ang)
    ones, zeros = jnp.ones((seq, MLA_NOPE), F32), jnp.zeros((seq, MLA_NOPE), F32)
    tail = jnp.zeros((seq, LANES - MLA_NOPE - MLA_ROPE), F32)
    scale = (MLA_NOPE + MLA_ROPE) ** -0.5 * LOG2_E
    cq = jnp.concatenate([ones, cos, cos, tail], axis=1) * scale
    sq = jnp.concatenate([zeros, sin, sin, tail], axis=1) * scale
    ck = jnp.concatenate([zeros, cos, cos, tail], axis=1)
    sk = jnp.concatenate([zeros, sin, sin, tail], axis=1)
    return cq, sq, ck, sk


def kernel(x, mix_pre_g, mix_post_g, ffn_pre_g, ffn_post_g, ab_w_in, mla_q_norm_g, mla_w_uq,
           mla_kv_norm_g, mla_w_ukv, ab_w_out, rel_bias, sb_w_qkv, sb_w_out, ffn_w_gate_up, ffn_w_down):
    b, s, d = x.shape
    n = b * s
    h = x.reshape(n, d)
    tables = _rope_tables(s)
    bias_tables = _moba_bias_tables(rel_bias)
    ab_w_out, sb_w_qkv, sb_w_out, ffn_w_gate_up, ffn_w_down = (
        w.astype(BF16) for w in (ab_w_out, sb_w_qkv, sb_w_out, ffn_w_gate_up, ffn_w_down))
    for layer in range(DEPTH):
        if layer % 2 == 0:
            e = layer // 2
            w_in, w_uq, w_k, w_v = _prep_even_weights(ab_w_in[e], mla_w_uq[e], mla_w_ukv[e])
            q, k, v, moba_qkv = _even_in(h, mix_pre_g[layer], w_in, mla_q_norm_g[e], w_uq,
                                         mla_kv_norm_g[e], w_k, w_v, tables, s)
            mla_out = _mla_attention(q.reshape(b, s, -1), k.reshape(b, s, -1), v.reshape(b, s, -1))
            moba_out = _moba_attention(moba_qkv.reshape(b, s, -1), bias_tables)
            mixed, w_out = [mla_out.reshape(n, -1), moba_out.reshape(n, -1)], (ab_w_out, e)
        else:
            o = layer // 2
            qkv = _norm_matmul(h, mix_pre_g[layer], sb_w_qkv, o,
                               SB_HEADS * HEAD_DIM, HEAD_DIM ** -0.5 * LOG2_E)
            mixed, w_out = [_sb_attention(qkv.reshape(b, s, -1)).reshape(n, -1)], (sb_w_out, o)
        h = _mix_ffn(mixed, h, w_out, mix_post_g[layer], ffn_pre_g[layer],
                     (ffn_w_gate_up, layer), (ffn_w_down, layer), ffn_post_g[layer])
    return h.reshape(b, s, d)
```

```python
import functools
import math

import numpy as np
import jax
import jax.numpy as jnp
from jax import lax
from jax.experimental import pallas as pl
from jax.experimental.pallas import tpu as pltpu

D_MODEL = 1024
DEPTH = 4
HEAD_DIM = 64
MLA_HEADS = 8
MLA_NOPE = 64
MLA_ROPE = 32
MLA_V = 64
MLA_Q_LORA = 384
MLA_KV_LORA = 256
ROPE_THETA = 10000.0
MOBA_HEADS = 8
MOBA_BLOCK = 256
MOBA_TOPK = 3
SB_HEADS = 16
NUM_BUCKETS = 32
MAX_DISTANCE = 128
RMS_EPS = 1e-6
D_FF = 2816

LANES = 128
TQ = 256
TM = 512
TM_PROJ = 1024
FF_CHUNK = 256
MASK_PENALTY = -1e30
LOG2_E = 1.0 / math.log(2.0)
VMEM_LIMIT = 48 * 1024 * 1024

F32 = jnp.float32
BF16 = jnp.bfloat16


def _rms(x, g):
    return x * lax.rsqrt(jnp.mean(x * x, axis=-1, keepdims=True) + RMS_EPS) * g


def _dot(a, b):
    return jnp.dot(a, b, preferred_element_type=F32)


def _dot_nt(a, b):
    return lax.dot_general(a, b, (((1,), (1,)), ((), ())), preferred_element_type=F32)


def _params(*semantics):
    return pltpu.CompilerParams(dimension_semantics=semantics, vmem_limit_bytes=VMEM_LIMIT)


def _whole(shape):
    nd = len(shape)
    return pl.BlockSpec(shape, lambda *_: (0,) * nd, pipeline_mode=pl.Buffered(1))


def _one_layer(stacked_shape, layer):
    return pl.BlockSpec((None,) + tuple(stacked_shape[1:]), lambda *_: (layer, 0, 0),
                        pipeline_mode=pl.Buffered(1))


def _norm_matmul_kernel(h_ref, g_ref, w_ref, o_ref, *, q_cols, q_scale):
    u = _rms(h_ref[...], g_ref[...]).astype(BF16)
    for c in range(0, o_ref.shape[-1], 512):
        y = _dot(u, w_ref[:, c:c + 512])
        if c < q_cols:
            y = y * q_scale
        o_ref[:, c:c + 512] = y.astype(o_ref.dtype)


def _norm_matmul(h, g, w_stack, layer, q_cols, q_scale):
    n, d = h.shape
    n_out = w_stack.shape[2]
    return pl.pallas_call(
        functools.partial(_norm_matmul_kernel, q_cols=q_cols, q_scale=q_scale),
        grid=(n // TM_PROJ,),
        in_specs=[pl.BlockSpec((TM_PROJ, d), lambda t: (t, 0)), _whole((1, d)),
                  _one_layer(w_stack.shape, layer)],
        out_specs=pl.BlockSpec((TM_PROJ, n_out), lambda t: (t, 0)),
        out_shape=jax.ShapeDtypeStruct((n, n_out), BF16),
        compiler_params=_params("arbitrary"),
        name="norm_matmul",
    )(h, g.reshape(1, d), w_stack)


def _head_lanes(shape, hh):
    lane = lax.broadcasted_iota(jnp.int32, shape, 1)
    return (lane >= hh * HEAD_DIM) & (lane < (hh + 1) * HEAD_DIM)


def _spare_lane_base(hh):
    return HEAD_DIM * (1 - hh)


def _normalized(acc, hh):
    base = _spare_lane_base(hh)
    return jnp.where(_head_lanes(acc.shape, hh), acc / acc[:, base:base + 1], 0.0)


_O_CQ = 0
_O_KR = MLA_Q_LORA
_O_CKV = _O_KR + LANES
_O_MOBA = _O_CKV + MLA_KV_LORA
_W_IN_EXT = _O_MOBA + 3 * MOBA_HEADS * HEAD_DIM
_QW = MLA_HEADS * LANES


def _even_in_kernel(h_ref, g_ref, w_in_ref, gq_ref, w_uq_ref, gkv_ref, w_k_ref, w_v_ref,
                    cq_ref, sq_ref, ck_ref, sk_ref, q_ref, k_ref, v_ref, moba_ref):
    u = _rms(h_ref[...], g_ref[...]).astype(BF16)
    cq_kr = _dot(u, w_in_ref[:, _O_CQ:_O_CKV])
    c_q, kr = cq_kr[:, :_O_KR], cq_kr[:, _O_KR:]
    c_kv = _dot(u, w_in_ref[:, _O_CKV:_O_MOBA])
    hw = MOBA_HEADS * HEAD_DIM
    for c in range(0, 3 * hw, hw):
        y = _dot(u, w_in_ref[:, _O_MOBA + c:_O_MOBA + c + hw])
        if c == 0:
            y = y * LOG2_E
        moba_ref[:, c:c + hw] = y.astype(BF16)

    cqn = _rms(c_q, gq_ref[...]).astype(BF16)
    qa = _dot(cqn, w_uq_ref[...])
    qb = pltpu.roll(qa, _QW - MLA_ROPE, axis=1)
    cq, sq = cq_ref[...], sq_ref[...]
    for h in range(MLA_HEADS):
        s = slice(h * LANES, (h + 1) * LANES)
        q_ref[:, s] = (qa[:, s] * cq + qb[:, s] * sq).astype(BF16)

    ckvn = _rms(c_kv, gkv_ref[...]).astype(BF16)
    kn = _dot(ckvn, w_k_ref[...])
    v = _dot(ckvn, w_v_ref[...])
    lane = lax.broadcasted_iota(jnp.int32, v.shape, 1)
    spare = jnp.where((lane // LANES) % 2 == 0, _spare_lane_base(0), _spare_lane_base(1))
    v_ref[...] = jnp.where(lane % LANES == spare, 1.0, v).astype(BF16)
    k_rope = kr * ck_ref[...] + pltpu.roll(kr, LANES - MLA_ROPE, axis=1) * sk_ref[...]
    for h in range(MLA_HEADS):
        s = slice(h * LANES, (h + 1) * LANES)
        k_ref[:, s] = (kn[:, s] + k_rope).astype(BF16)


def _even_in(h, g, w_in, gq, w_uq, gkv, w_k, w_v, tables, seq):
    n, d = h.shape
    row = lambda t: (t, 0)
    pos = lambda t: (t % (seq // TM_PROJ), 0)
    tab = pl.BlockSpec((TM_PROJ, LANES), pos)
    outs = [(_QW, "q"), (_QW, "k"), (_QW, "v"), (3 * MOBA_HEADS * HEAD_DIM, "moba")]
    return pl.pallas_call(
        _even_in_kernel,
        grid=(n // TM_PROJ,),
        in_specs=[pl.BlockSpec((TM_PROJ, d), row), _whole((1, d)), _whole(w_in.shape),
                  _whole((1, MLA_Q_LORA)), _whole(w_uq.shape), _whole((1, MLA_KV_LORA)),
                  _whole(w_k.shape), _whole(w_v.shape), tab, tab, tab, tab],
        out_specs=[pl.BlockSpec((TM_PROJ, w), row) for w, _ in outs],
        out_shape=[jax.ShapeDtypeStruct((n, w), BF16) for w, _ in outs],
        compiler_params=_params("arbitrary"),
        name="even_in",
    )(h, g.reshape(1, d), w_in, gq.reshape(1, -1), w_uq, gkv.reshape(1, -1), w_k, w_v, *tables)


def _two_block_attention(qs, k_of, v_ofs, r0, add_nears):
    mid, end = r0 + TQ, r0 + 2 * TQ
    stack = lambda parts: jnp.concatenate(parts, axis=0)
    s_main_all = _dot_nt(stack(list(qs)), k_of(0, mid))
    s_last_all = _dot_nt(stack([q[TQ:] for q in qs]), k_of(mid, end))
    accs = []
    for h, add_near in enumerate(add_nears):
        s_main = s_main_all[h * 2 * TQ:(h + 1) * 2 * TQ]
        a_near = [add_near(s_main[:TQ, r0 - TQ:r0], 1)] if r0 > 0 else []
        a_far = [s_main[:TQ, :r0 - TQ]] if r0 > TQ else []
        s_a = jnp.concatenate(a_far + a_near + [add_near(s_main[:TQ, r0:], 0)], axis=1)
        b_far = [s_main[TQ:, :r0]] if r0 > 0 else []
        s_b = jnp.concatenate(b_far + [add_near(s_main[TQ:, r0:], 1),
                                       add_near(s_last_all[h * TQ:(h + 1) * TQ], 0)], axis=1)
        p_a = jnp.exp2(s_a - jnp.max(s_a, axis=-1, keepdims=True)).astype(BF16)
        p_b = jnp.exp2(s_b - jnp.max(s_b, axis=-1, keepdims=True)).astype(BF16)
        acc = _dot(stack([p_a, p_b[:, :mid]]), v_ofs[h](0, mid))
        accs.append(stack([acc[:TQ], acc[TQ:] + _dot(p_b[:, mid:], v_ofs[h](mid, end))]))
    return accs


def _mla_kernel(q_ref, k_ref, v_ref, o_ref):
    s_len = q_ref.shape[1]
    row = lax.broadcasted_iota(jnp.int32, (TQ, TQ), 0)
    col = lax.broadcasted_iota(jnp.int32, (TQ, TQ), 1)
    causal = col <= row
    add_near = lambda s, d: jnp.where(causal, s, -jnp.inf) if d == 0 else s
    zeros = jnp.zeros((2 * TQ, LANES), BF16)
    for r0 in range(0, s_len, 2 * TQ):
        rows = slice(r0, r0 + 2 * TQ)
        qs = [jnp.concatenate([q_ref[0, rows, :LANES], zeros], axis=1),
              jnp.concatenate([zeros, q_ref[0, rows, LANES:]], axis=1)]
        v_ofs = [lambda a, b, lanes=slice(hh * LANES, (hh + 1) * LANES): v_ref[0, a:b, lanes]
                 for hh in range(2)]
        accs = _two_block_attention(qs, lambda a, b: k_ref[0, a:b, :], v_ofs, r0, [add_near] * 2)
        out = _normalized(accs[0], 0) + _normalized(accs[1], 1)
        o_ref[0, rows, :] = out.astype(o_ref.dtype)


def _mla_attention(q, k, v):
    b, s, _ = q.shape
    pairs = MLA_HEADS // 2
    qkv = pl.BlockSpec((1, s, 2 * LANES), lambda bi, p: (bi, 0, p))
    return pl.pallas_call(
        _mla_kernel,
        grid=(b, pairs),
        in_specs=[qkv, qkv, qkv],
        out_specs=pl.BlockSpec((1, s, LANES), lambda bi, p: (bi, 0, p)),
        out_shape=jax.ShapeDtypeStruct((b, s, MLA_HEADS * MLA_V), BF16),
        compiler_params=_params("arbitrary", "arbitrary"),
        name="mla_attention",
    )(q, k, v)


_SOFTPLUS2_CLAMP = 64.0


def _softplus2(z):
    return jnp.maximum(z, jnp.log2(1.0 + jnp.exp2(jnp.minimum(z, _SOFTPLUS2_CLAMP))))


SB_UNDERFLOW = 160.0
SB_NEAR_BLOCKS = 2
SB_PROBE_STEPS = 8


def _sb_kernel(q_ref, k_ref, v_ref, o_ref, qm_scr, vm_scr, acc_scr, carry_scr, limit_scr, plan_scr):
    s_len = q_ref.shape[1]
    n_blk = s_len // TQ
    def prepare():
        q, v = q_ref[0], v_ref[0]
        for hh in range(2):
            lane_mask = jnp.where(_head_lanes((1, LANES), hh), 1.0, 0.0).astype(BF16)
            qm_scr[hh] = q * lane_mask
            vm_scr[hh] = v * lane_mask
        acc_scr[...] = jnp.zeros(acc_scr.shape, F32)

    def set_limits():
        q, k = q_ref[0], k_ref[0]
        feat = lax.broadcasted_iota(jnp.int32, (LANES, LANES), 0) // HEAD_DIM
        lane = lax.broadcasted_iota(jnp.int32, (LANES, LANES), 1) // HEAD_DIM
        same_head = jnp.where(feat == lane, 1.0, 0.0).astype(BF16)
        q_max2 = jnp.max(_dot(q * q, same_head), axis=0, keepdims=True)
        k_max2 = jnp.max(_dot(k * k, same_head), axis=0, keepdims=True)
        z_bound = jnp.sqrt(q_max2 * k_max2) * 1.02
        for hh in range(2):
            bound = jnp.max(jnp.where(_head_lanes((1, LANES), hh), z_bound, 0.0), axis=1, keepdims=True)
            limit_scr[hh] = jnp.broadcast_to(bound + SB_UNDERFLOW, limit_scr.shape[1:])

    row = lax.broadcasted_iota(jnp.int32, (TQ, TQ), 0)
    col = lax.broadcasted_iota(jnp.int32, (TQ, TQ), 1)
    strict = col < row
    tri = jnp.where(row >= col, 1.0, 0.0).astype(BF16)

    def tile_rows(j, r0, r1):
        keys = slice(j * TQ, (j + 1) * TQ)
        rows = slice(r0 * TQ, r1 * TQ)
        m = (r1 - r0) * TQ
        first = TQ if r0 == j else 0
        z = _dot_nt(jnp.concatenate([qm_scr[0, rows, :], qm_scr[1, rows, :]], axis=0),
                    k_ref[0, keys, :])
        sp = _softplus2(z)
        if first:
            parts = []
            for base in (0, m):
                parts.append(jnp.where(strict, sp[base:base + TQ], 0.0))
                if m > TQ:
                    parts.append(sp[base + TQ:base + m])
            sp = jnp.concatenate(parts, axis=0)
        c = _dot(sp.astype(BF16), tri)
        tot = jnp.broadcast_to(c[:, 0:1], (2 * m, LANES))
        w = []
        for h, base in enumerate((0, m)):
            w_h = []
            if first:
                d = slice(base, base + TQ)
                w_h.append(jnp.where(strict, jnp.exp2(z[d] - c[d]), 0.0))
                carry_scr[h, keys, :] = tot[d]
            if m > first:
                below = slice(r0 * TQ + first, r1 * TQ)
                o = slice(base + first, base + m)
                carry = carry_scr[h, below, :]
                w_h.append(jnp.exp2(z[o] - c[o] - jnp.concatenate([carry, carry], axis=1)))
                carry_scr[h, below, :] = carry + tot[o]
            w.append(w_h[0] if len(w_h) == 1 else jnp.concatenate(w_h, axis=0))
        vv = jnp.concatenate([vm_scr[0, keys, :], vm_scr[1, keys, :]], axis=0)
        acc_scr[rows, :] += _dot(jnp.concatenate(w, axis=1).astype(BF16), vv)

    def near_tiles(n_near):
        for j in range(n_blk - 1, -1, -1):
            tile_rows(j, j, min(j + n_near, n_blk))

    def live(r0, r1):
        slack = [jnp.min(carry_scr[h, r0 * TQ:r1 * TQ, :], axis=0, keepdims=True) - limit_scr[h, 0:1, :]
                 for h in range(2)]
        return jnp.min(jnp.minimum(slack[0], slack[1])) < 0.0

    def diagonal(dist):
        for i in range(dist, n_blk):
            tile_rows(i - dist, i, i + 1)

    step = pl.program_id(0) * pl.num_programs(1) + pl.program_id(1)

    @pl.when(step == 0)
    def _():
        plan_scr[0] = 0
        plan_scr[1] = 0

    wide = plan_scr[0] == 1

    @pl.when(jnp.logical_not(wide))
    def _():
        prepare()
        set_limits()
        near_tiles(SB_NEAR_BLOCKS)
        flags = [live(i, i + 1) for i in range(SB_NEAR_BLOCKS, n_blk)]
        n_live = sum(f.astype(jnp.int32) for f in flags)
        plan_scr[1] += (n_live >= 2).astype(jnp.int32)
        plan_scr[2] = n_live

        @pl.when(n_live >= 2)
        def _():
            diagonal(SB_NEAR_BLOCKS)

        @pl.when(n_live == 1)
        def _():
            for i, f in zip(range(SB_NEAR_BLOCKS, n_blk), flags):
                pl.when(f)(functools.partial(tile_rows, i - SB_NEAR_BLOCKS, i, i + 1))

    @pl.when(wide)
    def _():
        prepare()
        set_limits()
        near_tiles(SB_NEAR_BLOCKS + 1)
        plan_scr[2] = 1

    def farther(dist):
        @pl.when(live(dist, n_blk))
        def _():
            diagonal(dist)
            if dist + 1 < n_blk:
                farther(dist + 1)

    if SB_NEAR_BLOCKS + 1 < n_blk:
        pl.when(plan_scr[2] > 0)(functools.partial(farther, SB_NEAR_BLOCKS + 1))

    @pl.when(step == SB_PROBE_STEPS - 1)
    def _():
        plan_scr[0] = (5 * plan_scr[1] >= 3 * SB_PROBE_STEPS).astype(jnp.int32)

    o_ref[0] = acc_scr[...].astype(o_ref.dtype)


def _sb_attention(qkv):
    b, s, _ = qkv.shape
    pairs = SB_HEADS // 2
    return pl.pallas_call(
        _sb_kernel,
        grid=(b, pairs),
        in_specs=[pl.BlockSpec((1, s, LANES), lambda bi, p: (bi, 0, p)),
                  pl.BlockSpec((1, s, LANES), lambda bi, p: (bi, 0, pairs + p)),
                  pl.BlockSpec((1, s, LANES), lambda bi, p: (bi, 0, 2 * pairs + p))],
        out_specs=pl.BlockSpec((1, s, LANES), lambda bi, p: (bi, 0, p)),
        out_shape=jax.ShapeDtypeStruct((b, s, SB_HEADS * HEAD_DIM), BF16),
        scratch_shapes=[pltpu.VMEM((2, s, LANES), BF16), pltpu.VMEM((2, s, LANES), BF16),
                        pltpu.VMEM((s, LANES), F32), pltpu.VMEM((2, s, LANES), F32),
                        pltpu.VMEM((2, 8, LANES), F32), pltpu.SMEM((3,), jnp.int32)],
        compiler_params=_params("arbitrary", "arbitrary"),
        name="sb_attention",
    )(qkv, qkv, qkv)


def _t5_bucket_np(rel):
    max_exact = NUM_BUCKETS // 2
    rel = np.maximum(rel, 0)
    large = max_exact + (np.log(np.maximum(rel, max_exact) / max_exact)
                         / math.log(MAX_DISTANCE / max_exact) * (NUM_BUCKETS - max_exact)).astype(np.int64)
    return np.where(rel < max_exact, rel, np.minimum(large, NUM_BUCKETS - 1)).astype(np.int32)


def _bias_kernel(bucket_ref, rel_bias_ref, o_ref):
    h = pl.program_id(0)
    for plane in range(2):
        bk = bucket_ref[plane]
        acc = jnp.full(bk.shape, -jnp.inf, F32)
        for b in range(NUM_BUCKETS):
            acc = jnp.where(bk == b, rel_bias_ref[b, h] * LOG2_E, acc)
        o_ref[0, plane] = acc
    o_ref[0, 2] = jnp.full((TQ, TQ), rel_bias_ref[NUM_BUCKETS - 1, h] * LOG2_E, F32)


def _moba_bias_tables(rel_bias):
    off = np.arange(TQ)
    rel_own = off[:, None] - off[None, :]
    own = np.where(rel_own >= 0, _t5_bucket_np(rel_own), -1)
    adj = _t5_bucket_np(rel_own + MOBA_BLOCK)
    buckets = jnp.asarray(np.stack([own, adj]).astype(np.int32))
    return pl.pallas_call(
        _bias_kernel,
        grid=(MOBA_HEADS,),
        in_specs=[_whole(buckets.shape), pl.BlockSpec(memory_space=pltpu.SMEM)],
        out_specs=pl.BlockSpec((1, 3, TQ, TQ), lambda h: (h, 0, 0, 0)),
        out_shape=jax.ShapeDtypeStruct((MOBA_HEADS, 3, TQ, TQ), F32),
        compiler_params=_params("arbitrary"),
        name="moba_bias_tables",
    )(buckets, rel_bias)


_N_BLK_LANES = 8


def _moba_constants(nb):
    assert nb <= _N_BLK_LANES
    diff = np.zeros((LANES, LANES), np.float32)
    count = np.zeros((LANES, LANES), np.float32)
    for hh in range(2):
        for a in range(nb):
            for b in range(nb):
                if a != b:
                    pair = hh * HEAD_DIM + a * 8 + b
                    diff[pair, a] += 1.0
                    diff[pair, b] -= 1.0
                    for part in range(2):
                        count[pair, (2 * hh + part) * _N_BLK_LANES + b] = 1.0
    return jnp.asarray(diff, BF16), jnp.asarray(count, BF16)


_T_FAR_HI, _T_FAR_LO, _T_FUTURE, _T_UNSEL, _N_HEAD_TABLES = 0, 1, 2, 3, 4


def _moba_fill_tables(head_tab, key_tab, gate_tab, s_len):
    lane = lax.broadcasted_iota(jnp.int32, (s_len, LANES), 1)
    row_blk = lax.broadcasted_iota(jnp.int32, (s_len, LANES), 0) // MOBA_BLOCK
    group, blk = lane // _N_BLK_LANES, lane % _N_BLK_LANES
    key_tab[...] = jnp.where((group < 4) & (blk == row_blk), 1.0, 0.0).astype(BF16)
    far = blk < row_blk - 1
    for hh in range(2):
        in_hi, in_lo = group == 2 * hh, group == 2 * hh + 1
        store = lambda t, x: head_tab.__setitem__((hh, t), x.astype(BF16))
        store(_T_FAR_HI, jnp.where(in_hi & far, 1.0, 0.0))
        store(_T_FAR_LO, jnp.where(in_lo & far, 1.0, 0.0))
        store(_T_FUTURE, jnp.where(in_hi & (blk > row_blk), MASK_PENALTY, 0.0))
        store(_T_UNSEL, jnp.where(in_hi & (blk != row_blk), MASK_PENALTY, 0.0))
    pair_a, pair_b = (lane % HEAD_DIM) // 8, lane % 8
    valid = pair_a < row_blk
    gate_tab[0] = jnp.where(valid, 1.0, 0.0)
    gate_tab[1] = jnp.where(valid & (pair_a < pair_b), 1.0, 0.0)


def _moba_kernel(q_ref, k_ref, v_ref, bias_ref, diff_ref, count_ref, o_ref,
                 qaug_scr, vm_scr, head_tab, key_tab, gate_tab, *, nb):
    s_len = q_ref.shape[1]

    @pl.when((pl.program_id(0) == 0) & (pl.program_id(1) == 0))
    def _():
        _moba_fill_tables(head_tab, key_tab, gate_tab, s_len)

    q, k, v = q_ref[0], k_ref[0], v_ref[0]
    kbar = jnp.mean(k.astype(F32).reshape(nb, MOBA_BLOCK, LANES), axis=1)
    kbar = jnp.concatenate([kbar, jnp.zeros((LANES - nb, LANES), F32)], axis=0)
    p0 = kbar.astype(BF16)
    r1 = kbar - p0.astype(F32)
    p1 = r1.astype(BF16)
    p2 = (r1 - p1.astype(F32)).astype(BF16)
    d = diff_ref[...]
    kdiff = _dot(d, p0) + _dot(d, p1) + _dot(d, p2)
    pair_head = lax.broadcasted_iota(jnp.int32, (LANES, LANES), 0) // HEAD_DIM
    feat_head = lax.broadcasted_iota(jnp.int32, (LANES, LANES), 1) // HEAD_DIM
    kdiff = jnp.where(pair_head == feat_head, kdiff, 0.0).astype(BF16)

    g = _dot_nt(q, kdiff)
    beats = jnp.where(g > 0.0, gate_tab[0], jnp.where(g == 0.0, gate_tab[1], 0.0))
    rank = _dot(beats.astype(BF16), count_ref[...]).astype(BF16)
    top = rank < float(MOBA_TOPK)

    lane = lax.broadcasted_iota(jnp.int32, (1, LANES), 1)
    for hh in range(2):
        lane_mask = jnp.where(_head_lanes((1, LANES), hh), 1.0, 0.0).astype(BF16)
        ones_lane = jnp.where(lane == _spare_lane_base(hh), 1.0, 0.0).astype(BF16)
        vm_scr[hh] = v * lane_mask + ones_lane
        far = bias_ref[hh, 2, 0:1, 0:LANES]
        far_hi = far.astype(BF16)
        far_lo = (far - far_hi.astype(F32)).astype(BF16)
        chosen_val = head_tab[hh, _T_FAR_HI] * far_hi + head_tab[hh, _T_FAR_LO] * far_lo + head_tab[hh, _T_FUTURE]
        qaug_scr[hh, :, :LANES] = q * lane_mask
        qaug_scr[hh, :, LANES:] = jnp.where(top, chosen_val, head_tab[hh, _T_UNSEL])

    keys_of = lambda a, b: jnp.concatenate([k_ref[0, a:b, :], key_tab[a:b, :]], axis=1)
    for r0 in range(0, s_len, 2 * TQ):
        rows = slice(r0, r0 + 2 * TQ)
        accs = _two_block_attention(
            [qaug_scr[hh, rows, :] for hh in range(2)], keys_of,
            [lambda a, b, hh=hh: vm_scr[hh, a:b, :] for hh in range(2)], r0,
            [lambda s, d, hh=hh: s + bias_ref[hh, d] for hh in range(2)])
        out = _normalized(accs[0], 0) + _normalized(accs[1], 1)
        o_ref[0, rows, :] = out.astype(o_ref.dtype)


def _moba_attention(qkv, bias_tables):
    b, s, _ = qkv.shape
    nb = s // MOBA_BLOCK
    pairs = MOBA_HEADS // 2
    diff, count = _moba_constants(nb)
    return pl.pallas_call(
        functools.partial(_moba_kernel, nb=nb),
        grid=(pairs, b),
        in_specs=[pl.BlockSpec((1, s, LANES), lambda p, bi: (bi, 0, p)),
                  pl.BlockSpec((1, s, LANES), lambda p, bi: (bi, 0, pairs + p)),
                  pl.BlockSpec((1, s, LANES), lambda p, bi: (bi, 0, 2 * pairs + p)),
                  pl.BlockSpec((2, 3, TQ, TQ), lambda p, bi: (p, 0, 0, 0)),
                  _whole(diff.shape), _whole(count.shape)],
        out_specs=pl.BlockSpec((1, s, LANES), lambda p, bi: (bi, 0, p)),
        out_shape=jax.ShapeDtypeStruct((b, s, MOBA_HEADS * HEAD_DIM), BF16),
        scratch_shapes=[pltpu.VMEM((2, s, 2 * LANES), BF16), pltpu.VMEM((2, s, LANES), BF16),
                        pltpu.VMEM((2, _N_HEAD_TABLES, s, LANES), BF16),
                        pltpu.VMEM((s, LANES), BF16), pltpu.VMEM((2, s, LANES), F32)],
        compiler_params=_params("arbitrary", "arbitrary"),
        name="moba_attention",
    )(qkv, qkv, qkv, bias_tables, diff, count)


def _mix_ffn_kernel(*refs, n_in):
    a_refs = refs[:n_in]
    h_ref, w_out_ref, gmix_ref, gpre_ref, wgu_ref, wd_ref, gpost_ref, o_ref, act_scr = refs[n_in:]
    mixed, k0 = None, 0
    for a_ref in a_refs:
        k1 = k0 + a_ref.shape[1]
        part = _dot(a_ref[...], w_out_ref[k0:k1, :])
        mixed, k0 = part if mixed is None else mixed + part, k1
    x = h_ref[...] + _rms(mixed, gmix_ref[...])
    u = _rms(x, gpre_ref[...]).astype(BF16)
    for c in range(0, D_FF, FF_CHUNK):
        gate = _dot(u, wgu_ref[:, c:c + FF_CHUNK])
        up = _dot(u, wgu_ref[:, D_FF + c:D_FF + c + FF_CHUNK])
        act_scr[:, c:c + FF_CHUNK] = (gate * jax.nn.sigmoid(gate) * up).astype(BF16)
    f = _dot(act_scr[...], wd_ref[...])
    o_ref[...] = x + _rms(f, gpost_ref[...])


def _mix_ffn(acts, h, w_out, gmix, gpre, wgu, wd, gpost):
    n, d = h.shape
    row = lambda t: (t, 0)
    vec = _whole((1, d))
    return pl.pallas_call(
        functools.partial(_mix_ffn_kernel, n_in=len(acts)),
        grid=(n // TM,),
        in_specs=([pl.BlockSpec((TM, a.shape[1]), row) for a in acts]
                  + [pl.BlockSpec((TM, d), row), _one_layer(w_out[0].shape, w_out[1]), vec, vec,
                     _one_layer(wgu[0].shape, wgu[1]), _one_layer(wd[0].shape, wd[1]), vec]),
        out_specs=pl.BlockSpec((TM, d), row),
        out_shape=jax.ShapeDtypeStruct((n, d), F32),
        scratch_shapes=[pltpu.VMEM((TM, D_FF), BF16)],
        compiler_params=_params("arbitrary"),
        name="mix_ffn",
    )(*acts, h, w_out[0], gmix.reshape(1, d), gpre.reshape(1, d), wgu[0], wd[0], gpost.reshape(1, d))


def _rot_half_cols(w):
    half = w.shape[-1] // 2
    return jnp.concatenate([-w[..., half:], w[..., :half]], axis=-1)


def _prep_even_weights(w_in, w_uq, w_ukv):
    d = w_in.shape[0]
    o1, o2, o3 = MLA_Q_LORA, MLA_Q_LORA + MLA_KV_LORA, MLA_Q_LORA + MLA_KV_LORA + MLA_ROPE
    hw = MOBA_HEADS * HEAD_DIM
    z = lambda *shape: jnp.zeros(shape, F32)
    kr = w_in[:, o2:o3]
    w_in_ext = jnp.concatenate(
        [w_in[:, :o1], z(d, MLA_NOPE), kr, _rot_half_cols(kr), w_in[:, o1:o2],
         w_in[:, o3:o3 + hw] * (HEAD_DIM ** -0.5), w_in[:, o3 + hw:]], axis=1).astype(BF16)

    wq = w_uq.reshape(MLA_Q_LORA, MLA_HEADS, MLA_NOPE + MLA_ROPE)
    nope, rp = wq[..., :MLA_NOPE], wq[..., MLA_NOPE:]
    assert MLA_NOPE + 2 * MLA_ROPE == LANES
    w_uq_ext = jnp.concatenate([nope, rp, _rot_half_cols(rp)], axis=-1).reshape(MLA_Q_LORA, _QW).astype(BF16)

    wkv = w_ukv.reshape(MLA_KV_LORA, MLA_HEADS, MLA_NOPE + MLA_V)
    k_nope, v = wkv[..., :MLA_NOPE], wkv[..., MLA_NOPE:]
    w_k = jnp.concatenate([k_nope, jnp.zeros_like(k_nope)], axis=-1).reshape(MLA_KV_LORA, _QW).astype(BF16)
    v = v.reshape(MLA_KV_LORA, MLA_HEADS // 2, 2, MLA_V)
    zero = jnp.zeros_like(v[:, :, 0])
    w_v = jnp.stack([jnp.concatenate([v[:, :, 0], zero], axis=-1),
                     jnp.concatenate([zero, v[:, :, 1]], axis=-1)], axis=2)
    w_v = w_v.reshape(MLA_KV_LORA, _QW).astype(BF16)
    return w_in_ext, w_uq_ext, w_k, w_v


def _rope_tables(seq):
    half = MLA_ROPE // 2
    inv_freq = ROPE_THETA ** (-jnp.arange(half, dtype=F32) / half)
    ang = jnp.arange(seq).astype(F32)[:, None] * inv_freq[None, :]
    cos, sin = jnp.cos(ang), jnp.sin(ang)
    ones, zeros = jnp.ones((seq, MLA_NOPE), F32), jnp.zeros((seq, MLA_NOPE), F32)
    tail = jnp.zeros((seq, LANES - MLA_NOPE - MLA_ROPE), F32)
    scale = (MLA_NOPE + MLA_ROPE) ** -0.5 * LOG2_E
    cq = jnp.concatenate([ones, cos, cos, tail], axis=1) * scale
    sq = jnp.concatenate([zeros, sin, sin, tail], axis=1) * scale
    ck = jnp.concatenate([zeros, cos, cos, tail], axis=1)
    sk = jnp.concatenate([zeros, sin, sin, tail], axis=1)
    return cq, sq, ck, sk


def kernel(x, mix_pre_g, mix_post_g, ffn_pre_g, ffn_post_g, ab_w_in, mla_q_norm_g, mla_w_uq,
           mla_kv_norm_g, mla_w_ukv, ab_w_out, rel_bias, sb_w_qkv, sb_w_out, ffn_w_gate_up, ffn_w_down):
    b, s, d = x.shape
    n = b * s
    h = x.reshape(n, d)
    tables = _rope_tables(s)
    bias_tables = _moba_bias_tables(rel_bias)
    ab_w_out, sb_w_qkv, sb_w_out, ffn_w_gate_up, ffn_w_down = (
        w.astype(BF16) for w in (ab_w_out, sb_w_qkv, sb_w_out, ffn_w_gate_up, ffn_w_down))
    for layer in range(DEPTH):
        if layer % 2 == 0:
            e = layer // 2
            w_in, w_uq, w_k, w_v = _prep_even_weights(ab_w_in[e], mla_w_uq[e], mla_w_ukv[e])
            q, k, v, moba_qkv = _even_in(h, mix_pre_g[layer], w_in, mla_q_norm_g[e], w_uq,
                                         mla_kv_norm_g[e], w_k, w_v, tables, s)
            mla_out = _mla_attention(q.reshape(b, s, -1), k.reshape(b, s, -1), v.reshape(b, s, -1))
            moba_out = _moba_attention(moba_qkv.reshape(b, s, -1), bias_tables)
            mixed, w_out = [mla_out.reshape(n, -1), moba_out.reshape(n, -1)], (ab_w_out, e)
        else:
            o = layer // 2
            qkv = _norm_matmul(h, mix_pre_g[layer], sb_w_qkv, o,
                               SB_HEADS * HEAD_DIM, HEAD_DIM ** -0.5 * LOG2_E)
            mixed, w_out = [_sb_attention(qkv.reshape(b, s, -1)).reshape(n, -1)], (sb_w_out, o)
        h = _mix_ffn(mixed, h, w_out, mix_post_g[layer], ffn_pre_g[layer],
                     (ffn_w_gate_up, layer), (ffn_w_down, layer), ffn_post_g[layer])
    return h.reshape(b, s, d)
```

```python
import functools
import math

import numpy as np
import jax
import jax.numpy as jnp
from jax import lax
from jax.experimental import pallas as pl
from jax.experimental.pallas import tpu as pltpu

D_MODEL = 1024
DEPTH = 4
HEAD_DIM = 64
MLA_HEADS = 8
MLA_NOPE = 64
MLA_ROPE = 32
MLA_V = 64
MLA_Q_LORA = 384
MLA_KV_LORA = 256
ROPE_THETA = 10000.0
MOBA_HEADS = 8
MOBA_BLOCK = 256
MOBA_TOPK = 3
SB_HEADS = 16
NUM_BUCKETS = 32
MAX_DISTANCE = 128
RMS_EPS = 1e-6
D_FF = 2816

LANES = 128
TQ = 256
TM = 512
TM_PROJ = 1024
FF_CHUNK = 256
MASK_PENALTY = -1e30
LOG2_E = 1.0 / math.log(2.0)
VMEM_LIMIT = 48 * 1024 * 1024

F32 = jnp.float32
BF16 = jnp.bfloat16


def _rms(x, g):
    return x * lax.rsqrt(jnp.mean(x * x, axis=-1, keepdims=True) + RMS_EPS) * g


def _dot(a, b):
    return jnp.dot(a, b, preferred_element_type=F32)


def _dot_nt(a, b):
    return lax.dot_general(a, b, (((1,), (1,)), ((), ())), preferred_element_type=F32)


def _params(*semantics):
    return pltpu.CompilerParams(dimension_semantics=semantics, vmem_limit_bytes=VMEM_LIMIT)


def _whole(shape):
    nd = len(shape)
    return pl.BlockSpec(shape, lambda *_: (0,) * nd, pipeline_mode=pl.Buffered(1))


def _one_layer(stacked_shape, layer):
    return pl.BlockSpec((None,) + tuple(stacked_shape[1:]), lambda *_: (layer, 0, 0),
                        pipeline_mode=pl.Buffered(1))


def _norm_matmul_kernel(h_ref, g_ref, w_ref, o_ref, *, q_cols, q_scale):
    u = _rms(h_ref[...], g_ref[...]).astype(BF16)
    for c in range(0, o_ref.shape[-1], 512):
        y = _dot(u, w_ref[:, c:c + 512])
        if c < q_cols:
            y = y * q_scale
        o_ref[:, c:c + 512] = y.astype(o_ref.dtype)


def _norm_matmul(h, g, w_stack, layer, q_cols, q_scale):
    n, d = h.shape
    n_out = w_stack.shape[2]
    return pl.pallas_call(
        functools.partial(_norm_matmul_kernel, q_cols=q_cols, q_scale=q_scale),
        grid=(n // TM_PROJ,),
        in_specs=[pl.BlockSpec((TM_PROJ, d), lambda t: (t, 0)), _whole((1, d)),
                  _one_layer(w_stack.shape, layer)],
        out_specs=pl.BlockSpec((TM_PROJ, n_out), lambda t: (t, 0)),
        out_shape=jax.ShapeDtypeStruct((n, n_out), BF16),
        compiler_params=_params("arbitrary"),
        name="norm_matmul",
    )(h, g.reshape(1, d), w_stack)


def _head_lanes(shape, hh):
    lane = lax.broadcasted_iota(jnp.int32, shape, 1)
    return (lane >= hh * HEAD_DIM) & (lane < (hh + 1) * HEAD_DIM)


def _spare_lane_base(hh):
    return HEAD_DIM * (1 - hh)


def _normalized(acc, hh):
    base = _spare_lane_base(hh)
    return jnp.where(_head_lanes(acc.shape, hh), acc / acc[:, base:base + 1], 0.0)


_O_CQ = 0
_O_KR = MLA_Q_LORA
_O_CKV = _O_KR + LANES
_O_MOBA = _O_CKV + MLA_KV_LORA
_W_IN_EXT = _O_MOBA + 3 * MOBA_HEADS * HEAD_DIM
_QW = MLA_HEADS * LANES


def _even_in_kernel(h_ref, g_ref, w_in_ref, gq_ref, w_uq_ref, gkv_ref, w_k_ref, w_v_ref,
                    cq_ref, sq_ref, ck_ref, sk_ref, q_ref, k_ref, v_ref, moba_ref):
    u = _rms(h_ref[...], g_ref[...]).astype(BF16)
    cq_kr = _dot(u, w_in_ref[:, _O_CQ:_O_CKV])
    c_q, kr = cq_kr[:, :_O_KR], cq_kr[:, _O_KR:]
    c_kv = _dot(u, w_in_ref[:, _O_CKV:_O_MOBA])
    hw = MOBA_HEADS * HEAD_DIM
    for c in range(0, 3 * hw, hw):
        y = _dot(u, w_in_ref[:, _O_MOBA + c:_O_MOBA + c + hw])
        if c == 0:
            y = y * LOG2_E
        moba_ref[:, c:c + hw] = y.astype(BF16)

    cqn = _rms(c_q, gq_ref[...]).astype(BF16)
    qa = _dot(cqn, w_uq_ref[...])
    qb = pltpu.roll(qa, _QW - MLA_ROPE, axis=1)
    cq, sq = cq_ref[...], sq_ref[...]
    for h in range(MLA_HEADS):
        s = slice(h * LANES, (h + 1) * LANES)
        q_ref[:, s] = (qa[:, s] * cq + qb[:, s] * sq).astype(BF16)

    ckvn = _rms(c_kv, gkv_ref[...]).astype(BF16)
    kn = _dot(ckvn, w_k_ref[...])
    v = _dot(ckvn, w_v_ref[...])
    lane = lax.broadcasted_iota(jnp.int32, v.shape, 1)
    spare = jnp.where((lane // LANES) % 2 == 0, _spare_lane_base(0), _spare_lane_base(1))
    v_ref[...] = jnp.where(lane % LANES == spare, 1.0, v).astype(BF16)
    k_rope = kr * ck_ref[...] + pltpu.roll(kr, LANES - MLA_ROPE, axis=1) * sk_ref[...]
    for h in range(MLA_HEADS):
        s = slice(h * LANES, (h + 1) * LANES)
        k_ref[:, s] = (kn[:, s] + k_rope).astype(BF16)


def _even_in(h, g, w_in, gq, w_uq, gkv, w_k, w_v, tables, seq):
    n, d = h.shape
    row = lambda t: (t, 0)
    pos = lambda t: (t % (seq // TM_PROJ), 0)
    tab = pl.BlockSpec((TM_PROJ, LANES), pos)
    outs = [(_QW, "q"), (_QW, "k"), (_QW, "v"), (3 * MOBA_HEADS * HEAD_DIM, "moba")]
    return pl.pallas_call(
        _even_in_kernel,
        grid=(n // TM_PROJ,),
        in_specs=[pl.BlockSpec((TM_PROJ, d), row), _whole((1, d)), _whole(w_in.shape),
                  _whole((1, MLA_Q_LORA)), _whole(w_uq.shape), _whole((1, MLA_KV_LORA)),
                  _whole(w_k.shape), _whole(w_v.shape), tab, tab, tab, tab],
        out_specs=[pl.BlockSpec((TM_PROJ, w), row) for w, _ in outs],
        out_shape=[jax.ShapeDtypeStruct((n, w), BF16) for w, _ in outs],
        compiler_params=_params("arbitrary"),
        name="even_in",
    )(h, g.reshape(1, d), w_in, gq.reshape(1, -1), w_uq, gkv.reshape(1, -1), w_k, w_v, *tables)


def _two_block_attention(qs, k_of, v_ofs, r0, add_nears):
    mid, end = r0 + TQ, r0 + 2 * TQ
    stack = lambda parts: jnp.concatenate(parts, axis=0)
    s_main_all = _dot_nt(stack(list(qs)), k_of(0, mid))
    s_last_all = _dot_nt(stack([q[TQ:] for q in qs]), k_of(mid, end))
    accs = []
    for h, add_near in enumerate(add_nears):
        s_main = s_main_all[h * 2 * TQ:(h + 1) * 2 * TQ]
        a_near = [add_near(s_main[:TQ, r0 - TQ:r0], 1)] if r0 > 0 else []
        a_far = [s_main[:TQ, :r0 - TQ]] if r0 > TQ else []
        s_a = jnp.concatenate(a_far + a_near + [add_near(s_main[:TQ, r0:], 0)], axis=1)
        b_far = [s_main[TQ:, :r0]] if r0 > 0 else []
        s_b = jnp.concatenate(b_far + [add_near(s_main[TQ:, r0:], 1),
                                       add_near(s_last_all[h * TQ:(h + 1) * TQ], 0)], axis=1)
        p_a = jnp.exp2(s_a - jnp.max(s_a, axis=-1, keepdims=True)).astype(BF16)
        p_b = jnp.exp2(s_b - jnp.max(s_b, axis=-1, keepdims=True)).astype(BF16)
        acc = _dot(stack([p_a, p_b[:, :mid]]), v_ofs[h](0, mid))
        accs.append(stack([acc[:TQ], acc[TQ:] + _dot(p_b[:, mid:], v_ofs[h](mid, end))]))
    return accs


def _mla_kernel(q_ref, k_ref, v_ref, o_ref):
    s_len = q_ref.shape[1]
    row = lax.broadcasted_iota(jnp.int32, (TQ, TQ), 0)
    col = lax.broadcasted_iota(jnp.int32, (TQ, TQ), 1)
    causal = col <= row
    add_near = lambda s, d: jnp.where(causal, s, -jnp.inf) if d == 0 else s
    zeros = jnp.zeros((2 * TQ, LANES), BF16)
    for r0 in range(0, s_len, 2 * TQ):
        rows = slice(r0, r0 + 2 * TQ)
        qs = [jnp.concatenate([q_ref[0, rows, :LANES], zeros], axis=1),
              jnp.concatenate([zeros, q_ref[0, rows, LANES:]], axis=1)]
        v_ofs = [lambda a, b, lanes=slice(hh * LANES, (hh + 1) * LANES): v_ref[0, a:b, lanes]
                 for hh in range(2)]
        accs = _two_block_attention(qs, lambda a, b: k_ref[0, a:b, :], v_ofs, r0, [add_near] * 2)
        out = _normalized(accs[0], 0) + _normalized(accs[1], 1)
        o_ref[0, rows, :] = out.astype(o_ref.dtype)


def _mla_attention(q, k, v):
    b, s, _ = q.shape
    pairs = MLA_HEADS // 2
    qkv = pl.BlockSpec((1, s, 2 * LANES), lambda bi, p: (bi, 0, p))
    return pl.pallas_call(
        _mla_kernel,
        grid=(b, pairs),
        in_specs=[qkv, qkv, qkv],
        out_specs=pl.BlockSpec((1, s, LANES), lambda bi, p: (bi, 0, p)),
        out_shape=jax.ShapeDtypeStruct((b, s, MLA_HEADS * MLA_V), BF16),
        compiler_params=_params("arbitrary", "arbitrary"),
        name="mla_attention",
    )(q, k, v)


_SOFTPLUS2_CLAMP = 64.0


def _softplus2(z):
    return jnp.maximum(z, jnp.log2(1.0 + jnp.exp2(jnp.minimum(z, _SOFTPLUS2_CLAMP))))


SB_UNDERFLOW = 160.0
SB_NEAR_BLOCKS = 2
SB_PROBE_STEPS = 8


def _sb_kernel(q_ref, k_ref, v_ref, o_ref, qm_scr, vm_scr, acc_scr, carry_scr, limit_scr, plan_scr):
    s_len = q_ref.shape[1]
    n_blk = s_len // TQ
    def prepare():
        q, v = q_ref[0], v_ref[0]
        for hh in range(2):
            lane_mask = jnp.where(_head_lanes((1, LANES), hh), 1.0, 0.0).astype(BF16)
            qm_scr[hh] = q * lane_mask
            vm_scr[hh] = v * lane_mask
        acc_scr[...] = jnp.zeros(acc_scr.shape, F32)

    def set_limits():
        q, k = q_ref[0], k_ref[0]
        feat = lax.broadcasted_iota(jnp.int32, (LANES, LANES), 0) // HEAD_DIM
        lane = lax.broadcasted_iota(jnp.int32, (LANES, LANES), 1) // HEAD_DIM
        same_head = jnp.where(feat == lane, 1.0, 0.0).astype(BF16)
        q_max2 = jnp.max(_dot(q * q, same_head), axis=0, keepdims=True)
        k_max2 = jnp.max(_dot(k * k, same_head), axis=0, keepdims=True)
        z_bound = jnp.sqrt(q_max2 * k_max2) * 1.02
        for hh in range(2):
            bound = jnp.max(jnp.where(_head_lanes((1, LANES), hh), z_bound, 0.0), axis=1, keepdims=True)
            limit_scr[hh] = jnp.broadcast_to(bound + SB_UNDERFLOW, limit_scr.shape[1:])

    row = lax.broadcasted_iota(jnp.int32, (TQ, TQ), 0)
    col = lax.broadcasted_iota(jnp.int32, (TQ, TQ), 1)
    strict = col < row
    tri = jnp.where(row >= col, 1.0, 0.0).astype(BF16)

    def tile_rows(j, r0, r1):
        keys = slice(j * TQ, (j + 1) * TQ)
        rows = slice(r0 * TQ, r1 * TQ)
        m = (r1 - r0) * TQ
        first = TQ if r0 == j else 0
        z = _dot_nt(jnp.concatenate([qm_scr[0, rows, :], qm_scr[1, rows, :]], axis=0),
                    k_ref[0, keys, :])
        sp = _softplus2(z)
        if first:
            parts = []
            for base in (0, m):
                parts.append(jnp.where(strict, sp[base:base + TQ], 0.0))
                if m > TQ:
                    parts.append(sp[base + TQ:base + m])
            sp = jnp.concatenate(parts, axis=0)
        c = _dot(sp.astype(BF16), tri)
        tot = jnp.broadcast_to(c[:, 0:1], (2 * m, LANES))
        w = []
        for h, base in enumerate((0, m)):
            w_h = []
            if first:
                d = slice(base, base + TQ)
                w_h.append(jnp.where(strict, jnp.exp2(z[d] - c[d]), 0.0))
                carry_scr[h, keys, :] = tot[d]
            if m > first:
                below = slice(r0 * TQ + first, r1 * TQ)
                o = slice(base + first, base + m)
                carry = carry_scr[h, below, :]
                w_h.append(jnp.exp2(z[o] - c[o] - jnp.concatenate([carry, carry], axis=1)))
                carry_scr[h, below, :] = carry + tot[o]
            w.append(w_h[0] if len(w_h) == 1 else jnp.concatenate(w_h, axis=0))
        vv = jnp.concatenate([vm_scr[0, keys, :], vm_scr[1, keys, :]], axis=0)
        acc_scr[rows, :] += _dot(jnp.concatenate(w, axis=1).astype(BF16), vv)

    def near_tiles(n_near):
        for j in range(n_blk - 1, -1, -1):
            tile_rows(j, j, min(j + n_near, n_blk))

    def live(r0, r1):
        slack = [jnp.min(carry_scr[h, r0 * TQ:r1 * TQ, :], axis=0, keepdims=True) - limit_scr[h, 0:1, :]
                 for h in range(2)]
        return jnp.min(jnp.minimum(slack[0], slack[1])) < 0.0

    def diagonal(dist):
        for i in range(dist, n_blk):
            tile_rows(i - dist, i, i + 1)

    step = pl.program_id(0) * pl.num_programs(1) + pl.program_id(1)

    @pl.when(step == 0)
    def _():
        plan_scr[0] = 0
        plan_scr[1] = 0

    def at_distance(dist, vote):
        flags = [live(i, i + 1) for i in range(dist, n_blk)]
        n_live = sum(f.astype(jnp.int32) for f in flags)
        plan_scr[2] = n_live
        if vote:
            plan_scr[1] += (n_live >= 2).astype(jnp.int32)

        @pl.when(n_live >= 2)
        def _():
            diagonal(dist)

        @pl.when(n_live == 1)
        def _():
            for i, f in zip(range(dist, n_blk), flags):
                pl.when(f)(functools.partial(tile_rows, i - dist, i, i + 1))

    wide = plan_scr[0] == 1

    @pl.when(jnp.logical_not(wide))
    def _():
        prepare()
        set_limits()
        near_tiles(SB_NEAR_BLOCKS)
        at_distance(SB_NEAR_BLOCKS, vote=True)

    @pl.when(wide)
    def _():
        prepare()
        set_limits()
        near_tiles(SB_NEAR_BLOCKS + 1)
        plan_scr[2] = 1

    for dist in range(SB_NEAR_BLOCKS + 1, n_blk):
        pl.when(plan_scr[2] > 0)(functools.partial(at_distance, dist, vote=False))

    @pl.when(step == SB_PROBE_STEPS - 1)
    def _():
        plan_scr[0] = (5 * plan_scr[1] >= 3 * SB_PROBE_STEPS).astype(jnp.int32)

    o_ref[0] = acc_scr[...].astype(o_ref.dtype)


def _sb_attention(qkv):
    b, s, _ = qkv.shape
    pairs = SB_HEADS // 2
    return pl.pallas_call(
        _sb_kernel,
        grid=(b, pairs),
        in_specs=[pl.BlockSpec((1, s, LANES), lambda bi, p: (bi, 0, p)),
                  pl.BlockSpec((1, s, LANES), lambda bi, p: (bi, 0, pairs + p)),
                  pl.BlockSpec((1, s, LANES), lambda bi, p: (bi, 0, 2 * pairs + p))],
        out_specs=pl.BlockSpec((1, s, LANES), lambda bi, p: (bi, 0, p)),
        out_shape=jax.ShapeDtypeStruct((b, s, SB_HEADS * HEAD_DIM), BF16),
        scratch_shapes=[pltpu.VMEM((2, s, LANES), BF16), pltpu.VMEM((2, s, LANES), BF16),
                        pltpu.VMEM((s, LANES), F32), pltpu.VMEM((2, s, LANES), F32),
                        pltpu.VMEM((2, 8, LANES), F32), pltpu.SMEM((3,), jnp.int32)],
        compiler_params=_params("arbitrary", "arbitrary"),
        name="sb_attention",
    )(qkv, qkv, qkv)


def _t5_bucket_np(rel):
    max_exact = NUM_BUCKETS // 2
    rel = np.maximum(rel, 0)
    large = max_exact + (np.log(np.maximum(rel, max_exact) / max_exact)
                         / math.log(MAX_DISTANCE / max_exact) * (NUM_BUCKETS - max_exact)).astype(np.int64)
    return np.where(rel < max_exact, rel, np.minimum(large, NUM_BUCKETS - 1)).astype(np.int32)


def _bias_kernel(bucket_ref, rel_bias_ref, o_ref):
    h = pl.program_id(0)
    for plane in range(2):
        bk = bucket_ref[plane]
        acc = jnp.full(bk.shape, -jnp.inf, F32)
        for b in range(NUM_BUCKETS):
            acc = jnp.where(bk == b, rel_bias_ref[b, h] * LOG2_E, acc)
        o_ref[0, plane] = acc
    o_ref[0, 2] = jnp.full((TQ, TQ), rel_bias_ref[NUM_BUCKETS - 1, h] * LOG2_E, F32)


def _moba_bias_tables(rel_bias):
    off = np.arange(TQ)
    rel_own = off[:, None] - off[None, :]
    own = np.where(rel_own >= 0, _t5_bucket_np(rel_own), -1)
    adj = _t5_bucket_np(rel_own + MOBA_BLOCK)
    buckets = jnp.asarray(np.stack([own, adj]).astype(np.int32))
    return pl.pallas_call(
        _bias_kernel,
        grid=(MOBA_HEADS,),
        in_specs=[_whole(buckets.shape), pl.BlockSpec(memory_space=pltpu.SMEM)],
        out_specs=pl.BlockSpec((1, 3, TQ, TQ), lambda h: (h, 0, 0, 0)),
        out_shape=jax.ShapeDtypeStruct((MOBA_HEADS, 3, TQ, TQ), F32),
        compiler_params=_params("arbitrary"),
        name="moba_bias_tables",
    )(buckets, rel_bias)


_N_BLK_LANES = 8


def _moba_constants(nb):
    assert nb <= _N_BLK_LANES
    diff = np.zeros((LANES, LANES), np.float32)
    count = np.zeros((LANES, LANES), np.float32)
    for hh in range(2):
        for a in range(nb):
            for b in range(nb):
                if a != b:
                    pair = hh * HEAD_DIM + a * 8 + b
                    diff[pair, a] += 1.0
                    diff[pair, b] -= 1.0
                    for part in range(2):
                        count[pair, (2 * hh + part) * _N_BLK_LANES + b] = 1.0
    return jnp.asarray(diff, BF16), jnp.asarray(count, BF16)


_T_FAR_HI, _T_FAR_LO, _T_FUTURE, _T_UNSEL, _N_HEAD_TABLES = 0, 1, 2, 3, 4


def _moba_fill_tables(head_tab, key_tab, gate_tab, s_len):
    lane = lax.broadcasted_iota(jnp.int32, (s_len, LANES), 1)
    row_blk = lax.broadcasted_iota(jnp.int32, (s_len, LANES), 0) // MOBA_BLOCK
    group, blk = lane // _N_BLK_LANES, lane % _N_BLK_LANES
    key_tab[...] = jnp.where((group < 4) & (blk == row_blk), 1.0, 0.0).astype(BF16)
    far = blk < row_blk - 1
    for hh in range(2):
        in_hi, in_lo = group == 2 * hh, group == 2 * hh + 1
        store = lambda t, x: head_tab.__setitem__((hh, t), x.astype(BF16))
        store(_T_FAR_HI, jnp.where(in_hi & far, 1.0, 0.0))
        store(_T_FAR_LO, jnp.where(in_lo & far, 1.0, 0.0))
        store(_T_FUTURE, jnp.where(in_hi & (blk > row_blk), MASK_PENALTY, 0.0))
        store(_T_UNSEL, jnp.where(in_hi & (blk != row_blk), MASK_PENALTY, 0.0))
    pair_a, pair_b = (lane % HEAD_DIM) // 8, lane % 8
    valid = pair_a < row_blk
    gate_tab[0] = jnp.where(valid, 1.0, 0.0)
    gate_tab[1] = jnp.where(valid & (pair_a < pair_b), 1.0, 0.0)


def _moba_kernel(q_ref, k_ref, v_ref, bias_ref, diff_ref, count_ref, o_ref,
                 qaug_scr, vm_scr, head_tab, key_tab, gate_tab, *, nb):
    s_len = q_ref.shape[1]

    @pl.when((pl.program_id(0) == 0) & (pl.program_id(1) == 0))
    def _():
        _moba_fill_tables(head_tab, key_tab, gate_tab, s_len)

    q, k, v = q_ref[0], k_ref[0], v_ref[0]
    kbar = jnp.mean(k.astype(F32).reshape(nb, MOBA_BLOCK, LANES), axis=1)
    kbar = jnp.concatenate([kbar, jnp.zeros((LANES - nb, LANES), F32)], axis=0)
    p0 = kbar.astype(BF16)
    r1 = kbar - p0.astype(F32)
    p1 = r1.astype(BF16)
    p2 = (r1 - p1.astype(F32)).astype(BF16)
    d = diff_ref[...]
    kdiff = _dot(d, p0) + _dot(d, p1) + _dot(d, p2)
    pair_head = lax.broadcasted_iota(jnp.int32, (LANES, LANES), 0) // HEAD_DIM
    feat_head = lax.broadcasted_iota(jnp.int32, (LANES, LANES), 1) // HEAD_DIM
    kdiff = jnp.where(pair_head == feat_head, kdiff, 0.0).astype(BF16)

    g = _dot_nt(q, kdiff)
    beats = jnp.where(g > 0.0, gate_tab[0], jnp.where(g == 0.0, gate_tab[1], 0.0))
    rank = _dot(beats.astype(BF16), count_ref[...]).astype(BF16)
    top = rank < float(MOBA_TOPK)

    lane = lax.broadcasted_iota(jnp.int32, (1, LANES), 1)
    for hh in range(2):
        lane_mask = jnp.where(_head_lanes((1, LANES), hh), 1.0, 0.0).astype(BF16)
        ones_lane = jnp.where(lane == _spare_lane_base(hh), 1.0, 0.0).astype(BF16)
        vm_scr[hh] = v * lane_mask + ones_lane
        far = bias_ref[hh, 2, 0:1, 0:LANES]
        far_hi = far.astype(BF16)
        far_lo = (far - far_hi.astype(F32)).astype(BF16)
        chosen_val = head_tab[hh, _T_FAR_HI] * far_hi + head_tab[hh, _T_FAR_LO] * far_lo + head_tab[hh, _T_FUTURE]
        qaug_scr[hh, :, :LANES] = q * lane_mask
        qaug_scr[hh, :, LANES:] = jnp.where(top, chosen_val, head_tab[hh, _T_UNSEL])

    keys_of = lambda a, b: jnp.concatenate([k_ref[0, a:b, :], key_tab[a:b, :]], axis=1)
    for r0 in range(0, s_len, 2 * TQ):
        rows = slice(r0, r0 + 2 * TQ)
        accs = _two_block_attention(
            [qaug_scr[hh, rows, :] for hh in range(2)], keys_of,
            [lambda a, b, hh=hh: vm_scr[hh, a:b, :] for hh in range(2)], r0,
            [lambda s, d, hh=hh: s + bias_ref[hh, d] for hh in range(2)])
        out = _normalized(accs[0], 0) + _normalized(accs[1], 1)
        o_ref[0, rows, :] = out.astype(o_ref.dtype)


def _moba_attention(qkv, bias_tables):
    b, s, _ = qkv.shape
    nb = s // MOBA_BLOCK
    pairs = MOBA_HEADS // 2
    diff, count = _moba_constants(nb)
    return pl.pallas_call(
        functools.partial(_moba_kernel, nb=nb),
        grid=(pairs, b),
        in_specs=[pl.BlockSpec((1, s, LANES), lambda p, bi: (bi, 0, p)),
                  pl.BlockSpec((1, s, LANES), lambda p, bi: (bi, 0, pairs + p)),
                  pl.BlockSpec((1, s, LANES), lambda p, bi: (bi, 0, 2 * pairs + p)),
                  pl.BlockSpec((2, 3, TQ, TQ), lambda p, bi: (p, 0, 0, 0)),
                  _whole(diff.shape), _whole(count.shape)],
        out_specs=pl.BlockSpec((1, s, LANES), lambda p, bi: (bi, 0, p)),
        out_shape=jax.ShapeDtypeStruct((b, s, MOBA_HEADS * HEAD_DIM), BF16),
        scratch_shapes=[pltpu.VMEM((2, s, 2 * LANES), BF16), pltpu.VMEM((2, s, LANES), BF16),
                        pltpu.VMEM((2, _N_HEAD_TABLES, s, LANES), BF16),
                        pltpu.VMEM((s, LANES), BF16), pltpu.VMEM((2, s, LANES), F32)],
        compiler_params=_params("arbitrary", "arbitrary"),
        name="moba_attention",
    )(qkv, qkv, qkv, bias_tables, diff, count)


def _mix_ffn_kernel(*refs, n_in):
    a_refs = refs[:n_in]
    h_ref, w_out_ref, gmix_ref, gpre_ref, wgu_ref, wd_ref, gpost_ref, o_ref, act_scr = refs[n_in:]
    mixed, k0 = None, 0
    for a_ref in a_refs:
        k1 = k0 + a_ref.shape[1]
        part = _dot(a_ref[...], w_out_ref[k0:k1, :])
        mixed, k0 = part if mixed is None else mixed + part, k1
    x = h_ref[...] + _rms(mixed, gmix_ref[...])
    u = _rms(x, gpre_ref[...]).astype(BF16)
    for c in range(0, D_FF, FF_CHUNK):
        gate = _dot(u, wgu_ref[:, c:c + FF_CHUNK])
        up = _dot(u, wgu_ref[:, D_FF + c:D_FF + c + FF_CHUNK])
        act_scr[:, c:c + FF_CHUNK] = (gate * jax.nn.sigmoid(gate) * up).astype(BF16)
    f = _dot(act_scr[...], wd_ref[...])
    o_ref[...] = x + _rms(f, gpost_ref[...])


def _mix_ffn(acts, h, w_out, gmix, gpre, wgu, wd, gpost):
    n, d = h.shape
    row = lambda t: (t, 0)
    vec = _whole((1, d))
    return pl.pallas_call(
        functools.partial(_mix_ffn_kernel, n_in=len(acts)),
        grid=(n // TM,),
        in_specs=([pl.BlockSpec((TM, a.shape[1]), row) for a in acts]
                  + [pl.BlockSpec((TM, d), row), _one_layer(w_out[0].shape, w_out[1]), vec, vec,
                     _one_layer(wgu[0].shape, wgu[1]), _one_layer(wd[0].shape, wd[1]), vec]),
        out_specs=pl.BlockSpec((TM, d), row),
        out_shape=jax.ShapeDtypeStruct((n, d), F32),
        scratch_shapes=[pltpu.VMEM((TM, D_FF), BF16)],
        compiler_params=_params("arbitrary"),
        name="mix_ffn",
    )(*acts, h, w_out[0], gmix.reshape(1, d), gpre.reshape(1, d), wgu[0], wd[0], gpost.reshape(1, d))


def _rot_half_cols(w):
    half = w.shape[-1] // 2
    return jnp.concatenate([-w[..., half:], w[..., :half]], axis=-1)


def _prep_even_weights(w_in, w_uq, w_ukv):
    d = w_in.shape[0]
    o1, o2, o3 = MLA_Q_LORA, MLA_Q_LORA + MLA_KV_LORA, MLA_Q_LORA + MLA_KV_LORA + MLA_ROPE
    hw = MOBA_HEADS * HEAD_DIM
    z = lambda *shape: jnp.zeros(shape, F32)
    kr = w_in[:, o2:o3]
    w_in_ext = jnp.concatenate(
        [w_in[:, :o1], z(d, MLA_NOPE), kr, _rot_half_cols(kr), w_in[:, o1:o2],
         w_in[:, o3:o3 + hw] * (HEAD_DIM ** -0.5), w_in[:, o3 + hw:]], axis=1).astype(BF16)

    wq = w_uq.reshape(MLA_Q_LORA, MLA_HEADS, MLA_NOPE + MLA_ROPE)
    nope, rp = wq[..., :MLA_NOPE], wq[..., MLA_NOPE:]
    assert MLA_NOPE + 2 * MLA_ROPE == LANES
    w_uq_ext = jnp.concatenate([nope, rp, _rot_half_cols(rp)], axis=-1).reshape(MLA_Q_LORA, _QW).astype(BF16)

    wkv = w_ukv.reshape(MLA_KV_LORA, MLA_HEADS, MLA_NOPE + MLA_V)
    k_nope, v = wkv[..., :MLA_NOPE], wkv[..., MLA_NOPE:]
    w_k = jnp.concatenate([k_nope, jnp.zeros_like(k_nope)], axis=-1).reshape(MLA_KV_LORA, _QW).astype(BF16)
    v = v.reshape(MLA_KV_LORA, MLA_HEADS // 2, 2, MLA_V)
    zero = jnp.zeros_like(v[:, :, 0])
    w_v = jnp.stack([jnp.concatenate([v[:, :, 0], zero], axis=-1),
                     jnp.concatenate([zero, v[:, :, 1]], axis=-1)], axis=2)
    w_v = w_v.reshape(MLA_KV_LORA, _QW).astype(BF16)
    return w_in_ext, w_uq_ext, w_k, w_v


def _rope_tables(seq):
    half = MLA_ROPE // 2
    inv_freq = ROPE_THETA ** (-jnp.arange(half, dtype=F32) / half)
    ang = jnp.arange(seq).astype(F32)[:, None] * inv_freq[None, :]
    cos, sin = jnp.cos(ang), jnp.sin(ang)
    ones, zeros = jnp.ones((seq, MLA_NOPE), F32), jnp.zeros((seq, MLA_NOPE), F32)
    tail = jnp.zeros((seq, LANES - MLA_NOPE - MLA_ROPE), F32)
    scale = (MLA_NOPE + MLA_ROPE) ** -0.5 * LOG2_E
    cq = jnp.concatenate([ones, cos, cos, tail], axis=1) * scale
    sq = jnp.concatenate([zeros, sin, sin, tail], axis=1) * scale
    ck = jnp.concatenate([zeros, cos, cos, tail], axis=1)
    sk = jnp.concatenate([zeros, sin, sin, tail], axis=1)
    return cq, sq, ck, sk


def kernel(x, mix_pre_g, mix_post_g, ffn_pre_g, ffn_post_g, ab_w_in, mla_q_norm_g, mla_w_uq,
           mla_kv_norm_g, mla_w_ukv, ab_w_out, rel_bias, sb_w_qkv, sb_w_out, ffn_w_gate_up, ffn_w_down):
    b, s, d = x.shape
    n = b * s
    h = x.reshape(n, d)
    tables = _rope_tables(s)
    bias_tables = _moba_bias_tables(rel_bias)
    ab_w_out, sb_w_qkv, sb_w_out, ffn_w_gate_up, ffn_w_down = (
        w.astype(BF16) for w in (ab_w_out, sb_w_qkv, sb_w_out, ffn_w_gate_up, ffn_w_down))
    for layer in range(DEPTH):
        if layer % 2 == 0:
            e = layer // 2
            w_in, w_uq, w_k, w_v = _prep_even_weights(ab_w_in[e], mla_w_uq[e], mla_w_ukv[e])
            q, k, v, moba_qkv = _even_in(h, mix_pre_g[layer], w_in, mla_q_norm_g[e], w_uq,
                                         mla_kv_norm_g[e], w_k, w_v, tables, s)
            mla_out = _mla_attention(q.reshape(b, s, -1), k.reshape(b, s, -1), v.reshape(b, s, -1))
            moba_out = _moba_attention(moba_qkv.reshape(b, s, -1), bias_tables)
            mixed, w_out = [mla_out.reshape(n, -1), moba_out.reshape(n, -1)], (ab_w_out, e)
        else:
            o = layer // 2
            qkv = _norm_matmul(h, mix_pre_g[layer], sb_w_qkv, o,
                               SB_HEADS * HEAD_DIM, HEAD_DIM ** -0.5 * LOG2_E)
            mixed, w_out = [_sb_attention(qkv.reshape(b, s, -1)).reshape(n, -1)], (sb_w_out, o)
        h = _mix_ffn(mixed, h, w_out, mix_post_g[layer], ffn_pre_g[layer],
                     (ffn_w_gate_up, layer), (ffn_w_down, layer), ffn_post_g[layer])
    return h.reshape(b, s, d)
```

```python
import functools
import math

import numpy as np
import jax
import jax.numpy as jnp
from jax import lax
from jax.experimental import pallas as pl
from jax.experimental.pallas import tpu as pltpu

D_MODEL = 1024
DEPTH = 4
HEAD_DIM = 64
MLA_HEADS = 8
MLA_NOPE = 64
MLA_ROPE = 32
MLA_V = 64
MLA_Q_LORA = 384
MLA_KV_LORA = 256
ROPE_THETA = 10000.0
MOBA_HEADS = 8
MOBA_BLOCK = 256
MOBA_TOPK = 3
SB_HEADS = 16
NUM_BUCKETS = 32
MAX_DISTANCE = 128
RMS_EPS = 1e-6
D_FF = 2816

LANES = 128
TQ = 256
TM = 1024
TM_PROJ = 1024
FF_CHUNK = 256
MASK_PENALTY = -1e30
LOG2_E = 1.0 / math.log(2.0)
VMEM_LIMIT = 58 * 1024 * 1024

F32 = jnp.float32
BF16 = jnp.bfloat16


def _rms(x, g):
    return x * lax.rsqrt(jnp.mean(x * x, axis=-1, keepdims=True) + RMS_EPS) * g


def _dot(a, b):
    return jnp.dot(a, b, preferred_element_type=F32)


def _dot_nt(a, b):
    return lax.dot_general(a, b, (((1,), (1,)), ((), ())), preferred_element_type=F32)


def _params(*semantics):
    return pltpu.CompilerParams(dimension_semantics=semantics, vmem_limit_bytes=VMEM_LIMIT)


def _whole(shape):
    nd = len(shape)
    return pl.BlockSpec(shape, lambda *_: (0,) * nd, pipeline_mode=pl.Buffered(1))


def _one_layer(stacked_shape, layer):
    return pl.BlockSpec((None,) + tuple(stacked_shape[1:]), lambda *_: (layer, 0, 0),
                        pipeline_mode=pl.Buffered(1))


def _norm_matmul_kernel(h_ref, g_ref, w_ref, o_ref, *, q_cols, q_scale):
    u = _rms(h_ref[...], g_ref[...]).astype(BF16)
    for c in range(0, o_ref.shape[-1], 512):
        y = _dot(u, w_ref[:, c:c + 512])
        if c < q_cols:
            y = y * q_scale
        o_ref[:, c:c + 512] = y.astype(o_ref.dtype)


def _norm_matmul(h, g, w_stack, layer, q_cols, q_scale):
    n, d = h.shape
    n_out = w_stack.shape[2]
    return pl.pallas_call(
        functools.partial(_norm_matmul_kernel, q_cols=q_cols, q_scale=q_scale),
        grid=(n // TM_PROJ,),
        in_specs=[pl.BlockSpec((TM_PROJ, d), lambda t: (t, 0)), _whole((1, d)),
                  _one_layer(w_stack.shape, layer)],
        out_specs=pl.BlockSpec((TM_PROJ, n_out), lambda t: (t, 0)),
        out_shape=jax.ShapeDtypeStruct((n, n_out), BF16),
        compiler_params=_params("arbitrary"),
        name="norm_matmul",
    )(h, g.reshape(1, d), w_stack)


def _head_lanes(shape, hh):
    lane = lax.broadcasted_iota(jnp.int32, shape, 1)
    return (lane >= hh * HEAD_DIM) & (lane < (hh + 1) * HEAD_DIM)


def _spare_lane_base(hh):
    return HEAD_DIM * (1 - hh)


def _normalized(acc, hh):
    base = _spare_lane_base(hh)
    return jnp.where(_head_lanes(acc.shape, hh), acc / acc[:, base:base + 1], 0.0)


_O_CQ = 0
_O_KR = MLA_Q_LORA
_O_CKV = _O_KR + LANES
_O_MOBA = _O_CKV + MLA_KV_LORA
_W_IN_EXT = _O_MOBA + 3 * MOBA_HEADS * HEAD_DIM
_QW = MLA_HEADS * LANES


def _even_in_kernel(h_ref, g_ref, w_in_ref, gq_ref, w_uq_ref, gkv_ref, w_k_ref, w_v_ref,
                    cq_ref, sq_ref, ck_ref, sk_ref, q_ref, k_ref, v_ref, moba_ref):
    u = _rms(h_ref[...], g_ref[...]).astype(BF16)
    cq_kr = _dot(u, w_in_ref[:, _O_CQ:_O_CKV])
    c_q, kr = cq_kr[:, :_O_KR], cq_kr[:, _O_KR:]
    c_kv = _dot(u, w_in_ref[:, _O_CKV:_O_MOBA])
    hw = MOBA_HEADS * HEAD_DIM
    for c in range(0, 3 * hw, hw):
        y = _dot(u, w_in_ref[:, _O_MOBA + c:_O_MOBA + c + hw])
        if c == 0:
            y = y * LOG2_E
        moba_ref[:, c:c + hw] = y.astype(BF16)

    cqn = _rms(c_q, gq_ref[...]).astype(BF16)
    qa = _dot(cqn, w_uq_ref[...])
    qb = pltpu.roll(qa, _QW - MLA_ROPE, axis=1)
    cq, sq = cq_ref[...], sq_ref[...]
    for h in range(MLA_HEADS):
        s = slice(h * LANES, (h + 1) * LANES)
        q_ref[:, s] = (qa[:, s] * cq + qb[:, s] * sq).astype(BF16)

    ckvn = _rms(c_kv, gkv_ref[...]).astype(BF16)
    kn = _dot(ckvn, w_k_ref[...])
    v = _dot(ckvn, w_v_ref[...])
    lane = lax.broadcasted_iota(jnp.int32, v.shape, 1)
    spare = jnp.where((lane // LANES) % 2 == 0, _spare_lane_base(0), _spare_lane_base(1))
    v_ref[...] = jnp.where(lane % LANES == spare, 1.0, v).astype(BF16)
    k_rope = kr * ck_ref[...] + pltpu.roll(kr, LANES - MLA_ROPE, axis=1) * sk_ref[...]
    for h in range(MLA_HEADS):
        s = slice(h * LANES, (h + 1) * LANES)
        k_ref[:, s] = (kn[:, s] + k_rope).astype(BF16)


def _even_in(h, g, w_in, gq, w_uq, gkv, w_k, w_v, tables, seq):
    n, d = h.shape
    row = lambda t: (t, 0)
    pos = lambda t: (t % (seq // TM_PROJ), 0)
    tab = pl.BlockSpec((TM_PROJ, LANES), pos)
    outs = [(_QW, "q"), (_QW, "k"), (_QW, "v"), (3 * MOBA_HEADS * HEAD_DIM, "moba")]
    return pl.pallas_call(
        _even_in_kernel,
        grid=(n // TM_PROJ,),
        in_specs=[pl.BlockSpec((TM_PROJ, d), row), _whole((1, d)), _whole(w_in.shape),
                  _whole((1, MLA_Q_LORA)), _whole(w_uq.shape), _whole((1, MLA_KV_LORA)),
                  _whole(w_k.shape), _whole(w_v.shape), tab, tab, tab, tab],
        out_specs=[pl.BlockSpec((TM_PROJ, w), row) for w, _ in outs],
        out_shape=[jax.ShapeDtypeStruct((n, w), BF16) for w, _ in outs],
        compiler_params=_params("arbitrary"),
        name="even_in",
    )(h, g.reshape(1, d), w_in, gq.reshape(1, -1), w_uq, gkv.reshape(1, -1), w_k, w_v, *tables)


def _two_block_attention(qs, k_of, v_ofs, r0, add_nears):
    mid, end = r0 + TQ, r0 + 2 * TQ
    stack = lambda parts: jnp.concatenate(parts, axis=0)
    s_main_all = _dot_nt(stack(list(qs)), k_of(0, mid))
    s_last_all = _dot_nt(stack([q[TQ:] for q in qs]), k_of(mid, end))
    accs = []
    for h, add_near in enumerate(add_nears):
        s_main = s_main_all[h * 2 * TQ:(h + 1) * 2 * TQ]
        a_near = [add_near(s_main[:TQ, r0 - TQ:r0], 1)] if r0 > 0 else []
        a_far = [s_main[:TQ, :r0 - TQ]] if r0 > TQ else []
        s_a = jnp.concatenate(a_far + a_near + [add_near(s_main[:TQ, r0:], 0)], axis=1)
        b_far = [s_main[TQ:, :r0]] if r0 > 0 else []
        s_b = jnp.concatenate(b_far + [add_near(s_main[TQ:, r0:], 1),
                                       add_near(s_last_all[h * TQ:(h + 1) * TQ], 0)], axis=1)
        p_a = jnp.exp2(s_a - jnp.max(s_a, axis=-1, keepdims=True)).astype(BF16)
        p_b = jnp.exp2(s_b - jnp.max(s_b, axis=-1, keepdims=True)).astype(BF16)
        acc = _dot(stack([p_a, p_b[:, :mid]]), v_ofs[h](0, mid))
        accs.append(stack([acc[:TQ], acc[TQ:] + _dot(p_b[:, mid:], v_ofs[h](mid, end))]))
    return accs


def _mla_kernel(q_ref, k_ref, v_ref, o_ref):
    s_len = q_ref.shape[1]
    row = lax.broadcasted_iota(jnp.int32, (TQ, TQ), 0)
    col = lax.broadcasted_iota(jnp.int32, (TQ, TQ), 1)
    causal = col <= row
    add_near = lambda s, d: jnp.where(causal, s, -jnp.inf) if d == 0 else s
    zeros = jnp.zeros((2 * TQ, LANES), BF16)
    for r0 in range(0, s_len, 2 * TQ):
        rows = slice(r0, r0 + 2 * TQ)
        qs = [jnp.concatenate([q_ref[0, rows, :LANES], zeros], axis=1),
              jnp.concatenate([zeros, q_ref[0, rows, LANES:]], axis=1)]
        v_ofs = [lambda a, b, lanes=slice(hh * LANES, (hh + 1) * LANES): v_ref[0, a:b, lanes]
                 for hh in range(2)]
        accs = _two_block_attention(qs, lambda a, b: k_ref[0, a:b, :], v_ofs, r0, [add_near] * 2)
        out = _normalized(accs[0], 0) + _normalized(accs[1], 1)
        o_ref[0, rows, :] = out.astype(o_ref.dtype)


def _mla_attention(q, k, v):
    b, s, _ = q.shape
    pairs = MLA_HEADS // 2
    qkv = pl.BlockSpec((1, s, 2 * LANES), lambda bi, p: (bi, 0, p))
    return pl.pallas_call(
        _mla_kernel,
        grid=(b, pairs),
        in_specs=[qkv, qkv, qkv],
        out_specs=pl.BlockSpec((1, s, LANES), lambda bi, p: (bi, 0, p)),
        out_shape=jax.ShapeDtypeStruct((b, s, MLA_HEADS * MLA_V), BF16),
        compiler_params=_params("arbitrary", "arbitrary"),
        name="mla_attention",
    )(q, k, v)


_SOFTPLUS2_CLAMP = 64.0


def _softplus2(z):
    return jnp.maximum(z, jnp.log2(1.0 + jnp.exp2(jnp.minimum(z, _SOFTPLUS2_CLAMP))))


SB_UNDERFLOW = 160.0
SB_NEAR_BLOCKS = 2
SB_PROBE_STEPS = 8


def _sb_kernel(q_ref, k_ref, v_ref, o_ref, qm_scr, vm_scr, acc_scr, carry_scr, limit_scr, plan_scr):
    s_len = q_ref.shape[1]
    n_blk = s_len // TQ
    def prepare():
        q, v = q_ref[0], v_ref[0]
        for hh in range(2):
            lane_mask = jnp.where(_head_lanes((1, LANES), hh), 1.0, 0.0).astype(BF16)
            qm_scr[hh] = q * lane_mask
            vm_scr[hh] = v * lane_mask
        acc_scr[...] = jnp.zeros(acc_scr.shape, F32)

    def set_limits():
        q, k = q_ref[0], k_ref[0]
        feat = lax.broadcasted_iota(jnp.int32, (LANES, LANES), 0) // HEAD_DIM
        lane = lax.broadcasted_iota(jnp.int32, (LANES, LANES), 1) // HEAD_DIM
        same_head = jnp.where(feat == lane, 1.0, 0.0).astype(BF16)
        q_max2 = jnp.max(_dot(q * q, same_head), axis=0, keepdims=True)
        k_max2 = jnp.max(_dot(k * k, same_head), axis=0, keepdims=True)
        z_bound = jnp.sqrt(q_max2 * k_max2) * 1.02
        for hh in range(2):
            bound = jnp.max(jnp.where(_head_lanes((1, LANES), hh), z_bound, 0.0), axis=1, keepdims=True)
            limit_scr[hh] = jnp.broadcast_to(bound + SB_UNDERFLOW, limit_scr.shape[1:])

    row = lax.broadcasted_iota(jnp.int32, (TQ, TQ), 0)
    col = lax.broadcasted_iota(jnp.int32, (TQ, TQ), 1)
    strict = col < row
    tri = jnp.where(row >= col, 1.0, 0.0).astype(BF16)

    def tile_rows(j, r0, r1):
        keys = slice(j * TQ, (j + 1) * TQ)
        rows = slice(r0 * TQ, r1 * TQ)
        m = (r1 - r0) * TQ
        first = TQ if r0 == j else 0
        z = _dot_nt(jnp.concatenate([qm_scr[0, rows, :], qm_scr[1, rows, :]], axis=0),
                    k_ref[0, keys, :])
        sp = _softplus2(z)
        if first:
            parts = []
            for base in (0, m):
                parts.append(jnp.where(strict, sp[base:base + TQ], 0.0))
                if m > TQ:
                    parts.append(sp[base + TQ:base + m])
            sp = jnp.concatenate(parts, axis=0)
        c = _dot(sp.astype(BF16), tri)
        tot = jnp.broadcast_to(c[:, 0:1], (2 * m, LANES))
        w = []
        for h, base in enumerate((0, m)):
            w_h = []
            if first:
                d = slice(base, base + TQ)
                w_h.append(jnp.where(strict, jnp.exp2(z[d] - c[d]), 0.0))
                carry_scr[h, keys, :] = tot[d]
            if m > first:
                below = slice(r0 * TQ + first, r1 * TQ)
                o = slice(base + first, base + m)
                carry = carry_scr[h, below, :]
                w_h.append(jnp.exp2(z[o] - c[o] - jnp.concatenate([carry, carry], axis=1)))
                carry_scr[h, below, :] = carry + tot[o]
            w.append(w_h[0] if len(w_h) == 1 else jnp.concatenate(w_h, axis=0))
        vv = jnp.concatenate([vm_scr[0, keys, :], vm_scr[1, keys, :]], axis=0)
        acc_scr[rows, :] += _dot(jnp.concatenate(w, axis=1).astype(BF16), vv)

    def near_tiles(n_near):
        for j in range(n_blk - 1, -1, -1):
            tile_rows(j, j, min(j + n_near, n_blk))

    def live(r0, r1):
        slack = [jnp.min(carry_scr[h, r0 * TQ:r1 * TQ, :], axis=0, keepdims=True) - limit_scr[h, 0:1, :]
                 for h in range(2)]
        return jnp.min(jnp.minimum(slack[0], slack[1])) < 0.0

    def diagonal(dist):
        for i in range(dist, n_blk):
            tile_rows(i - dist, i, i + 1)

    step = pl.program_id(0) * pl.num_programs(1) + pl.program_id(1)

    @pl.when(step == 0)
    def _():
        plan_scr[0] = 0
        plan_scr[1] = 0

    wide = plan_scr[0] == 1

    @pl.when(jnp.logical_not(wide))
    def _():
        prepare()
        set_limits()
        near_tiles(SB_NEAR_BLOCKS)
        flags = [live(i, i + 1) for i in range(SB_NEAR_BLOCKS, n_blk)]
        n_live = sum(f.astype(jnp.int32) for f in flags)
        plan_scr[1] += (n_live >= 2).astype(jnp.int32)
        plan_scr[2] = n_live

        @pl.when(n_live >= 2)
        def _():
            diagonal(SB_NEAR_BLOCKS)

        @pl.when(n_live == 1)
        def _():
            for i, f in zip(range(SB_NEAR_BLOCKS, n_blk), flags):
                pl.when(f)(functools.partial(tile_rows, i - SB_NEAR_BLOCKS, i, i + 1))

    @pl.when(wide)
    def _():
        prepare()
        set_limits()
        near_tiles(SB_NEAR_BLOCKS + 1)
        plan_scr[2] = 1

    def farther(dist):
        @pl.when(live(dist, n_blk))
        def _():
            diagonal(dist)
            if dist + 1 < n_blk:
                farther(dist + 1)

    if SB_NEAR_BLOCKS + 1 < n_blk:
        pl.when(plan_scr[2] > 0)(functools.partial(farther, SB_NEAR_BLOCKS + 1))

    @pl.when(step == SB_PROBE_STEPS - 1)
    def _():
        plan_scr[0] = (5 * plan_scr[1] >= 3 * SB_PROBE_STEPS).astype(jnp.int32)

    o_ref[0] = acc_scr[...].astype(o_ref.dtype)


def _sb_attention(qkv):
    b, s, _ = qkv.shape
    pairs = SB_HEADS // 2
    return pl.pallas_call(
        _sb_kernel,
        grid=(b, pairs),
        in_specs=[pl.BlockSpec((1, s, LANES), lambda bi, p: (bi, 0, p)),
                  pl.BlockSpec((1, s, LANES), lambda bi, p: (bi, 0, pairs + p)),
                  pl.BlockSpec((1, s, LANES), lambda bi, p: (bi, 0, 2 * pairs + p))],
        out_specs=pl.BlockSpec((1, s, LANES), lambda bi, p: (bi, 0, p)),
        out_shape=jax.ShapeDtypeStruct((b, s, SB_HEADS * HEAD_DIM), BF16),
        scratch_shapes=[pltpu.VMEM((2, s, LANES), BF16), pltpu.VMEM((2, s, LANES), BF16),
                        pltpu.VMEM((s, LANES), F32), pltpu.VMEM((2, s, LANES), F32),
                        pltpu.VMEM((2, 8, LANES), F32), pltpu.SMEM((3,), jnp.int32)],
        compiler_params=_params("arbitrary", "arbitrary"),
        name="sb_attention",
    )(qkv, qkv, qkv)


def _t5_bucket_np(rel):
    max_exact = NUM_BUCKETS // 2
    rel = np.maximum(rel, 0)
    large = max_exact + (np.log(np.maximum(rel, max_exact) / max_exact)
                         / math.log(MAX_DISTANCE / max_exact) * (NUM_BUCKETS - max_exact)).astype(np.int64)
    return np.where(rel < max_exact, rel, np.minimum(large, NUM_BUCKETS - 1)).astype(np.int32)


def _bias_kernel(bucket_ref, rel_bias_ref, o_ref):
    h = pl.program_id(0)
    for plane in range(2):
        bk = bucket_ref[plane]
        acc = jnp.full(bk.shape, -jnp.inf, F32)
        for b in range(NUM_BUCKETS):
            acc = jnp.where(bk == b, rel_bias_ref[b, h] * LOG2_E, acc)
        o_ref[0, plane] = acc
    o_ref[0, 2] = jnp.full((TQ, TQ), rel_bias_ref[NUM_BUCKETS - 1, h] * LOG2_E, F32)


def _moba_bias_tables(rel_bias):
    off = np.arange(TQ)
    rel_own = off[:, None] - off[None, :]
    own = np.where(rel_own >= 0, _t5_bucket_np(rel_own), -1)
    adj = _t5_bucket_np(rel_own + MOBA_BLOCK)
    buckets = jnp.asarray(np.stack([own, adj]).astype(np.int32))
    return pl.pallas_call(
        _bias_kernel,
        grid=(MOBA_HEADS,),
        in_specs=[_whole(buckets.shape), pl.BlockSpec(memory_space=pltpu.SMEM)],
        out_specs=pl.BlockSpec((1, 3, TQ, TQ), lambda h: (h, 0, 0, 0)),
        out_shape=jax.ShapeDtypeStruct((MOBA_HEADS, 3, TQ, TQ), F32),
        compiler_params=_params("arbitrary"),
        name="moba_bias_tables",
    )(buckets, rel_bias)


_N_BLK_LANES = 8


def _moba_constants(nb):
    assert nb <= _N_BLK_LANES
    diff = np.zeros((LANES, LANES), np.float32)
    count = np.zeros((LANES, LANES), np.float32)
    for hh in range(2):
        for a in range(nb):
            for b in range(nb):
                if a != b:
                    pair = hh * HEAD_DIM + a * 8 + b
                    diff[pair, a] += 1.0
                    diff[pair, b] -= 1.0
                    for part in range(2):
                        count[pair, (2 * hh + part) * _N_BLK_LANES + b] = 1.0
    return jnp.asarray(diff, BF16), jnp.asarray(count, BF16)


_T_FAR_HI, _T_FAR_LO, _T_FUTURE, _T_UNSEL, _N_HEAD_TABLES = 0, 1, 2, 3, 4


def _moba_fill_tables(head_tab, key_tab, gate_tab, s_len):
    lane = lax.broadcasted_iota(jnp.int32, (s_len, LANES), 1)
    row_blk = lax.broadcasted_iota(jnp.int32, (s_len, LANES), 0) // MOBA_BLOCK
    group, blk = lane // _N_BLK_LANES, lane % _N_BLK_LANES
    key_tab[...] = jnp.where((group < 4) & (blk == row_blk), 1.0, 0.0).astype(BF16)
    far = blk < row_blk - 1
    for hh in range(2):
        in_hi, in_lo = group == 2 * hh, group == 2 * hh + 1
        store = lambda t, x: head_tab.__setitem__((hh, t), x.astype(BF16))
        store(_T_FAR_HI, jnp.where(in_hi & far, 1.0, 0.0))
        store(_T_FAR_LO, jnp.where(in_lo & far, 1.0, 0.0))
        store(_T_FUTURE, jnp.where(in_hi & (blk > row_blk), MASK_PENALTY, 0.0))
        store(_T_UNSEL, jnp.where(in_hi & (blk != row_blk), MASK_PENALTY, 0.0))
    pair_a, pair_b = (lane % HEAD_DIM) // 8, lane % 8
    valid = pair_a < row_blk
    gate_tab[0] = jnp.where(valid, 1.0, 0.0)
    gate_tab[1] = jnp.where(valid & (pair_a < pair_b), 1.0, 0.0)


def _moba_kernel(q_ref, k_ref, v_ref, bias_ref, diff_ref, count_ref, o_ref,
                 qaug_scr, vm_scr, head_tab, key_tab, gate_tab, *, nb):
    s_len = q_ref.shape[1]

    @pl.when((pl.program_id(0) == 0) & (pl.program_id(1) == 0))
    def _():
        _moba_fill_tables(head_tab, key_tab, gate_tab, s_len)

    q, k, v = q_ref[0], k_ref[0], v_ref[0]
    kbar = jnp.mean(k.astype(F32).reshape(nb, MOBA_BLOCK, LANES), axis=1)
    kbar = jnp.concatenate([kbar, jnp.zeros((LANES - nb, LANES), F32)], axis=0)
    p0 = kbar.astype(BF16)
    r1 = kbar - p0.astype(F32)
    p1 = r1.astype(BF16)
    p2 = (r1 - p1.astype(F32)).astype(BF16)
    d = diff_ref[...]
    kdiff = _dot(d, p0) + _dot(d, p1) + _dot(d, p2)
    pair_head = lax.broadcasted_iota(jnp.int32, (LANES, LANES), 0) // HEAD_DIM
    feat_head = lax.broadcasted_iota(jnp.int32, (LANES, LANES), 1) // HEAD_DIM
    kdiff = jnp.where(pair_head == feat_head, kdiff, 0.0).astype(BF16)

    g = _dot_nt(q, kdiff)
    beats = jnp.where(g > 0.0, gate_tab[0], jnp.where(g == 0.0, gate_tab[1], 0.0))
    rank = _dot(beats.astype(BF16), count_ref[...]).astype(BF16)
    top = rank < float(MOBA_TOPK)

    lane = lax.broadcasted_iota(jnp.int32, (1, LANES), 1)
    for hh in range(2):
        lane_mask = jnp.where(_head_lanes((1, LANES), hh), 1.0, 0.0).astype(BF16)
        ones_lane = jnp.where(lane == _spare_lane_base(hh), 1.0, 0.0).astype(BF16)
        vm_scr[hh] = v * lane_mask + ones_lane
        far = bias_ref[hh, 2, 0:1, 0:LANES]
        far_hi = far.astype(BF16)
        far_lo = (far - far_hi.astype(F32)).astype(BF16)
        chosen_val = head_tab[hh, _T_FAR_HI] * far_hi + head_tab[hh, _T_FAR_LO] * far_lo + head_tab[hh, _T_FUTURE]
        qaug_scr[hh, :, :LANES] = q * lane_mask
        qaug_scr[hh, :, LANES:] = jnp.where(top, chosen_val, head_tab[hh, _T_UNSEL])

    keys_of = lambda a, b: jnp.concatenate([k_ref[0, a:b, :], key_tab[a:b, :]], axis=1)
    for r0 in range(0, s_len, 2 * TQ):
        rows = slice(r0, r0 + 2 * TQ)
        accs = _two_block_attention(
            [qaug_scr[hh, rows, :] for hh in range(2)], keys_of,
            [lambda a, b, hh=hh: vm_scr[hh, a:b, :] for hh in range(2)], r0,
            [lambda s, d, hh=hh: s + bias_ref[hh, d] for hh in range(2)])
        out = _normalized(accs[0], 0) + _normalized(accs[1], 1)
        o_ref[0, rows, :] = out.astype(o_ref.dtype)


def _moba_attention(qkv, bias_tables):
    b, s, _ = qkv.shape
    nb = s // MOBA_BLOCK
    pairs = MOBA_HEADS // 2
    diff, count = _moba_constants(nb)
    return pl.pallas_call(
        functools.partial(_moba_kernel, nb=nb),
        grid=(pairs, b),
        in_specs=[pl.BlockSpec((1, s, LANES), lambda p, bi: (bi, 0, p)),
                  pl.BlockSpec((1, s, LANES), lambda p, bi: (bi, 0, pairs + p)),
                  pl.BlockSpec((1, s, LANES), lambda p, bi: (bi, 0, 2 * pairs + p)),
                  pl.BlockSpec((2, 3, TQ, TQ), lambda p, bi: (p, 0, 0, 0)),
                  _whole(diff.shape), _whole(count.shape)],
        out_specs=pl.BlockSpec((1, s, LANES), lambda p, bi: (bi, 0, p)),
        out_shape=jax.ShapeDtypeStruct((b, s, MOBA_HEADS * HEAD_DIM), BF16),
        scratch_shapes=[pltpu.VMEM((2, s, 2 * LANES), BF16), pltpu.VMEM((2, s, LANES), BF16),
                        pltpu.VMEM((2, _N_HEAD_TABLES, s, LANES), BF16),
                        pltpu.VMEM((s, LANES), BF16), pltpu.VMEM((2, s, LANES), F32)],
        compiler_params=_params("arbitrary", "arbitrary"),
        name="moba_attention",
    )(qkv, qkv, qkv, bias_tables, diff, count)


def _mix_ffn_kernel(*refs, n_in):
    a_refs = refs[:n_in]
    h_ref, w_out_ref, gmix_ref, gpre_ref, wgu_ref, wd_ref, gpost_ref, o_ref, act_scr = refs[n_in:]
    mixed, k0 = None, 0
    for a_ref in a_refs:
        k1 = k0 + a_ref.shape[1]
        part = _dot(a_ref[...], w_out_ref[k0:k1, :])
        mixed, k0 = part if mixed is None else mixed + part, k1
    x = h_ref[...] + _rms(mixed, gmix_ref[...])
    u = _rms(x, gpre_ref[...]).astype(BF16)
    for c in range(0, D_FF, FF_CHUNK):
        gate = _dot(u, wgu_ref[:, c:c + FF_CHUNK])
        up = _dot(u, wgu_ref[:, D_FF + c:D_FF + c + FF_CHUNK])
        act_scr[:, c:c + FF_CHUNK] = (gate * jax.nn.sigmoid(gate) * up).astype(BF16)
    f = _dot(act_scr[...], wd_ref[...])
    o_ref[...] = x + _rms(f, gpost_ref[...])


def _mix_ffn(acts, h, w_out, gmix, gpre, wgu, wd, gpost):
    n, d = h.shape
    row = lambda t: (t, 0)
    vec = _whole((1, d))
    return pl.pallas_call(
        functools.partial(_mix_ffn_kernel, n_in=len(acts)),
        grid=(n // TM,),
        in_specs=([pl.BlockSpec((TM, a.shape[1]), row) for a in acts]
                  + [pl.BlockSpec((TM, d), row), _one_layer(w_out[0].shape, w_out[1]), vec, vec,
                     _one_layer(wgu[0].shape, wgu[1]), _one_layer(wd[0].shape, wd[1]), vec]),
        out_specs=pl.BlockSpec((TM, d), row),
        out_shape=jax.ShapeDtypeStruct((n, d), F32),
        scratch_shapes=[pltpu.VMEM((TM, D_FF), BF16)],
        compiler_params=_params("arbitrary"),
        name="mix_ffn",
    )(*acts, h, w_out[0], gmix.reshape(1, d), gpre.reshape(1, d), wgu[0], wd[0], gpost.reshape(1, d))


def _rot_half_cols(w):
    half = w.shape[-1] // 2
    return jnp.concatenate([-w[..., half:], w[..., :half]], axis=-1)


def _prep_even_weights(w_in, w_uq, w_ukv):
    d = w_in.shape[0]
    o1, o2, o3 = MLA_Q_LORA, MLA_Q_LORA + MLA_KV_LORA, MLA_Q_LORA + MLA_KV_LORA + MLA_ROPE
    hw = MOBA_HEADS * HEAD_DIM
    z = lambda *shape: jnp.zeros(shape, F32)
    kr = w_in[:, o2:o3]
    w_in_ext = jnp.concatenate(
        [w_in[:, :o1], z(d, MLA_NOPE), kr, _rot_half_cols(kr), w_in[:, o1:o2],
         w_in[:, o3:o3 + hw] * (HEAD_DIM ** -0.5), w_in[:, o3 + hw:]], axis=1).astype(BF16)

    wq = w_uq.reshape(MLA_Q_LORA, MLA_HEADS, MLA_NOPE + MLA_ROPE)
    nope, rp = wq[..., :MLA_NOPE], wq[..., MLA_NOPE:]
    assert MLA_NOPE + 2 * MLA_ROPE == LANES
    w_uq_ext = jnp.concatenate([nope, rp, _rot_half_cols(rp)], axis=-1).reshape(MLA_Q_LORA, _QW).astype(BF16)

    wkv = w_ukv.reshape(MLA_KV_LORA, MLA_HEADS, MLA_NOPE + MLA_V)
    k_nope, v = wkv[..., :MLA_NOPE], wkv[..., MLA_NOPE:]
    w_k = jnp.concatenate([k_nope, jnp.zeros_like(k_nope)], axis=-1).reshape(MLA_KV_LORA, _QW).astype(BF16)
    v = v.reshape(MLA_KV_LORA, MLA_HEADS // 2, 2, MLA_V)
    zero = jnp.zeros_like(v[:, :, 0])
    w_v = jnp.stack([jnp.concatenate([v[:, :, 0], zero], axis=-1),
                     jnp.concatenate([zero, v[:, :, 1]], axis=-1)], axis=2)
    w_v = w_v.reshape(MLA_KV_LORA, _QW).astype(BF16)
    return w_in_ext, w_uq_ext, w_k, w_v


def _rope_tables(seq):
    half = MLA_ROPE // 2
    inv_freq = ROPE_THETA ** (-jnp.arange(half, dtype=F32) / half)
    ang = jnp.arange(seq).astype(F32)[:, None] * inv_freq[None, :]
    cos, sin = jnp.cos(ang), jnp.sin(ang)
    ones, zeros = jnp.ones((seq, MLA_NOPE), F32), jnp.zeros((seq, MLA_NOPE), F32)
    tail = jnp.zeros((seq, LANES - MLA_NOPE - MLA_ROPE), F32)
    scale = (MLA_NOPE + MLA_ROPE) ** -0.5 * LOG2_E
    cq = jnp.concatenate([ones, cos, cos, tail], axis=1) * scale
    sq = jnp.concatenate([zeros, sin, sin, tail], axis=1) * scale
    ck = jnp.concatenate([zeros, cos, cos, tail], axis=1)
    sk = jnp.concatenate([zeros, sin, sin, tail], axis=1)
    return cq, sq, ck, sk


def kernel(x, mix_pre_g, mix_post_g, ffn_pre_g, ffn_post_g, ab_w_in, mla_q_norm_g, mla_w_uq,
           mla_kv_norm_g, mla_w_ukv, ab_w_out, rel_bias, sb_w_qkv, sb_w_out, ffn_w_gate_up, ffn_w_down):
    b, s, d = x.shape
    n = b * s
    h = x.reshape(n, d)
    tables = _rope_tables(s)
    bias_tables = _moba_bias_tables(rel_bias)
    ab_w_out, sb_w_qkv, sb_w_out, ffn_w_gate_up, ffn_w_down = (
        w.astype(BF16) for w in (ab_w_out, sb_w_qkv, sb_w_out, ffn_w_gate_up, ffn_w_down))
    for layer in range(DEPTH):
        if layer % 2 == 0:
            e = layer // 2
            w_in, w_uq, w_k, w_v = _prep_even_weights(ab_w_in[e], mla_w_uq[e], mla_w_ukv[e])
            q, k, v, moba_qkv = _even_in(h, mix_pre_g[layer], w_in, mla_q_norm_g[e], w_uq,
                                         mla_kv_norm_g[e], w_k, w_v, tables, s)
            mla_out = _mla_attention(q.reshape(b, s, -1), k.reshape(b, s, -1), v.reshape(b, s, -1))
            moba_out = _moba_attention(moba_qkv.reshape(b, s, -1), bias_tables)
            mixed, w_out = [mla_out.reshape(n, -1), moba_out.reshape(n, -1)], (ab_w_out, e)
        else:
            o = layer // 2
            qkv = _norm_matmul(h, mix_pre_g[layer], sb_w_qkv, o,
                               SB_HEADS * HEAD_DIM, HEAD_DIM ** -0.5 * LOG2_E)
            mixed, w_out = [_sb_attention(qkv.reshape(b, s, -1)).reshape(n, -1)], (sb_w_out, o)
        h = _mix_ffn(mixed, h, w_out, mix_post_g[layer], ffn_pre_g[layer],
                     (ffn_w_gate_up, layer), (ffn_w_down, layer), ffn_post_g[layer])
    return h.reshape(b, s, d)
```

```python
import functools
import math

import numpy as np
import jax
import jax.numpy as jnp
from jax import lax
from jax.experimental import pallas as pl
from jax.experimental.pallas import tpu as pltpu

D_MODEL = 1024
DEPTH = 4
HEAD_DIM = 64
MLA_HEADS = 8
MLA_NOPE = 64
MLA_ROPE = 32
MLA_V = 64
MLA_Q_LORA = 384
MLA_KV_LORA = 256
ROPE_THETA = 10000.0
MOBA_HEADS = 8
MOBA_BLOCK = 256
MOBA_TOPK = 3
SB_HEADS = 16
NUM_BUCKETS = 32
MAX_DISTANCE = 128
RMS_EPS = 1e-6
D_FF = 2816

LANES = 128
TQ = 256
TM = 1024
TM_PROJ = 1024
FF_CHUNK = 256
MASK_PENALTY = -1e30
LOG2_E = 1.0 / math.log(2.0)
VMEM_LIMIT = 58 * 1024 * 1024

F32 = jnp.float32
BF16 = jnp.bfloat16


def _rms(x, g):
    return x * lax.rsqrt(jnp.mean(x * x, axis=-1, keepdims=True) + RMS_EPS) * g


def _dot(a, b):
    return jnp.dot(a, b, preferred_element_type=F32)


def _dot_nt(a, b):
    return lax.dot_general(a, b, (((1,), (1,)), ((), ())), preferred_element_type=F32)


def _params(*semantics):
    return pltpu.CompilerParams(dimension_semantics=semantics, vmem_limit_bytes=VMEM_LIMIT)


def _whole(shape):
    nd = len(shape)
    return pl.BlockSpec(shape, lambda *_: (0,) * nd, pipeline_mode=pl.Buffered(1))


def _one_layer(stacked_shape, layer):
    return pl.BlockSpec((None,) + tuple(stacked_shape[1:]), lambda *_: (layer, 0, 0),
                        pipeline_mode=pl.Buffered(1))


def _norm_matmul_kernel(h_ref, g_ref, w_ref, o_ref, *, q_cols, q_scale):
    u = _rms(h_ref[...], g_ref[...]).astype(BF16)
    for c in range(0, o_ref.shape[-1], 512):
        y = _dot(u, w_ref[:, c:c + 512])
        if c < q_cols:
            y = y * q_scale
        o_ref[:, c:c + 512] = y.astype(o_ref.dtype)


def _norm_matmul(h, g, w_stack, layer, q_cols, q_scale):
    n, d = h.shape
    n_out = w_stack.shape[2]
    return pl.pallas_call(
        functools.partial(_norm_matmul_kernel, q_cols=q_cols, q_scale=q_scale),
        grid=(n // TM_PROJ,),
        in_specs=[pl.BlockSpec((TM_PROJ, d), lambda t: (t, 0)), _whole((1, d)),
                  _one_layer(w_stack.shape, layer)],
        out_specs=pl.BlockSpec((TM_PROJ, n_out), lambda t: (t, 0)),
        out_shape=jax.ShapeDtypeStruct((n, n_out), BF16),
        compiler_params=_params("arbitrary"),
        name="norm_matmul",
    )(h, g.reshape(1, d), w_stack)


def _head_lanes(shape, hh):
    lane = lax.broadcasted_iota(jnp.int32, shape, 1)
    return (lane >= hh * HEAD_DIM) & (lane < (hh + 1) * HEAD_DIM)


def _spare_lane_base(hh):
    return HEAD_DIM * (1 - hh)


def _normalized(acc, hh):
    base = _spare_lane_base(hh)
    return jnp.where(_head_lanes(acc.shape, hh), acc / acc[:, base:base + 1], 0.0)


_O_CQ = 0
_O_KR = MLA_Q_LORA
_O_CKV = _O_KR + LANES
_O_MOBA = _O_CKV + MLA_KV_LORA
_W_IN_EXT = _O_MOBA + 3 * MOBA_HEADS * HEAD_DIM
_QW = MLA_HEADS * LANES


def _even_in_kernel(h_ref, g_ref, w_in_ref, gq_ref, w_uq_ref, gkv_ref, w_k_ref, w_v_ref,
                    cq_ref, sq_ref, ck_ref, sk_ref, q_ref, k_ref, v_ref, moba_ref):
    u = _rms(h_ref[...], g_ref[...]).astype(BF16)
    cq_kr = _dot(u, w_in_ref[:, _O_CQ:_O_CKV])
    c_q, kr = cq_kr[:, :_O_KR], cq_kr[:, _O_KR:]
    c_kv = _dot(u, w_in_ref[:, _O_CKV:_O_MOBA])
    hw = MOBA_HEADS * HEAD_DIM
    for c in range(0, 3 * hw, hw):
        y = _dot(u, w_in_ref[:, _O_MOBA + c:_O_MOBA + c + hw])
        if c == 0:
            y = y * LOG2_E
        moba_ref[:, c:c + hw] = y.astype(BF16)

    cqn = _rms(c_q, gq_ref[...]).astype(BF16)
    qa = _dot(cqn, w_uq_ref[...])
    qb = pltpu.roll(qa, _QW - MLA_ROPE, axis=1)
    cq, sq = cq_ref[...], sq_ref[...]
    for h in range(MLA_HEADS):
        s = slice(h * LANES, (h + 1) * LANES)
        q_ref[:, s] = (qa[:, s] * cq + qb[:, s] * sq).astype(BF16)

    ckvn = _rms(c_kv, gkv_ref[...]).astype(BF16)
    kn = _dot(ckvn, w_k_ref[...])
    v = _dot(ckvn, w_v_ref[...])
    lane = lax.broadcasted_iota(jnp.int32, v.shape, 1)
    spare = jnp.where((lane // LANES) % 2 == 0, _spare_lane_base(0), _spare_lane_base(1))
    v_ref[...] = jnp.where(lane % LANES == spare, 1.0, v).astype(BF16)
    k_rope = kr * ck_ref[...] + pltpu.roll(kr, LANES - MLA_ROPE, axis=1) * sk_ref[...]
    for h in range(MLA_HEADS):
        s = slice(h * LANES, (h + 1) * LANES)
        k_ref[:, s] = (kn[:, s] + k_rope).astype(BF16)


def _even_in(h, g, w_in, gq, w_uq, gkv, w_k, w_v, tables, seq):
    n, d = h.shape
    row = lambda t: (t, 0)
    pos = lambda t: (t % (seq // TM_PROJ), 0)
    tab = pl.BlockSpec((TM_PROJ, LANES), pos)
    outs = [(_QW, "q"), (_QW, "k"), (_QW, "v"), (3 * MOBA_HEADS * HEAD_DIM, "moba")]
    return pl.pallas_call(
        _even_in_kernel,
        grid=(n // TM_PROJ,),
        in_specs=[pl.BlockSpec((TM_PROJ, d), row), _whole((1, d)), _whole(w_in.shape),
                  _whole((1, MLA_Q_LORA)), _whole(w_uq.shape), _whole((1, MLA_KV_LORA)),
                  _whole(w_k.shape), _whole(w_v.shape), tab, tab, tab, tab],
        out_specs=[pl.BlockSpec((TM_PROJ, w), row) for w, _ in outs],
        out_shape=[jax.ShapeDtypeStruct((n, w), BF16) for w, _ in outs],
        compiler_params=_params("arbitrary"),
        name="even_in",
    )(h, g.reshape(1, d), w_in, gq.reshape(1, -1), w_uq, gkv.reshape(1, -1), w_k, w_v, *tables)


def _two_block_attention(qs, k_of, v_ofs, r0, add_nears):
    mid, end = r0 + TQ, r0 + 2 * TQ
    stack = lambda parts: jnp.concatenate(parts, axis=0)
    s_main_all = _dot_nt(stack(list(qs)), k_of(0, mid))
    s_last_all = _dot_nt(stack([q[TQ:] for q in qs]), k_of(mid, end))
    accs = []
    for h, add_near in enumerate(add_nears):
        s_main = s_main_all[h * 2 * TQ:(h + 1) * 2 * TQ]
        a_near = [add_near(s_main[:TQ, r0 - TQ:r0], 1)] if r0 > 0 else []
        a_far = [s_main[:TQ, :r0 - TQ]] if r0 > TQ else []
        s_a = jnp.concatenate(a_far + a_near + [add_near(s_main[:TQ, r0:], 0)], axis=1)
        b_far = [s_main[TQ:, :r0]] if r0 > 0 else []
        s_b = jnp.concatenate(b_far + [add_near(s_main[TQ:, r0:], 1),
                                       add_near(s_last_all[h * TQ:(h + 1) * TQ], 0)], axis=1)
        p_a = jnp.exp2(s_a - jnp.max(s_a, axis=-1, keepdims=True)).astype(BF16)
        p_b = jnp.exp2(s_b - jnp.max(s_b, axis=-1, keepdims=True)).astype(BF16)
        acc = _dot(stack([p_a, p_b[:, :mid]]), v_ofs[h](0, mid))
        accs.append(stack([acc[:TQ], acc[TQ:] + _dot(p_b[:, mid:], v_ofs[h](mid, end))]))
    return accs


def _mla_kernel(q_ref, k_ref, v_ref, o_ref):
    s_len = q_ref.shape[1]
    row = lax.broadcasted_iota(jnp.int32, (TQ, TQ), 0)
    col = lax.broadcasted_iota(jnp.int32, (TQ, TQ), 1)
    causal = col <= row
    add_near = lambda s, d: jnp.where(causal, s, -jnp.inf) if d == 0 else s
    zeros = jnp.zeros((2 * TQ, LANES), BF16)
    for r0 in range(0, s_len, 2 * TQ):
        rows = slice(r0, r0 + 2 * TQ)
        qs = [jnp.concatenate([q_ref[0, rows, :LANES], zeros], axis=1),
              jnp.concatenate([zeros, q_ref[0, rows, LANES:]], axis=1)]
        v_ofs = [lambda a, b, lanes=slice(hh * LANES, (hh + 1) * LANES): v_ref[0, a:b, lanes]
                 for hh in range(2)]
        accs = _two_block_attention(qs, lambda a, b: k_ref[0, a:b, :], v_ofs, r0, [add_near] * 2)
        out = _normalized(accs[0], 0) + _normalized(accs[1], 1)
        o_ref[0, rows, :] = out.astype(o_ref.dtype)


def _mla_attention(q, k, v):
    b, s, _ = q.shape
    pairs = MLA_HEADS // 2
    qkv = pl.BlockSpec((1, s, 2 * LANES), lambda bi, p: (bi, 0, p))
    return pl.pallas_call(
        _mla_kernel,
        grid=(b, pairs),
        in_specs=[qkv, qkv, qkv],
        out_specs=pl.BlockSpec((1, s, LANES), lambda bi, p: (bi, 0, p)),
        out_shape=jax.ShapeDtypeStruct((b, s, MLA_HEADS * MLA_V), BF16),
        compiler_params=_params("arbitrary", "arbitrary"),
        name="mla_attention",
    )(q, k, v)


_SOFTPLUS2_CLAMP = 64.0


def _softplus2(z):
    return jnp.maximum(z, jnp.log2(1.0 + jnp.exp2(jnp.minimum(z, _SOFTPLUS2_CLAMP))))


SB_UNDERFLOW = 160.0
SB_NEAR_BLOCKS = 2
SB_PROBE_STEPS = 8


def _sb_kernel(q_ref, k_ref, v_ref, o_ref, qm_scr, vm_scr, acc_scr, carry_scr, limit_scr, plan_scr):
    s_len = q_ref.shape[1]
    n_blk = s_len // TQ
    def prepare():
        q, v = q_ref[0], v_ref[0]
        for hh in range(2):
            lane_mask = jnp.where(_head_lanes((1, LANES), hh), 1.0, 0.0).astype(BF16)
            qm_scr[hh] = q * lane_mask
            vm_scr[hh] = v * lane_mask
        acc_scr[...] = jnp.zeros(acc_scr.shape, F32)

    def set_limits():
        q, k = q_ref[0], k_ref[0]
        feat = lax.broadcasted_iota(jnp.int32, (LANES, LANES), 0) // HEAD_DIM
        lane = lax.broadcasted_iota(jnp.int32, (LANES, LANES), 1) // HEAD_DIM
        same_head = jnp.where(feat == lane, 1.0, 0.0).astype(BF16)
        q_max2 = jnp.max(_dot(q * q, same_head), axis=0, keepdims=True)
        k_max2 = jnp.max(_dot(k * k, same_head), axis=0, keepdims=True)
        z_bound = jnp.sqrt(q_max2 * k_max2) * 1.02
        for hh in range(2):
            bound = jnp.max(jnp.where(_head_lanes((1, LANES), hh), z_bound, 0.0), axis=1, keepdims=True)
            limit_scr[hh] = jnp.broadcast_to(bound + SB_UNDERFLOW, limit_scr.shape[1:])

    row = lax.broadcasted_iota(jnp.int32, (TQ, TQ), 0)
    col = lax.broadcasted_iota(jnp.int32, (TQ, TQ), 1)
    strict = col < row
    tri = jnp.where(row >= col, 1.0, 0.0).astype(BF16)

    def tile_rows(j, r0, r1):
        keys = slice(j * TQ, (j + 1) * TQ)
        rows = slice(r0 * TQ, r1 * TQ)
        m = (r1 - r0) * TQ
        first = TQ if r0 == j else 0
        z = _dot_nt(jnp.concatenate([qm_scr[0, rows, :], qm_scr[1, rows, :]], axis=0),
                    k_ref[0, keys, :])
        half = TQ // 2
        blank = jnp.zeros((half, half), F32)

        def on_diagonal(fn, base):
            top = jnp.where(strict[:half, :half], fn(slice(base, base + half), slice(0, half)), 0.0)
            bottom = jnp.where(strict[half:], fn(slice(base + half, base + TQ), slice(0, TQ)), 0.0)
            return jnp.concatenate([jnp.concatenate([top, blank], axis=1), bottom], axis=0)

        if first:
            parts = []
            for base in (0, m):
                parts.append(on_diagonal(lambda r, l: _softplus2(z[r, l]), base))
                if m > TQ:
                    parts.append(_softplus2(z[base + TQ:base + m]))
            sp = jnp.concatenate(parts, axis=0)
        else:
            sp = _softplus2(z)
        c = _dot(sp.astype(BF16), tri)
        tot = jnp.broadcast_to(c[:, 0:1], (2 * m, LANES))
        w = []
        for h, base in enumerate((0, m)):
            w_h = []
            if first:
                d = slice(base, base + TQ)
                w_h.append(on_diagonal(lambda r, l: jnp.exp2(z[r, l] - c[r, l]), base))
                carry_scr[h, keys, :] = tot[d]
            if m > first:
                below = slice(r0 * TQ + first, r1 * TQ)
                o = slice(base + first, base + m)
                carry = carry_scr[h, below, :]
                w_h.append(jnp.exp2(z[o] - c[o] - jnp.concatenate([carry, carry], axis=1)))
                carry_scr[h, below, :] = carry + tot[o]
            w.append(w_h[0] if len(w_h) == 1 else jnp.concatenate(w_h, axis=0))
        vv = jnp.concatenate([vm_scr[0, keys, :], vm_scr[1, keys, :]], axis=0)
        acc_scr[rows, :] += _dot(jnp.concatenate(w, axis=1).astype(BF16), vv)

    def near_tiles(n_near):
        for j in range(n_blk - 1, -1, -1):
            tile_rows(j, j, min(j + n_near, n_blk))

    def live(r0, r1):
        slack = [jnp.min(carry_scr[h, r0 * TQ:r1 * TQ, :], axis=0, keepdims=True) - limit_scr[h, 0:1, :]
                 for h in range(2)]
        return jnp.min(jnp.minimum(slack[0], slack[1])) < 0.0

    def diagonal(dist):
        for i in range(dist, n_blk):
            tile_rows(i - dist, i, i + 1)

    step = pl.program_id(0) * pl.num_programs(1) + pl.program_id(1)

    @pl.when(step == 0)
    def _():
        plan_scr[0] = 0
        plan_scr[1] = 0

    wide = plan_scr[0] == 1

    @pl.when(jnp.logical_not(wide))
    def _():
        prepare()
        set_limits()
        near_tiles(SB_NEAR_BLOCKS)
        flags = [live(i, i + 1) for i in range(SB_NEAR_BLOCKS, n_blk)]
        n_live = sum(f.astype(jnp.int32) for f in flags)
        plan_scr[1] += (n_live >= 2).astype(jnp.int32)
        plan_scr[2] = n_live

        @pl.when(n_live >= 2)
        def _():
            diagonal(SB_NEAR_BLOCKS)

        @pl.when(n_live == 1)
        def _():
            for i, f in zip(range(SB_NEAR_BLOCKS, n_blk), flags):
                pl.when(f)(functools.partial(tile_rows, i - SB_NEAR_BLOCKS, i, i + 1))

    @pl.when(wide)
    def _():
        prepare()
        set_limits()
        near_tiles(SB_NEAR_BLOCKS + 1)
        plan_scr[2] = 1

    def farther(dist):
        @pl.when(live(dist, n_blk))
        def _():
            diagonal(dist)
            if dist + 1 < n_blk:
                farther(dist + 1)

    if SB_NEAR_BLOCKS + 1 < n_blk:
        pl.when(plan_scr[2] > 0)(functools.partial(farther, SB_NEAR_BLOCKS + 1))

    @pl.when(step == SB_PROBE_STEPS - 1)
    def _():
        plan_scr[0] = (5 * plan_scr[1] >= 3 * SB_PROBE_STEPS).astype(jnp.int32)

    o_ref[0] = acc_scr[...].astype(o_ref.dtype)


def _sb_attention(qkv):
    b, s, _ = qkv.shape
    pairs = SB_HEADS // 2
    return pl.pallas_call(
        _sb_kernel,
        grid=(b, pairs),
        in_specs=[pl.BlockSpec((1, s, LANES), lambda bi, p: (bi, 0, p)),
                  pl.BlockSpec((1, s, LANES), lambda bi, p: (bi, 0, pairs + p)),
                  pl.BlockSpec((1, s, LANES), lambda bi, p: (bi, 0, 2 * pairs + p))],
        out_specs=pl.BlockSpec((1, s, LANES), lambda bi, p: (bi, 0, p)),
        out_shape=jax.ShapeDtypeStruct((b, s, SB_HEADS * HEAD_DIM), BF16),
        scratch_shapes=[pltpu.VMEM((2, s, LANES), BF16), pltpu.VMEM((2, s, LANES), BF16),
                        pltpu.VMEM((s, LANES), F32), pltpu.VMEM((2, s, LANES), F32),
                        pltpu.VMEM((2, 8, LANES), F32), pltpu.SMEM((3,), jnp.int32)],
        compiler_params=_params("arbitrary", "arbitrary"),
        name="sb_attention",
    )(qkv, qkv, qkv)


def _t5_bucket_np(rel):
    max_exact = NUM_BUCKETS // 2
    rel = np.maximum(rel, 0)
    large = max_exact + (np.log(np.maximum(rel, max_exact) / max_exact)
                         / math.log(MAX_DISTANCE / max_exact) * (NUM_BUCKETS - max_exact)).astype(np.int64)
    return np.where(rel < max_exact, rel, np.minimum(large, NUM_BUCKETS - 1)).astype(np.int32)


def _bias_kernel(bucket_ref, rel_bias_ref, o_ref):
    h = pl.program_id(0)
    for plane in range(2):
        bk = bucket_ref[plane]
        acc = jnp.full(bk.shape, -jnp.inf, F32)
        for b in range(NUM_BUCKETS):
            acc = jnp.where(bk == b, rel_bias_ref[b, h] * LOG2_E, acc)
        o_ref[0, plane] = acc
    o_ref[0, 2] = jnp.full((TQ, TQ), rel_bias_ref[NUM_BUCKETS - 1, h] * LOG2_E, F32)


def _moba_bias_tables(rel_bias):
    off = np.arange(TQ)
    rel_own = off[:, None] - off[None, :]
    own = np.where(rel_own >= 0, _t5_bucket_np(rel_own), -1)
    adj = _t5_bucket_np(rel_own + MOBA_BLOCK)
    buckets = jnp.asarray(np.stack([own, adj]).astype(np.int32))
    return pl.pallas_call(
        _bias_kernel,
        grid=(MOBA_HEADS,),
        in_specs=[_whole(buckets.shape), pl.BlockSpec(memory_space=pltpu.SMEM)],
        out_specs=pl.BlockSpec((1, 3, TQ, TQ), lambda h: (h, 0, 0, 0)),
        out_shape=jax.ShapeDtypeStruct((MOBA_HEADS, 3, TQ, TQ), F32),
        compiler_params=_params("arbitrary"),
        name="moba_bias_tables",
    )(buckets, rel_bias)


_N_BLK_LANES = 8


def _moba_constants(nb):
    assert nb <= _N_BLK_LANES
    diff = np.zeros((LANES, LANES), np.float32)
    count = np.zeros((LANES, LANES), np.float32)
    for hh in range(2):
        for a in range(nb):
            for b in range(nb):
                if a != b:
                    pair = hh * HEAD_DIM + a * 8 + b
                    diff[pair, a] += 1.0
                    diff[pair, b] -= 1.0
                    for part in range(2):
                        count[pair, (2 * hh + part) * _N_BLK_LANES + b] = 1.0
    return jnp.asarray(diff, BF16), jnp.asarray(count, BF16)


_T_FAR_HI, _T_FAR_LO, _T_FUTURE, _T_UNSEL, _N_HEAD_TABLES = 0, 1, 2, 3, 4


def _moba_fill_tables(head_tab, key_tab, gate_tab, s_len):
    lane = lax.broadcasted_iota(jnp.int32, (s_len, LANES), 1)
    row_blk = lax.broadcasted_iota(jnp.int32, (s_len, LANES), 0) // MOBA_BLOCK
    group, blk = lane // _N_BLK_LANES, lane % _N_BLK_LANES
    key_tab[...] = jnp.where((group < 4) & (blk == row_blk), 1.0, 0.0).astype(BF16)
    far = blk < row_blk - 1
    for hh in range(2):
        in_hi, in_lo = group == 2 * hh, group == 2 * hh + 1
        store = lambda t, x: head_tab.__setitem__((hh, t), x.astype(BF16))
        store(_T_FAR_HI, jnp.where(in_hi & far, 1.0, 0.0))
        store(_T_FAR_LO, jnp.where(in_lo & far, 1.0, 0.0))
        store(_T_FUTURE, jnp.where(in_hi & (blk > row_blk), MASK_PENALTY, 0.0))
        store(_T_UNSEL, jnp.where(in_hi & (blk != row_blk), MASK_PENALTY, 0.0))
    pair_a, pair_b = (lane % HEAD_DIM) // 8, lane % 8
    valid = pair_a < row_blk
    gate_tab[0] = jnp.where(valid, 1.0, 0.0)
    gate_tab[1] = jnp.where(valid & (pair_a < pair_b), 1.0, 0.0)


def _moba_kernel(q_ref, k_ref, v_ref, bias_ref, diff_ref, count_ref, o_ref,
                 qaug_scr, vm_scr, head_tab, key_tab, gate_tab, *, nb):
    s_len = q_ref.shape[1]

    @pl.when((pl.program_id(0) == 0) & (pl.program_id(1) == 0))
    def _():
        _moba_fill_tables(head_tab, key_tab, gate_tab, s_len)

    q, k, v = q_ref[0], k_ref[0], v_ref[0]
    kbar = jnp.mean(k.astype(F32).reshape(nb, MOBA_BLOCK, LANES), axis=1)
    kbar = jnp.concatenate([kbar, jnp.zeros((LANES - nb, LANES), F32)], axis=0)
    p0 = kbar.astype(BF16)
    r1 = kbar - p0.astype(F32)
    p1 = r1.astype(BF16)
    p2 = (r1 - p1.astype(F32)).astype(BF16)
    d = diff_ref[...]
    kdiff = _dot(d, p0) + _dot(d, p1) + _dot(d, p2)
    pair_head = lax.broadcasted_iota(jnp.int32, (LANES, LANES), 0) // HEAD_DIM
    feat_head = lax.broadcasted_iota(jnp.int32, (LANES, LANES), 1) // HEAD_DIM
    kdiff = jnp.where(pair_head == feat_head, kdiff, 0.0).astype(BF16)

    g = _dot_nt(q, kdiff)
    beats = jnp.where(g > 0.0, gate_tab[0], jnp.where(g == 0.0, gate_tab[1], 0.0))
    rank = _dot(beats.astype(BF16), count_ref[...]).astype(BF16)
    top = rank < float(MOBA_TOPK)

    lane = lax.broadcasted_iota(jnp.int32, (1, LANES), 1)
    for hh in range(2):
        lane_mask = jnp.where(_head_lanes((1, LANES), hh), 1.0, 0.0).astype(BF16)
        ones_lane = jnp.where(lane == _spare_lane_base(hh), 1.0, 0.0).astype(BF16)
        vm_scr[hh] = v * lane_mask + ones_lane
        far = bias_ref[hh, 2, 0:1, 0:LANES]
        far_hi = far.astype(BF16)
        far_lo = (far - far_hi.astype(F32)).astype(BF16)
        chosen_val = head_tab[hh, _T_FAR_HI] * far_hi + head_tab[hh, _T_FAR_LO] * far_lo + head_tab[hh, _T_FUTURE]
        qaug_scr[hh, :, :LANES] = q * lane_mask
        qaug_scr[hh, :, LANES:] = jnp.where(top, chosen_val, head_tab[hh, _T_UNSEL])

    keys_of = lambda a, b: jnp.concatenate([k_ref[0, a:b, :], key_tab[a:b, :]], axis=1)
    for r0 in range(0, s_len, 2 * TQ):
        rows = slice(r0, r0 + 2 * TQ)
        accs = _two_block_attention(
            [qaug_scr[hh, rows, :] for hh in range(2)], keys_of,
            [lambda a, b, hh=hh: vm_scr[hh, a:b, :] for hh in range(2)], r0,
            [lambda s, d, hh=hh: s + bias_ref[hh, d] for hh in range(2)])
        out = _normalized(accs[0], 0) + _normalized(accs[1], 1)
        o_ref[0, rows, :] = out.astype(o_ref.dtype)


def _moba_attention(qkv, bias_tables):
    b, s, _ = qkv.shape
    nb = s // MOBA_BLOCK
    pairs = MOBA_HEADS // 2
    diff, count = _moba_constants(nb)
    return pl.pallas_call(
        functools.partial(_moba_kernel, nb=nb),
        grid=(pairs, b),
        in_specs=[pl.BlockSpec((1, s, LANES), lambda p, bi: (bi, 0, p)),
                  pl.BlockSpec((1, s, LANES), lambda p, bi: (bi, 0, pairs + p)),
                  pl.BlockSpec((1, s, LANES), lambda p, bi: (bi, 0, 2 * pairs + p)),
                  pl.BlockSpec((2, 3, TQ, TQ), lambda p, bi: (p, 0, 0, 0)),
                  _whole(diff.shape), _whole(count.shape)],
        out_specs=pl.BlockSpec((1, s, LANES), lambda p, bi: (bi, 0, p)),
        out_shape=jax.ShapeDtypeStruct((b, s, MOBA_HEADS * HEAD_DIM), BF16),
        scratch_shapes=[pltpu.VMEM((2, s, 2 * LANES), BF16), pltpu.VMEM((2, s, LANES), BF16),
                        pltpu.VMEM((2, _N_HEAD_TABLES, s, LANES), BF16),
                        pltpu.VMEM((s, LANES), BF16), pltpu.VMEM((2, s, LANES), F32)],
        compiler_params=_params("arbitrary", "arbitrary"),
        name="moba_attention",
    )(qkv, qkv, qkv, bias_tables, diff, count)


def _mix_ffn_kernel(*refs, n_in):
    a_refs = refs[:n_in]
    h_ref, w_out_ref, gmix_ref, gpre_ref, wgu_ref, wd_ref, gpost_ref, o_ref, act_scr = refs[n_in:]
    mixed, k0 = None, 0
    for a_ref in a_refs:
        k1 = k0 + a_ref.shape[1]
        part = _dot(a_ref[...], w_out_ref[k0:k1, :])
        mixed, k0 = part if mixed is None else mixed + part, k1
    x = h_ref[...] + _rms(mixed, gmix_ref[...])
    u = _rms(x, gpre_ref[...]).astype(BF16)
    for c in range(0, D_FF, FF_CHUNK):
        gate = _dot(u, wgu_ref[:, c:c + FF_CHUNK])
        up = _dot(u, wgu_ref[:, D_FF + c:D_FF + c + FF_CHUNK])
        act_scr[:, c:c + FF_CHUNK] = (gate * jax.nn.sigmoid(gate) * up).astype(BF16)
    f = _dot(act_scr[...], wd_ref[...])
    o_ref[...] = x + _rms(f, gpost_ref[...])


def _mix_ffn(acts, h, w_out, gmix, gpre, wgu, wd, gpost):
    n, d = h.shape
    row = lambda t: (t, 0)
    vec = _whole((1, d))
    return pl.pallas_call(
        functools.partial(_mix_ffn_kernel, n_in=len(acts)),
        grid=(n // TM,),
        in_specs=([pl.BlockSpec((TM, a.shape[1]), row) for a in acts]
                  + [pl.BlockSpec((TM, d), row), _one_layer(w_out[0].shape, w_out[1]), vec, vec,
                     _one_layer(wgu[0].shape, wgu[1]), _one_layer(wd[0].shape, wd[1]), vec]),
        out_specs=pl.BlockSpec((TM, d), row),
        out_shape=jax.ShapeDtypeStruct((n, d), F32),
        scratch_shapes=[pltpu.VMEM((TM, D_FF), BF16)],
        compiler_params=_params("arbitrary"),
        name="mix_ffn",
    )(*acts, h, w_out[0], gmix.reshape(1, d), gpre.reshape(1, d), wgu[0], wd[0], gpost.reshape(1, d))


def _rot_half_cols(w):
    half = w.shape[-1] // 2
    return jnp.concatenate([-w[..., half:], w[..., :half]], axis=-1)


def _prep_even_weights(w_in, w_uq, w_ukv):
    d = w_in.shape[0]
    o1, o2, o3 = MLA_Q_LORA, MLA_Q_LORA + MLA_KV_LORA, MLA_Q_LORA + MLA_KV_LORA + MLA_ROPE
    hw = MOBA_HEADS * HEAD_DIM
    z = lambda *shape: jnp.zeros(shape, F32)
    kr = w_in[:, o2:o3]
    w_in_ext = jnp.concatenate(
        [w_in[:, :o1], z(d, MLA_NOPE), kr, _rot_half_cols(kr), w_in[:, o1:o2],
         w_in[:, o3:o3 + hw] * (HEAD_DIM ** -0.5), w_in[:, o3 + hw:]], axis=1).astype(BF16)

    wq = w_uq.reshape(MLA_Q_LORA, MLA_HEADS, MLA_NOPE + MLA_ROPE)
    nope, rp = wq[..., :MLA_NOPE], wq[..., MLA_NOPE:]
    assert MLA_NOPE + 2 * MLA_ROPE == LANES
    w_uq_ext = jnp.concatenate([nope, rp, _rot_half_cols(rp)], axis=-1).reshape(MLA_Q_LORA, _QW).astype(BF16)

    wkv = w_ukv.reshape(MLA_KV_LORA, MLA_HEADS, MLA_NOPE + MLA_V)
    k_nope, v = wkv[..., :MLA_NOPE], wkv[..., MLA_NOPE:]
    w_k = jnp.concatenate([k_nope, jnp.zeros_like(k_nope)], axis=-1).reshape(MLA_KV_LORA, _QW).astype(BF16)
    v = v.reshape(MLA_KV_LORA, MLA_HEADS // 2, 2, MLA_V)
    zero = jnp.zeros_like(v[:, :, 0])
    w_v = jnp.stack([jnp.concatenate([v[:, :, 0], zero], axis=-1),
                     jnp.concatenate([zero, v[:, :, 1]], axis=-1)], axis=2)
    w_v = w_v.reshape(MLA_KV_LORA, _QW).astype(BF16)
    return w_in_ext, w_uq_ext, w_k, w_v


def _rope_tables(seq):
    half = MLA_ROPE // 2
    inv_freq = ROPE_THETA ** (-jnp.arange(half, dtype=F32) / half)
    ang = jnp.arange(seq).astype(F32)[:, None] * inv_freq[None, :]
    cos, sin = jnp.cos(ang), jnp.sin(ang)
    ones, zeros = jnp.ones((seq, MLA_NOPE), F32), jnp.zeros((seq, MLA_NOPE), F32)
    tail = jnp.zeros((seq, LANES - MLA_NOPE - MLA_ROPE), F32)
    scale = (MLA_NOPE + MLA_ROPE) ** -0.5 * LOG2_E
    cq = jnp.concatenate([ones, cos, cos, tail], axis=1) * scale
    sq = jnp.concatenate([zeros, sin, sin, tail], axis=1) * scale
    ck = jnp.concatenate([zeros, cos, cos, tail], axis=1)
    sk = jnp.concatenate([zeros, sin, sin, tail], axis=1)
    return cq, sq, ck, sk


def kernel(x, mix_pre_g, mix_post_g, ffn_pre_g, ffn_post_g, ab_w_in, mla_q_norm_g, mla_w_uq,
           mla_kv_norm_g, mla_w_ukv, ab_w_out, rel_bias, sb_w_qkv, sb_w_out, ffn_w_gate_up, ffn_w_down):
    b, s, d = x.shape
    n = b * s
    h = x.reshape(n, d)
    tables = _rope_tables(s)
    bias_tables = _moba_bias_tables(rel_bias)
    ab_w_out, sb_w_qkv, sb_w_out, ffn_w_gate_up, ffn_w_down = (
        w.astype(BF16) for w in (ab_w_out, sb_w_qkv, sb_w_out, ffn_w_gate_up, ffn_w_down))
    for layer in range(DEPTH):
        if layer % 2 == 0:
            e = layer // 2
            w_in, w_uq, w_k, w_v = _prep_even_weights(ab_w_in[e], mla_w_uq[e], mla_w_ukv[e])
            q, k, v, moba_qkv = _even_in(h, mix_pre_g[layer], w_in, mla_q_norm_g[e], w_uq,
                                         mla_kv_norm_g[e], w_k, w_v, tables, s)
            mla_out = _mla_attention(q.reshape(b, s, -1), k.reshape(b, s, -1), v.reshape(b, s, -1))
            moba_out = _moba_attention(moba_qkv.reshape(b, s, -1), bias_tables)
            mixed, w_out = [mla_out.reshape(n, -1), moba_out.reshape(n, -1)], (ab_w_out, e)
        else:
            o = layer // 2
            qkv = _norm_matmul(h, mix_pre_g[layer], sb_w_qkv, o,
                               SB_HEADS * HEAD_DIM, HEAD_DIM ** -0.5 * LOG2_E)
            mixed, w_out = [_sb_attention(qkv.reshape(b, s, -1)).reshape(n, -1)], (sb_w_out, o)
        h = _mix_ffn(mixed, h, w_out, mix_post_g[layer], ffn_pre_g[layer],
                     (ffn_w_gate_up, layer), (ffn_w_down, layer), ffn_post_g[layer])
    return h.reshape(b, s, d)
```
